```python
import jax
import jax.numpy as jnp
from jax import lax
import numpy as np

D_MODEL = 1024
BATCH = 16
SEQ = 2048
DEPTH = 1
DEC_BATCH = 128
DEC_SEQ = 8
PAST_LEN = 8192
PAGE_SIZE = 128

D_CONV = D_MODEL // 2
N_HEADS = 8
QK_NOPE_DIM = 64
QK_ROPE_DIM = 32
QK_DIM = QK_NOPE_DIM + QK_ROPE_DIM
V_DIM = 64
D_ATTN = N_HEADS * V_DIM
D_MIX = D_CONV + D_ATTN
Q_LORA_RANK = 3 * D_MODEL // 8
KV_LORA_RANK = D_MODEL // 4
D_IN = 2 * D_CONV + Q_LORA_RANK + KV_LORA_RANK + QK_ROPE_DIM
CONV_WIDTH = 31
N_EXPERTS = 32
TOP_K = 4
D_EXPERT = D_MODEL
SWIGLU_LIMIT = 7.0
SWIGLU_ALPHA = 1.702
ROPE_THETA = 10000.0
Q_BLOCK = 128
EXPERT_BLOCK = 128
NORM_EPS = 1e-6
ATTN_SCALE = QK_DIM ** -0.5

kernel_name = "hymba_conformer_mla_moe_step"


def _rmsnorm(x, g):
    xf = x.astype(jnp.float32)
    y = xf * lax.rsqrt(jnp.mean(xf * xf, axis=-1, keepdims=True) + NORM_EPS)
    return (y * g.astype(jnp.float32)).astype(x.dtype)


def _layernorm(x, g, b):
    xf = x.astype(jnp.float32)
    mu = jnp.mean(xf, axis=-1, keepdims=True)
    xc = xf - mu
    var = jnp.mean(xc * xc, axis=-1, keepdims=True)
    return (xc * lax.rsqrt(var + NORM_EPS) * g.astype(jnp.float32) + b.astype(jnp.float32)).astype(x.dtype)


def _rope_tables(pos):
    half = QK_ROPE_DIM // 2
    inv_freq = ROPE_THETA ** (-jnp.arange(half, dtype=jnp.float32) / half)
    ang = pos.astype(jnp.float32)[:, None] * inv_freq[None, :]
    return jnp.cos(ang), jnp.sin(ang)


def _rope(x, cos, sin):
    half = QK_ROPE_DIM // 2
    x1, x2 = x[..., :half], x[..., half:]
    cos = cos.astype(x.dtype)
    sin = sin.astype(x.dtype)
    return jnp.concatenate([x1 * cos - x2 * sin, x2 * cos + x1 * sin], axis=-1)


def _qk_norm(x, g_pairs):
    g_rot = g_pairs[QK_NOPE_DIM:]
    g = jnp.concatenate([g_pairs[:QK_NOPE_DIM], g_rot, g_rot])
    return _rmsnorm(x, g)


def _mixer_inputs(xn, pos, w_in, g_q_lat, w_uq, g_kv_lat, g_qk_q):
    proj = xn @ w_in
    a, gt, q_lat, kv_lat, k_rope = jnp.split(
        proj, [D_CONV, 2 * D_CONV, 2 * D_CONV + Q_LORA_RANK, 2 * D_CONV + Q_LORA_RANK + KV_LORA_RANK], axis=-1)
    glu = a * jax.nn.sigmoid(gt)
    cos, sin = _rope_tables(pos)
    q = jnp.einsum('btr,rhd->bthd', _rmsnorm(q_lat, g_q_lat), w_uq)
    q_rot = _rope(q[..., QK_NOPE_DIM:], cos[:, None, :], sin[:, None, :])
    q = _qk_norm(jnp.concatenate([q[..., :QK_NOPE_DIM], q_rot], axis=-1), g_qk_q)
    c_kv = _rmsnorm(kv_lat, g_kv_lat)
    k_rope = _rope(k_rope, cos, sin)
    return glu, q, c_kv, k_rope


def _latent_keys_values(c_kv, k_rope, w_ukv, g_qk_k):
    kv = jnp.einsum('...lc,chd->...lhd', c_kv, w_ukv)
    k_nope, v = kv[..., :QK_NOPE_DIM], kv[..., QK_NOPE_DIM:]
    k_r = jnp.broadcast_to(k_rope[..., None, :], k_nope.shape[:-1] + (QK_ROPE_DIM,))
    k = _qk_norm(jnp.concatenate([k_nope, k_r], axis=-1), g_qk_k)
    return k, v


def _attend(q, k, v, q_pos, k_pos):
    s = jnp.einsum('...qhd,...khd->...hqk', q, k).astype(jnp.float32) * ATTN_SCALE
    s = jnp.where(k_pos[None, :] <= q_pos[:, None], s, -jnp.inf)
    p = jax.nn.softmax(s, axis=-1).astype(v.dtype)
    return jnp.einsum('...hqk,...khd->...qhd', p, v)


def _mla_prompt(q, k, v):
    b, s = q.shape[0], q.shape[1]
    n_blk = s // Q_BLOCK
    q_blocks = q.reshape(b, n_blk, Q_BLOCK, N_HEADS, QK_DIM).swapaxes(0, 1)
    k_pos = jnp.arange(s)

    def one_block(args):
        qb, blk = args
        q_pos = blk * Q_BLOCK + jnp.arange(Q_BLOCK)
        return _attend(qb, k, v, q_pos, k_pos)

    o = lax.map(one_block, (q_blocks, jnp.arange(n_blk)))
    return o.swapaxes(0, 1).reshape(b, s, D_ATTN)


def _mla_sample(q, c_new, kr_new, page_table, cache_c, cache_kr, w_ukv, g_qk_k):
    n_past = page_table.shape[1] * PAGE_SIZE
    t = q.shape[1]
    q_pos = n_past + jnp.arange(t)
    k_pos = jnp.arange(n_past + t)

    def one_seq(args):
        pages, qs, cn, krn = args
        c = jnp.concatenate([cache_c[pages].reshape(n_past, KV_LORA_RANK), cn], axis=0)
        kr = jnp.concatenate([cache_kr[pages].reshape(n_past, QK_ROPE_DIM), krn], axis=0)
        k, v = _latent_keys_values(c, kr, w_ukv, g_qk_k)
        return _attend(qs, k, v, q_pos, k_pos)

    o = lax.map(one_seq, (page_table, q, c_new, kr_new))
    return o.reshape(q.shape[0], t, D_ATTN)


def _conv_module(buf, conv_w, conv_b, ln_g, ln_b):
    y = lax.conv_general_dilated(buf, conv_w[:, None, :], window_strides=(1,), padding='VALID',
                                 dimension_numbers=('NWC', 'WIO', 'NWC'),
                                 feature_group_count=D_CONV) + conv_b
    return jax.nn.silu(_layernorm(y, ln_g, ln_b))


def _merge(conv_out, attn_out, g_out_conv, g_out_attn, w_out):
    y = jnp.concatenate([_rmsnorm(conv_out, g_out_conv), _rmsnorm(attn_out, g_out_attn)], axis=-1)
    return y @ w_out


def _moe(xt, w_router, b_router, w_gate_up, b_gate_up, w_down, b_down):
    n_tok = xt.shape[0]
    n_assign = n_tok * TOP_K
    logits = (xt @ w_router).astype(jnp.float32) + b_router.astype(jnp.float32)
    top_logit, top_expert = lax.top_k(logits, TOP_K)
    gates = jax.nn.softmax(top_logit, axis=-1)
    flat_expert = top_expert.reshape(-1)
    order = jnp.argsort(flat_expert)
    sorted_expert = flat_expert[order]
    counts = jnp.bincount(flat_expert, length=N_EXPERTS)
    padded = (counts + EXPERT_BLOCK - 1) // EXPERT_BLOCK * EXPERT_BLOCK
    start_sorted = jnp.cumsum(counts) - counts
    padded_end = jnp.cumsum(padded)
    start_padded = padded_end - padded
    slot = start_padded[sorted_expert] + jnp.arange(n_assign) - start_sorted[sorted_expert]
    n_blocks = -(-n_assign // EXPERT_BLOCK) + N_EXPERTS
    x_slots = jnp.zeros((n_blocks * EXPERT_BLOCK, D_MODEL), xt.dtype).at[slot].set(xt[order // TOP_K])
    block_expert = jnp.minimum(
        jnp.searchsorted(padded_end, jnp.arange(n_blocks) * EXPERT_BLOCK, side='right'), N_EXPERTS - 1)

    def expert_block(args):
        xb, e = args
        gu = xb @ w_gate_up[e] + b_gate_up[e]
        gate = jnp.minimum(gu[:, :D_EXPERT], SWIGLU_LIMIT)
        up = jnp.clip(gu[:, D_EXPERT:], -SWIGLU_LIMIT, SWIGLU_LIMIT)
        h = (up + 1.0) * gate * jax.nn.sigmoid(SWIGLU_ALPHA * gate)
        return h @ w_down[e] + b_down[e]

    y_slots = lax.map(expert_block, (x_slots.reshape(n_blocks, EXPERT_BLOCK, D_MODEL), block_expert))
    y_slots = y_slots.reshape(n_blocks * EXPERT_BLOCK, D_MODEL)
    y_assign = jnp.zeros((n_assign, D_MODEL), xt.dtype).at[order].set(y_slots[slot])
    return jnp.einsum('tkd,tk->td', y_assign.reshape(n_tok, TOP_K, D_MODEL), gates.astype(xt.dtype))


def setup_inputs(seed: int = 0) -> dict:
    key = jax.random.key(seed)
    ks = jax.random.split(key, 32)
    f32 = jnp.float32
    n_pages = PAST_LEN // PAGE_SIZE
    n_used = DEC_BATCH * n_pages
    n_pool = n_used + n_used // 4 + 1

    def normal(k, shape, scale):
        return scale * jax.random.normal(k, shape, f32)

    def gain(k, n):
        return 1.0 + normal(k, (DEPTH, n), 0.1)

    page_table = jax.random.permutation(ks[5], n_pool)[:n_used].reshape(DEC_BATCH, n_pages).astype(jnp.int32)
    return {
        "x_prompt": normal(ks[0], (BATCH, SEQ, D_MODEL), 1.0),
        "x_sample": normal(ks[1], (DEC_BATCH, DEC_SEQ, D_MODEL), 1.0),
        "cache_kv_latent": normal(ks[2], (DEPTH, n_pool, PAGE_SIZE, KV_LORA_RANK), 1.0),
        "cache_k_rope": normal(ks[3], (DEPTH, n_pool, PAGE_SIZE, QK_ROPE_DIM), 1.0),
        "state_conv": normal(ks[4], (DEPTH, DEC_BATCH, CONV_WIDTH - 1, D_CONV), 0.5),
        "page_table": page_table,
        "norm_mix": gain(ks[6], D_MODEL),
        "w_in": normal(ks[7], (DEPTH, D_MODEL, D_IN), D_MODEL ** -0.5),
        "norm_q_lat": gain(ks[8], Q_LORA_RANK),
        "w_uq": normal(ks[9], (DEPTH, Q_LORA_RANK, N_HEADS, QK_DIM), Q_LORA_RANK ** -0.5),
        "norm_kv_lat": gain(ks[10], KV_LORA_RANK),
        "w_ukv": normal(ks[11], (DEPTH, KV_LORA_RANK, N_HEADS, QK_NOPE_DIM + V_DIM), KV_LORA_RANK ** -0.5),
        "norm_qk_q": gain(ks[12], QK_NOPE_DIM + QK_ROPE_DIM // 2),
        "norm_qk_k": gain(ks[13], QK_NOPE_DIM + QK_ROPE_DIM // 2),
        "conv_w": normal(ks[14], (DEPTH, CONV_WIDTH, D_CONV), CONV_WIDTH ** -0.5),
        "conv_b": normal(ks[15], (DEPTH, D_CONV), 0.01),
        "conv_ln_g": gain(ks[16], D_CONV),
        "conv_ln_b": normal(ks[17], (DEPTH, D_CONV), 0.01),
        "norm_out_conv": gain(ks[18], D_CONV),
        "norm_out_attn": gain(ks[19], D_ATTN),
        "w_out": normal(ks[20], (DEPTH, D_MIX, D_MODEL), D_MIX ** -0.5),
        "norm_ffn": gain(ks[21], D_MODEL),
        "w_router": normal(ks[22], (DEPTH, D_MODEL, N_EXPERTS), D_MODEL ** -0.5),
        "b_router": normal(ks[23], (DEPTH, N_EXPERTS), 0.01),
        "w_gate_up": normal(ks[24], (DEPTH, N_EXPERTS, D_MODEL, 2 * D_EXPERT), D_MODEL ** -0.5),
        "b_gate_up": normal(ks[25], (DEPTH, N_EXPERTS, 2 * D_EXPERT), 0.01),
        "w_down": normal(ks[26], (DEPTH, N_EXPERTS, D_EXPERT, D_MODEL), D_EXPERT ** -0.5),
        "b_down": normal(ks[27], (DEPTH, N_EXPERTS, D_MODEL), 0.01),
    }


def reference(x_prompt, x_sample, cache_kv_latent, cache_k_rope, state_conv, page_table,
              norm_mix, w_in, norm_q_lat, w_uq, norm_kv_lat, w_ukv, norm_qk_q, norm_qk_k,
              conv_w, conv_b, conv_ln_g, conv_ln_b, norm_out_conv, norm_out_attn, w_out,
              norm_ffn, w_router, b_router, w_gate_up, b_gate_up, w_down, b_down):
    pos_prompt = jnp.arange(x_prompt.shape[1])
    pos_sample = page_table.shape[1] * PAGE_SIZE + jnp.arange(x_sample.shape[1])
    h_p, h_s = x_prompt, x_sample
    lat_p, rope_p, conv_p, lat_s, rope_s, conv_s = [], [], [], [], [], []
    for l in range(DEPTH):
        glu_p, q_p, c_p, kr_p = _mixer_inputs(_rmsnorm(h_p, norm_mix[l]), pos_prompt, w_in[l],
                                              norm_q_lat[l], w_uq[l], norm_kv_lat[l], norm_qk_q[l])
        glu_s, q_s, c_s, kr_s = _mixer_inputs(_rmsnorm(h_s, norm_mix[l]), pos_sample, w_in[l],
                                              norm_q_lat[l], w_uq[l], norm_kv_lat[l], norm_qk_q[l])
        buf_p = jnp.concatenate(
            [jnp.zeros((glu_p.shape[0], CONV_WIDTH - 1, D_CONV), glu_p.dtype), glu_p], axis=1)
        buf_s = jnp.concatenate([state_conv[l].astype(glu_s.dtype), glu_s], axis=1)
        cv_p = _conv_module(buf_p, conv_w[l], conv_b[l], conv_ln_g[l], conv_ln_b[l])
        cv_s = _conv_module(buf_s, conv_w[l], conv_b[l], conv_ln_g[l], conv_ln_b[l])
        k_p, v_p = _latent_keys_values(c_p, kr_p, w_ukv[l], norm_qk_k[l])
        at_p = _mla_prompt(q_p, k_p, v_p)
        at_s = _mla_sample(q_s, c_s, kr_s, page_table, cache_kv_latent[l], cache_k_rope[l],
                           w_ukv[l], norm_qk_k[l])
        h_p = h_p + _merge(cv_p, at_p, norm_out_conv[l], norm_out_attn[l], w_out[l])
        h_s = h_s + _merge(cv_s, at_s, norm_out_conv[l], norm_out_attn[l], w_out[l])
        lat_p.append(c_p)
        rope_p.append(kr_p)
        conv_p.append(buf_p[:, -(CONV_WIDTH - 1):])
        lat_s.append(c_s)
        rope_s.append(kr_s)
        conv_s.append(buf_s[:, -(CONV_WIDTH - 1):])
        xn_p = _rmsnorm(h_p, norm_ffn[l])
        xn_s = _rmsnorm(h_s, norm_ffn[l])
        n_p = xn_p.shape[0] * xn_p.shape[1]
        xt = jnp.concatenate([xn_p.reshape(-1, D_MODEL), xn_s.reshape(-1, D_MODEL)], axis=0)
        yt = _moe(xt, w_router[l], b_router[l], w_gate_up[l], b_gate_up[l], w_down[l], b_down[l])
        h_p = h_p + yt[:n_p].reshape(h_p.shape)
        h_s = h_s + yt[n_p:].reshape(h_s.shape)
    prompt_kv_latent = jnp.stack(lat_p)
    prompt_k_rope = jnp.stack(rope_p)
    prompt_conv_state = jnp.stack(conv_p)
    sample_kv_latent = jnp.stack(lat_s)
    sample_k_rope = jnp.stack(rope_s)
    sample_conv_state = jnp.stack(conv_s)
    return (h_p, h_s, prompt_kv_latent, prompt_k_rope, prompt_conv_state,
            sample_kv_latent, sample_k_rope, sample_conv_state)
```

```python
import functools

import jax
import jax.numpy as jnp
import numpy as np
from jax import lax
from jax.experimental import pallas as pl
from jax.experimental.pallas import tpu as pltpu

N_HEADS = 8
QK_NOPE_DIM = 64
QK_ROPE_DIM = 32
ROPE_HALF = QK_ROPE_DIM // 2
QK_DIM = QK_NOPE_DIM + QK_ROPE_DIM
V_DIM = 64
CONV_WIDTH = 31
N_EXPERTS = 32
TOP_K = 4
SWIGLU_LIMIT = 7.0
SWIGLU_ALPHA = 1.702
ROPE_THETA = 10000.0
NORM_EPS = 1e-6
ATTN_SCALE = QK_DIM ** -0.5
PAGE_SIZE = 128

LANES = 128
HEAD_PAD = LANES
VMEM_LIMIT = 56 * 1024 * 1024

f32 = jnp.float32
bf16 = jnp.bfloat16


def _cparams(*sem):
    return pltpu.CompilerParams(dimension_semantics=sem, vmem_limit_bytes=VMEM_LIMIT)


def _rope_head_tables(pos):
    inv_freq = ROPE_THETA ** (-jnp.arange(ROPE_HALF, dtype=f32) / ROPE_HALF)
    ang = pos.astype(f32)[:, None] * inv_freq[None, :]
    cos, sin = jnp.cos(ang), jnp.sin(ang)
    n = pos.shape[0]
    z = lambda w: jnp.zeros((n, w), f32)
    c = jnp.concatenate([jnp.ones((n, QK_NOPE_DIM), f32), cos, cos, z(32)], axis=1)
    s1 = jnp.concatenate([z(QK_NOPE_DIM + ROPE_HALF), sin, z(32)], axis=1)
    s2 = jnp.concatenate([z(QK_NOPE_DIM), -sin, z(ROPE_HALF + 32)], axis=1)
    return jnp.concatenate([c, s1, s2], axis=1)


def _rope_head(x, tab):
    c, s1, s2 = tab[:, :LANES], tab[:, LANES:2 * LANES], tab[:, 2 * LANES:]
    return x * c + pltpu.roll(x, ROPE_HALF, 1) * s1 + pltpu.roll(x, LANES - ROPE_HALF, 1) * s2


def _mix_in_kernel(x_ref, tab_ref, gmix_ref, win_ref, gq_ref, wuq_ref, gkv_ref, gqq_ref,
                   wk_ref, wv_ref, gqk_ref,
                   glu_ref, q_ref, c_ref, kr_ref, k_ref, v_ref, *, d_conv, q_rank, kv_rank):
    x = x_ref[...]
    xn = x * lax.rsqrt(jnp.mean(x * x, axis=-1, keepdims=True) + NORM_EPS) * gmix_ref[...]
    proj = jnp.dot(xn.astype(bf16), win_ref[...], preferred_element_type=f32)
    o = 0
    a = proj[:, o:o + d_conv]; o += d_conv
    gt = proj[:, o:o + d_conv]; o += d_conv
    q_lat = proj[:, o:o + q_rank]; o += q_rank
    kv_lat = proj[:, o:o + kv_rank]; o += kv_rank
    kr_raw = proj[:, o:o + LANES]
    glu_ref[...] = a * jax.nn.sigmoid(gt)

    tab = tab_ref[...]
    qn = q_lat * lax.rsqrt(jnp.mean(q_lat * q_lat, axis=-1, keepdims=True) + NORM_EPS) * gq_ref[...]
    q = jnp.dot(qn.astype(bf16), wuq_ref[...], preferred_element_type=f32)
    c_kv = kv_lat * lax.rsqrt(jnp.mean(kv_lat * kv_lat, axis=-1, keepdims=True) + NORM_EPS) * gkv_ref[...]
    c_ref[...] = c_kv
    kr = _rope_head(kr_raw, tab)
    kr_ref[...] = kr[:, QK_NOPE_DIM:QK_NOPE_DIM + QK_ROPE_DIM]
    c_bf = c_kv.astype(bf16)
    kn = jnp.dot(c_bf, wk_ref[...], preferred_element_type=f32)
    v_ref[...] = jnp.dot(c_bf, wv_ref[...], preferred_element_type=f32).astype(v_ref.dtype)
    gqq = gqq_ref[...] * ATTN_SCALE
    gqk = gqk_ref[...]
    for h in range(N_HEADS):
        sl = slice(h * HEAD_PAD, (h + 1) * HEAD_PAD)
        qh = _rope_head(q[:, sl], tab)
        qh = qh * lax.rsqrt(jnp.sum(qh * qh, axis=-1, keepdims=True) * (1.0 / QK_DIM) + NORM_EPS) * gqq
        q_ref[:, sl] = qh.astype(q_ref.dtype)
        kh = kn[:, sl] + kr
        kh = kh * lax.rsqrt(jnp.sum(kh * kh, axis=-1, keepdims=True) * (1.0 / QK_DIM) + NORM_EPS) * gqk
        k_ref[:, sl] = kh.astype(k_ref.dtype)


def _head_gain(g_pairs):
    g_rot = g_pairs[QK_NOPE_DIM:]
    return jnp.concatenate([g_pairs[:QK_NOPE_DIM], g_rot, g_rot, jnp.zeros((32,), f32)])[None, :]


def _mix_in_weights(w_in, w_uq, w_ukv, d_conv, q_rank, kv_rank):
    d_model = w_in.shape[0]
    base = 2 * d_conv + q_rank + kv_rank
    w_kr = w_in[:, base:base + QK_ROPE_DIM]
    w_kr_pad = jnp.concatenate([jnp.zeros((d_model, QK_NOPE_DIM), f32), w_kr,
                                jnp.zeros((d_model, 32), f32)], axis=1)
    win = jnp.concatenate([w_in[:, :base], w_kr_pad], axis=1).astype(bf16)
    wuq = jnp.pad(w_uq, ((0, 0), (0, 0), (0, HEAD_PAD - QK_DIM))).reshape(q_rank, N_HEADS * HEAD_PAD).astype(bf16)
    wk = jnp.pad(w_ukv[:, :, :QK_NOPE_DIM], ((0, 0), (0, 0), (0, HEAD_PAD - QK_NOPE_DIM)))
    wk = wk.reshape(kv_rank, N_HEADS * HEAD_PAD).astype(bf16)
    wv = w_ukv[:, :, QK_NOPE_DIM:].reshape(kv_rank, N_HEADS * V_DIM).astype(bf16)
    return win, wuq, wk, wv


def _mix_in(x2d, tab, tm, g_mix, win, g_q_lat, wuq, g_kv_lat, g_qk_q, wk, wv, g_qk_k, d_conv, q_rank, kv_rank):
    n, d_model = x2d.shape
    n_tab_blocks = tab.shape[0] // tm
    row = lambda i: (i, 0)
    const = lambda i: (0, 0)
    full = lambda a: pl.BlockSpec(a.shape, const)
    gm, gq, gkv = g_mix[None, :], g_q_lat[None, :], g_kv_lat[None, :]
    gqq, gqk = _head_gain(g_qk_q), _head_gain(g_qk_k)
    hp = N_HEADS * HEAD_PAD
    out_shape = (
        jax.ShapeDtypeStruct((n, d_conv), f32),
        jax.ShapeDtypeStruct((n, hp), bf16),
        jax.ShapeDtypeStruct((n, kv_rank), f32),
        jax.ShapeDtypeStruct((n, QK_ROPE_DIM), f32),
        jax.ShapeDtypeStruct((n, hp), bf16),
        jax.ShapeDtypeStruct((n, N_HEADS * V_DIM), bf16),
    )
    return pl.pallas_call(
        functools.partial(_mix_in_kernel, d_conv=d_conv, q_rank=q_rank, kv_rank=kv_rank),
        grid=(n // tm,),
        in_specs=[pl.BlockSpec((tm, d_model), row),
                  pl.BlockSpec((tm, tab.shape[1]), lambda i: (i % n_tab_blocks, 0)),
                  full(gm), full(win), full(gq), full(wuq), full(gkv), full(gqq), full(wk), full(wv), full(gqk)],
        out_specs=tuple(pl.BlockSpec((tm, s.shape[1]), row) for s in out_shape),
        out_shape=out_shape,
        compiler_params=_cparams("parallel"),
        name="mix_in",
    )(x2d, tab, gm, win, gq, wuq, gkv, gqq, wk, wv, gqk)


CONV_PAD = 32


SUBLANES = 8


def _conv_taps(window, w_ref, b_ref, ls, tc):
    lead = CONV_PAD - (CONV_WIDTH - 1)
    acc = jnp.zeros((tc, LANES), f32) + b_ref[:, ls]
    for r in range(SUBLANES):
        taps = [(a, a * SUBLANES + r - lead) for a in range(CONV_PAD // SUBLANES + 1)
                if 0 <= a * SUBLANES + r - lead < CONV_WIDTH]
        shifted = window[r:r + tc + SUBLANES * taps[-1][0]]
        for a, j in taps:
            acc = acc + shifted[a * SUBLANES:a * SUBLANES + tc] * w_ref[j:j + 1, ls]
    return acc


def _ln_swish(y, g_ref, beta_ref):
    mu = jnp.mean(y, axis=-1, keepdims=True)
    yc = y - mu
    var = jnp.mean(yc * yc, axis=-1, keepdims=True)
    z = yc * lax.rsqrt(var + NORM_EPS) * g_ref[...] + beta_ref[...]
    return z * jax.nn.sigmoid(z)


def _conv_prompt_kernel(x_ref, w_ref, b_ref, g_ref, beta_ref, o_ref, *, tc):
    t_len, n_ch = x_ref.shape[1], x_ref.shape[2]

    def chunk(idx, carry):
        t0 = pl.multiple_of(idx * tc, tc)
        h0 = pl.multiple_of(jnp.maximum(t0 - CONV_PAD, 0), SUBLANES)
        has_hist = (idx > 0).astype(f32)
        for lt in range(n_ch // LANES):
            ls = slice(lt * LANES, (lt + 1) * LANES)
            window = jnp.concatenate([x_ref[0, pl.ds(h0, CONV_PAD), ls] * has_hist,
                                      x_ref[0, pl.ds(t0, tc), ls]], axis=0)
            o_ref[0, pl.ds(t0, tc), ls] = _conv_taps(window, w_ref, b_ref, ls, tc)
        o_ref[0, pl.ds(t0, tc), :] = _ln_swish(o_ref[0, pl.ds(t0, tc), :], g_ref, beta_ref)
        return carry

    lax.fori_loop(0, t_len // tc, chunk, 0)


def _conv_sample_kernel(x_ref, st_ref, w_ref, b_ref, g_ref, beta_ref, o_ref):
    bb, tc, n_ch = x_ref.shape

    def one(b, carry):
        for lt in range(n_ch // LANES):
            ls = slice(lt * LANES, (lt + 1) * LANES)
            window = jnp.concatenate([st_ref[b, :, ls], x_ref[b, :, ls]], axis=0)
            o_ref[b, :, ls] = _conv_taps(window, w_ref, b_ref, ls, tc)
        o_ref[b] = _ln_swish(o_ref[b], g_ref, beta_ref)
        return carry

    lax.fori_loop(0, bb, one, 0)


def _conv_specs(conv_w, n_ch):
    const = lambda i: (0, 0)
    vec = pl.BlockSpec((1, n_ch), const)
    return [pl.BlockSpec(conv_w.shape, const), vec, vec, vec]


def _conv_prompt(glu, conv_w, conv_b, ln_g, ln_b, tc=128):
    n_b, t_len, n_ch = glu.shape
    blk = pl.BlockSpec((1, t_len, n_ch), lambda i: (i, 0, 0))
    return pl.pallas_call(
        functools.partial(_conv_prompt_kernel, tc=tc),
        grid=(n_b,),
        in_specs=[blk] + _conv_specs(conv_w, n_ch),
        out_specs=blk,
        out_shape=jax.ShapeDtypeStruct(glu.shape, f32),
        compiler_params=_cparams("parallel"),
        name="conv_prompt",
    )(glu, conv_w, conv_b[None, :], ln_g[None, :], ln_b[None, :])


def _conv_sample(glu, state, conv_w, conv_b, ln_g, ln_b, bb=32):
    n_b, t_len, n_ch = glu.shape
    hist = jnp.pad(state, ((0, 0), (CONV_PAD - state.shape[1], 0), (0, 0)))
    return pl.pallas_call(
        _conv_sample_kernel,
        grid=(n_b // bb,),
        in_specs=[pl.BlockSpec((bb, t_len, n_ch), lambda i: (i, 0, 0)),
                  pl.BlockSpec((bb, CONV_PAD, n_ch), lambda i: (i, 0, 0))] + _conv_specs(conv_w, n_ch),
        out_specs=pl.BlockSpec((bb, t_len, n_ch), lambda i: (i, 0, 0)),
        out_shape=jax.ShapeDtypeStruct(glu.shape, f32),
        compiler_params=_cparams("parallel"),
        name="conv_sample",
    )(glu, hist, conv_w, conv_b[None, :], ln_g[None, :], ln_b[None, :])


NT_DIMS = (((1,), (1,)), ((), ()))


def _attn_prompt_kernel(q_ref, k_ref, v_ref, o_ref, *, tq, tk):
    qi = pl.program_id(2)
    sub = tq // tk
    outs = []
    for j in range(2):
        hs = slice(j * HEAD_PAD, (j + 1) * HEAD_PAD)
        q = q_ref[0, :, hs]

        def step(kb, carry, masked, q=q, hs=hs):
            m, l, acc = carry
            k0 = pl.multiple_of(kb * tk, tk)
            s = lax.dot_general(q, k_ref[0, pl.ds(k0, tk), hs], NT_DIMS, preferred_element_type=f32)
            if masked:
                row = qi * tq + lax.broadcasted_iota(jnp.int32, (tq, tk), 0)
                col = k0 + lax.broadcasted_iota(jnp.int32, (tq, tk), 1)
                s = jnp.where(col <= row, s, -jnp.inf)
            m_new = jnp.maximum(m, jnp.max(s, axis=-1, keepdims=True))
            alpha = jnp.exp(m - m_new)
            p = jnp.exp(s - m_new)
            l = alpha * l + jnp.sum(p, axis=-1, keepdims=True)
            acc = alpha * acc + jnp.dot(p.astype(bf16), v_ref[0, pl.ds(k0, tk), :], preferred_element_type=f32)
            return m_new, l, acc

        carry = (jnp.full((tq, 1), -jnp.inf, f32), jnp.zeros((tq, 1), f32), jnp.zeros((tq, LANES), f32))
        carry = lax.fori_loop(0, qi * sub, functools.partial(step, masked=False), carry)
        for d in range(sub):
            carry = step(qi * sub + d, carry, True)
        outs.append(carry[2] / carry[1])
    lane = lax.broadcasted_iota(jnp.int32, (tq, LANES), 1)
    o_ref[0] = jnp.where(lane < V_DIM, outs[0], outs[1])


def _attn_prompt(q, k, v, tq, tk):
    n_b, t_len, _ = q.shape
    return pl.pallas_call(
        functools.partial(_attn_prompt_kernel, tq=tq, tk=tk),
        grid=(n_b, N_HEADS // 2, t_len // tq),
        in_specs=[pl.BlockSpec((1, tq, 2 * HEAD_PAD), lambda b, h, i: (b, i, h)),
                  pl.BlockSpec((1, t_len, 2 * HEAD_PAD), lambda b, h, i: (b, 0, h)),
                  pl.BlockSpec((1, t_len, 2 * V_DIM), lambda b, h, i: (b, 0, h))],
        out_specs=pl.BlockSpec((1, tq, 2 * V_DIM), lambda b, h, i: (b, i, h)),
        out_shape=jax.ShapeDtypeStruct((n_b, t_len, N_HEADS * V_DIM), f32),
        compiler_params=_cparams("parallel", "parallel", "arbitrary"),
        name="attn_prompt",
    )(q, k, v)


TN_DIMS = (((0,), (0,)), ((), ()))
N_TH = 64


def _q_absorb_kernel(q_ref, gk_ref, wukt_ref, sel_ref, qabs_ref, qr_ref):
    gk = gk_ref[...]
    kv_rank = wukt_ref.shape[-1]
    for h in range(N_HEADS):
        qg = (q_ref[:, h * HEAD_PAD:(h + 1) * HEAD_PAD].astype(f32) * gk).astype(bf16)
        qabs_ref[:, h * kv_rank:(h + 1) * kv_rank] = jnp.dot(
            qg, wukt_ref[h], preferred_element_type=f32).astype(qabs_ref.dtype)
        qr_ref[:, h * QK_ROPE_DIM:(h + 1) * QK_ROPE_DIM] = jnp.dot(
            qg, sel_ref[...], preferred_element_type=f32).astype(qr_ref.dtype)


def _q_absorb(q2d, g_qk_k, w_ukv):
    n = q2d.shape[0]
    kv_rank = w_ukv.shape[0]
    wukt = jnp.transpose(w_ukv[:, :, :QK_NOPE_DIM], (1, 2, 0))
    wukt = jnp.pad(wukt, ((0, 0), (0, HEAD_PAD - QK_NOPE_DIM), (0, 0))).astype(bf16)
    sel = jnp.zeros((HEAD_PAD, QK_ROPE_DIM), f32).at[
        QK_NOPE_DIM + jnp.arange(QK_ROPE_DIM), jnp.arange(QK_ROPE_DIM)].set(1.0).astype(bf16)
    gk = _head_gain(g_qk_k)
    full = lambda a: pl.BlockSpec(a.shape, lambda i: (0,) * a.ndim)
    return pl.pallas_call(
        _q_absorb_kernel,
        grid=(1,),
        in_specs=[full(q2d), full(gk), full(wukt), full(sel)],
        out_specs=(pl.BlockSpec((n, N_HEADS * kv_rank), lambda i: (0, 0)),
                   pl.BlockSpec((n, N_HEADS * QK_ROPE_DIM), lambda i: (0, 0))),
        out_shape=(jax.ShapeDtypeStruct((n, N_HEADS * kv_rank), bf16),
                   jax.ShapeDtypeStruct((n, N_HEADS * QK_ROPE_DIM), bf16)),
        compiler_params=_cparams("arbitrary"),
        name="q_absorb",
    )(q2d, gk, wukt, sel)


def _lat_chunk(c_bf, kr, qabs, qr, wuk, state, mask=None):
    m, l, acc = state
    kn = jnp.dot(c_bf, wuk, preferred_element_type=f32)
    sq = kn * kn
    t = sq[:, :LANES]
    for i in range(1, sq.shape[1] // LANES):
        t = t + sq[:, i * LANES:(i + 1) * LANES]
    for sh in (N_HEADS, 2 * N_HEADS, 4 * N_HEADS, 8 * N_HEADS):
        t = t + pltpu.roll(t, sh, 1)
    ssq = t + jnp.sum(kr * kr, axis=-1, keepdims=True)
    rinv = lax.rsqrt(ssq * (1.0 / QK_DIM) + NORM_EPS)
    s = lax.dot_general(c_bf, qabs, NT_DIMS, preferred_element_type=f32)
    s = s + lax.dot_general(kr.astype(bf16), qr, NT_DIMS, preferred_element_type=f32)
    s = s * rinv[:, :N_TH]
    if mask is not None:
        s = jnp.where(mask, s, -jnp.inf)
    m_new = jnp.maximum(m, jnp.max(s, axis=0, keepdims=True))
    alpha = jnp.exp(m - m_new)
    p = jnp.exp(s - m_new)
    l = alpha * l + jnp.sum(p, axis=0, keepdims=True)
    acc = _row_to_col(alpha) * acc + lax.dot_general(p.astype(bf16), c_bf, TN_DIMS, preferred_element_type=f32)
    return m_new, l, acc


def _row_to_col(v):
    n = v.shape[1]
    eye = lax.broadcasted_iota(jnp.int32, (n, n), 0) == lax.broadcasted_iota(jnp.int32, (n, n), 1)
    return jnp.sum(jnp.where(eye, jnp.broadcast_to(v, (n, n)), 0.0), axis=1, keepdims=True)


def _attn_sample_kernel(pt_ref, qabs_ref, qr_ref, cnew_ref, krnew_ref, wuk_ref, *rest, pp):
    c_pages = rest[:pp]
    kr_pages = rest[pp:2 * pp]
    o_ref = rest[2 * pp]
    cbf_s, kr_s, m_s, l_s, acc_s = rest[2 * pp + 1:]
    j = pl.program_id(1)

    @pl.when(j == 0)
    def _():
        m_s[...] = jnp.full(m_s.shape, -jnp.inf, f32)
        l_s[...] = jnp.zeros(l_s.shape, f32)
        acc_s[...] = jnp.zeros(acc_s.shape, f32)

    for i in range(pp):
        cbf_s[i * PAGE_SIZE:(i + 1) * PAGE_SIZE, :] = c_pages[i][0].astype(bf16)
        kr_s[i * PAGE_SIZE:(i + 1) * PAGE_SIZE, :] = kr_pages[i][0]
    qabs, qr, wuk = qabs_ref[0], qr_ref[0], wuk_ref[...]
    state = _lat_chunk(cbf_s[...], kr_s[...], qabs, qr, wuk, (m_s[...], l_s[...], acc_s[...]))
    m_s[...], l_s[...], acc_s[...] = state

    @pl.when(j == pl.num_programs(1) - 1)
    def _():
        t_new = cnew_ref.shape[1]
        key_t = lax.broadcasted_iota(jnp.int32, (t_new, N_TH), 0)
        q_t = lax.broadcasted_iota(jnp.int32, (t_new, N_TH), 1) // N_HEADS
        m, l, acc = _lat_chunk(cnew_ref[0].astype(bf16), krnew_ref[0], qabs, qr, wuk, state, mask=key_t <= q_t)
        o_ref[0] = acc / _row_to_col(l)


def _attn_sample(page_table, qabs, qr, c_new, kr_new, cache_c, cache_kr, wuk_perm, pp):
    n_seq, n_pages = page_table.shape
    kv_rank = cache_c.shape[-1]
    t_new = c_new.shape[1]
    seq3 = lambda s, j, pt: (s, 0, 0)

    def page(i):
        return lambda s, j, pt: (pt[s, j * pp + i], 0, 0)

    in_specs = [pl.BlockSpec((1, N_TH, kv_rank), seq3),
                pl.BlockSpec((1, N_TH, QK_ROPE_DIM), seq3),
                pl.BlockSpec((1, t_new, kv_rank), seq3),
                pl.BlockSpec((1, t_new, QK_ROPE_DIM), seq3),
                pl.BlockSpec(wuk_perm.shape, lambda s, j, pt: (0, 0))]
    in_specs += [pl.BlockSpec((1, PAGE_SIZE, kv_rank), page(i)) for i in range(pp)]
    in_specs += [pl.BlockSpec((1, PAGE_SIZE, QK_ROPE_DIM), page(i)) for i in range(pp)]
    grid_spec = pltpu.PrefetchScalarGridSpec(
        num_scalar_prefetch=1,
        grid=(n_seq, n_pages // pp),
        in_specs=in_specs,
        out_specs=pl.BlockSpec((1, N_TH, kv_rank), seq3),
        scratch_shapes=[pltpu.VMEM((pp * PAGE_SIZE, kv_rank), bf16),
                        pltpu.VMEM((pp * PAGE_SIZE, QK_ROPE_DIM), f32),
                        pltpu.VMEM((1, N_TH), f32), pltpu.VMEM((1, N_TH), f32),
                        pltpu.VMEM((N_TH, kv_rank), f32)],
    )
    return pl.pallas_call(
        functools.partial(_attn_sample_kernel, pp=pp),
        grid_spec=grid_spec,
        out_shape=jax.ShapeDtypeStruct((n_seq, N_TH, kv_rank), f32),
        compiler_params=_cparams("parallel", "arbitrary"),
        name="attn_sample",
    )(page_table, qabs, qr, c_new, kr_new, wuk_perm, *([cache_c] * pp), *([cache_kr] * pp))


def _v_up_kernel(lat_ref, wuv_ref, o_ref):
    rows = lat_ref.shape[0]
    full = jnp.dot(lat_ref[...].astype(bf16), wuv_ref[...], preferred_element_type=f32)
    head_of_row = lax.broadcasted_iota(jnp.int32, full.shape, 0) % N_HEADS
    head_of_lane = lax.broadcasted_iota(jnp.int32, full.shape, 1) // V_DIM
    own = jnp.where(head_of_row == head_of_lane, full, 0.0)
    o_ref[...] = jnp.sum(own.reshape(rows // N_HEADS, N_HEADS, full.shape[1]), axis=1)


def _v_up(lat2d, wuv, rows):
    n, kv_rank = lat2d.shape
    d_attn = wuv.shape[1]
    return pl.pallas_call(
        _v_up_kernel,
        grid=(n // rows,),
        in_specs=[pl.BlockSpec((rows, kv_rank), lambda i: (i, 0)), pl.BlockSpec(wuv.shape, lambda i: (0, 0))],
        out_specs=pl.BlockSpec((rows // N_HEADS, d_attn), lambda i: (i, 0)),
        out_shape=jax.ShapeDtypeStruct((n // N_HEADS, d_attn), f32),
        compiler_params=_cparams("parallel"),
        name="v_up",
    )(lat2d, wuv)


def _attn_sample_all(q_s2d, c_new, kr_new, page_table, cache_c, cache_kr, w_ukv, g_qk_k, pp):
    n_seq, t_new, kv_rank = c_new.shape
    qabs, qr = _q_absorb(q_s2d, g_qk_k, w_ukv)
    qabs = qabs.reshape(n_seq, t_new * N_HEADS, kv_rank)
    qr = qr.reshape(n_seq, t_new * N_HEADS, QK_ROPE_DIM)
    wuk_perm = jnp.transpose(w_ukv[:, :, :QK_NOPE_DIM], (0, 2, 1)).reshape(kv_rank, -1).astype(bf16)
    lat = _attn_sample(page_table, qabs, qr, c_new, kr_new, cache_c, cache_kr, wuk_perm, pp)
    wuv = w_ukv[:, :, QK_NOPE_DIM:].reshape(kv_rank, N_HEADS * V_DIM).astype(bf16)
    rows = min(512, n_seq * N_TH)
    return _v_up(lat.reshape(n_seq * N_TH, kv_rank), wuv, rows)


def _rms(x, g):
    return x * lax.rsqrt(jnp.mean(x * x, axis=-1, keepdims=True) + NORM_EPS) * g


def _split_bf16(x):
    hi = x.astype(bf16)
    return hi, (x - hi.astype(f32)).astype(bf16)


def _merge_kernel(x_ref, cv_ref, at_ref, gc_ref, ga_ref, woc_ref, woa_ref, gf_ref, wrh_ref, wrl_ref, br_ref,
                  cnt0_ref, h_ref, xn_ref, exp_ref, gate_ref, rank_ref, cnt_ref, cnt_s):
    i = pl.program_id(0)
    tm = x_ref.shape[0]

    @pl.when(i == 0)
    def _():
        cnt_s[...] = cnt0_ref[...]

    yc = _rms(cv_ref[...], gc_ref[...]).astype(bf16)
    ya = _rms(at_ref[...], ga_ref[...]).astype(bf16)
    h = x_ref[...] + jnp.dot(yc, woc_ref[...], preferred_element_type=f32) \
        + jnp.dot(ya, woa_ref[...], preferred_element_type=f32)
    h_ref[...] = h
    xn = _rms(h, gf_ref[...])
    xh, xl = _split_bf16(xn)
    xn_ref[...] = xh
    logits = (jnp.dot(xh, wrh_ref[...], preferred_element_type=f32)
              + jnp.dot(xl, wrh_ref[...], preferred_element_type=f32)
              + jnp.dot(xh, wrl_ref[...], preferred_element_type=f32)) + br_ref[...]
    lane = lax.broadcasted_iota(jnp.int32, logits.shape, 1)
    tops, idxs = [], []
    cur = logits
    for _k in range(TOP_K):
        mk = jnp.max(cur, axis=-1, keepdims=True)
        ik = jnp.min(jnp.where(cur == mk, lane, LANES), axis=-1, keepdims=True)
        tops.append(mk)
        idxs.append(ik)
        cur = jnp.where(lane == ik, -jnp.inf, cur)
    es = [jnp.exp(t - tops[0]) for t in tops]
    den = es[0] + es[1] + es[2] + es[3]
    onehots = [(lane == ik).astype(f32) for ik in idxs]
    oh = onehots[0] + onehots[1] + onehots[2] + onehots[3]
    r_i = lax.broadcasted_iota(jnp.int32, (tm, tm), 0)
    c_i = lax.broadcasted_iota(jnp.int32, (tm, tm), 1)
    before = jnp.dot((c_i < r_i).astype(bf16), oh.astype(bf16), preferred_element_type=f32) + cnt_s[...]
    exp_out = jnp.zeros(logits.shape, jnp.int32)
    gate_out = jnp.zeros(logits.shape, f32)
    rank_out = jnp.zeros(logits.shape, jnp.int32)
    for k in range(TOP_K):
        rk = jnp.sum(onehots[k] * before, axis=-1, keepdims=True)
        exp_out = jnp.where(lane == k, idxs[k], exp_out)
        gate_out = jnp.where(lane == k, es[k] / den, gate_out)
        rank_out = jnp.where(lane == k, rk.astype(jnp.int32), rank_out)
    exp_ref[...] = exp_out
    gate_ref[...] = gate_out
    rank_ref[...] = rank_out
    cnt_s[...] = cnt_s[...] + jnp.sum(oh, axis=0, keepdims=True)
    cnt_ref[...] = cnt_s[...]


def _merge(x2d, cv, at, g_out_conv, g_out_attn, w_out, g_ffn, w_router, b_router, cnt0, tm):
    n, d_model = x2d.shape
    d_conv = cv.shape[1]
    woc, woa = w_out[:d_conv].astype(bf16), w_out[d_conv:].astype(bf16)
    wr = jnp.pad(w_router, ((0, 0), (0, LANES - N_EXPERTS)))
    wrh, wrl = _split_bf16(wr)
    br = jnp.concatenate([b_router.astype(f32), jnp.full((LANES - N_EXPERTS,), -jnp.inf, f32)])[None, :]
    gc, ga, gf = g_out_conv[None, :], g_out_attn[None, :], g_ffn[None, :]
    row = lambda i: (i, 0)
    full = lambda a: pl.BlockSpec(a.shape, lambda i: (0, 0))
    rows = lambda w: pl.BlockSpec((tm, w), row)
    out_shape = (jax.ShapeDtypeStruct((n, d_model), f32), jax.ShapeDtypeStruct((n, d_model), bf16),
                 jax.ShapeDtypeStruct((n, LANES), jnp.int32), jax.ShapeDtypeStruct((n, LANES), f32),
                 jax.ShapeDtypeStruct((n, LANES), jnp.int32), jax.ShapeDtypeStruct((1, LANES), f32))
    return pl.pallas_call(
        _merge_kernel,
        grid=(n // tm,),
        in_specs=[rows(d_model), rows(d_conv), rows(at.shape[1]), full(gc), full(ga), full(woc), full(woa),
                  full(gf), full(wrh), full(wrl), full(br), full(cnt0)],
        out_specs=(rows(d_model), rows(d_model), rows(LANES), rows(LANES), rows(LANES),
                   pl.BlockSpec((1, LANES), lambda i: (0, 0))),
        out_shape=out_shape,
        scratch_shapes=[pltpu.VMEM((1, LANES), f32)],
        compiler_params=_cparams("arbitrary"),
        name="merge",
    )(x2d, cv, at, gc, ga, woc, woa, gf, wrh, wrl, br, cnt0)


def _experts_kernel(be_ref, bv_ref, x_ref, wgu_ref, bgu_ref, wd_ref, bd_ref, y_ref, wgu_s, wd_s):
    i = pl.program_id(0)
    d_expert = wd_ref.shape[1]

    @pl.when((i == 0) | (be_ref[i] != be_ref[jnp.maximum(i - 1, 0)]))
    def _():
        wgu_s[...] = wgu_ref[0].astype(bf16)
        wd_s[...] = wd_ref[0].astype(bf16)

    @pl.when(bv_ref[i] != 0)
    def _():
        gu = jnp.dot(x_ref[...], wgu_s[...], preferred_element_type=f32) + bgu_ref[0]
        gate = jnp.minimum(gu[:, :d_expert], SWIGLU_LIMIT)
        up = jnp.clip(gu[:, d_expert:], -SWIGLU_LIMIT, SWIGLU_LIMIT)
        hid = (up + 1.0) * gate * jax.nn.sigmoid(SWIGLU_ALPHA * gate)
        y_ref[...] = jnp.dot(hid.astype(bf16), wd_s[...], preferred_element_type=f32) + bd_ref[0]

    @pl.when(bv_ref[i] == 0)
    def _():
        y_ref[...] = jnp.zeros(y_ref.shape, y_ref.dtype)


def _experts(block_expert, block_valid, x_slots, w_gate_up, b_gate_up, w_down, b_down, blk):
    n_slots, d_model = x_slots.shape
    n_exp, _, d_gu = w_gate_up.shape
    d_expert = w_down.shape[1]
    grid_spec = pltpu.PrefetchScalarGridSpec(
        num_scalar_prefetch=2,
        grid=(n_slots // blk,),
        in_specs=[pl.BlockSpec((blk, d_model), lambda i, be, bv: (i, 0)),
                  pl.BlockSpec((1, d_model, d_gu), lambda i, be, bv: (be[i], 0, 0)),
                  pl.BlockSpec((1, 1, d_gu), lambda i, be, bv: (be[i], 0, 0)),
                  pl.BlockSpec((1, d_expert, d_model), lambda i, be, bv: (be[i], 0, 0)),
                  pl.BlockSpec((1, 1, d_model), lambda i, be, bv: (be[i], 0, 0))],
        out_specs=pl.BlockSpec((blk, d_model), lambda i, be, bv: (i, 0)),
        scratch_shapes=[pltpu.VMEM((d_model, d_gu), bf16), pltpu.VMEM((d_expert, d_model), bf16)],
    )
    return pl.pallas_call(
        _experts_kernel,
        grid_spec=grid_spec,
        out_shape=jax.ShapeDtypeStruct((n_slots, d_model), f32),
        compiler_params=_cparams("arbitrary"),
        name="experts",
    )(block_expert, block_valid, x_slots, w_gate_up, b_gate_up[:, None, :], w_down, b_down[:, None, :])


def _combine_kernel(h_ref, ya_ref, gate_ref, y_ref):
    g = gate_ref[...]
    y = h_ref[...]
    for k in range(TOP_K):
        y = y + ya_ref[k] * g[:, k:k + 1]
    y_ref[...] = y


def _combine(h, y_assign, gates, tm):
    n, d_model = h.shape
    return pl.pallas_call(
        _combine_kernel,
        grid=(n // tm,),
        in_specs=[pl.BlockSpec((tm, d_model), lambda i: (i, 0)),
                  pl.BlockSpec((TOP_K, tm, d_model), lambda i: (0, i, 0)),
                  pl.BlockSpec((tm, LANES), lambda i: (i, 0))],
        out_specs=pl.BlockSpec((tm, d_model), lambda i: (i, 0)),
        out_shape=jax.ShapeDtypeStruct((n, d_model), f32),
        compiler_params=_cparams("parallel"),
        name="combine",
    )(h, y_assign, gates)


EXPERT_BLK = 512


def _moe(h_p, xn_p, route_p, h_s, xn_s, route_s, counts, w_gate_up, b_gate_up, w_down, b_down):
    n_p = h_p.shape[0]
    h = jnp.concatenate([h_p, h_s], axis=0)
    xn = jnp.concatenate([xn_p, xn_s], axis=0)
    expert = jnp.concatenate([route_p[0][:, :TOP_K], route_s[0][:, :TOP_K]], axis=0)
    gates = jnp.concatenate([route_p[1], route_s[1]], axis=0)
    rank = jnp.concatenate([route_p[2][:, :TOP_K], route_s[2][:, :TOP_K]], axis=0)
    n = h.shape[0]
    blk = EXPERT_BLK
    padded = (counts + blk - 1) // blk * blk
    padded_end = jnp.cumsum(padded)
    start_padded = padded_end - padded
    slot = start_padded[expert] + rank
    n_blocks = -(-(n * TOP_K + N_EXPERTS * (blk - 1)) // blk)
    n_slots = n_blocks * blk
    tok_of_slot = jnp.zeros((n_slots,), jnp.int32).at[slot.reshape(-1)].set(
        jnp.arange(n * TOP_K, dtype=jnp.int32) // TOP_K)
    block_start = jnp.arange(n_blocks, dtype=jnp.int32) * blk
    block_expert = jnp.minimum(jnp.searchsorted(padded_end, block_start, side='right'),
                               N_EXPERTS - 1).astype(jnp.int32)
    block_valid = (block_start < padded_end[-1]).astype(jnp.int32)
    x_slots = jnp.take(xn, tok_of_slot, axis=0)
    y_slots = _experts(block_expert, block_valid, x_slots, w_gate_up, b_gate_up, w_down, b_down, blk)
    y_assign = jnp.take(y_slots, slot.T, axis=0)
    y = _combine(h, y_assign, gates, 512)
    return y[:n_p], y[n_p:]


def _layer(x_p, x_s, cache_c, cache_kr, state, page_table, norm_mix, w_in, norm_q_lat, w_uq, norm_kv_lat, w_ukv,
           norm_qk_q, norm_qk_k, conv_w, conv_b, conv_ln_g, conv_ln_b, norm_out_conv, norm_out_attn, w_out,
           norm_ffn, w_router, b_router, w_gate_up, b_gate_up, w_down, b_down):
    n_b, t_p, d_model = x_p.shape
    n_seq, t_s, _ = x_s.shape
    d_conv = conv_w.shape[1]
    q_rank, kv_rank = norm_q_lat.shape[0], norm_kv_lat.shape[0]
    n_past = page_table.shape[1] * PAGE_SIZE
    win, wuq, wk, wv = _mix_in_weights(w_in, w_uq, w_ukv, d_conv, q_rank, kv_rank)
    mix = functools.partial(_mix_in, g_mix=norm_mix, win=win, g_q_lat=norm_q_lat, wuq=wuq, g_kv_lat=norm_kv_lat,
                            g_qk_q=norm_qk_q, wk=wk, wv=wv, g_qk_k=norm_qk_k,
                            d_conv=d_conv, q_rank=q_rank, kv_rank=kv_rank)
    x_p2, x_s2 = x_p.reshape(n_b * t_p, d_model), x_s.reshape(n_seq * t_s, d_model)
    tab_p = _rope_head_tables(jnp.arange(t_p))
    tab_s = _rope_head_tables(n_past + jnp.arange(n_seq * t_s) % t_s)
    glu_p, q_p, c_p, kr_p, k_p, v_p = mix(x_p2, tab_p, 512)
    glu_s, q_s, c_s, kr_s, _, _ = mix(x_s2, tab_s, 512)

    glu_p3, glu_s3 = glu_p.reshape(n_b, t_p, d_conv), glu_s.reshape(n_seq, t_s, d_conv)
    cv_p = _conv_prompt(glu_p3, conv_w, conv_b, conv_ln_g, conv_ln_b)
    cv_s = _conv_sample(glu_s3, state, conv_w, conv_b, conv_ln_g, conv_ln_b)

    hp = N_HEADS * HEAD_PAD
    at_p = _attn_prompt(q_p.reshape(n_b, t_p, hp), k_p.reshape(n_b, t_p, hp),
                        v_p.reshape(n_b, t_p, N_HEADS * V_DIM), 512, 512)
    c_s3, kr_s3 = c_s.reshape(n_seq, t_s, kv_rank), kr_s.reshape(n_seq, t_s, QK_ROPE_DIM)
    at_s = _attn_sample_all(q_s, c_s3, kr_s3, page_table, cache_c, cache_kr, w_ukv, norm_qk_k, 16)

    mrg = functools.partial(_merge, g_out_conv=norm_out_conv, g_out_attn=norm_out_attn, w_out=w_out,
                            g_ffn=norm_ffn, w_router=w_router, b_router=b_router, tm=512)
    cnt0 = jnp.zeros((1, LANES), f32)
    h_p, xn_p, e_p, g_p, r_p, cnt_p = mrg(x_p2, cv_p.reshape(n_b * t_p, d_conv), at_p.reshape(n_b * t_p, -1),
                                          cnt0=cnt0)
    h_s, xn_s, e_s, g_s, r_s, cnt = mrg(x_s2, cv_s.reshape(n_seq * t_s, d_conv), at_s, cnt0=cnt_p)
    counts = cnt[0, :N_EXPERTS].astype(jnp.int32)
    y_p, y_s = _moe(h_p, xn_p, (e_p, g_p, r_p), h_s, xn_s, (e_s, g_s, r_s), counts,
                    w_gate_up, b_gate_up, w_down, b_down)
    n_hist = CONV_WIDTH - 1
    conv_state_p = jnp.concatenate([jnp.zeros((n_b, max(n_hist - t_p, 0), d_conv), f32),
                                    glu_p3[:, max(t_p - n_hist, 0):]], axis=1)
    conv_state_s = jnp.concatenate([state, glu_s3], axis=1)[:, -n_hist:]
    return (y_p.reshape(x_p.shape), y_s.reshape(x_s.shape), c_p.reshape(n_b, t_p, kv_rank),
            kr_p.reshape(n_b, t_p, QK_ROPE_DIM), conv_state_p, c_s3, kr_s3, conv_state_s)


def kernel(x_prompt, x_sample, cache_kv_latent, cache_k_rope, state_conv, page_table, norm_mix, w_in, norm_q_lat,
           w_uq, norm_kv_lat, w_ukv, norm_qk_q, norm_qk_k, conv_w, conv_b, conv_ln_g, conv_ln_b, norm_out_conv,
           norm_out_attn, w_out, norm_ffn, w_router, b_router, w_gate_up, b_gate_up, w_down, b_down):
    h_p, h_s = x_prompt, x_sample
    per_layer = []
    for l in range(w_in.shape[0]):
        outs = _layer(h_p, h_s, cache_kv_latent[l], cache_k_rope[l], state_conv[l], page_table, norm_mix[l],
                      w_in[l], norm_q_lat[l], w_uq[l], norm_kv_lat[l], w_ukv[l], norm_qk_q[l], norm_qk_k[l],
                      conv_w[l], conv_b[l], conv_ln_g[l], conv_ln_b[l], norm_out_conv[l], norm_out_attn[l],
                      w_out[l], norm_ffn[l], w_router[l], b_router[l], w_gate_up[l], b_gate_up[l], w_down[l],
                      b_down[l])
        h_p, h_s = outs[0], outs[1]
        per_layer.append(outs[2:])
    stacked = [jnp.stack([pl_[i] for pl_ in per_layer]) for i in range(6)]
    return (h_p, h_s, *stacked)
```

```python
import functools

import jax
import jax.numpy as jnp
import numpy as np
from jax import lax
from jax.experimental import pallas as pl
from jax.experimental.pallas import tpu as pltpu

N_HEADS = 8
QK_NOPE_DIM = 64
QK_ROPE_DIM = 32
ROPE_HALF = QK_ROPE_DIM // 2
QK_DIM = QK_NOPE_DIM + QK_ROPE_DIM
V_DIM = 64
CONV_WIDTH = 31
N_EXPERTS = 32
TOP_K = 4
SWIGLU_LIMIT = 7.0
SWIGLU_ALPHA = 1.702
ROPE_THETA = 10000.0
NORM_EPS = 1e-6
ATTN_SCALE = QK_DIM ** -0.5
PAGE_SIZE = 128

LANES = 128
HEAD_PAD = LANES
VMEM_LIMIT = 56 * 1024 * 1024

f32 = jnp.float32
bf16 = jnp.bfloat16


def _cparams(*sem):
    return pltpu.CompilerParams(dimension_semantics=sem, vmem_limit_bytes=VMEM_LIMIT)


def _rope_head_tables(pos):
    inv_freq = ROPE_THETA ** (-jnp.arange(ROPE_HALF, dtype=f32) / ROPE_HALF)
    ang = pos.astype(f32)[:, None] * inv_freq[None, :]
    cos, sin = jnp.cos(ang), jnp.sin(ang)
    n = pos.shape[0]
    z = lambda w: jnp.zeros((n, w), f32)
    c = jnp.concatenate([jnp.ones((n, QK_NOPE_DIM), f32), cos, cos, z(32)], axis=1)
    s1 = jnp.concatenate([z(QK_NOPE_DIM + ROPE_HALF), sin, z(32)], axis=1)
    s2 = jnp.concatenate([z(QK_NOPE_DIM), -sin, z(ROPE_HALF + 32)], axis=1)
    return jnp.concatenate([c, s1, s2], axis=1)


def _rope_head(x, tab):
    c, s1, s2 = tab[:, :LANES], tab[:, LANES:2 * LANES], tab[:, 2 * LANES:]
    return x * c + pltpu.roll(x, ROPE_HALF, 1) * s1 + pltpu.roll(x, LANES - ROPE_HALF, 1) * s2


def _mix_in_kernel(x_ref, tab_ref, gmix_ref, win_ref, gq_ref, wuq_ref, gkv_ref, gqq_ref,
                   wk_ref, wv_ref, gqk_ref,
                   glu_ref, q_ref, c_ref, kr_ref, k_ref, v_ref, *, d_conv, q_rank, kv_rank):
    x = x_ref[...]
    xn = x * lax.rsqrt(jnp.mean(x * x, axis=-1, keepdims=True) + NORM_EPS) * gmix_ref[...]
    proj = jnp.dot(xn.astype(bf16), win_ref[...], preferred_element_type=f32)
    o = 0
    a = proj[:, o:o + d_conv]; o += d_conv
    gt = proj[:, o:o + d_conv]; o += d_conv
    q_lat = proj[:, o:o + q_rank]; o += q_rank
    kv_lat = proj[:, o:o + kv_rank]; o += kv_rank
    kr_raw = proj[:, o:o + LANES]
    glu_ref[...] = a * jax.nn.sigmoid(gt)

    tab = tab_ref[...]
    qn = q_lat * lax.rsqrt(jnp.mean(q_lat * q_lat, axis=-1, keepdims=True) + NORM_EPS) * gq_ref[...]
    q = jnp.dot(qn.astype(bf16), wuq_ref[...], preferred_element_type=f32)
    c_kv = kv_lat * lax.rsqrt(jnp.mean(kv_lat * kv_lat, axis=-1, keepdims=True) + NORM_EPS) * gkv_ref[...]
    c_ref[...] = c_kv
    kr = _rope_head(kr_raw, tab)
    kr_ref[...] = kr[:, QK_NOPE_DIM:QK_NOPE_DIM + QK_ROPE_DIM]
    c_bf = c_kv.astype(bf16)
    kn = jnp.dot(c_bf, wk_ref[...], preferred_element_type=f32)
    v_ref[...] = jnp.dot(c_bf, wv_ref[...], preferred_element_type=f32).astype(v_ref.dtype)
    gqq = gqq_ref[...] * ATTN_SCALE
    gqk = gqk_ref[...]
    for h in range(N_HEADS):
        sl = slice(h * HEAD_PAD, (h + 1) * HEAD_PAD)
        qh = _rope_head(q[:, sl], tab)
        qh = qh * lax.rsqrt(jnp.sum(qh * qh, axis=-1, keepdims=True) * (1.0 / QK_DIM) + NORM_EPS) * gqq
        q_ref[:, sl] = qh.astype(q_ref.dtype)
        kh = kn[:, sl] + kr
        kh = kh * lax.rsqrt(jnp.sum(kh * kh, axis=-1, keepdims=True) * (1.0 / QK_DIM) + NORM_EPS) * gqk
        k_ref[:, sl] = kh.astype(k_ref.dtype)


def _head_gain(g_pairs):
    g_rot = g_pairs[QK_NOPE_DIM:]
    return jnp.concatenate([g_pairs[:QK_NOPE_DIM], g_rot, g_rot, jnp.zeros((32,), f32)])[None, :]


def _mix_in_weights(w_in, w_uq, w_ukv, d_conv, q_rank, kv_rank):
    d_model = w_in.shape[0]
    base = 2 * d_conv + q_rank + kv_rank
    w_kr = w_in[:, base:base + QK_ROPE_DIM]
    w_kr_pad = jnp.concatenate([jnp.zeros((d_model, QK_NOPE_DIM), f32), w_kr,
                                jnp.zeros((d_model, 32), f32)], axis=1)
    win = jnp.concatenate([w_in[:, :base], w_kr_pad], axis=1).astype(bf16)
    wuq = jnp.pad(w_uq, ((0, 0), (0, 0), (0, HEAD_PAD - QK_DIM))).reshape(q_rank, N_HEADS * HEAD_PAD).astype(bf16)
    wk = jnp.pad(w_ukv[:, :, :QK_NOPE_DIM], ((0, 0), (0, 0), (0, HEAD_PAD - QK_NOPE_DIM)))
    wk = wk.reshape(kv_rank, N_HEADS * HEAD_PAD).astype(bf16)
    wv = w_ukv[:, :, QK_NOPE_DIM:].reshape(kv_rank, N_HEADS * V_DIM).astype(bf16)
    return win, wuq, wk, wv


def _mix_in(x2d, tab, tm, g_mix, win, g_q_lat, wuq, g_kv_lat, g_qk_q, wk, wv, g_qk_k, d_conv, q_rank, kv_rank):
    n, d_model = x2d.shape
    n_tab_blocks = tab.shape[0] // tm
    row = lambda i: (i, 0)
    const = lambda i: (0, 0)
    full = lambda a: pl.BlockSpec(a.shape, const)
    gm, gq, gkv = g_mix[None, :], g_q_lat[None, :], g_kv_lat[None, :]
    gqq, gqk = _head_gain(g_qk_q), _head_gain(g_qk_k)
    hp = N_HEADS * HEAD_PAD
    out_shape = (
        jax.ShapeDtypeStruct((n, d_conv), f32),
        jax.ShapeDtypeStruct((n, hp), bf16),
        jax.ShapeDtypeStruct((n, kv_rank), f32),
        jax.ShapeDtypeStruct((n, QK_ROPE_DIM), f32),
        jax.ShapeDtypeStruct((n, hp), bf16),
        jax.ShapeDtypeStruct((n, N_HEADS * V_DIM), bf16),
    )
    return pl.pallas_call(
        functools.partial(_mix_in_kernel, d_conv=d_conv, q_rank=q_rank, kv_rank=kv_rank),
        grid=(n // tm,),
        in_specs=[pl.BlockSpec((tm, d_model), row),
                  pl.BlockSpec((tm, tab.shape[1]), lambda i: (i % n_tab_blocks, 0)),
                  full(gm), full(win), full(gq), full(wuq), full(gkv), full(gqq), full(wk), full(wv), full(gqk)],
        out_specs=tuple(pl.BlockSpec((tm, s.shape[1]), row) for s in out_shape),
        out_shape=out_shape,
        compiler_params=_cparams("parallel"),
        name="mix_in",
    )(x2d, tab, gm, win, gq, wuq, gkv, gqq, wk, wv, gqk)


CONV_PAD = 32


SUBLANES = 8


def _conv_taps(window, w_ref, b_ref, ls, tc):
    lead = CONV_PAD - (CONV_WIDTH - 1)
    acc = jnp.zeros((tc, LANES), f32) + b_ref[:, ls]
    for r in range(SUBLANES):
        taps = [(a, a * SUBLANES + r - lead) for a in range(CONV_PAD // SUBLANES + 1)
                if 0 <= a * SUBLANES + r - lead < CONV_WIDTH]
        shifted = window[r:r + tc + SUBLANES * taps[-1][0]]
        for a, j in taps:
            acc = acc + shifted[a * SUBLANES:a * SUBLANES + tc] * w_ref[j:j + 1, ls]
    return acc


def _ln_swish(y, g_ref, beta_ref):
    mu = jnp.mean(y, axis=-1, keepdims=True)
    yc = y - mu
    var = jnp.mean(yc * yc, axis=-1, keepdims=True)
    z = yc * lax.rsqrt(var + NORM_EPS) * g_ref[...] + beta_ref[...]
    return z * jax.nn.sigmoid(z)


def _conv_prompt_kernel(x_ref, w_ref, b_ref, g_ref, beta_ref, o_ref, *, tc):
    t_len, n_ch = x_ref.shape[1], x_ref.shape[2]

    def chunk(idx, carry):
        t0 = pl.multiple_of(idx * tc, tc)
        h0 = pl.multiple_of(jnp.maximum(t0 - CONV_PAD, 0), SUBLANES)
        has_hist = (idx > 0).astype(f32)
        for lt in range(n_ch // LANES):
            ls = slice(lt * LANES, (lt + 1) * LANES)
            window = jnp.concatenate([x_ref[0, pl.ds(h0, CONV_PAD), ls] * has_hist,
                                      x_ref[0, pl.ds(t0, tc), ls]], axis=0)
            o_ref[0, pl.ds(t0, tc), ls] = _conv_taps(window, w_ref, b_ref, ls, tc)
        o_ref[0, pl.ds(t0, tc), :] = _ln_swish(o_ref[0, pl.ds(t0, tc), :], g_ref, beta_ref)
        return carry

    lax.fori_loop(0, t_len // tc, chunk, 0)


def _conv_sample_kernel(x_ref, st_ref, w_ref, b_ref, g_ref, beta_ref, o_ref):
    bb, tc, n_ch = x_ref.shape

    def one(b, carry):
        for lt in range(n_ch // LANES):
            ls = slice(lt * LANES, (lt + 1) * LANES)
            window = jnp.concatenate([st_ref[b, :, ls], x_ref[b, :, ls]], axis=0)
            o_ref[b, :, ls] = _conv_taps(window, w_ref, b_ref, ls, tc)
        o_ref[b] = _ln_swish(o_ref[b], g_ref, beta_ref)
        return carry

    lax.fori_loop(0, bb, one, 0)


def _conv_specs(conv_w, n_ch):
    const = lambda i: (0, 0)
    vec = pl.BlockSpec((1, n_ch), const)
    return [pl.BlockSpec(conv_w.shape, const), vec, vec, vec]


def _conv_prompt(glu, conv_w, conv_b, ln_g, ln_b, tc=128):
    n_b, t_len, n_ch = glu.shape
    blk = pl.BlockSpec((1, t_len, n_ch), lambda i: (i, 0, 0))
    return pl.pallas_call(
        functools.partial(_conv_prompt_kernel, tc=tc),
        grid=(n_b,),
        in_specs=[blk] + _conv_specs(conv_w, n_ch),
        out_specs=blk,
        out_shape=jax.ShapeDtypeStruct(glu.shape, f32),
        compiler_params=_cparams("parallel"),
        name="conv_prompt",
    )(glu, conv_w, conv_b[None, :], ln_g[None, :], ln_b[None, :])


def _conv_sample(glu, state, conv_w, conv_b, ln_g, ln_b, bb=32):
    n_b, t_len, n_ch = glu.shape
    hist = jnp.pad(state, ((0, 0), (CONV_PAD - state.shape[1], 0), (0, 0)))
    return pl.pallas_call(
        _conv_sample_kernel,
        grid=(n_b // bb,),
        in_specs=[pl.BlockSpec((bb, t_len, n_ch), lambda i: (i, 0, 0)),
                  pl.BlockSpec((bb, CONV_PAD, n_ch), lambda i: (i, 0, 0))] + _conv_specs(conv_w, n_ch),
        out_specs=pl.BlockSpec((bb, t_len, n_ch), lambda i: (i, 0, 0)),
        out_shape=jax.ShapeDtypeStruct(glu.shape, f32),
        compiler_params=_cparams("parallel"),
        name="conv_sample",
    )(glu, hist, conv_w, conv_b[None, :], ln_g[None, :], ln_b[None, :])


NT_DIMS = (((1,), (1,)), ((), ()))


def _attn_prompt_kernel(q_ref, k_ref, v_ref, o_ref, *, tq, tk):
    qi = pl.program_id(2)
    sub = tq // tk
    outs = []
    for j in range(2):
        hs = slice(j * HEAD_PAD, (j + 1) * HEAD_PAD)
        q = q_ref[0, :, hs]

        def step(kb, carry, masked, q=q, hs=hs):
            m, l, acc = carry
            k0 = pl.multiple_of(kb * tk, tk)
            s = lax.dot_general(q, k_ref[0, pl.ds(k0, tk), hs], NT_DIMS, preferred_element_type=f32)
            if masked:
                row = qi * tq + lax.broadcasted_iota(jnp.int32, (tq, tk), 0)
                col = k0 + lax.broadcasted_iota(jnp.int32, (tq, tk), 1)
                s = jnp.where(col <= row, s, -jnp.inf)
            m_new = jnp.maximum(m, jnp.max(s, axis=-1, keepdims=True))
            alpha = jnp.exp(m - m_new)
            p = jnp.exp(s - m_new)
            l = alpha * l + jnp.sum(p, axis=-1, keepdims=True)
            acc = alpha * acc + jnp.dot(p.astype(bf16), v_ref[0, pl.ds(k0, tk), :], preferred_element_type=f32)
            return m_new, l, acc

        carry = (jnp.full((tq, 1), -jnp.inf, f32), jnp.zeros((tq, 1), f32), jnp.zeros((tq, LANES), f32))
        carry = lax.fori_loop(0, qi * sub, functools.partial(step, masked=False), carry)
        for d in range(sub):
            carry = step(qi * sub + d, carry, True)
        outs.append(carry[2] / carry[1])
    lane = lax.broadcasted_iota(jnp.int32, (tq, LANES), 1)
    o_ref[0] = jnp.where(lane < V_DIM, outs[0], outs[1])


def _attn_prompt(q, k, v, tq, tk):
    n_b, t_len, _ = q.shape
    return pl.pallas_call(
        functools.partial(_attn_prompt_kernel, tq=tq, tk=tk),
        grid=(n_b, N_HEADS // 2, t_len // tq),
        in_specs=[pl.BlockSpec((1, tq, 2 * HEAD_PAD), lambda b, h, i: (b, i, h)),
                  pl.BlockSpec((1, t_len, 2 * HEAD_PAD), lambda b, h, i: (b, 0, h)),
                  pl.BlockSpec((1, t_len, 2 * V_DIM), lambda b, h, i: (b, 0, h))],
        out_specs=pl.BlockSpec((1, tq, 2 * V_DIM), lambda b, h, i: (b, i, h)),
        out_shape=jax.ShapeDtypeStruct((n_b, t_len, N_HEADS * V_DIM), f32),
        compiler_params=_cparams("parallel", "parallel", "arbitrary"),
        name="attn_prompt",
    )(q, k, v)


TN_DIMS = (((0,), (0,)), ((), ()))
N_TH = 64


def _q_absorb_kernel(q_ref, gk_ref, wukt_ref, sel_ref, qabs_ref, qr_ref):
    gk = gk_ref[...]
    kv_rank = wukt_ref.shape[-1]
    for h in range(N_HEADS):
        qg = (q_ref[:, h * HEAD_PAD:(h + 1) * HEAD_PAD].astype(f32) * gk).astype(bf16)
        qabs_ref[:, h * kv_rank:(h + 1) * kv_rank] = jnp.dot(
            qg, wukt_ref[h], preferred_element_type=f32).astype(qabs_ref.dtype)
        qr_ref[:, h * QK_ROPE_DIM:(h + 1) * QK_ROPE_DIM] = jnp.dot(
            qg, sel_ref[...], preferred_element_type=f32).astype(qr_ref.dtype)


def _q_absorb(q2d, g_qk_k, w_ukv):
    n = q2d.shape[0]
    kv_rank = w_ukv.shape[0]
    wukt = jnp.transpose(w_ukv[:, :, :QK_NOPE_DIM], (1, 2, 0))
    wukt = jnp.pad(wukt, ((0, 0), (0, HEAD_PAD - QK_NOPE_DIM), (0, 0))).astype(bf16)
    sel = jnp.zeros((HEAD_PAD, QK_ROPE_DIM), f32).at[
        QK_NOPE_DIM + jnp.arange(QK_ROPE_DIM), jnp.arange(QK_ROPE_DIM)].set(1.0).astype(bf16)
    gk = _head_gain(g_qk_k)
    full = lambda a: pl.BlockSpec(a.shape, lambda i: (0,) * a.ndim)
    return pl.pallas_call(
        _q_absorb_kernel,
        grid=(1,),
        in_specs=[full(q2d), full(gk), full(wukt), full(sel)],
        out_specs=(pl.BlockSpec((n, N_HEADS * kv_rank), lambda i: (0, 0)),
                   pl.BlockSpec((n, N_HEADS * QK_ROPE_DIM), lambda i: (0, 0))),
        out_shape=(jax.ShapeDtypeStruct((n, N_HEADS * kv_rank), bf16),
                   jax.ShapeDtypeStruct((n, N_HEADS * QK_ROPE_DIM), bf16)),
        compiler_params=_cparams("arbitrary"),
        name="q_absorb",
    )(q2d, gk, wukt, sel)


AUX_W = 2 * LANES


def _aux_weights(qr):
    n_seq = qr.shape[0]
    lane = jnp.arange(LANES)
    head_sum = (lane[:, None] % N_HEADS == lane[None, :] % N_HEADS).astype(f32)
    top = jnp.concatenate([head_sum, jnp.zeros((LANES, LANES), f32)], axis=1)
    mid = jnp.concatenate([jnp.ones((QK_ROPE_DIM, LANES), f32), jnp.zeros((QK_ROPE_DIM, LANES), f32)], axis=1)
    const = jnp.broadcast_to(jnp.concatenate([top, mid], axis=0).astype(bf16)[None],
                             (n_seq, LANES + QK_ROPE_DIM, AUX_W))
    qr_t = jnp.swapaxes(qr, 1, 2)
    rot = jnp.concatenate([jnp.zeros((n_seq, QK_ROPE_DIM, LANES), bf16), qr_t,
                           jnp.zeros((n_seq, QK_ROPE_DIM, LANES - N_TH), bf16)], axis=2)
    tail = jnp.zeros((n_seq, AUX_W - LANES - 2 * QK_ROPE_DIM, AUX_W), bf16)
    return jnp.concatenate([const, rot, tail], axis=1)


def _lat_chunk(c_bf, kr, qabs, auxw, wuk, state, mask=None):
    m, l, acc = state
    n = c_bf.shape[0]
    kn = jnp.dot(c_bf, wuk, preferred_element_type=f32)
    sq = kn * kn
    t = sq[:, :LANES]
    for i in range(1, sq.shape[1] // LANES):
        t = t + sq[:, i * LANES:(i + 1) * LANES]
    side = jnp.concatenate([kr * kr, kr, jnp.zeros((n, LANES - 2 * QK_ROPE_DIM), f32)], axis=1)
    aux = jnp.concatenate([t, side], axis=1).astype(bf16)
    r = jnp.dot(aux, auxw, preferred_element_type=f32)
    rinv = lax.rsqrt(r[:, :N_TH] * (1.0 / QK_DIM) + NORM_EPS)
    s = lax.dot_general(c_bf, qabs, NT_DIMS, preferred_element_type=f32) + r[:, LANES:LANES + N_TH]
    s = s * rinv
    if mask is not None:
        s = jnp.where(mask, s, -jnp.inf)
    m_new = jnp.maximum(m, jnp.max(s, axis=0, keepdims=True))
    alpha = jnp.exp(m - m_new)
    p = jnp.exp(s - m_new)
    l = alpha * l + jnp.sum(p, axis=0, keepdims=True)
    acc = _row_to_col(alpha) * acc + lax.dot_general(p.astype(bf16), c_bf, TN_DIMS, preferred_element_type=f32)
    return m_new, l, acc


def _row_to_col(v):
    n = v.shape[1]
    eye = lax.broadcasted_iota(jnp.int32, (n, n), 0) == lax.broadcasted_iota(jnp.int32, (n, n), 1)
    return jnp.sum(jnp.where(eye, jnp.broadcast_to(v, (n, n)), 0.0), axis=1, keepdims=True)


SUB_PAGES = 8


def _attn_sample_kernel(pt_ref, qabs_ref, auxw_ref, cnew_ref, krnew_ref, wuk_ref, *rest, pp):
    c_pages = rest[:pp]
    kr_pages = rest[pp:2 * pp]
    o_ref = rest[2 * pp]
    m_s, l_s, acc_s = rest[2 * pp + 1:]
    j = pl.program_id(1)

    @pl.when(j == 0)
    def _():
        m_s[...] = jnp.full(m_s.shape, -jnp.inf, f32)
        l_s[...] = jnp.zeros(l_s.shape, f32)
        acc_s[...] = jnp.zeros(acc_s.shape, f32)

    qabs, auxw, wuk = qabs_ref[0], auxw_ref[0], wuk_ref[...]
    state = (m_s[...], l_s[...], acc_s[...])
    sub = min(SUB_PAGES, pp)
    for b in range(0, pp, sub):
        c_bf = jnp.concatenate([c_pages[i][0].astype(bf16) for i in range(b, b + sub)], axis=0)
        kr = jnp.concatenate([kr_pages[i][0] for i in range(b, b + sub)], axis=0)
        state = _lat_chunk(c_bf, kr, qabs, auxw, wuk, state)
    m_s[...], l_s[...], acc_s[...] = state

    @pl.when(j == pl.num_programs(1) - 1)
    def _():
        t_new = cnew_ref.shape[1]
        key_t = lax.broadcasted_iota(jnp.int32, (t_new, N_TH), 0)
        q_t = lax.broadcasted_iota(jnp.int32, (t_new, N_TH), 1) // N_HEADS
        m, l, acc = _lat_chunk(cnew_ref[0].astype(bf16), krnew_ref[0], qabs, auxw, wuk, state, mask=key_t <= q_t)
        o_ref[0] = acc / _row_to_col(l)


def _attn_sample(page_table, qabs, auxw, c_new, kr_new, cache_c, cache_kr, wuk_perm, pp):
    n_seq, n_pages = page_table.shape
    kv_rank = cache_c.shape[-1]
    t_new = c_new.shape[1]
    seq3 = lambda s, j, pt: (s, 0, 0)

    def page(i):
        return lambda s, j, pt: (pt[s, j * pp + i], 0, 0)

    in_specs = [pl.BlockSpec((1, N_TH, kv_rank), seq3),
                pl.BlockSpec((1, AUX_W, AUX_W), seq3),
                pl.BlockSpec((1, t_new, kv_rank), seq3),
                pl.BlockSpec((1, t_new, QK_ROPE_DIM), seq3),
                pl.BlockSpec(wuk_perm.shape, lambda s, j, pt: (0, 0))]
    in_specs += [pl.BlockSpec((1, PAGE_SIZE, kv_rank), page(i)) for i in range(pp)]
    in_specs += [pl.BlockSpec((1, PAGE_SIZE, QK_ROPE_DIM), page(i)) for i in range(pp)]
    grid_spec = pltpu.PrefetchScalarGridSpec(
        num_scalar_prefetch=1,
        grid=(n_seq, n_pages // pp),
        in_specs=in_specs,
        out_specs=pl.BlockSpec((1, N_TH, kv_rank), seq3),
        scratch_shapes=[pltpu.VMEM((1, N_TH), f32), pltpu.VMEM((1, N_TH), f32),
                        pltpu.VMEM((N_TH, kv_rank), f32)],
    )
    return pl.pallas_call(
        functools.partial(_attn_sample_kernel, pp=pp),
        grid_spec=grid_spec,
        out_shape=jax.ShapeDtypeStruct((n_seq, N_TH, kv_rank), f32),
        compiler_params=_cparams("parallel", "arbitrary"),
        name="attn_sample",
    )(page_table, qabs, auxw, c_new, kr_new, wuk_perm, *([cache_c] * pp), *([cache_kr] * pp))


def _v_up_kernel(lat_ref, wuv_ref, o_ref):
    rows = lat_ref.shape[0]
    full = jnp.dot(lat_ref[...].astype(bf16), wuv_ref[...], preferred_element_type=f32)
    head_of_row = lax.broadcasted_iota(jnp.int32, full.shape, 0) % N_HEADS
    head_of_lane = lax.broadcasted_iota(jnp.int32, full.shape, 1) // V_DIM
    own = jnp.where(head_of_row == head_of_lane, full, 0.0)
    o_ref[...] = jnp.sum(own.reshape(rows // N_HEADS, N_HEADS, full.shape[1]), axis=1)


def _v_up(lat2d, wuv, rows):
    n, kv_rank = lat2d.shape
    d_attn = wuv.shape[1]
    return pl.pallas_call(
        _v_up_kernel,
        grid=(n // rows,),
        in_specs=[pl.BlockSpec((rows, kv_rank), lambda i: (i, 0)), pl.BlockSpec(wuv.shape, lambda i: (0, 0))],
        out_specs=pl.BlockSpec((rows // N_HEADS, d_attn), lambda i: (i, 0)),
        out_shape=jax.ShapeDtypeStruct((n // N_HEADS, d_attn), f32),
        compiler_params=_cparams("parallel"),
        name="v_up",
    )(lat2d, wuv)


def _attn_sample_all(q_s2d, c_new, kr_new, page_table, cache_c, cache_kr, w_ukv, g_qk_k, pp):
    n_seq, t_new, kv_rank = c_new.shape
    qabs, qr = _q_absorb(q_s2d, g_qk_k, w_ukv)
    qabs = qabs.reshape(n_seq, t_new * N_HEADS, kv_rank)
    qr = qr.reshape(n_seq, t_new * N_HEADS, QK_ROPE_DIM)
    wuk_perm = jnp.transpose(w_ukv[:, :, :QK_NOPE_DIM], (0, 2, 1)).reshape(kv_rank, -1).astype(bf16)
    lat = _attn_sample(page_table, qabs, _aux_weights(qr), c_new, kr_new, cache_c, cache_kr, wuk_perm, pp)
    wuv = w_ukv[:, :, QK_NOPE_DIM:].reshape(kv_rank, N_HEADS * V_DIM).astype(bf16)
    rows = min(512, n_seq * N_TH)
    return _v_up(lat.reshape(n_seq * N_TH, kv_rank), wuv, rows)


def _rms(x, g):
    return x * lax.rsqrt(jnp.mean(x * x, axis=-1, keepdims=True) + NORM_EPS) * g


def _split_bf16(x):
    hi = x.astype(bf16)
    return hi, (x - hi.astype(f32)).astype(bf16)


def _merge_kernel(xp_ref, cvp_ref, atp_ref, xs_ref, cvs_ref, ats_ref, gc_ref, ga_ref, woc_ref, woa_ref, gf_ref,
                  wrh_ref, wrl_ref, br_ref, h_ref, xn_ref, exp_ref, gate_ref, rank_ref, cnt_ref, cnt_s,
                  *, n_prompt_tiles):
    i = pl.program_id(0)
    tm = xp_ref.shape[0]

    @pl.when(i == 0)
    def _():
        cnt_s[...] = jnp.zeros(cnt_s.shape, f32)

    is_p = i < n_prompt_tiles
    x = jnp.where(is_p, xp_ref[...], xs_ref[...])
    yc = _rms(jnp.where(is_p, cvp_ref[...], cvs_ref[...]), gc_ref[...]).astype(bf16)
    ya = _rms(jnp.where(is_p, atp_ref[...], ats_ref[...]), ga_ref[...]).astype(bf16)
    h = x + jnp.dot(yc, woc_ref[...], preferred_element_type=f32) \
        + jnp.dot(ya, woa_ref[...], preferred_element_type=f32)
    h_ref[...] = h
    xn = _rms(h, gf_ref[...])
    xh, xl = _split_bf16(xn)
    xn_ref[...] = xh
    logits = (jnp.dot(xh, wrh_ref[...], preferred_element_type=f32)
              + jnp.dot(xl, wrh_ref[...], preferred_element_type=f32)
              + jnp.dot(xh, wrl_ref[...], preferred_element_type=f32)) + br_ref[...]
    lane = lax.broadcasted_iota(jnp.int32, logits.shape, 1)
    tops, idxs = [], []
    cur = logits
    for _k in range(TOP_K):
        mk = jnp.max(cur, axis=-1, keepdims=True)
        ik = jnp.min(jnp.where(cur == mk, lane, LANES), axis=-1, keepdims=True)
        tops.append(mk)
        idxs.append(ik)
        cur = jnp.where(lane == ik, -jnp.inf, cur)
    es = [jnp.exp(t - tops[0]) for t in tops]
    den = es[0] + es[1] + es[2] + es[3]
    onehots = [(lane == ik).astype(f32) for ik in idxs]
    oh = onehots[0] + onehots[1] + onehots[2] + onehots[3]
    r_i = lax.broadcasted_iota(jnp.int32, (tm, tm), 0)
    c_i = lax.broadcasted_iota(jnp.int32, (tm, tm), 1)
    before = jnp.dot((c_i < r_i).astype(bf16), oh.astype(bf16), preferred_element_type=f32) + cnt_s[...]
    exp_out = jnp.zeros(logits.shape, jnp.int32)
    gate_out = jnp.zeros(logits.shape, f32)
    rank_out = jnp.zeros(logits.shape, jnp.int32)
    for k in range(TOP_K):
        rk = jnp.sum(onehots[k] * before, axis=-1, keepdims=True)
        exp_out = jnp.where(lane == k, idxs[k], exp_out)
        gate_out = jnp.where(lane == k, es[k] / den, gate_out)
        rank_out = jnp.where(lane == k, rk.astype(jnp.int32), rank_out)
    exp_ref[...] = exp_out
    gate_ref[...] = gate_out
    rank_ref[...] = rank_out
    cnt_s[...] = cnt_s[...] + jnp.sum(oh, axis=0, keepdims=True)
    cnt_ref[...] = cnt_s[...]


def _merge(x_p, cv_p, at_p, x_s, cv_s, at_s, g_out_conv, g_out_attn, w_out, g_ffn, w_router, b_router, tm):
    (n_p, d_model), n_s = x_p.shape, x_s.shape[0]
    n = n_p + n_s
    d_conv, d_attn = cv_p.shape[1], at_p.shape[1]
    woc, woa = w_out[:d_conv].astype(bf16), w_out[d_conv:].astype(bf16)
    wr = jnp.pad(w_router, ((0, 0), (0, LANES - N_EXPERTS)))
    wrh, wrl = _split_bf16(wr)
    br = jnp.concatenate([b_router.astype(f32), jnp.full((LANES - N_EXPERTS,), -jnp.inf, f32)])[None, :]
    gc, ga, gf = g_out_conv[None, :], g_out_attn[None, :], g_ffn[None, :]
    tiles_p, tiles_s = n_p // tm, n_s // tm
    full = lambda a: pl.BlockSpec(a.shape, lambda i: (0, 0))
    rows = lambda w: pl.BlockSpec((tm, w), lambda i: (i, 0))
    rows_p = lambda w: pl.BlockSpec((tm, w), lambda i: (jnp.minimum(i, tiles_p - 1), 0))
    rows_s = lambda w: pl.BlockSpec((tm, w), lambda i: (jnp.maximum(i - tiles_p, 0), 0))
    out_shape = (jax.ShapeDtypeStruct((n, d_model), f32), jax.ShapeDtypeStruct((n, d_model), bf16),
                 jax.ShapeDtypeStruct((n, LANES), jnp.int32), jax.ShapeDtypeStruct((n, LANES), f32),
                 jax.ShapeDtypeStruct((n, LANES), jnp.int32), jax.ShapeDtypeStruct((1, LANES), f32))
    return pl.pallas_call(
        functools.partial(_merge_kernel, n_prompt_tiles=tiles_p),
        grid=(tiles_p + tiles_s,),
        in_specs=[rows_p(d_model), rows_p(d_conv), rows_p(d_attn), rows_s(d_model), rows_s(d_conv), rows_s(d_attn),
                  full(gc), full(ga), full(woc), full(woa), full(gf), full(wrh), full(wrl), full(br)],
        out_specs=(rows(d_model), rows(d_model), rows(LANES), rows(LANES), rows(LANES),
                   pl.BlockSpec((1, LANES), lambda i: (0, 0))),
        out_shape=out_shape,
        scratch_shapes=[pltpu.VMEM((1, LANES), f32)],
        compiler_params=_cparams("arbitrary"),
        name="merge",
    )(x_p, cv_p, at_p, x_s, cv_s, at_s, gc, ga, woc, woa, gf, wrh, wrl, br)


def _experts_kernel(be_ref, bv_ref, x_ref, wgu_ref, bgu_ref, wd_ref, bd_ref, y_ref, wgu_s, wd_s):
    i = pl.program_id(0)
    d_expert = wd_ref.shape[1]

    @pl.when((i == 0) | (be_ref[i] != be_ref[jnp.maximum(i - 1, 0)]))
    def _():
        wgu_s[...] = wgu_ref[0].astype(bf16)
        wd_s[...] = wd_ref[0].astype(bf16)

    @pl.when(bv_ref[i] != 0)
    def _():
        gu = jnp.dot(x_ref[...], wgu_s[...], preferred_element_type=f32) + bgu_ref[0]
        gate = jnp.minimum(gu[:, :d_expert], SWIGLU_LIMIT)
        up = jnp.clip(gu[:, d_expert:], -SWIGLU_LIMIT, SWIGLU_LIMIT)
        hid = (up + 1.0) * gate * jax.nn.sigmoid(SWIGLU_ALPHA * gate)
        y = jnp.dot(hid.astype(bf16), wd_s[...], preferred_element_type=f32) + bd_ref[0]
        y_ref[...] = y.astype(y_ref.dtype)

    @pl.when(bv_ref[i] == 0)
    def _():
        y_ref[...] = jnp.zeros(y_ref.shape, y_ref.dtype)


def _experts(block_expert, block_valid, x_slots, w_gate_up, b_gate_up, w_down, b_down, blk):
    n_slots, d_model = x_slots.shape
    n_exp, _, d_gu = w_gate_up.shape
    d_expert = w_down.shape[1]
    grid_spec = pltpu.PrefetchScalarGridSpec(
        num_scalar_prefetch=2,
        grid=(n_slots // blk,),
        in_specs=[pl.BlockSpec((blk, d_model), lambda i, be, bv: (i, 0)),
                  pl.BlockSpec((1, d_model, d_gu), lambda i, be, bv: (be[i], 0, 0)),
                  pl.BlockSpec((1, 1, d_gu), lambda i, be, bv: (be[i], 0, 0)),
                  pl.BlockSpec((1, d_expert, d_model), lambda i, be, bv: (be[i], 0, 0)),
                  pl.BlockSpec((1, 1, d_model), lambda i, be, bv: (be[i], 0, 0))],
        out_specs=pl.BlockSpec((blk, d_model), lambda i, be, bv: (i, 0)),
        scratch_shapes=[pltpu.VMEM((d_model, d_gu), bf16), pltpu.VMEM((d_expert, d_model), bf16)],
    )
    return pl.pallas_call(
        _experts_kernel,
        grid_spec=grid_spec,
        out_shape=jax.ShapeDtypeStruct((n_slots, d_model), bf16),
        compiler_params=_cparams("arbitrary"),
        name="experts",
    )(block_expert, block_valid, x_slots, w_gate_up, b_gate_up[:, None, :], w_down, b_down[:, None, :])


def _combine_kernel(h_ref, ya_ref, gate_ref, yp_ref, ys_ref, *, n_prompt_tiles):
    i = pl.program_id(0)
    g = gate_ref[...]
    y = h_ref[...]
    for k in range(TOP_K):
        y = y + ya_ref[k].astype(f32) * g[:, k:k + 1]

    @pl.when(i < n_prompt_tiles)
    def _():
        yp_ref[...] = y

    @pl.when(i >= n_prompt_tiles)
    def _():
        ys_ref[...] = y


def _combine(h, y_assign, gates, n_p, tm):
    n, d_model = h.shape
    tiles_p = n_p // tm
    return pl.pallas_call(
        functools.partial(_combine_kernel, n_prompt_tiles=tiles_p),
        grid=(n // tm,),
        in_specs=[pl.BlockSpec((tm, d_model), lambda i: (i, 0)),
                  pl.BlockSpec((TOP_K, tm, d_model), lambda i: (0, i, 0)),
                  pl.BlockSpec((tm, LANES), lambda i: (i, 0))],
        out_specs=(pl.BlockSpec((tm, d_model), lambda i: (jnp.minimum(i, tiles_p - 1), 0)),
                   pl.BlockSpec((tm, d_model), lambda i: (jnp.maximum(i - tiles_p, 0), 0))),
        out_shape=(jax.ShapeDtypeStruct((n_p, d_model), f32), jax.ShapeDtypeStruct((n - n_p, d_model), f32)),
        compiler_params=_cparams("arbitrary"),
        name="combine",
    )(h, y_assign, gates)


EXPERT_BLK = 512


def _moe(h, xn, expert, gates, rank, counts, n_p, w_gate_up, b_gate_up, w_down, b_down):
    n = h.shape[0]
    blk = EXPERT_BLK
    padded = (counts + blk - 1) // blk * blk
    padded_end = jnp.cumsum(padded)
    start_padded = padded_end - padded
    slot = start_padded[expert] + rank
    n_blocks = -(-(n * TOP_K + N_EXPERTS * (blk - 1)) // blk)
    n_slots = n_blocks * blk
    tok_of_slot = jnp.zeros((n_slots,), jnp.int32).at[slot.reshape(-1)].set(
        jnp.arange(n * TOP_K, dtype=jnp.int32) // TOP_K, unique_indices=True, mode="promise_in_bounds")
    block_start = jnp.arange(n_blocks, dtype=jnp.int32) * blk
    block_expert = jnp.minimum(jnp.sum((block_start[:, None] >= padded_end[None, :]).astype(jnp.int32), axis=1),
                               N_EXPERTS - 1)
    block_valid = (block_start < padded_end[-1]).astype(jnp.int32)
    x_slots = xn.at[tok_of_slot].get(mode="promise_in_bounds")
    y_slots = _experts(block_expert, block_valid, x_slots, w_gate_up, b_gate_up, w_down, b_down, blk)
    y_assign = y_slots.at[slot.T].get(mode="promise_in_bounds")
    return _combine(h, y_assign, gates, n_p, 512)


def _layer(x_p, x_s, cache_c, cache_kr, state, page_table, norm_mix, w_in, norm_q_lat, w_uq, norm_kv_lat, w_ukv,
           norm_qk_q, norm_qk_k, conv_w, conv_b, conv_ln_g, conv_ln_b, norm_out_conv, norm_out_attn, w_out,
           norm_ffn, w_router, b_router, w_gate_up, b_gate_up, w_down, b_down):
    n_b, t_p, d_model = x_p.shape
    n_seq, t_s, _ = x_s.shape
    d_conv = conv_w.shape[1]
    q_rank, kv_rank = norm_q_lat.shape[0], norm_kv_lat.shape[0]
    n_past = page_table.shape[1] * PAGE_SIZE
    win, wuq, wk, wv = _mix_in_weights(w_in, w_uq, w_ukv, d_conv, q_rank, kv_rank)
    mix = functools.partial(_mix_in, g_mix=norm_mix, win=win, g_q_lat=norm_q_lat, wuq=wuq, g_kv_lat=norm_kv_lat,
                            g_qk_q=norm_qk_q, wk=wk, wv=wv, g_qk_k=norm_qk_k,
                            d_conv=d_conv, q_rank=q_rank, kv_rank=kv_rank)
    x_p2, x_s2 = x_p.reshape(n_b * t_p, d_model), x_s.reshape(n_seq * t_s, d_model)
    tab_p = _rope_head_tables(jnp.arange(t_p))
    tab_s = _rope_head_tables(n_past + jnp.arange(n_seq * t_s) % t_s)
    glu_p, q_p, c_p, kr_p, k_p, v_p = mix(x_p2, tab_p, 512)
    glu_s, q_s, c_s, kr_s, _, _ = mix(x_s2, tab_s, 512)

    glu_p3, glu_s3 = glu_p.reshape(n_b, t_p, d_conv), glu_s.reshape(n_seq, t_s, d_conv)
    cv_p = _conv_prompt(glu_p3, conv_w, conv_b, conv_ln_g, conv_ln_b)
    cv_s = _conv_sample(glu_s3, state, conv_w, conv_b, conv_ln_g, conv_ln_b)

    hp = N_HEADS * HEAD_PAD
    at_p = _attn_prompt(q_p.reshape(n_b, t_p, hp), k_p.reshape(n_b, t_p, hp),
                        v_p.reshape(n_b, t_p, N_HEADS * V_DIM), 512, 512)
    c_s3, kr_s3 = c_s.reshape(n_seq, t_s, kv_rank), kr_s.reshape(n_seq, t_s, QK_ROPE_DIM)
    at_s = _attn_sample_all(q_s, c_s3, kr_s3, page_table, cache_c, cache_kr, w_ukv, norm_qk_k, 32)

    h, xn, expert, gates, rank, cnt = _merge(
        x_p2, cv_p.reshape(n_b * t_p, d_conv), at_p.reshape(n_b * t_p, -1),
        x_s2, cv_s.reshape(n_seq * t_s, d_conv), at_s,
        norm_out_conv, norm_out_attn, w_out, norm_ffn, w_router, b_router, 512)
    counts = cnt[0, :N_EXPERTS].astype(jnp.int32)
    y_p, y_s = _moe(h, xn, expert[:, :TOP_K], gates, rank[:, :TOP_K], counts, n_b * t_p,
                    w_gate_up, b_gate_up, w_down, b_down)
    n_hist = CONV_WIDTH - 1
    conv_state_p = jnp.concatenate([jnp.zeros((n_b, max(n_hist - t_p, 0), d_conv), f32),
                                    glu_p3[:, max(t_p - n_hist, 0):]], axis=1)
    conv_state_s = jnp.concatenate([state, glu_s3], axis=1)[:, -n_hist:]
    return (y_p.reshape(x_p.shape), y_s.reshape(x_s.shape), c_p.reshape(n_b, t_p, kv_rank),
            kr_p.reshape(n_b, t_p, QK_ROPE_DIM), conv_state_p, c_s3, kr_s3, conv_state_s)


def kernel(x_prompt, x_sample, cache_kv_latent, cache_k_rope, state_conv, page_table, norm_mix, w_in, norm_q_lat,
           w_uq, norm_kv_lat, w_ukv, norm_qk_q, norm_qk_k, conv_w, conv_b, conv_ln_g, conv_ln_b, norm_out_conv,
           norm_out_attn, w_out, norm_ffn, w_router, b_router, w_gate_up, b_gate_up, w_down, b_down):
    h_p, h_s = x_prompt, x_sample
    per_layer = []
    for l in range(w_in.shape[0]):
        outs = _layer(h_p, h_s, cache_kv_latent[l], cache_k_rope[l], state_conv[l], page_table, norm_mix[l],
                      w_in[l], norm_q_lat[l], w_uq[l], norm_kv_lat[l], w_ukv[l], norm_qk_q[l], norm_qk_k[l],
                      conv_w[l], conv_b[l], conv_ln_g[l], conv_ln_b[l], norm_out_conv[l], norm_out_attn[l],
                      w_out[l], norm_ffn[l], w_router[l], b_router[l], w_gate_up[l], b_gate_up[l], w_down[l],
                      b_down[l])
        h_p, h_s = outs[0], outs[1]
        per_layer.append(outs[2:])
    stacked = [jnp.stack([pl_[i] for pl_ in per_layer]) for i in range(6)]
    return (h_p, h_s, *stacked)
```

```python
import functools

import jax
import jax.numpy as jnp
import numpy as np
from jax import lax
from jax.experimental import pallas as pl
from jax.experimental.pallas import tpu as pltpu
from jax.experimental.pallas import tpu_sc as plsc

N_HEADS = 8
QK_NOPE_DIM = 64
QK_ROPE_DIM = 32
ROPE_HALF = QK_ROPE_DIM // 2
QK_DIM = QK_NOPE_DIM + QK_ROPE_DIM
V_DIM = 64
CONV_WIDTH = 31
N_EXPERTS = 32
TOP_K = 4
SWIGLU_LIMIT = 7.0
SWIGLU_ALPHA = 1.702
ROPE_THETA = 10000.0
NORM_EPS = 1e-6
ATTN_SCALE = QK_DIM ** -0.5
PAGE_SIZE = 128

LANES = 128
HEAD_PAD = LANES
VMEM_LIMIT = 56 * 1024 * 1024

f32 = jnp.float32
bf16 = jnp.bfloat16


def _cparams(*sem):
    return pltpu.CompilerParams(dimension_semantics=sem, vmem_limit_bytes=VMEM_LIMIT)


def _rope_head_tables(pos):
    inv_freq = ROPE_THETA ** (-jnp.arange(ROPE_HALF, dtype=f32) / ROPE_HALF)
    ang = pos.astype(f32)[:, None] * inv_freq[None, :]
    cos, sin = jnp.cos(ang), jnp.sin(ang)
    n = pos.shape[0]
    z = lambda w: jnp.zeros((n, w), f32)
    c = jnp.concatenate([jnp.ones((n, QK_NOPE_DIM), f32), cos, cos, z(32)], axis=1)
    s1 = jnp.concatenate([z(QK_NOPE_DIM + ROPE_HALF), sin, z(32)], axis=1)
    s2 = jnp.concatenate([z(QK_NOPE_DIM), -sin, z(ROPE_HALF + 32)], axis=1)
    return jnp.concatenate([c, s1, s2], axis=1)


def _rope_head(x, tab):
    c, s1, s2 = tab[:, :LANES], tab[:, LANES:2 * LANES], tab[:, 2 * LANES:]
    return x * c + pltpu.roll(x, ROPE_HALF, 1) * s1 + pltpu.roll(x, LANES - ROPE_HALF, 1) * s2


def _mix_in_kernel(x_ref, tab_ref, gmix_ref, win_ref, gq_ref, wuq_ref, gkv_ref, gqq_ref,
                   wk_ref, wv_ref, gqk_ref,
                   glu_ref, q_ref, c_ref, kr_ref, k_ref, v_ref, *, d_conv, q_rank, kv_rank):
    x = x_ref[...]
    xn = x * lax.rsqrt(jnp.mean(x * x, axis=-1, keepdims=True) + NORM_EPS) * gmix_ref[...]
    proj = jnp.dot(xn.astype(bf16), win_ref[...], preferred_element_type=f32)
    o = 0
    a = proj[:, o:o + d_conv]; o += d_conv
    gt = proj[:, o:o + d_conv]; o += d_conv
    q_lat = proj[:, o:o + q_rank]; o += q_rank
    kv_lat = proj[:, o:o + kv_rank]; o += kv_rank
    kr_raw = proj[:, o:o + LANES]
    glu_ref[...] = a * jax.nn.sigmoid(gt)

    tab = tab_ref[...]
    qn = q_lat * lax.rsqrt(jnp.mean(q_lat * q_lat, axis=-1, keepdims=True) + NORM_EPS) * gq_ref[...]
    q = jnp.dot(qn.astype(bf16), wuq_ref[...], preferred_element_type=f32)
    c_kv = kv_lat * lax.rsqrt(jnp.mean(kv_lat * kv_lat, axis=-1, keepdims=True) + NORM_EPS) * gkv_ref[...]
    c_ref[...] = c_kv
    kr = _rope_head(kr_raw, tab)
    kr_ref[...] = kr[:, QK_NOPE_DIM:QK_NOPE_DIM + QK_ROPE_DIM]
    c_bf = c_kv.astype(bf16)
    kn = jnp.dot(c_bf, wk_ref[...], preferred_element_type=f32)
    v_ref[...] = jnp.dot(c_bf, wv_ref[...], preferred_element_type=f32).astype(v_ref.dtype)
    gqq = gqq_ref[...] * ATTN_SCALE
    gqk = gqk_ref[...]
    for h in range(N_HEADS):
        sl = slice(h * HEAD_PAD, (h + 1) * HEAD_PAD)
        qh = _rope_head(q[:, sl], tab)
        qh = qh * lax.rsqrt(jnp.sum(qh * qh, axis=-1, keepdims=True) * (1.0 / QK_DIM) + NORM_EPS) * gqq
        q_ref[:, sl] = qh.astype(q_ref.dtype)
        kh = kn[:, sl] + kr
        kh = kh * lax.rsqrt(jnp.sum(kh * kh, axis=-1, keepdims=True) * (1.0 / QK_DIM) + NORM_EPS) * gqk
        k_ref[:, sl] = kh.astype(k_ref.dtype)


def _head_gain(g_pairs):
    g_rot = g_pairs[QK_NOPE_DIM:]
    return jnp.concatenate([g_pairs[:QK_NOPE_DIM], g_rot, g_rot, jnp.zeros((32,), f32)])[None, :]


def _mix_in_weights(w_in, w_uq, w_ukv, d_conv, q_rank, kv_rank):
    d_model = w_in.shape[0]
    base = 2 * d_conv + q_rank + kv_rank
    w_kr = w_in[:, base:base + QK_ROPE_DIM]
    w_kr_pad = jnp.concatenate([jnp.zeros((d_model, QK_NOPE_DIM), f32), w_kr,
                                jnp.zeros((d_model, 32), f32)], axis=1)
    win = jnp.concatenate([w_in[:, :base], w_kr_pad], axis=1).astype(bf16)
    wuq = jnp.pad(w_uq, ((0, 0), (0, 0), (0, HEAD_PAD - QK_DIM))).reshape(q_rank, N_HEADS * HEAD_PAD).astype(bf16)
    wk = jnp.pad(w_ukv[:, :, :QK_NOPE_DIM], ((0, 0), (0, 0), (0, HEAD_PAD - QK_NOPE_DIM)))
    wk = wk.reshape(kv_rank, N_HEADS * HEAD_PAD).astype(bf16)
    wv = w_ukv[:, :, QK_NOPE_DIM:].reshape(kv_rank, N_HEADS * V_DIM).astype(bf16)
    return win, wuq, wk, wv


def _mix_in(x2d, tab, tm, g_mix, win, g_q_lat, wuq, g_kv_lat, g_qk_q, wk, wv, g_qk_k, d_conv, q_rank, kv_rank):
    n, d_model = x2d.shape
    n_tab_blocks = tab.shape[0] // tm
    row = lambda i: (i, 0)
    const = lambda i: (0, 0)
    full = lambda a: pl.BlockSpec(a.shape, const)
    gm, gq, gkv = g_mix[None, :], g_q_lat[None, :], g_kv_lat[None, :]
    gqq, gqk = _head_gain(g_qk_q), _head_gain(g_qk_k)
    hp = N_HEADS * HEAD_PAD
    out_shape = (
        jax.ShapeDtypeStruct((n, d_conv), f32),
        jax.ShapeDtypeStruct((n, hp), bf16),
        jax.ShapeDtypeStruct((n, kv_rank), f32),
        jax.ShapeDtypeStruct((n, QK_ROPE_DIM), f32),
        jax.ShapeDtypeStruct((n, hp), bf16),
        jax.ShapeDtypeStruct((n, N_HEADS * V_DIM), bf16),
    )
    return pl.pallas_call(
        functools.partial(_mix_in_kernel, d_conv=d_conv, q_rank=q_rank, kv_rank=kv_rank),
        grid=(n // tm,),
        in_specs=[pl.BlockSpec((tm, d_model), row),
                  pl.BlockSpec((tm, tab.shape[1]), lambda i: (i % n_tab_blocks, 0)),
                  full(gm), full(win), full(gq), full(wuq), full(gkv), full(gqq), full(wk), full(wv), full(gqk)],
        out_specs=tuple(pl.BlockSpec((tm, s.shape[1]), row) for s in out_shape),
        out_shape=out_shape,
        compiler_params=_cparams("parallel"),
        name="mix_in",
    )(x2d, tab, gm, win, gq, wuq, gkv, gqq, wk, wv, gqk)


CONV_PAD = 32


SUBLANES = 8


def _conv_taps(window, w_ref, b_ref, ls, tc):
    lead = CONV_PAD - (CONV_WIDTH - 1)
    acc = jnp.zeros((tc, LANES), f32) + b_ref[:, ls]
    for r in range(SUBLANES):
        taps = [(a, a * SUBLANES + r - lead) for a in range(CONV_PAD // SUBLANES + 1)
                if 0 <= a * SUBLANES + r - lead < CONV_WIDTH]
        shifted = window[r:r + tc + SUBLANES * taps[-1][0]]
        for a, j in taps:
            acc = acc + shifted[a * SUBLANES:a * SUBLANES + tc] * w_ref[j:j + 1, ls]
    return acc


def _ln_swish(y, g_ref, beta_ref):
    mu = jnp.mean(y, axis=-1, keepdims=True)
    yc = y - mu
    var = jnp.mean(yc * yc, axis=-1, keepdims=True)
    z = yc * lax.rsqrt(var + NORM_EPS) * g_ref[...] + beta_ref[...]
    return z * jax.nn.sigmoid(z)


def _conv_prompt_kernel(x_ref, w_ref, b_ref, g_ref, beta_ref, o_ref, *, tc):
    t_len, n_ch = x_ref.shape[1], x_ref.shape[2]

    def chunk(idx, carry):
        t0 = pl.multiple_of(idx * tc, tc)
        h0 = pl.multiple_of(jnp.maximum(t0 - CONV_PAD, 0), SUBLANES)
        has_hist = jnp.where(idx > 0, 1.0, 0.0)
        for lt in range(n_ch // LANES):
            ls = slice(lt * LANES, (lt + 1) * LANES)
            window = jnp.concatenate([x_ref[0, pl.ds(h0, CONV_PAD), ls] * has_hist,
                                      x_ref[0, pl.ds(t0, tc), ls]], axis=0)
            o_ref[0, pl.ds(t0, tc), ls] = _conv_taps(window, w_ref, b_ref, ls, tc)
        o_ref[0, pl.ds(t0, tc), :] = _ln_swish(o_ref[0, pl.ds(t0, tc), :], g_ref, beta_ref)
        return carry

    lax.fori_loop(0, t_len // tc, chunk, 0)


def _conv_sample_kernel(x_ref, st_ref, w_ref, b_ref, g_ref, beta_ref, o_ref):
    bb, tc, n_ch = x_ref.shape

    def one(b, carry):
        for lt in range(n_ch // LANES):
            ls = slice(lt * LANES, (lt + 1) * LANES)
            window = jnp.concatenate([st_ref[b, :, ls], x_ref[b, :, ls]], axis=0)
            o_ref[b, :, ls] = _conv_taps(window, w_ref, b_ref, ls, tc)
        o_ref[b] = _ln_swish(o_ref[b], g_ref, beta_ref)
        return carry

    lax.fori_loop(0, bb, one, 0)


def _conv_specs(conv_w, n_ch):
    const = lambda i: (0, 0)
    vec = pl.BlockSpec((1, n_ch), const)
    return [pl.BlockSpec(conv_w.shape, const), vec, vec, vec]


def _conv_prompt(glu, conv_w, conv_b, ln_g, ln_b, tc=128):
    n_b, t_len, n_ch = glu.shape
    blk = pl.BlockSpec((1, t_len, n_ch), lambda i: (i, 0, 0))
    return pl.pallas_call(
        functools.partial(_conv_prompt_kernel, tc=tc),
        grid=(n_b,),
        in_specs=[blk] + _conv_specs(conv_w, n_ch),
        out_specs=blk,
        out_shape=jax.ShapeDtypeStruct(glu.shape, f32),
        compiler_params=_cparams("parallel"),
        name="conv_prompt",
    )(glu, conv_w, conv_b[None, :], ln_g[None, :], ln_b[None, :])


def _conv_sample(glu, state, conv_w, conv_b, ln_g, ln_b, bb=32):
    n_b, t_len, n_ch = glu.shape
    hist = jnp.pad(state, ((0, 0), (CONV_PAD - state.shape[1], 0), (0, 0)))
    return pl.pallas_call(
        _conv_sample_kernel,
        grid=(n_b // bb,),
        in_specs=[pl.BlockSpec((bb, t_len, n_ch), lambda i: (i, 0, 0)),
                  pl.BlockSpec((bb, CONV_PAD, n_ch), lambda i: (i, 0, 0))] + _conv_specs(conv_w, n_ch),
        out_specs=pl.BlockSpec((bb, t_len, n_ch), lambda i: (i, 0, 0)),
        out_shape=jax.ShapeDtypeStruct(glu.shape, f32),
        compiler_params=_cparams("parallel"),
        name="conv_sample",
    )(glu, hist, conv_w, conv_b[None, :], ln_g[None, :], ln_b[None, :])


NT_DIMS = (((1,), (1,)), ((), ()))


def _attn_prompt_kernel(q_ref, k_ref, v_ref, o_ref, *, tq, tk):
    qi = pl.program_id(2)
    sub = tq // tk
    outs = []
    for j in range(2):
        hs = slice(j * HEAD_PAD, (j + 1) * HEAD_PAD)
        q = q_ref[0, :, hs]

        def step(kb, carry, masked, q=q, hs=hs):
            m, l, acc = carry
            k0 = pl.multiple_of(kb * tk, tk)
            s = lax.dot_general(q, k_ref[0, pl.ds(k0, tk), hs], NT_DIMS, preferred_element_type=f32)
            if masked:
                row = qi * tq + lax.broadcasted_iota(jnp.int32, (tq, tk), 0)
                col = k0 + lax.broadcasted_iota(jnp.int32, (tq, tk), 1)
                s = jnp.where(col <= row, s, -jnp.inf)
            m_new = jnp.maximum(m, jnp.max(s, axis=-1, keepdims=True))
            alpha = jnp.exp(m - m_new)
            p = jnp.exp(s - m_new)
            l = alpha * l + jnp.sum(p, axis=-1, keepdims=True)
            acc = alpha * acc + jnp.dot(p.astype(bf16), v_ref[0, pl.ds(k0, tk), :], preferred_element_type=f32)
            return m_new, l, acc

        carry = (jnp.full((tq, 1), -jnp.inf, f32), jnp.zeros((tq, 1), f32), jnp.zeros((tq, LANES), f32))
        carry = lax.fori_loop(0, qi * sub, functools.partial(step, masked=False), carry)
        for d in range(sub):
            carry = step(qi * sub + d, carry, True)
        outs.append(carry[2] / carry[1])
    lane = lax.broadcasted_iota(jnp.int32, (tq, LANES), 1)
    o_ref[0] = jnp.where(lane < V_DIM, outs[0], outs[1])


def _attn_prompt(q, k, v, tq, tk):
    n_b, t_len, _ = q.shape
    return pl.pallas_call(
        functools.partial(_attn_prompt_kernel, tq=tq, tk=tk),
        grid=(n_b, N_HEADS // 2, t_len // tq),
        in_specs=[pl.BlockSpec((1, tq, 2 * HEAD_PAD), lambda b, h, i: (b, i, h)),
                  pl.BlockSpec((1, t_len, 2 * HEAD_PAD), lambda b, h, i: (b, 0, h)),
                  pl.BlockSpec((1, t_len, 2 * V_DIM), lambda b, h, i: (b, 0, h))],
        out_specs=pl.BlockSpec((1, tq, 2 * V_DIM), lambda b, h, i: (b, i, h)),
        out_shape=jax.ShapeDtypeStruct((n_b, t_len, N_HEADS * V_DIM), f32),
        compiler_params=_cparams("parallel", "parallel", "arbitrary"),
        name="attn_prompt",
    )(q, k, v)


TN_DIMS = (((0,), (0,)), ((), ()))
N_TH = 64


def _q_absorb_kernel(q_ref, gk_ref, wukt_ref, sel_ref, qabs_ref, qr_ref):
    gk = gk_ref[...]
    kv_rank = wukt_ref.shape[-1]
    for h in range(N_HEADS):
        qg = (q_ref[:, h * HEAD_PAD:(h + 1) * HEAD_PAD].astype(f32) * gk).astype(bf16)
        qabs_ref[:, h * kv_rank:(h + 1) * kv_rank] = jnp.dot(
            qg, wukt_ref[h], preferred_element_type=f32).astype(qabs_ref.dtype)
        qr_ref[:, h * QK_ROPE_DIM:(h + 1) * QK_ROPE_DIM] = jnp.dot(
            qg, sel_ref[...], preferred_element_type=f32).astype(qr_ref.dtype)


def _q_absorb(q2d, g_qk_k, w_ukv):
    n = q2d.shape[0]
    kv_rank = w_ukv.shape[0]
    wukt = jnp.transpose(w_ukv[:, :, :QK_NOPE_DIM], (1, 2, 0))
    wukt = jnp.pad(wukt, ((0, 0), (0, HEAD_PAD - QK_NOPE_DIM), (0, 0))).astype(bf16)
    sel = jnp.zeros((HEAD_PAD, QK_ROPE_DIM), f32).at[
        QK_NOPE_DIM + jnp.arange(QK_ROPE_DIM), jnp.arange(QK_ROPE_DIM)].set(1.0).astype(bf16)
    gk = _head_gain(g_qk_k)
    full = lambda a: pl.BlockSpec(a.shape, lambda i: (0,) * a.ndim)
    return pl.pallas_call(
        _q_absorb_kernel,
        grid=(1,),
        in_specs=[full(q2d), full(gk), full(wukt), full(sel)],
        out_specs=(pl.BlockSpec((n, N_HEADS * kv_rank), lambda i: (0, 0)),
                   pl.BlockSpec((n, N_HEADS * QK_ROPE_DIM), lambda i: (0, 0))),
        out_shape=(jax.ShapeDtypeStruct((n, N_HEADS * kv_rank), bf16),
                   jax.ShapeDtypeStruct((n, N_HEADS * QK_ROPE_DIM), bf16)),
        compiler_params=_cparams("arbitrary"),
        name="q_absorb",
    )(q2d, gk, wukt, sel)


AUX_W = 2 * LANES


def _aux_weights(qr):
    n_seq = qr.shape[0]
    lane = jnp.arange(LANES)
    head_sum = (lane[:, None] % N_HEADS == lane[None, :] % N_HEADS).astype(f32)
    top = jnp.concatenate([head_sum, jnp.zeros((LANES, LANES), f32)], axis=1)
    mid = jnp.concatenate([jnp.ones((QK_ROPE_DIM, LANES), f32), jnp.zeros((QK_ROPE_DIM, LANES), f32)], axis=1)
    const = jnp.broadcast_to(jnp.concatenate([top, mid], axis=0).astype(bf16)[None],
                             (n_seq, LANES + QK_ROPE_DIM, AUX_W))
    qr_t = jnp.swapaxes(qr, 1, 2)
    rot = jnp.concatenate([jnp.zeros((n_seq, QK_ROPE_DIM, LANES), bf16), qr_t,
                           jnp.zeros((n_seq, QK_ROPE_DIM, LANES - N_TH), bf16)], axis=2)
    tail = jnp.zeros((n_seq, AUX_W - LANES - 2 * QK_ROPE_DIM, AUX_W), bf16)
    return jnp.concatenate([const, rot, tail], axis=1)


def _lat_chunk(c_bf, kr, qabs, auxw, wuk, state, mask=None):
    m, l, acc = state
    n = c_bf.shape[0]
    kn = jnp.dot(c_bf, wuk, preferred_element_type=f32)
    sq = kn * kn
    t = sq[:, :LANES]
    for i in range(1, sq.shape[1] // LANES):
        t = t + sq[:, i * LANES:(i + 1) * LANES]
    side = jnp.concatenate([kr * kr, kr, jnp.zeros((n, LANES - 2 * QK_ROPE_DIM), f32)], axis=1)
    aux = jnp.concatenate([t, side], axis=1).astype(bf16)
    r = jnp.dot(aux, auxw, preferred_element_type=f32)
    rinv = lax.rsqrt(r[:, :N_TH] * (1.0 / QK_DIM) + NORM_EPS)
    s = lax.dot_general(c_bf, qabs, NT_DIMS, preferred_element_type=f32) + r[:, LANES:LANES + N_TH]
    s = s * rinv
    if mask is not None:
        s = jnp.where(mask, s, -jnp.inf)
    m_new = jnp.maximum(m, jnp.max(s, axis=0, keepdims=True))
    alpha = jnp.exp(m - m_new)
    p = jnp.exp(s - m_new)
    l = alpha * l + jnp.sum(p, axis=0, keepdims=True)
    acc = _row_to_col(alpha) * acc + lax.dot_general(p.astype(bf16), c_bf, TN_DIMS, preferred_element_type=f32)
    return m_new, l, acc


def _row_to_col(v):
    n = v.shape[1]
    eye = lax.broadcasted_iota(jnp.int32, (n, n), 0) == lax.broadcasted_iota(jnp.int32, (n, n), 1)
    return jnp.sum(jnp.where(eye, jnp.broadcast_to(v, (n, n)), 0.0), axis=1, keepdims=True)


SUB_PAGES = 8


def _attn_sample_kernel(pt_ref, qabs_ref, auxw_ref, cnew_ref, krnew_ref, wuk_ref, *rest, pp):
    c_pages = rest[:pp]
    kr_pages = rest[pp:2 * pp]
    o_ref = rest[2 * pp]
    m_s, l_s, acc_s = rest[2 * pp + 1:]
    j = pl.program_id(1)

    @pl.when(j == 0)
    def _():
        m_s[...] = jnp.full(m_s.shape, -jnp.inf, f32)
        l_s[...] = jnp.zeros(l_s.shape, f32)
        acc_s[...] = jnp.zeros(acc_s.shape, f32)

    qabs, auxw, wuk = qabs_ref[0], auxw_ref[0], wuk_ref[...]
    state = (m_s[...], l_s[...], acc_s[...])
    sub = min(SUB_PAGES, pp)
    for b in range(0, pp, sub):
        c_bf = jnp.concatenate([c_pages[i][0].astype(bf16) for i in range(b, b + sub)], axis=0)
        kr = jnp.concatenate([kr_pages[i][0] for i in range(b, b + sub)], axis=0)
        state = _lat_chunk(c_bf, kr, qabs, auxw, wuk, state)
    m_s[...], l_s[...], acc_s[...] = state

    @pl.when(j == pl.num_programs(1) - 1)
    def _():
        t_new = cnew_ref.shape[1]
        key_t = lax.broadcasted_iota(jnp.int32, (t_new, N_TH), 0)
        q_t = lax.broadcasted_iota(jnp.int32, (t_new, N_TH), 1) // N_HEADS
        m, l, acc = _lat_chunk(cnew_ref[0].astype(bf16), krnew_ref[0], qabs, auxw, wuk, state, mask=key_t <= q_t)
        o_ref[0] = acc / _row_to_col(l)


def _attn_sample(page_table, qabs, auxw, c_new, kr_new, cache_c, cache_kr, wuk_perm, pp):
    n_seq, n_pages = page_table.shape
    kv_rank = cache_c.shape[-1]
    t_new = c_new.shape[1]
    seq3 = lambda s, j, pt: (s, 0, 0)

    def page(i):
        return lambda s, j, pt: (pt[s, j * pp + i], 0, 0)

    in_specs = [pl.BlockSpec((1, N_TH, kv_rank), seq3),
                pl.BlockSpec((1, AUX_W, AUX_W), seq3),
                pl.BlockSpec((1, t_new, kv_rank), seq3),
                pl.BlockSpec((1, t_new, QK_ROPE_DIM), seq3),
                pl.BlockSpec(wuk_perm.shape, lambda s, j, pt: (0, 0))]
    in_specs += [pl.BlockSpec((1, PAGE_SIZE, kv_rank), page(i)) for i in range(pp)]
    in_specs += [pl.BlockSpec((1, PAGE_SIZE, QK_ROPE_DIM), page(i)) for i in range(pp)]
    grid_spec = pltpu.PrefetchScalarGridSpec(
        num_scalar_prefetch=1,
        grid=(n_seq, n_pages // pp),
        in_specs=in_specs,
        out_specs=pl.BlockSpec((1, N_TH, kv_rank), seq3),
        scratch_shapes=[pltpu.VMEM((1, N_TH), f32), pltpu.VMEM((1, N_TH), f32),
                        pltpu.VMEM((N_TH, kv_rank), f32)],
    )
    return pl.pallas_call(
        functools.partial(_attn_sample_kernel, pp=pp),
        grid_spec=grid_spec,
        out_shape=jax.ShapeDtypeStruct((n_seq, N_TH, kv_rank), f32),
        compiler_params=_cparams("parallel", "arbitrary"),
        name="attn_sample",
    )(page_table, qabs, auxw, c_new, kr_new, wuk_perm, *([cache_c] * pp), *([cache_kr] * pp))


def _v_up_kernel(lat_ref, wuv_ref, o_ref):
    rows = lat_ref.shape[0]
    full = jnp.dot(lat_ref[...].astype(bf16), wuv_ref[...], preferred_element_type=f32)
    head_of_row = lax.broadcasted_iota(jnp.int32, full.shape, 0) % N_HEADS
    head_of_lane = lax.broadcasted_iota(jnp.int32, full.shape, 1) // V_DIM
    own = jnp.where(head_of_row == head_of_lane, full, 0.0)
    o_ref[...] = jnp.sum(own.reshape(rows // N_HEADS, N_HEADS, full.shape[1]), axis=1)


def _v_up(lat2d, wuv, rows):
    n, kv_rank = lat2d.shape
    d_attn = wuv.shape[1]
    return pl.pallas_call(
        _v_up_kernel,
        grid=(n // rows,),
        in_specs=[pl.BlockSpec((rows, kv_rank), lambda i: (i, 0)), pl.BlockSpec(wuv.shape, lambda i: (0, 0))],
        out_specs=pl.BlockSpec((rows // N_HEADS, d_attn), lambda i: (i, 0)),
        out_shape=jax.ShapeDtypeStruct((n // N_HEADS, d_attn), f32),
        compiler_params=_cparams("parallel"),
        name="v_up",
    )(lat2d, wuv)


def _attn_sample_all(q_s2d, c_new, kr_new, page_table, cache_c, cache_kr, w_ukv, g_qk_k, pp):
    n_seq, t_new, kv_rank = c_new.shape
    qabs, qr = _q_absorb(q_s2d, g_qk_k, w_ukv)
    qabs = qabs.reshape(n_seq, t_new * N_HEADS, kv_rank)
    qr = qr.reshape(n_seq, t_new * N_HEADS, QK_ROPE_DIM)
    wuk_perm = jnp.transpose(w_ukv[:, :, :QK_NOPE_DIM], (0, 2, 1)).reshape(kv_rank, -1).astype(bf16)
    lat = _attn_sample(page_table, qabs, _aux_weights(qr), c_new, kr_new, cache_c, cache_kr, wuk_perm, pp)
    wuv = w_ukv[:, :, QK_NOPE_DIM:].reshape(kv_rank, N_HEADS * V_DIM).astype(bf16)
    rows = min(512, n_seq * N_TH)
    return _v_up(lat.reshape(n_seq * N_TH, kv_rank), wuv, rows)


def _rms(x, g):
    return x * lax.rsqrt(jnp.mean(x * x, axis=-1, keepdims=True) + NORM_EPS) * g


def _split_bf16(x):
    hi = x.astype(bf16)
    return hi, (x - hi.astype(f32)).astype(bf16)


PACK_W = 256
u32 = jnp.uint32


def _pack_bf16_pairs(x):
    bits = lax.bitcast_convert_type(x.astype(bf16).astype(f32), u32)
    pieces = []
    for p in range(x.shape[1] // (2 * PACK_W)):
        lo = bits[:, 2 * p * PACK_W:(2 * p + 1) * PACK_W]
        hi = bits[:, (2 * p + 1) * PACK_W:(2 * p + 2) * PACK_W]
        pieces.append((lo >> 16) | (hi & jnp.uint32(0xFFFF0000)))
    return pieces


def _unpack_bf16_pairs(words):
    return (lax.bitcast_convert_type(words << 16, f32),
            lax.bitcast_convert_type(words & jnp.uint32(0xFFFF0000), f32))


def _merge_kernel(xp_ref, cvp_ref, atp_ref, xs_ref, cvs_ref, ats_ref, gc_ref, ga_ref, woc_ref, woa_ref, gf_ref,
                  wrh_ref, wrl_ref, br_ref, h_ref, xna_ref, xnb_ref, exp_ref, gate_ref, rank_ref, cnt_ref, cnt_s,
                  *, n_prompt_tiles):
    i = pl.program_id(0)
    tm = xp_ref.shape[0]

    @pl.when(i == 0)
    def _():
        cnt_s[...] = jnp.zeros(cnt_s.shape, f32)

    is_p = i < n_prompt_tiles
    x = jnp.where(is_p, xp_ref[...], xs_ref[...])
    yc = _rms(jnp.where(is_p, cvp_ref[...], cvs_ref[...]), gc_ref[...]).astype(bf16)
    ya = _rms(jnp.where(is_p, atp_ref[...], ats_ref[...]), ga_ref[...]).astype(bf16)
    h = x + jnp.dot(yc, woc_ref[...], preferred_element_type=f32) \
        + jnp.dot(ya, woa_ref[...], preferred_element_type=f32)
    h_ref[...] = h
    xn = _rms(h, gf_ref[...])
    xh, xl = _split_bf16(xn)
    xna_ref[...], xnb_ref[...] = _pack_bf16_pairs(xn)
    logits = (jnp.dot(xh, wrh_ref[...], preferred_element_type=f32)
              + jnp.dot(xl, wrh_ref[...], preferred_element_type=f32)
              + jnp.dot(xh, wrl_ref[...], preferred_element_type=f32)) + br_ref[...]
    lane = lax.broadcasted_iota(jnp.int32, logits.shape, 1)
    tops, idxs = [], []
    cur = logits
    for _k in range(TOP_K):
        mk = jnp.max(cur, axis=-1, keepdims=True)
        ik = jnp.min(jnp.where(cur == mk, lane, LANES), axis=-1, keepdims=True)
        tops.append(mk)
        idxs.append(ik)
        cur = jnp.where(lane == ik, -jnp.inf, cur)
    es = [jnp.exp(t - tops[0]) for t in tops]
    den = es[0] + es[1] + es[2] + es[3]
    onehots = [(lane == ik).astype(f32) for ik in idxs]
    oh = onehots[0] + onehots[1] + onehots[2] + onehots[3]
    r_i = lax.broadcasted_iota(jnp.int32, (tm, tm), 0)
    c_i = lax.broadcasted_iota(jnp.int32, (tm, tm), 1)
    before = jnp.dot((c_i < r_i).astype(bf16), oh.astype(bf16), preferred_element_type=f32) + cnt_s[...]
    exp_out = jnp.zeros(logits.shape, jnp.int32)
    gate_out = jnp.zeros(logits.shape, f32)
    rank_out = jnp.zeros(logits.shape, jnp.int32)
    for k in range(TOP_K):
        rk = jnp.sum(onehots[k] * before, axis=-1, keepdims=True)
        exp_out = jnp.where(lane == k, idxs[k], exp_out)
        gate_out = jnp.where(lane == k, es[k] / den, gate_out)
        rank_out = jnp.where(lane == k, rk.astype(jnp.int32), rank_out)
    exp_ref[...] = exp_out
    gate_ref[...] = gate_out
    rank_ref[...] = rank_out
    cnt_s[...] = cnt_s[...] + jnp.sum(oh, axis=0, keepdims=True)
    cnt_ref[...] = cnt_s[...]


def _merge(x_p, cv_p, at_p, x_s, cv_s, at_s, g_out_conv, g_out_attn, w_out, g_ffn, w_router, b_router, tm):
    (n_p, d_model), n_s = x_p.shape, x_s.shape[0]
    n = n_p + n_s
    d_conv, d_attn = cv_p.shape[1], at_p.shape[1]
    woc, woa = w_out[:d_conv].astype(bf16), w_out[d_conv:].astype(bf16)
    wr = jnp.pad(w_router, ((0, 0), (0, LANES - N_EXPERTS)))
    wrh, wrl = _split_bf16(wr)
    br = jnp.concatenate([b_router.astype(f32), jnp.full((LANES - N_EXPERTS,), -jnp.inf, f32)])[None, :]
    gc, ga, gf = g_out_conv[None, :], g_out_attn[None, :], g_ffn[None, :]
    tiles_p, tiles_s = n_p // tm, n_s // tm
    full = lambda a: pl.BlockSpec(a.shape, lambda i: (0, 0))
    rows = lambda w: pl.BlockSpec((tm, w), lambda i: (i, 0))
    rows_p = lambda w: pl.BlockSpec((tm, w), lambda i: (jnp.minimum(i, tiles_p - 1), 0))
    rows_s = lambda w: pl.BlockSpec((tm, w), lambda i: (jnp.maximum(i - tiles_p, 0), 0))
    assert d_model == 4 * PACK_W
    out_shape = (jax.ShapeDtypeStruct((n, d_model), f32),
                 jax.ShapeDtypeStruct((n, PACK_W), u32), jax.ShapeDtypeStruct((n, PACK_W), u32),
                 jax.ShapeDtypeStruct((n, LANES), jnp.int32), jax.ShapeDtypeStruct((n, LANES), f32),
                 jax.ShapeDtypeStruct((n, LANES), jnp.int32), jax.ShapeDtypeStruct((1, LANES), f32))
    return pl.pallas_call(
        functools.partial(_merge_kernel, n_prompt_tiles=tiles_p),
        grid=(tiles_p + tiles_s,),
        in_specs=[rows_p(d_model), rows_p(d_conv), rows_p(d_attn), rows_s(d_model), rows_s(d_conv), rows_s(d_attn),
                  full(gc), full(ga), full(woc), full(woa), full(gf), full(wrh), full(wrl), full(br)],
        out_specs=(rows(d_model), rows(PACK_W), rows(PACK_W), rows(LANES), rows(LANES), rows(LANES),
                   pl.BlockSpec((1, LANES), lambda i: (0, 0))),
        out_shape=out_shape,
        scratch_shapes=[pltpu.VMEM((1, LANES), f32)],
        compiler_params=_cparams("arbitrary"),
        name="merge",
    )(x_p, cv_p, at_p, x_s, cv_s, at_s, gc, ga, woc, woa, gf, wrh, wrl, br)


def _experts_kernel(be_ref, br_ref, xa_ref, xb_ref, wgu_ref, bgu_ref, wd_ref, bd_ref, ya_ref, yb_ref, wgu_s, wd_s):
    i = pl.program_id(0)
    d_expert = wd_ref.shape[1]
    blk = xa_ref.shape[0]

    @pl.when((i == 0) | (be_ref[i] != be_ref[jnp.maximum(i - 1, 0)]))
    def _():
        wgu_s[...] = wgu_ref[0].astype(bf16)
        wd_s[...] = wd_ref[0].astype(bf16)

    @pl.when(br_ref[i] > 0)
    def _():
        x = jnp.concatenate(_unpack_bf16_pairs(xa_ref[...]) + _unpack_bf16_pairs(xb_ref[...]), axis=1)
        live = lax.broadcasted_iota(jnp.int32, (blk, 1), 0) < br_ref[i]
        x = jnp.where(live, x, 0.0).astype(bf16)
        gu = jnp.dot(x, wgu_s[...], preferred_element_type=f32) + bgu_ref[0]
        gate = jnp.minimum(gu[:, :d_expert], SWIGLU_LIMIT)
        up = jnp.clip(gu[:, d_expert:], -SWIGLU_LIMIT, SWIGLU_LIMIT)
        hid = (up + 1.0) * gate * jax.nn.sigmoid(SWIGLU_ALPHA * gate)
        y = jnp.dot(hid.astype(bf16), wd_s[...], preferred_element_type=f32) + bd_ref[0]
        ya_ref[...], yb_ref[...] = _pack_bf16_pairs(y)

    @pl.when(br_ref[i] <= 0)
    def _():
        ya_ref[...] = jnp.zeros(ya_ref.shape, ya_ref.dtype)
        yb_ref[...] = jnp.zeros(yb_ref.shape, yb_ref.dtype)


def _experts(block_expert, block_rows, xs_a, xs_b, w_gate_up, b_gate_up, w_down, b_down, blk):
    n_slots = xs_a.shape[0]
    n_exp, d_model, d_gu = w_gate_up.shape
    d_expert = w_down.shape[1]
    slots = pl.BlockSpec((blk, PACK_W), lambda i, be, br: (i, 0))
    grid_spec = pltpu.PrefetchScalarGridSpec(
        num_scalar_prefetch=2,
        grid=(n_slots // blk,),
        in_specs=[slots, slots,
                  pl.BlockSpec((1, d_model, d_gu), lambda i, be, br: (be[i], 0, 0)),
                  pl.BlockSpec((1, 1, d_gu), lambda i, be, br: (be[i], 0, 0)),
                  pl.BlockSpec((1, d_expert, d_model), lambda i, be, br: (be[i], 0, 0)),
                  pl.BlockSpec((1, 1, d_model), lambda i, be, br: (be[i], 0, 0))],
        out_specs=(slots, slots),
        scratch_shapes=[pltpu.VMEM((d_model, d_gu), bf16), pltpu.VMEM((d_expert, d_model), bf16)],
    )
    packed = jax.ShapeDtypeStruct((n_slots, PACK_W), u32)
    return pl.pallas_call(
        _experts_kernel,
        grid_spec=grid_spec,
        out_shape=(packed, packed),
        compiler_params=_cparams("arbitrary"),
        name="experts",
    )(block_expert, block_rows, xs_a, xs_b, w_gate_up, b_gate_up[:, None, :], w_down, b_down[:, None, :])


def _combine_kernel(h_ref, ya_ref, yb_ref, gate_ref, yp_ref, ys_ref, *, n_prompt_tiles):
    i = pl.program_id(0)
    g = gate_ref[...]
    acc = None
    for k in range(TOP_K):
        gk = g[:, k:k + 1]
        pieces = [p * gk for p in _unpack_bf16_pairs(ya_ref[k]) + _unpack_bf16_pairs(yb_ref[k])]
        acc = pieces if acc is None else [a + p for a, p in zip(acc, pieces)]
    y = h_ref[...] + jnp.concatenate(acc, axis=1)

    @pl.when(i < n_prompt_tiles)
    def _():
        yp_ref[...] = y

    @pl.when(i >= n_prompt_tiles)
    def _():
        ys_ref[...] = y


def _combine(h, ya, yb, gates, n_p, tm):
    n, d_model = h.shape
    tiles_p = n_p // tm
    packed = pl.BlockSpec((TOP_K, tm, PACK_W), lambda i: (0, i, 0))
    return pl.pallas_call(
        functools.partial(_combine_kernel, n_prompt_tiles=tiles_p),
        grid=(n // tm,),
        in_specs=[pl.BlockSpec((tm, d_model), lambda i: (i, 0)), packed, packed,
                  pl.BlockSpec((tm, LANES), lambda i: (i, 0))],
        out_specs=(pl.BlockSpec((tm, d_model), lambda i: (jnp.minimum(i, tiles_p - 1), 0)),
                   pl.BlockSpec((tm, d_model), lambda i: (jnp.maximum(i - tiles_p, 0), 0))),
        out_shape=(jax.ShapeDtypeStruct((n_p, d_model), f32), jax.ShapeDtypeStruct((n - n_p, d_model), f32)),
        compiler_params=_cparams("arbitrary"),
        name="combine",
    )(h, ya, yb, gates)


SC_WINDOW = 128


def _sc_mesh():
    return plsc.VectorSubcoreMesh(core_axis_name="core", subcore_axis_name="subcore")


def _sc_scatter_rows(x, idx, n_out):
    n, w = x.shape
    n_k = idx.shape[0]

    @pl.kernel(out_type=jax.ShapeDtypeStruct((n_out, w), x.dtype), mesh=_sc_mesh(), scratch_types=[])
    def scatter(x_hbm, i_hbm, o_hbm):
        def body(x_vmem, *i_vmem):
            for iv in i_vmem:
                pltpu.sync_copy(x_vmem, o_hbm.at[iv.at[0]])

        pltpu.emit_pipeline(
            body,
            grid=(n // SC_WINDOW,),
            in_specs=[pl.BlockSpec((SC_WINDOW, w), lambda i: (i, 0))]
            + [pl.BlockSpec((1, SC_WINDOW), functools.partial(lambda k, i: (k, i), k)) for k in range(n_k)],
            out_specs=[],
            core_axis_name=("core", "subcore"),
            dimension_semantics=(pltpu.PARALLEL,),
        )(x_hbm, *([i_hbm] * n_k))

    return scatter(x, idx)


def _sc_gather_rows(x, idx):
    n_idx, w = idx.shape[0], x.shape[1]

    @pl.kernel(out_type=jax.ShapeDtypeStruct((n_idx, w), x.dtype), mesh=_sc_mesh(), scratch_types=[])
    def gather(x_hbm, i_hbm, o_hbm):
        def body(i_vmem, o_vmem):
            pltpu.sync_copy(x_hbm.at[i_vmem.at[0]], o_vmem)

        pltpu.emit_pipeline(
            body,
            grid=(n_idx // SC_WINDOW,),
            in_specs=[pl.BlockSpec((1, SC_WINDOW), lambda i: (0, i))],
            out_specs=[pl.BlockSpec((SC_WINDOW, w), lambda i: (i, 0))],
            core_axis_name=("core", "subcore"),
            dimension_semantics=(pltpu.PARALLEL,),
        )(i_hbm, o_hbm)

    return gather(x, idx.reshape(1, n_idx))


EXPERT_BLK = 512


def _moe(h, xn_a, xn_b, expert, gates, rank, counts, n_p, w_gate_up, b_gate_up, w_down, b_down):
    n = h.shape[0]
    blk = EXPERT_BLK
    padded = (counts + blk - 1) // blk * blk
    padded_end = jnp.cumsum(padded)
    start_padded = padded_end - padded
    one_hot = (expert[:, :, None] == jnp.arange(N_EXPERTS)[None, None, :]).astype(jnp.int32)
    slot_t = (jnp.sum(one_hot * start_padded[None, None, :], axis=-1) + rank).T
    n_blocks = -(-(n * TOP_K + N_EXPERTS * (blk - 1)) // blk)
    n_slots = n_blocks * blk
    block_start = jnp.arange(n_blocks, dtype=jnp.int32) * blk
    block_expert = jnp.minimum(jnp.sum((block_start[:, None] >= padded_end[None, :]).astype(jnp.int32), axis=1),
                               N_EXPERTS - 1)
    block_rows = jnp.clip(counts[block_expert] - (block_start - start_padded[block_expert]), 0, blk)
    xs_a = _sc_scatter_rows(xn_a, slot_t, n_slots)
    xs_b = _sc_scatter_rows(xn_b, slot_t, n_slots)
    ys_a, ys_b = _experts(block_expert, block_rows, xs_a, xs_b, w_gate_up, b_gate_up, w_down, b_down, blk)
    flat = slot_t.reshape(-1)
    ya = _sc_gather_rows(ys_a, flat).reshape(TOP_K, n, PACK_W)
    yb = _sc_gather_rows(ys_b, flat).reshape(TOP_K, n, PACK_W)
    return _combine(h, ya, yb, gates, n_p, 512)


def _layer(x_p, x_s, cache_c, cache_kr, state, page_table, norm_mix, w_in, norm_q_lat, w_uq, norm_kv_lat, w_ukv,
           norm_qk_q, norm_qk_k, conv_w, conv_b, conv_ln_g, conv_ln_b, norm_out_conv, norm_out_attn, w_out,
           norm_ffn, w_router, b_router, w_gate_up, b_gate_up, w_down, b_down):
    n_b, t_p, d_model = x_p.shape
    n_seq, t_s, _ = x_s.shape
    d_conv = conv_w.shape[1]
    q_rank, kv_rank = norm_q_lat.shape[0], norm_kv_lat.shape[0]
    n_past = page_table.shape[1] * PAGE_SIZE
    win, wuq, wk, wv = _mix_in_weights(w_in, w_uq, w_ukv, d_conv, q_rank, kv_rank)
    mix = functools.partial(_mix_in, g_mix=norm_mix, win=win, g_q_lat=norm_q_lat, wuq=wuq, g_kv_lat=norm_kv_lat,
                            g_qk_q=norm_qk_q, wk=wk, wv=wv, g_qk_k=norm_qk_k,
                            d_conv=d_conv, q_rank=q_rank, kv_rank=kv_rank)
    x_p2, x_s2 = x_p.reshape(n_b * t_p, d_model), x_s.reshape(n_seq * t_s, d_model)
    tab_p = _rope_head_tables(jnp.arange(t_p))
    tab_s = _rope_head_tables(n_past + jnp.arange(n_seq * t_s) % t_s)
    glu_p, q_p, c_p, kr_p, k_p, v_p = mix(x_p2, tab_p, 512)
    glu_s, q_s, c_s, kr_s, _, _ = mix(x_s2, tab_s, 512)

    glu_p3, glu_s3 = glu_p.reshape(n_b, t_p, d_conv), glu_s.reshape(n_seq, t_s, d_conv)
    cv_p = _conv_prompt(glu_p3, conv_w, conv_b, conv_ln_g, conv_ln_b)
    cv_s = _conv_sample(glu_s3, state, conv_w, conv_b, conv_ln_g, conv_ln_b)

    hp = N_HEADS * HEAD_PAD
    at_p = _attn_prompt(q_p.reshape(n_b, t_p, hp), k_p.reshape(n_b, t_p, hp),
                        v_p.reshape(n_b, t_p, N_HEADS * V_DIM), 512, 512)
    c_s3, kr_s3 = c_s.reshape(n_seq, t_s, kv_rank), kr_s.reshape(n_seq, t_s, QK_ROPE_DIM)
    at_s = _attn_sample_all(q_s, c_s3, kr_s3, page_table, cache_c, cache_kr, w_ukv, norm_qk_k, 32)

    h, xn_a, xn_b, expert, gates, rank, cnt = _merge(
        x_p2, cv_p.reshape(n_b * t_p, d_conv), at_p.reshape(n_b * t_p, -1),
        x_s2, cv_s.reshape(n_seq * t_s, d_conv), at_s,
        norm_out_conv, norm_out_attn, w_out, norm_ffn, w_router, b_router, 512)
    counts = cnt[0, :N_EXPERTS].astype(jnp.int32)
    y_p, y_s = _moe(h, xn_a, xn_b, expert[:, :TOP_K], gates, rank[:, :TOP_K], counts, n_b * t_p,
                    w_gate_up, b_gate_up, w_down, b_down)
    n_hist = CONV_WIDTH - 1
    conv_state_p = jnp.concatenate([jnp.zeros((n_b, max(n_hist - t_p, 0), d_conv), f32),
                                    glu_p3[:, max(t_p - n_hist, 0):]], axis=1)
    conv_state_s = jnp.concatenate([state, glu_s3], axis=1)[:, -n_hist:]
    return (y_p.reshape(x_p.shape), y_s.reshape(x_s.shape), c_p.reshape(n_b, t_p, kv_rank),
            kr_p.reshape(n_b, t_p, QK_ROPE_DIM), conv_state_p, c_s3, kr_s3, conv_state_s)


def kernel(x_prompt, x_sample, cache_kv_latent, cache_k_rope, state_conv, page_table, norm_mix, w_in, norm_q_lat,
           w_uq, norm_kv_lat, w_ukv, norm_qk_q, norm_qk_k, conv_w, conv_b, conv_ln_g, conv_ln_b, norm_out_conv,
           norm_out_attn, w_out, norm_ffn, w_router, b_router, w_gate_up, b_gate_up, w_down, b_down):
    h_p, h_s = x_prompt, x_sample
    per_layer = []
    for l in range(w_in.shape[0]):
        outs = _layer(h_p, h_s, cache_kv_latent[l], cache_k_rope[l], state_conv[l], page_table, norm_mix[l],
                      w_in[l], norm_q_lat[l], w_uq[l], norm_kv_lat[l], w_ukv[l], norm_qk_q[l], norm_qk_k[l],
                      conv_w[l], conv_b[l], conv_ln_g[l], conv_ln_b[l], norm_out_conv[l], norm_out_attn[l],
                      w_out[l], norm_ffn[l], w_router[l], b_router[l], w_gate_up[l], b_gate_up[l], w_down[l],
                      b_down[l])
        h_p, h_s = outs[0], outs[1]
        per_layer.append(outs[2:])
    stacked = [jnp.stack([pl_[i] for pl_ in per_layer]) for i in range(6)]
    return (h_p, h_s, *stacked)
```

```python
import functools

import jax
import jax.numpy as jnp
import numpy as np
from jax import lax
from jax.experimental import pallas as pl
from jax.experimental.pallas import tpu as pltpu
from jax.experimental.pallas import tpu_sc as plsc

N_HEADS = 8
QK_NOPE_DIM = 64
QK_ROPE_DIM = 32
ROPE_HALF = QK_ROPE_DIM // 2
QK_DIM = QK_NOPE_DIM + QK_ROPE_DIM
V_DIM = 64
CONV_WIDTH = 31
N_EXPERTS = 32
TOP_K = 4
SWIGLU_LIMIT = 7.0
SWIGLU_ALPHA = 1.702
ROPE_THETA = 10000.0
NORM_EPS = 1e-6
ATTN_SCALE = QK_DIM ** -0.5
LOG2_E = 1.4426950408889634
PAGE_SIZE = 128

LANES = 128
HEAD_PAD = LANES
VMEM_LIMIT = 56 * 1024 * 1024

f32 = jnp.float32
bf16 = jnp.bfloat16


def _cparams(*sem):
    return pltpu.CompilerParams(dimension_semantics=sem, vmem_limit_bytes=VMEM_LIMIT)


def _rope_head_tables(pos):
    inv_freq = ROPE_THETA ** (-jnp.arange(ROPE_HALF, dtype=f32) / ROPE_HALF)
    ang = pos.astype(f32)[:, None] * inv_freq[None, :]
    cos, sin = jnp.cos(ang), jnp.sin(ang)
    n = pos.shape[0]
    z = lambda w: jnp.zeros((n, w), f32)
    c = jnp.concatenate([jnp.ones((n, QK_NOPE_DIM), f32), cos, cos, z(32)], axis=1)
    s1 = jnp.concatenate([z(QK_NOPE_DIM + ROPE_HALF), sin, z(32)], axis=1)
    s2 = jnp.concatenate([z(QK_NOPE_DIM), -sin, z(ROPE_HALF + 32)], axis=1)
    return jnp.concatenate([c, s1, s2], axis=1)


def _rope_head(x, tab):
    c, s1, s2 = tab[:, :LANES], tab[:, LANES:2 * LANES], tab[:, 2 * LANES:]
    return x * c + pltpu.roll(x, ROPE_HALF, 1) * s1 + pltpu.roll(x, LANES - ROPE_HALF, 1) * s2


def _mix_in_kernel(x_ref, tab_ref, gmix_ref, win_ref, gq_ref, wuq_ref, gkv_ref, gqq_ref,
                   wk_ref, wvt_ref, gqk_ref,
                   glu_ref, q_ref, c_ref, kr_ref, k_ref, vt_ref, *, d_conv, q_rank, kv_rank, q_scale):
    x = x_ref[...]
    xn = x * lax.rsqrt(jnp.mean(x * x, axis=-1, keepdims=True) + NORM_EPS) * gmix_ref[...]
    proj = jnp.dot(xn.astype(bf16), win_ref[...], preferred_element_type=f32)
    o = 0
    a = proj[:, o:o + d_conv]; o += d_conv
    gt = proj[:, o:o + d_conv]; o += d_conv
    q_lat = proj[:, o:o + q_rank]; o += q_rank
    kv_lat = proj[:, o:o + kv_rank]; o += kv_rank
    kr_raw = proj[:, o:o + LANES]
    glu_ref[...] = a * jax.nn.sigmoid(gt)

    tab = tab_ref[...]
    qn = q_lat * lax.rsqrt(jnp.mean(q_lat * q_lat, axis=-1, keepdims=True) + NORM_EPS) * gq_ref[...]
    q = jnp.dot(qn.astype(bf16), wuq_ref[...], preferred_element_type=f32)
    c_kv = kv_lat * lax.rsqrt(jnp.mean(kv_lat * kv_lat, axis=-1, keepdims=True) + NORM_EPS) * gkv_ref[...]
    c_ref[...] = c_kv
    kr = _rope_head(kr_raw, tab)
    kr_ref[...] = kr[:, QK_NOPE_DIM:QK_NOPE_DIM + QK_ROPE_DIM]
    c_bf = c_kv.astype(bf16)
    kn = jnp.dot(c_bf, wk_ref[...], preferred_element_type=f32)
    vt_ref[0] = lax.dot_general(wvt_ref[...], c_bf, NT_DIMS, preferred_element_type=f32).astype(vt_ref.dtype)
    gqq = gqq_ref[...] * q_scale
    gqk = gqk_ref[...]
    for h in range(N_HEADS):
        sl = slice(h * HEAD_PAD, (h + 1) * HEAD_PAD)
        qh = _rope_head(q[:, sl], tab)
        qh = qh * lax.rsqrt(jnp.sum(qh * qh, axis=-1, keepdims=True) * (1.0 / QK_DIM) + NORM_EPS) * gqq
        q_ref[:, sl] = qh.astype(q_ref.dtype)
        kh = kn[:, sl] + kr
        kh = kh * lax.rsqrt(jnp.sum(kh * kh, axis=-1, keepdims=True) * (1.0 / QK_DIM) + NORM_EPS) * gqk
        k_ref[:, sl] = kh.astype(k_ref.dtype)


def _head_gain(g_pairs):
    g_rot = g_pairs[QK_NOPE_DIM:]
    return jnp.concatenate([g_pairs[:QK_NOPE_DIM], g_rot, g_rot, jnp.zeros((32,), f32)])[None, :]


def _mix_in_weights(w_in, w_uq, w_ukv, d_conv, q_rank, kv_rank):
    d_model = w_in.shape[0]
    base = 2 * d_conv + q_rank + kv_rank
    w_kr = w_in[:, base:base + QK_ROPE_DIM]
    w_kr_pad = jnp.concatenate([jnp.zeros((d_model, QK_NOPE_DIM), f32), w_kr,
                                jnp.zeros((d_model, 32), f32)], axis=1)
    win = jnp.concatenate([w_in[:, :base], w_kr_pad], axis=1).astype(bf16)
    wuq = jnp.pad(w_uq, ((0, 0), (0, 0), (0, HEAD_PAD - QK_DIM))).reshape(q_rank, N_HEADS * HEAD_PAD).astype(bf16)
    wk = jnp.pad(w_ukv[:, :, :QK_NOPE_DIM], ((0, 0), (0, 0), (0, HEAD_PAD - QK_NOPE_DIM)))
    wk = wk.reshape(kv_rank, N_HEADS * HEAD_PAD).astype(bf16)
    wvt = w_ukv[:, :, QK_NOPE_DIM:].reshape(kv_rank, N_HEADS * V_DIM).T.astype(bf16)
    return win, wuq, wk, wvt


def _mix_in(x2d, tab, tm, g_mix, win, g_q_lat, wuq, g_kv_lat, g_qk_q, wk, wvt, g_qk_k, d_conv, q_rank, kv_rank,
            seq_len, q_scale):
    n, d_model = x2d.shape
    n_tab_blocks = tab.shape[0] // tm
    seq_tiles = seq_len // tm
    row = lambda i: (i, 0)
    const = lambda i: (0, 0)
    full = lambda a: pl.BlockSpec(a.shape, const)
    gm, gq, gkv = g_mix[None, :], g_q_lat[None, :], g_kv_lat[None, :]
    gqq, gqk = _head_gain(g_qk_q), _head_gain(g_qk_k)
    hp = N_HEADS * HEAD_PAD
    out_shape = (
        jax.ShapeDtypeStruct((n, d_conv), f32),
        jax.ShapeDtypeStruct((n, hp), bf16),
        jax.ShapeDtypeStruct((n, kv_rank), f32),
        jax.ShapeDtypeStruct((n, QK_ROPE_DIM), f32),
        jax.ShapeDtypeStruct((n, hp), bf16),
        jax.ShapeDtypeStruct((n // seq_len, N_HEADS * V_DIM, seq_len), bf16),
    )
    out_specs = [pl.BlockSpec((tm, s.shape[1]), row) for s in out_shape[:-1]]
    out_specs.append(pl.BlockSpec((1, N_HEADS * V_DIM, tm), lambda i: (i // seq_tiles, 0, i % seq_tiles)))
    return pl.pallas_call(
        functools.partial(_mix_in_kernel, d_conv=d_conv, q_rank=q_rank, kv_rank=kv_rank, q_scale=q_scale),
        grid=(n // tm,),
        in_specs=[pl.BlockSpec((tm, d_model), row),
                  pl.BlockSpec((tm, tab.shape[1]), lambda i: (i % n_tab_blocks, 0)),
                  full(gm), full(win), full(gq), full(wuq), full(gkv), full(gqq), full(wk), full(wvt), full(gqk)],
        out_specs=tuple(out_specs),
        out_shape=out_shape,
        compiler_params=_cparams("parallel"),
        name="mix_in",
    )(x2d, tab, gm, win, gq, wuq, gkv, gqq, wk, wvt, gqk)


CONV_PAD = 32


SUBLANES = 8


def _conv_taps(window, w_ref, b_ref, ls, tc):
    lead = CONV_PAD - (CONV_WIDTH - 1)
    acc = jnp.zeros((tc, LANES), f32) + b_ref[:, ls]
    for r in range(SUBLANES):
        taps = [(a, a * SUBLANES + r - lead) for a in range(CONV_PAD // SUBLANES + 1)
                if 0 <= a * SUBLANES + r - lead < CONV_WIDTH]
        shifted = window[r:r + tc + SUBLANES * taps[-1][0]]
        for a, j in taps:
            acc = acc + shifted[a * SUBLANES:a * SUBLANES + tc] * w_ref[j:j + 1, ls]
    return acc


def _ln_swish(y, g_ref, beta_ref):
    mu = jnp.mean(y, axis=-1, keepdims=True)
    yc = y - mu
    var = jnp.mean(yc * yc, axis=-1, keepdims=True)
    z = yc * lax.rsqrt(var + NORM_EPS) * g_ref[...] + beta_ref[...]
    return z * jax.nn.sigmoid(z)


def _conv_prompt_kernel(x_ref, w_ref, b_ref, g_ref, beta_ref, o_ref, *, tc):
    t_len, n_ch = x_ref.shape[1], x_ref.shape[2]

    def chunk(idx, carry):
        t0 = pl.multiple_of(idx * tc, tc)
        h0 = pl.multiple_of(jnp.maximum(t0 - CONV_PAD, 0), SUBLANES)
        has_hist = jnp.where(idx > 0, 1.0, 0.0)
        for lt in range(n_ch // LANES):
            ls = slice(lt * LANES, (lt + 1) * LANES)
            window = jnp.concatenate([x_ref[0, pl.ds(h0, CONV_PAD), ls] * has_hist,
                                      x_ref[0, pl.ds(t0, tc), ls]], axis=0)
            o_ref[0, pl.ds(t0, tc), ls] = _conv_taps(window, w_ref, b_ref, ls, tc)
        o_ref[0, pl.ds(t0, tc), :] = _ln_swish(o_ref[0, pl.ds(t0, tc), :], g_ref, beta_ref)
        return carry

    lax.fori_loop(0, t_len // tc, chunk, 0)


def _conv_sample_kernel(x_ref, st_ref, w_ref, b_ref, g_ref, beta_ref, o_ref):
    bb, tc, n_ch = x_ref.shape

    def one(b, carry):
        for lt in range(n_ch // LANES):
            ls = slice(lt * LANES, (lt + 1) * LANES)
            window = jnp.concatenate([st_ref[b, :, ls], x_ref[b, :, ls]], axis=0)
            o_ref[b, :, ls] = _conv_taps(window, w_ref, b_ref, ls, tc)
        o_ref[b] = _ln_swish(o_ref[b], g_ref, beta_ref)
        return carry

    lax.fori_loop(0, bb, one, 0)


def _conv_specs(conv_w, n_ch):
    const = lambda i: (0, 0)
    vec = pl.BlockSpec((1, n_ch), const)
    return [pl.BlockSpec(conv_w.shape, const), vec, vec, vec]


def _conv_prompt(glu, conv_w, conv_b, ln_g, ln_b, tc=128):
    n_b, t_len, n_ch = glu.shape
    blk = pl.BlockSpec((1, t_len, n_ch), lambda i: (i, 0, 0))
    return pl.pallas_call(
        functools.partial(_conv_prompt_kernel, tc=tc),
        grid=(n_b,),
        in_specs=[blk] + _conv_specs(conv_w, n_ch),
        out_specs=blk,
        out_shape=jax.ShapeDtypeStruct(glu.shape, f32),
        compiler_params=_cparams("parallel"),
        name="conv_prompt",
    )(glu, conv_w, conv_b[None, :], ln_g[None, :], ln_b[None, :])


def _conv_sample(glu, state, conv_w, conv_b, ln_g, ln_b, bb=32):
    n_b, t_len, n_ch = glu.shape
    hist = jnp.pad(state, ((0, 0), (CONV_PAD - state.shape[1], 0), (0, 0)))
    return pl.pallas_call(
        _conv_sample_kernel,
        grid=(n_b // bb,),
        in_specs=[pl.BlockSpec((bb, t_len, n_ch), lambda i: (i, 0, 0)),
                  pl.BlockSpec((bb, CONV_PAD, n_ch), lambda i: (i, 0, 0))] + _conv_specs(conv_w, n_ch),
        out_specs=pl.BlockSpec((bb, t_len, n_ch), lambda i: (i, 0, 0)),
        out_shape=jax.ShapeDtypeStruct(glu.shape, f32),
        compiler_params=_cparams("parallel"),
        name="conv_sample",
    )(glu, hist, conv_w, conv_b[None, :], ln_g[None, :], ln_b[None, :])


NT_DIMS = (((1,), (1,)), ((), ()))


def _attn_prompt_kernel(q_ref, k_ref, vt_ref, o_ref, *, tq, tk):
    qi = pl.program_id(2)
    sub = tq // tk
    n_h = q_ref.shape[2] // HEAD_PAD
    qs = [q_ref[0, :, h * HEAD_PAD:(h + 1) * HEAD_PAD] for h in range(n_h)]

    def step(kb, carry, masked):
        k0 = pl.multiple_of(kb * tk, tk)
        if masked:
            key = k0 + lax.broadcasted_iota(jnp.int32, (tk, tq), 0)
            qry = qi * tq + lax.broadcasted_iota(jnp.int32, (tk, tq), 1)
            visible = key <= qry
        ones = jnp.ones((ONES_ROWS, tk), bf16)
        sts = []
        for h in range(n_h):
            k_h = k_ref[0, pl.ds(k0, tk), h * HEAD_PAD:(h + 1) * HEAD_PAD]
            sts.append(lax.dot_general(k_h, qs[h], NT_DIMS, preferred_element_type=f32))
        out = []
        for h, (m, acc) in enumerate(carry):
            st = jnp.where(visible, sts[h], -jnp.inf) if masked else sts[h]
            m_new = jnp.maximum(m, jnp.max(st, axis=0, keepdims=True))
            alpha = jnp.exp2(m - m_new)
            p = jnp.exp2(st - m_new).astype(bf16)
            vt_h = jnp.concatenate([vt_ref[0, h * V_DIM:(h + 1) * V_DIM, pl.ds(k0, tk)], ones], axis=0)
            acc = alpha * acc + jnp.dot(vt_h, p, preferred_element_type=f32)
            out.append((m_new, acc))
        return tuple(out)

    carry = tuple((jnp.full((1, tq), -jnp.inf, f32), jnp.zeros((V_DIM + ONES_ROWS, tq), f32)) for _ in range(n_h))
    carry = lax.fori_loop(0, qi * sub, functools.partial(step, masked=False), carry)
    for d in range(sub):
        carry = step(qi * sub + d, carry, True)
    o_t = jnp.concatenate([acc[:V_DIM] / acc[V_DIM:V_DIM + 1] for (_, acc) in carry], axis=0)
    o_ref[0] = o_t.T


ONES_ROWS = 16


ATTN_HEAD_GROUP = 4


def _attn_prompt(q, k, vt, tq, tk):
    n_b, t_len, _ = q.shape
    g = ATTN_HEAD_GROUP
    return pl.pallas_call(
        functools.partial(_attn_prompt_kernel, tq=tq, tk=tk),
        grid=(n_b, N_HEADS // g, t_len // tq),
        in_specs=[pl.BlockSpec((1, tq, g * HEAD_PAD), lambda b, h, i: (b, i, h)),
                  pl.BlockSpec((1, t_len, g * HEAD_PAD), lambda b, h, i: (b, 0, h)),
                  pl.BlockSpec((1, g * V_DIM, t_len), lambda b, h, i: (b, h, 0))],
        out_specs=pl.BlockSpec((1, tq, g * V_DIM), lambda b, h, i: (b, i, h)),
        out_shape=jax.ShapeDtypeStruct((n_b, t_len, N_HEADS * V_DIM), f32),
        compiler_params=_cparams("parallel", "parallel", "arbitrary"),
        name="attn_prompt",
    )(q, k, vt)


TN_DIMS = (((0,), (0,)), ((), ()))
N_TH = 64


def _q_absorb_kernel(q_ref, gk_ref, wukt_ref, sel_ref, qabs_ref, qr_ref):
    gk = gk_ref[...]
    kv_rank = wukt_ref.shape[-1]
    for h in range(N_HEADS):
        qg = (q_ref[:, h * HEAD_PAD:(h + 1) * HEAD_PAD].astype(f32) * gk).astype(bf16)
        qabs_ref[:, h * kv_rank:(h + 1) * kv_rank] = jnp.dot(
            qg, wukt_ref[h], preferred_element_type=f32).astype(qabs_ref.dtype)
        qr_ref[:, h * QK_ROPE_DIM:(h + 1) * QK_ROPE_DIM] = jnp.dot(
            qg, sel_ref[...], preferred_element_type=f32).astype(qr_ref.dtype)


def _q_absorb(q2d, g_qk_k, w_ukv):
    n = q2d.shape[0]
    kv_rank = w_ukv.shape[0]
    wukt = jnp.transpose(w_ukv[:, :, :QK_NOPE_DIM], (1, 2, 0))
    wukt = jnp.pad(wukt, ((0, 0), (0, HEAD_PAD - QK_NOPE_DIM), (0, 0))).astype(bf16)
    sel = jnp.zeros((HEAD_PAD, QK_ROPE_DIM), f32).at[
        QK_NOPE_DIM + jnp.arange(QK_ROPE_DIM), jnp.arange(QK_ROPE_DIM)].set(1.0).astype(bf16)
    gk = _head_gain(g_qk_k)
    full = lambda a: pl.BlockSpec(a.shape, lambda i: (0,) * a.ndim)
    return pl.pallas_call(
        _q_absorb_kernel,
        grid=(1,),
        in_specs=[full(q2d), full(gk), full(wukt), full(sel)],
        out_specs=(pl.BlockSpec((n, N_HEADS * kv_rank), lambda i: (0, 0)),
                   pl.BlockSpec((n, N_HEADS * QK_ROPE_DIM), lambda i: (0, 0))),
        out_shape=(jax.ShapeDtypeStruct((n, N_HEADS * kv_rank), bf16),
                   jax.ShapeDtypeStruct((n, N_HEADS * QK_ROPE_DIM), bf16)),
        compiler_params=_cparams("arbitrary"),
        name="q_absorb",
    )(q2d, gk, wukt, sel)


AUX_W = 2 * LANES


def _aux_weights(qr):
    n_seq = qr.shape[0]
    lane = jnp.arange(LANES)
    head_sum = (lane[:, None] % N_HEADS == lane[None, :] % N_HEADS).astype(f32)
    top = jnp.concatenate([head_sum, jnp.zeros((LANES, LANES), f32)], axis=1)
    mid = jnp.concatenate([jnp.ones((QK_ROPE_DIM, LANES), f32), jnp.zeros((QK_ROPE_DIM, LANES), f32)], axis=1)
    const = jnp.broadcast_to(jnp.concatenate([top, mid], axis=0).astype(bf16)[None],
                             (n_seq, LANES + QK_ROPE_DIM, AUX_W))
    qr_t = jnp.swapaxes(qr, 1, 2)
    rot = jnp.concatenate([jnp.zeros((n_seq, QK_ROPE_DIM, LANES), bf16), qr_t,
                           jnp.zeros((n_seq, QK_ROPE_DIM, LANES - N_TH), bf16)], axis=2)
    tail = jnp.zeros((n_seq, AUX_W - LANES - 2 * QK_ROPE_DIM, AUX_W), bf16)
    return jnp.concatenate([const, rot, tail], axis=1)


def _lat_chunk(c_bf, kr, qabs, auxw, wuk, state, mask=None):
    m, l, acc = state
    n = c_bf.shape[0]
    kn = jnp.dot(c_bf, wuk, preferred_element_type=f32)
    sq = kn * kn
    t = sq[:, :LANES]
    for i in range(1, sq.shape[1] // LANES):
        t = t + sq[:, i * LANES:(i + 1) * LANES]
    side = jnp.concatenate([kr * kr, kr, jnp.zeros((n, LANES - 2 * QK_ROPE_DIM), f32)], axis=1)
    aux = jnp.concatenate([t, side], axis=1).astype(bf16)
    r = jnp.dot(aux, auxw, preferred_element_type=f32)
    rinv = lax.rsqrt(r[:, :N_TH] * (1.0 / QK_DIM) + NORM_EPS)
    s = lax.dot_general(c_bf, qabs, NT_DIMS, preferred_element_type=f32) + r[:, LANES:LANES + N_TH]
    s = s * rinv
    if mask is not None:
        s = jnp.where(mask, s, -jnp.inf)
    m_new = jnp.maximum(m, jnp.max(s, axis=0, keepdims=True))
    alpha = jnp.exp(m - m_new)
    p = jnp.exp(s - m_new)
    l = alpha * l + jnp.sum(p, axis=0, keepdims=True)
    acc = _row_to_col(alpha) * acc + lax.dot_general(p.astype(bf16), c_bf, TN_DIMS, preferred_element_type=f32)
    return m_new, l, acc


def _row_to_col(v):
    n = v.shape[1]
    eye = lax.broadcasted_iota(jnp.int32, (n, n), 0) == lax.broadcasted_iota(jnp.int32, (n, n), 1)
    return jnp.sum(jnp.where(eye, jnp.broadcast_to(v, (n, n)), 0.0), axis=1, keepdims=True)


SUB_PAGES = 8


def _attn_sample_kernel(pt_ref, qabs_ref, auxw_ref, cnew_ref, krnew_ref, wuk_ref, *rest, pp):
    c_pages = rest[:pp]
    kr_pages = rest[pp:2 * pp]
    o_ref = rest[2 * pp]
    m_s, l_s, acc_s = rest[2 * pp + 1:]
    j = pl.program_id(1)

    @pl.when(j == 0)
    def _():
        m_s[...] = jnp.full(m_s.shape, -jnp.inf, f32)
        l_s[...] = jnp.zeros(l_s.shape, f32)
        acc_s[...] = jnp.zeros(acc_s.shape, f32)

    qabs, auxw, wuk = qabs_ref[0], auxw_ref[0], wuk_ref[...]
    state = (m_s[...], l_s[...], acc_s[...])
    sub = min(SUB_PAGES, pp)
    for b in range(0, pp, sub):
        c_bf = jnp.concatenate([c_pages[i][0].astype(bf16) for i in range(b, b + sub)], axis=0)
        kr = jnp.concatenate([kr_pages[i][0] for i in range(b, b + sub)], axis=1).T
        state = _lat_chunk(c_bf, kr, qabs, auxw, wuk, state)
    m_s[...], l_s[...], acc_s[...] = state

    @pl.when(j == pl.num_programs(1) - 1)
    def _():
        t_new = cnew_ref.shape[1]
        key_t = lax.broadcasted_iota(jnp.int32, (t_new, N_TH), 0)
        q_t = lax.broadcasted_iota(jnp.int32, (t_new, N_TH), 1) // N_HEADS
        m, l, acc = _lat_chunk(cnew_ref[0].astype(bf16), krnew_ref[0], qabs, auxw, wuk, state, mask=key_t <= q_t)
        o_ref[0] = acc / _row_to_col(l)


def _attn_sample(page_table, qabs, auxw, c_new, kr_new, cache_c, cache_kr, wuk_perm, pp):
    n_seq, n_pages = page_table.shape
    kv_rank = cache_c.shape[-1]
    t_new = c_new.shape[1]
    seq3 = lambda s, j, pt: (s, 0, 0)
    cache_kr_t = jnp.swapaxes(cache_kr, 1, 2)

    def page(i):
        return lambda s, j, pt: (pt[s, j * pp + i], 0, 0)

    in_specs = [pl.BlockSpec((1, N_TH, kv_rank), seq3),
                pl.BlockSpec((1, AUX_W, AUX_W), seq3),
                pl.BlockSpec((1, t_new, kv_rank), seq3),
                pl.BlockSpec((1, t_new, QK_ROPE_DIM), seq3),
                pl.BlockSpec(wuk_perm.shape, lambda s, j, pt: (0, 0))]
    in_specs += [pl.BlockSpec((1, PAGE_SIZE, kv_rank), page(i)) for i in range(pp)]
    in_specs += [pl.BlockSpec((1, QK_ROPE_DIM, PAGE_SIZE), page(i)) for i in range(pp)]
    grid_spec = pltpu.PrefetchScalarGridSpec(
        num_scalar_prefetch=1,
        grid=(n_seq, n_pages // pp),
        in_specs=in_specs,
        out_specs=pl.BlockSpec((1, N_TH, kv_rank), seq3),
        scratch_shapes=[pltpu.VMEM((1, N_TH), f32), pltpu.VMEM((1, N_TH), f32),
                        pltpu.VMEM((N_TH, kv_rank), f32)],
    )
    return pl.pallas_call(
        functools.partial(_attn_sample_kernel, pp=pp),
        grid_spec=grid_spec,
        out_shape=jax.ShapeDtypeStruct((n_seq, N_TH, kv_rank), f32),
        compiler_params=_cparams("parallel", "arbitrary"),
        name="attn_sample",
    )(page_table, qabs, auxw, c_new, kr_new, wuk_perm, *([cache_c] * pp), *([cache_kr_t] * pp))


def _v_up_kernel(lat_ref, wuv_ref, o_ref):
    rows = lat_ref.shape[0]
    full = jnp.dot(lat_ref[...].astype(bf16), wuv_ref[...], preferred_element_type=f32)
    head_of_row = lax.broadcasted_iota(jnp.int32, full.shape, 0) % N_HEADS
    head_of_lane = lax.broadcasted_iota(jnp.int32, full.shape, 1) // V_DIM
    own = jnp.where(head_of_row == head_of_lane, full, 0.0)
    o_ref[...] = jnp.sum(own.reshape(rows // N_HEADS, N_HEADS, full.shape[1]), axis=1)


def _v_up(lat2d, wuv, rows):
    n, kv_rank = lat2d.shape
    d_attn = wuv.shape[1]
    return pl.pallas_call(
        _v_up_kernel,
        grid=(n // rows,),
        in_specs=[pl.BlockSpec((rows, kv_rank), lambda i: (i, 0)), pl.BlockSpec(wuv.shape, lambda i: (0, 0))],
        out_specs=pl.BlockSpec((rows // N_HEADS, d_attn), lambda i: (i, 0)),
        out_shape=jax.ShapeDtypeStruct((n // N_HEADS, d_attn), f32),
        compiler_params=_cparams("parallel"),
        name="v_up",
    )(lat2d, wuv)


def _attn_sample_all(q_s2d, c_new, kr_new, page_table, cache_c, cache_kr, w_ukv, g_qk_k, pp):
    n_seq, t_new, kv_rank = c_new.shape
    qabs, qr = _q_absorb(q_s2d, g_qk_k, w_ukv)
    qabs = qabs.reshape(n_seq, t_new * N_HEADS, kv_rank)
    qr = qr.reshape(n_seq, t_new * N_HEADS, QK_ROPE_DIM)
    wuk_perm = jnp.transpose(w_ukv[:, :, :QK_NOPE_DIM], (0, 2, 1)).reshape(kv_rank, -1).astype(bf16)
    lat = _attn_sample(page_table, qabs, _aux_weights(qr), c_new, kr_new, cache_c, cache_kr, wuk_perm, pp)
    wuv = w_ukv[:, :, QK_NOPE_DIM:].reshape(kv_rank, N_HEADS * V_DIM).astype(bf16)
    rows = min(512, n_seq * N_TH)
    return _v_up(lat.reshape(n_seq * N_TH, kv_rank), wuv, rows)


def _rms(x, g):
    return x * lax.rsqrt(jnp.mean(x * x, axis=-1, keepdims=True) + NORM_EPS) * g


def _split_bf16(x):
    hi = x.astype(bf16)
    return hi, (x - hi.astype(f32)).astype(bf16)


PACK_W = 256
u32 = jnp.uint32


def _pack_bf16_pairs(x):
    bits = lax.bitcast_convert_type(x.astype(bf16).astype(f32), u32)
    pieces = []
    for p in range(x.shape[1] // (2 * PACK_W)):
        lo = bits[:, 2 * p * PACK_W:(2 * p + 1) * PACK_W]
        hi = bits[:, (2 * p + 1) * PACK_W:(2 * p + 2) * PACK_W]
        pieces.append((lo >> 16) | (hi & jnp.uint32(0xFFFF0000)))
    return pieces


def _unpack_bf16_pairs(words):
    return (lax.bitcast_convert_type(words << 16, f32),
            lax.bitcast_convert_type(words & jnp.uint32(0xFFFF0000), f32))


def _merge_kernel(xp_ref, cvp_ref, atp_ref, xs_ref, cvs_ref, ats_ref, gc_ref, ga_ref, woc_ref, woa_ref, gf_ref,
                  wrh_ref, wrl_ref, br_ref, h_ref, xna_ref, xnb_ref, exp_ref, gate_ref, rank_ref, cnt_ref, cnt_s,
                  *, n_prompt_tiles):
    i = pl.program_id(0)
    tm = xp_ref.shape[0]

    @pl.when(i == 0)
    def _():
        cnt_s[...] = jnp.zeros(cnt_s.shape, f32)

    is_p = i < n_prompt_tiles
    x = jnp.where(is_p, xp_ref[...], xs_ref[...])
    yc = _rms(jnp.where(is_p, cvp_ref[...], cvs_ref[...]), gc_ref[...]).astype(bf16)
    ya = _rms(jnp.where(is_p, atp_ref[...], ats_ref[...]), ga_ref[...]).astype(bf16)
    h = x + jnp.dot(yc, woc_ref[...], preferred_element_type=f32) \
        + jnp.dot(ya, woa_ref[...], preferred_element_type=f32)
    h_ref[...] = h
    xn = _rms(h, gf_ref[...])
    xh, xl = _split_bf16(xn)
    xna_ref[...], xnb_ref[...] = _pack_bf16_pairs(xn)
    logits = (jnp.dot(xh, wrh_ref[...], preferred_element_type=f32)
              + jnp.dot(xl, wrh_ref[...], preferred_element_type=f32)
              + jnp.dot(xh, wrl_ref[...], preferred_element_type=f32)) + br_ref[...]
    lane = lax.broadcasted_iota(jnp.int32, logits.shape, 1)
    tops, idxs = [], []
    cur = logits
    for _k in range(TOP_K):
        mk = jnp.max(cur, axis=-1, keepdims=True)
        ik = jnp.min(jnp.where(cur == mk, lane, LANES), axis=-1, keepdims=True)
        tops.append(mk)
        idxs.append(ik)
        cur = jnp.where(lane == ik, -jnp.inf, cur)
    es = [jnp.exp(t - tops[0]) for t in tops]
    den = es[0] + es[1] + es[2] + es[3]
    onehots = [(lane == ik).astype(f32) for ik in idxs]
    oh = onehots[0] + onehots[1] + onehots[2] + onehots[3]
    r_i = lax.broadcasted_iota(jnp.int32, (tm, tm), 0)
    c_i = lax.broadcasted_iota(jnp.int32, (tm, tm), 1)
    before = jnp.dot((c_i < r_i).astype(bf16), oh.astype(bf16), preferred_element_type=f32) + cnt_s[...]
    exp_out = jnp.zeros(logits.shape, jnp.int32)
    gate_out = jnp.zeros(logits.shape, f32)
    rank_out = jnp.zeros(logits.shape, jnp.int32)
    for k in range(TOP_K):
        rk = jnp.sum(onehots[k] * before, axis=-1, keepdims=True)
        exp_out = jnp.where(lane == k, idxs[k], exp_out)
        gate_out = jnp.where(lane == k, es[k] / den, gate_out)
        rank_out = jnp.where(lane == k, rk.astype(jnp.int32), rank_out)
    exp_ref[...] = exp_out
    gate_ref[...] = gate_out
    rank_ref[...] = rank_out
    cnt_s[...] = cnt_s[...] + jnp.sum(oh, axis=0, keepdims=True)
    cnt_ref[...] = cnt_s[...]


def _merge(x_p, cv_p, at_p, x_s, cv_s, at_s, g_out_conv, g_out_attn, w_out, g_ffn, w_router, b_router, tm):
    (n_p, d_model), n_s = x_p.shape, x_s.shape[0]
    n = n_p + n_s
    d_conv, d_attn = cv_p.shape[1], at_p.shape[1]
    woc, woa = w_out[:d_conv].astype(bf16), w_out[d_conv:].astype(bf16)
    wr = jnp.pad(w_router, ((0, 0), (0, LANES - N_EXPERTS)))
    wrh, wrl = _split_bf16(wr)
    br = jnp.concatenate([b_router.astype(f32), jnp.full((LANES - N_EXPERTS,), -jnp.inf, f32)])[None, :]
    gc, ga, gf = g_out_conv[None, :], g_out_attn[None, :], g_ffn[None, :]
    tiles_p, tiles_s = n_p // tm, n_s // tm
    full = lambda a: pl.BlockSpec(a.shape, lambda i: (0, 0))
    rows = lambda w: pl.BlockSpec((tm, w), lambda i: (i, 0))
    rows_p = lambda w: pl.BlockSpec((tm, w), lambda i: (jnp.minimum(i, tiles_p - 1), 0))
    rows_s = lambda w: pl.BlockSpec((tm, w), lambda i: (jnp.maximum(i - tiles_p, 0), 0))
    assert d_model == 4 * PACK_W
    out_shape = (jax.ShapeDtypeStruct((n, d_model), f32),
                 jax.ShapeDtypeStruct((n, PACK_W), u32), jax.ShapeDtypeStruct((n, PACK_W), u32),
                 jax.ShapeDtypeStruct((n, LANES), jnp.int32), jax.ShapeDtypeStruct((n, LANES), f32),
                 jax.ShapeDtypeStruct((n, LANES), jnp.int32), jax.ShapeDtypeStruct((1, LANES), f32))
    return pl.pallas_call(
        functools.partial(_merge_kernel, n_prompt_tiles=tiles_p),
        grid=(tiles_p + tiles_s,),
        in_specs=[rows_p(d_model), rows_p(d_conv), rows_p(d_attn), rows_s(d_model), rows_s(d_conv), rows_s(d_attn),
                  full(gc), full(ga), full(woc), full(woa), full(gf), full(wrh), full(wrl), full(br)],
        out_specs=(rows(d_model), rows(PACK_W), rows(PACK_W), rows(LANES), rows(LANES), rows(LANES),
                   pl.BlockSpec((1, LANES), lambda i: (0, 0))),
        out_shape=out_shape,
        scratch_shapes=[pltpu.VMEM((1, LANES), f32)],
        compiler_params=_cparams("arbitrary"),
        name="merge",
    )(x_p, cv_p, at_p, x_s, cv_s, at_s, gc, ga, woc, woa, gf, wrh, wrl, br)


def _experts_kernel(be_ref, br_ref, xa_ref, xb_ref, wgu_ref, bgu_ref, wd_ref, bd_ref, ya_ref, yb_ref, wgu_s, wd_s):
    i = pl.program_id(0)
    d_expert = wd_ref.shape[1]
    blk = xa_ref.shape[0]

    @pl.when((i == 0) | (be_ref[i] != be_ref[jnp.maximum(i - 1, 0)]))
    def _():
        wgu_s[...] = wgu_ref[0].astype(bf16)
        wd_s[...] = wd_ref[0].astype(bf16)

    @pl.when(br_ref[i] > 0)
    def _():
        x = jnp.concatenate(_unpack_bf16_pairs(xa_ref[...]) + _unpack_bf16_pairs(xb_ref[...]), axis=1)
        live = lax.broadcasted_iota(jnp.int32, (blk, 1), 0) < br_ref[i]
        x = jnp.where(live, x, 0.0).astype(bf16)
        gu = jnp.dot(x, wgu_s[...], preferred_element_type=f32) + bgu_ref[0]
        gate = jnp.minimum(gu[:, :d_expert], SWIGLU_LIMIT)
        up = jnp.clip(gu[:, d_expert:], -SWIGLU_LIMIT, SWIGLU_LIMIT)
        hid = (up + 1.0) * gate * jax.nn.sigmoid(SWIGLU_ALPHA * gate)
        y = jnp.dot(hid.astype(bf16), wd_s[...], preferred_element_type=f32) + bd_ref[0]
        ya_ref[...], yb_ref[...] = _pack_bf16_pairs(y)

    @pl.when(br_ref[i] <= 0)
    def _():
        ya_ref[...] = jnp.zeros(ya_ref.shape, ya_ref.dtype)
        yb_ref[...] = jnp.zeros(yb_ref.shape, yb_ref.dtype)


def _experts(block_expert, block_rows, xs_a, xs_b, w_gate_up, b_gate_up, w_down, b_down, blk):
    n_slots = xs_a.shape[0]
    n_exp, d_model, d_gu = w_gate_up.shape
    d_expert = w_down.shape[1]
    slots = pl.BlockSpec((blk, PACK_W), lambda i, be, br: (i, 0))
    grid_spec = pltpu.PrefetchScalarGridSpec(
        num_scalar_prefetch=2,
        grid=(n_slots // blk,),
        in_specs=[slots, slots,
                  pl.BlockSpec((1, d_model, d_gu), lambda i, be, br: (be[i], 0, 0)),
                  pl.BlockSpec((1, 1, d_gu), lambda i, be, br: (be[i], 0, 0)),
                  pl.BlockSpec((1, d_expert, d_model), lambda i, be, br: (be[i], 0, 0)),
                  pl.BlockSpec((1, 1, d_model), lambda i, be, br: (be[i], 0, 0))],
        out_specs=(slots, slots),
        scratch_shapes=[pltpu.VMEM((d_model, d_gu), bf16), pltpu.VMEM((d_expert, d_model), bf16)],
    )
    packed = jax.ShapeDtypeStruct((n_slots, PACK_W), u32)
    return pl.pallas_call(
        _experts_kernel,
        grid_spec=grid_spec,
        out_shape=(packed, packed),
        compiler_params=_cparams("arbitrary"),
        name="experts",
    )(block_expert, block_rows, xs_a, xs_b, w_gate_up, b_gate_up[:, None, :], w_down, b_down[:, None, :])


def _combine_kernel(h_ref, ya_ref, yb_ref, gate_ref, yp_ref, ys_ref, *, n_prompt_tiles):
    i = pl.program_id(0)
    g = gate_ref[...]
    acc = None
    for k in range(TOP_K):
        gk = g[:, k:k + 1]
        pieces = [p * gk for p in _unpack_bf16_pairs(ya_ref[k]) + _unpack_bf16_pairs(yb_ref[k])]
        acc = pieces if acc is None else [a + p for a, p in zip(acc, pieces)]
    y = h_ref[...] + jnp.concatenate(acc, axis=1)

    @pl.when(i < n_prompt_tiles)
    def _():
        yp_ref[...] = y

    @pl.when(i >= n_prompt_tiles)
    def _():
        ys_ref[...] = y


def _combine(h, ya, yb, gates, n_p, tm):
    n, d_model = h.shape
    tiles_p = n_p // tm
    packed = pl.BlockSpec((TOP_K, tm, PACK_W), lambda i: (0, i, 0))
    return pl.pallas_call(
        functools.partial(_combine_kernel, n_prompt_tiles=tiles_p),
        grid=(n // tm,),
        in_specs=[pl.BlockSpec((tm, d_model), lambda i: (i, 0)), packed, packed,
                  pl.BlockSpec((tm, LANES), lambda i: (i, 0))],
        out_specs=(pl.BlockSpec((tm, d_model), lambda i: (jnp.minimum(i, tiles_p - 1), 0)),
                   pl.BlockSpec((tm, d_model), lambda i: (jnp.maximum(i - tiles_p, 0), 0))),
        out_shape=(jax.ShapeDtypeStruct((n_p, d_model), f32), jax.ShapeDtypeStruct((n - n_p, d_model), f32)),
        compiler_params=_cparams("arbitrary"),
        name="combine",
    )(h, ya, yb, gates)


SC_WINDOW = 128


def _sc_mesh():
    return plsc.VectorSubcoreMesh(core_axis_name="core", subcore_axis_name="subcore")


def _sc_scatter_rows(x, idx, n_out):
    n, w = x.shape
    n_k = idx.shape[0]

    @pl.kernel(out_type=jax.ShapeDtypeStruct((n_out, w), x.dtype), mesh=_sc_mesh(), scratch_types=[])
    def scatter(x_hbm, i_hbm, o_hbm):
        def body(x_vmem, *i_vmem):
            for iv in i_vmem:
                pltpu.sync_copy(x_vmem, o_hbm.at[iv.at[0]])

        pltpu.emit_pipeline(
            body,
            grid=(n // SC_WINDOW,),
            in_specs=[pl.BlockSpec((SC_WINDOW, w), lambda i: (i, 0))]
            + [pl.BlockSpec((1, SC_WINDOW), functools.partial(lambda k, i: (k, i), k)) for k in range(n_k)],
            out_specs=[],
            core_axis_name=("core", "subcore"),
            dimension_semantics=(pltpu.PARALLEL,),
        )(x_hbm, *([i_hbm] * n_k))

    return scatter(x, idx)


def _sc_gather_rows(x, idx):
    n_idx, w = idx.shape[0], x.shape[1]

    @pl.kernel(out_type=jax.ShapeDtypeStruct((n_idx, w), x.dtype), mesh=_sc_mesh(), scratch_types=[])
    def gather(x_hbm, i_hbm, o_hbm):
        def body(i_vmem, o_vmem):
            pltpu.sync_copy(x_hbm.at[i_vmem.at[0]], o_vmem)

        pltpu.emit_pipeline(
            body,
            grid=(n_idx // SC_WINDOW,),
            in_specs=[pl.BlockSpec((1, SC_WINDOW), lambda i: (0, i))],
            out_specs=[pl.BlockSpec((SC_WINDOW, w), lambda i: (i, 0))],
            core_axis_name=("core", "subcore"),
            dimension_semantics=(pltpu.PARALLEL,),
        )(i_hbm, o_hbm)

    return gather(x, idx.reshape(1, n_idx))


EXPERT_BLK = 512


def _moe(h, xn_a, xn_b, expert, gates, rank, counts, n_p, w_gate_up, b_gate_up, w_down, b_down):
    n = h.shape[0]
    blk = EXPERT_BLK
    padded = (counts + blk - 1) // blk * blk
    padded_end = jnp.cumsum(padded)
    start_padded = padded_end - padded
    one_hot = (expert[:, :, None] == jnp.arange(N_EXPERTS)[None, None, :]).astype(jnp.int32)
    slot_t = (jnp.sum(one_hot * start_padded[None, None, :], axis=-1) + rank).T
    n_blocks = -(-(n * TOP_K + N_EXPERTS * (blk - 1)) // blk)
    n_slots = n_blocks * blk
    block_start = jnp.arange(n_blocks, dtype=jnp.int32) * blk
    block_expert = jnp.minimum(jnp.sum((block_start[:, None] >= padded_end[None, :]).astype(jnp.int32), axis=1),
                               N_EXPERTS - 1)
    block_rows = jnp.clip(counts[block_expert] - (block_start - start_padded[block_expert]), 0, blk)
    xs_a = _sc_scatter_rows(xn_a, slot_t, n_slots)
    xs_b = _sc_scatter_rows(xn_b, slot_t, n_slots)
    ys_a, ys_b = _experts(block_expert, block_rows, xs_a, xs_b, w_gate_up, b_gate_up, w_down, b_down, blk)
    flat = slot_t.reshape(-1)
    ya = _sc_gather_rows(ys_a, flat).reshape(TOP_K, n, PACK_W)
    yb = _sc_gather_rows(ys_b, flat).reshape(TOP_K, n, PACK_W)
    return _combine(h, ya, yb, gates, n_p, 512)


def _layer(x_p, x_s, cache_c, cache_kr, state, page_table, norm_mix, w_in, norm_q_lat, w_uq, norm_kv_lat, w_ukv,
           norm_qk_q, norm_qk_k, conv_w, conv_b, conv_ln_g, conv_ln_b, norm_out_conv, norm_out_attn, w_out,
           norm_ffn, w_router, b_router, w_gate_up, b_gate_up, w_down, b_down):
    n_b, t_p, d_model = x_p.shape
    n_seq, t_s, _ = x_s.shape
    d_conv = conv_w.shape[1]
    q_rank, kv_rank = norm_q_lat.shape[0], norm_kv_lat.shape[0]
    n_past = page_table.shape[1] * PAGE_SIZE
    win, wuq, wk, wvt = _mix_in_weights(w_in, w_uq, w_ukv, d_conv, q_rank, kv_rank)
    mix = functools.partial(_mix_in, g_mix=norm_mix, win=win, g_q_lat=norm_q_lat, wuq=wuq, g_kv_lat=norm_kv_lat,
                            g_qk_q=norm_qk_q, wk=wk, wvt=wvt, g_qk_k=norm_qk_k,
                            d_conv=d_conv, q_rank=q_rank, kv_rank=kv_rank)
    x_p2, x_s2 = x_p.reshape(n_b * t_p, d_model), x_s.reshape(n_seq * t_s, d_model)
    tab_p = _rope_head_tables(jnp.arange(t_p))
    tab_s = _rope_head_tables(n_past + jnp.arange(n_seq * t_s) % t_s)
    glu_p, q_p, c_p, kr_p, k_p, vt_p = mix(x_p2, tab_p, 512, seq_len=t_p, q_scale=ATTN_SCALE * LOG2_E)
    glu_s, q_s, c_s, kr_s, _, _ = mix(x_s2, tab_s, 512, seq_len=n_seq * t_s, q_scale=ATTN_SCALE)

    glu_p3, glu_s3 = glu_p.reshape(n_b, t_p, d_conv), glu_s.reshape(n_seq, t_s, d_conv)
    cv_p = _conv_prompt(glu_p3, conv_w, conv_b, conv_ln_g, conv_ln_b)
    cv_s = _conv_sample(glu_s3, state, conv_w, conv_b, conv_ln_g, conv_ln_b)

    hp = N_HEADS * HEAD_PAD
    at_p = _attn_prompt(q_p.reshape(n_b, t_p, hp), k_p.reshape(n_b, t_p, hp), vt_p, 512, 512)
    c_s3, kr_s3 = c_s.reshape(n_seq, t_s, kv_rank), kr_s.reshape(n_seq, t_s, QK_ROPE_DIM)
    at_s = _attn_sample_all(q_s, c_s3, kr_s3, page_table, cache_c, cache_kr, w_ukv, norm_qk_k, 32)

    h, xn_a, xn_b, expert, gates, rank, cnt = _merge(
        x_p2, cv_p.reshape(n_b * t_p, d_conv), at_p.reshape(n_b * t_p, -1),
        x_s2, cv_s.reshape(n_seq * t_s, d_conv), at_s,
        norm_out_conv, norm_out_attn, w_out, norm_ffn, w_router, b_router, 512)
    counts = cnt[0, :N_EXPERTS].astype(jnp.int32)
    y_p, y_s = _moe(h, xn_a, xn_b, expert[:, :TOP_K], gates, rank[:, :TOP_K], counts, n_b * t_p,
                    w_gate_up, b_gate_up, w_down, b_down)
    n_hist = CONV_WIDTH - 1
    conv_state_p = jnp.concatenate([jnp.zeros((n_b, max(n_hist - t_p, 0), d_conv), f32),
                                    glu_p3[:, max(t_p - n_hist, 0):]], axis=1)
    conv_state_s = jnp.concatenate([state, glu_s3], axis=1)[:, -n_hist:]
    return (y_p.reshape(x_p.shape), y_s.reshape(x_s.shape), c_p.reshape(n_b, t_p, kv_rank),
            kr_p.reshape(n_b, t_p, QK_ROPE_DIM), conv_state_p, c_s3, kr_s3, conv_state_s)


def kernel(x_prompt, x_sample, cache_kv_latent, cache_k_rope, state_conv, page_table, norm_mix, w_in, norm_q_lat,
           w_uq, norm_kv_lat, w_ukv, norm_qk_q, norm_qk_k, conv_w, conv_b, conv_ln_g, conv_ln_b, norm_out_conv,
           norm_out_attn, w_out, norm_ffn, w_router, b_router, w_gate_up, b_gate_up, w_down, b_down):
    h_p, h_s = x_prompt, x_sample
    per_layer = []
    for l in range(w_in.shape[0]):
        outs = _layer(h_p, h_s, cache_kv_latent[l], cache_k_rope[l], state_conv[l], page_table, norm_mix[l],
                      w_in[l], norm_q_lat[l], w_uq[l], norm_kv_lat[l], w_ukv[l], norm_qk_q[l], norm_qk_k[l],
                      conv_w[l], conv_b[l], conv_ln_g[l], conv_ln_b[l], norm_out_conv[l], norm_out_attn[l],
                      w_out[l], norm_ffn[l], w_router[l], b_router[l], w_gate_up[l], b_gate_up[l], w_down[l],
                      b_down[l])
        h_p, h_s = outs[0], outs[1]
        per_layer.append(outs[2:])
    stacked = [jnp.stack([pl_[i] for pl_ in per_layer]) for i in range(6)]
    return (h_p, h_s, *stacked)
```

```python
import functools

import jax
import jax.numpy as jnp
import numpy as np
from jax import lax
from jax.experimental import pallas as pl
from jax.experimental.pallas import tpu as pltpu
from jax.experimental.pallas import tpu_sc as plsc

N_HEADS = 8
QK_NOPE_DIM = 64
QK_ROPE_DIM = 32
ROPE_HALF = QK_ROPE_DIM // 2
QK_DIM = QK_NOPE_DIM + QK_ROPE_DIM
V_DIM = 64
CONV_WIDTH = 31
N_EXPERTS = 32
TOP_K = 4
SWIGLU_LIMIT = 7.0
SWIGLU_ALPHA = 1.702
ROPE_THETA = 10000.0
NORM_EPS = 1e-6
ATTN_SCALE = QK_DIM ** -0.5
LOG2_E = 1.4426950408889634
PAGE_SIZE = 128

LANES = 128
HEAD_PAD = LANES
VMEM_LIMIT = 56 * 1024 * 1024

f32 = jnp.float32
bf16 = jnp.bfloat16


def _cparams(*sem):
    return pltpu.CompilerParams(dimension_semantics=sem, vmem_limit_bytes=VMEM_LIMIT)


def _rope_head_tables(pos):
    inv_freq = ROPE_THETA ** (-jnp.arange(ROPE_HALF, dtype=f32) / ROPE_HALF)
    ang = pos.astype(f32)[:, None] * inv_freq[None, :]
    cos, sin = jnp.cos(ang), jnp.sin(ang)
    n = pos.shape[0]
    z = lambda w: jnp.zeros((n, w), f32)
    c = jnp.concatenate([jnp.ones((n, QK_NOPE_DIM), f32), cos, cos, z(32)], axis=1)
    s1 = jnp.concatenate([z(QK_NOPE_DIM + ROPE_HALF), sin, z(32)], axis=1)
    s2 = jnp.concatenate([z(QK_NOPE_DIM), -sin, z(ROPE_HALF + 32)], axis=1)
    return jnp.concatenate([c, s1, s2], axis=1)


def _rope_head(x, tab):
    c, s1, s2 = tab[:, :LANES], tab[:, LANES:2 * LANES], tab[:, 2 * LANES:]
    return x * c + pltpu.roll(x, ROPE_HALF, 1) * s1 + pltpu.roll(x, LANES - ROPE_HALF, 1) * s2


def _mix_in_kernel(x_ref, tab_ref, gmix_ref, win_ref, gq_ref, wuq_ref, gkv_ref, gqq_ref,
                   wk_ref, wvt_ref, gqk_ref,
                   glu_ref, q_ref, c_ref, kr_ref, k_ref, vt_ref, *, d_conv, q_rank, kv_rank, q_scale):
    tm = x_ref.shape[0]
    rows = tm // MIX_SUB_TILES
    gqq = gqq_ref[...] * q_scale
    gqk = gqk_ref[...]

    def project(r):
        x = x_ref[r, :]
        xn = x * lax.rsqrt(jnp.mean(x * x, axis=-1, keepdims=True) + NORM_EPS) * gmix_ref[...]
        return jnp.dot(xn.astype(bf16), win_ref[...], preferred_element_type=f32)

    def latents(r, proj):
        o = 0
        a = proj[:, o:o + d_conv]; o += d_conv
        gt = proj[:, o:o + d_conv]; o += d_conv
        q_lat = proj[:, o:o + q_rank]; o += q_rank
        kv_lat = proj[:, o:o + kv_rank]; o += kv_rank
        kr_raw = proj[:, o:o + LANES]
        glu_ref[r, :] = a * jax.nn.sigmoid(gt)
        tab = tab_ref[r, :]
        qn = q_lat * lax.rsqrt(jnp.mean(q_lat * q_lat, axis=-1, keepdims=True) + NORM_EPS) * gq_ref[...]
        q = jnp.dot(qn.astype(bf16), wuq_ref[...], preferred_element_type=f32)
        c_kv = kv_lat * lax.rsqrt(jnp.mean(kv_lat * kv_lat, axis=-1, keepdims=True) + NORM_EPS) * gkv_ref[...]
        c_ref[r, :] = c_kv
        kr = _rope_head(kr_raw, tab)
        kr_ref[r, :] = kr[:, QK_NOPE_DIM:QK_NOPE_DIM + QK_ROPE_DIM]
        c_bf = c_kv.astype(bf16)
        kn = jnp.dot(c_bf, wk_ref[...], preferred_element_type=f32)
        vt_ref[0, :, r] = lax.dot_general(wvt_ref[...], c_bf, NT_DIMS,
                                          preferred_element_type=f32).astype(vt_ref.dtype)
        return q, kn, kr, tab

    def heads(r, q, kn, kr, tab):
        for h in range(N_HEADS):
            sl = slice(h * HEAD_PAD, (h + 1) * HEAD_PAD)
            qh = _rope_head(q[:, sl], tab)
            qh = qh * lax.rsqrt(jnp.sum(qh * qh, axis=-1, keepdims=True) * (1.0 / QK_DIM) + NORM_EPS) * gqq
            q_ref[r, sl] = qh.astype(q_ref.dtype)
            kh = kn[:, sl] + kr
            kh = kh * lax.rsqrt(jnp.sum(kh * kh, axis=-1, keepdims=True) * (1.0 / QK_DIM) + NORM_EPS) * gqk
            k_ref[r, sl] = kh.astype(k_ref.dtype)

    slices = [pl.ds(i * rows, rows) for i in range(MIX_SUB_TILES)]
    proj = project(slices[0])
    for i, r in enumerate(slices):
        nxt = project(slices[i + 1]) if i + 1 < len(slices) else None
        heads(r, *latents(r, proj))
        proj = nxt


MIX_SUB_TILES = 2


def _head_gain(g_pairs):
    g_rot = g_pairs[QK_NOPE_DIM:]
    return jnp.concatenate([g_pairs[:QK_NOPE_DIM], g_rot, g_rot, jnp.zeros((32,), f32)])[None, :]


def _mix_in_weights(w_in, w_uq, w_ukv, d_conv, q_rank, kv_rank):
    d_model = w_in.shape[0]
    base = 2 * d_conv + q_rank + kv_rank
    w_kr = w_in[:, base:base + QK_ROPE_DIM]
    w_kr_pad = jnp.concatenate([jnp.zeros((d_model, QK_NOPE_DIM), f32), w_kr,
                                jnp.zeros((d_model, 32), f32)], axis=1)
    win = jnp.concatenate([w_in[:, :base], w_kr_pad], axis=1).astype(bf16)
    wuq = jnp.pad(w_uq, ((0, 0), (0, 0), (0, HEAD_PAD - QK_DIM))).reshape(q_rank, N_HEADS * HEAD_PAD).astype(bf16)
    wk = jnp.pad(w_ukv[:, :, :QK_NOPE_DIM], ((0, 0), (0, 0), (0, HEAD_PAD - QK_NOPE_DIM)))
    wk = wk.reshape(kv_rank, N_HEADS * HEAD_PAD).astype(bf16)
    wvt = w_ukv[:, :, QK_NOPE_DIM:].reshape(kv_rank, N_HEADS * V_DIM).T.astype(bf16)
    return win, wuq, wk, wvt


def _mix_in(x2d, tab, tm, g_mix, win, g_q_lat, wuq, g_kv_lat, g_qk_q, wk, wvt, g_qk_k, d_conv, q_rank, kv_rank,
            seq_len, q_scale):
    n, d_model = x2d.shape
    n_tab_blocks = tab.shape[0] // tm
    seq_tiles = seq_len // tm
    row = lambda i: (i, 0)
    const = lambda i: (0, 0)
    full = lambda a: pl.BlockSpec(a.shape, const)
    gm, gq, gkv = g_mix[None, :], g_q_lat[None, :], g_kv_lat[None, :]
    gqq, gqk = _head_gain(g_qk_q), _head_gain(g_qk_k)
    hp = N_HEADS * HEAD_PAD
    out_shape = (
        jax.ShapeDtypeStruct((n, d_conv), f32),
        jax.ShapeDtypeStruct((n, hp), bf16),
        jax.ShapeDtypeStruct((n, kv_rank), f32),
        jax.ShapeDtypeStruct((n, QK_ROPE_DIM), f32),
        jax.ShapeDtypeStruct((n, hp), bf16),
        jax.ShapeDtypeStruct((n // seq_len, N_HEADS * V_DIM, seq_len), bf16),
    )
    out_specs = [pl.BlockSpec((tm, s.shape[1]), row) for s in out_shape[:-1]]
    out_specs.append(pl.BlockSpec((1, N_HEADS * V_DIM, tm), lambda i: (i // seq_tiles, 0, i % seq_tiles)))
    return pl.pallas_call(
        functools.partial(_mix_in_kernel, d_conv=d_conv, q_rank=q_rank, kv_rank=kv_rank, q_scale=q_scale),
        grid=(n // tm,),
        in_specs=[pl.BlockSpec((tm, d_model), row),
                  pl.BlockSpec((tm, tab.shape[1]), lambda i: (i % n_tab_blocks, 0)),
                  full(gm), full(win), full(gq), full(wuq), full(gkv), full(gqq), full(wk), full(wvt), full(gqk)],
        out_specs=tuple(out_specs),
        out_shape=out_shape,
        compiler_params=_cparams("parallel"),
        name="mix_in",
    )(x2d, tab, gm, win, gq, wuq, gkv, gqq, wk, wvt, gqk)


CONV_PAD = 32


SUBLANES = 8


def _conv_taps(window, w_ref, b_ref, ls, tc):
    lead = CONV_PAD - (CONV_WIDTH - 1)
    acc = jnp.zeros((tc, LANES), f32) + b_ref[:, ls]
    for r in range(SUBLANES):
        taps = [(a, a * SUBLANES + r - lead) for a in range(CONV_PAD // SUBLANES + 1)
                if 0 <= a * SUBLANES + r - lead < CONV_WIDTH]
        shifted = window[r:r + tc + SUBLANES * taps[-1][0]]
        for a, j in taps:
            acc = acc + shifted[a * SUBLANES:a * SUBLANES + tc] * w_ref[j:j + 1, ls]
    return acc


def _ln_swish(y, g_ref, beta_ref):
    mu = jnp.mean(y, axis=-1, keepdims=True)
    yc = y - mu
    var = jnp.mean(yc * yc, axis=-1, keepdims=True)
    z = yc * lax.rsqrt(var + NORM_EPS) * g_ref[...] + beta_ref[...]
    return z * jax.nn.sigmoid(z)


def _conv_prompt_kernel(x_ref, w_ref, b_ref, g_ref, beta_ref, o_ref, *, tc):
    t_len, n_ch = x_ref.shape[1], x_ref.shape[2]

    def chunk(idx, carry):
        t0 = pl.multiple_of(idx * tc, tc)
        h0 = pl.multiple_of(jnp.maximum(t0 - CONV_PAD, 0), SUBLANES)
        has_hist = jnp.where(idx > 0, 1.0, 0.0)
        for lt in range(n_ch // LANES):
            ls = slice(lt * LANES, (lt + 1) * LANES)
            window = jnp.concatenate([x_ref[0, pl.ds(h0, CONV_PAD), ls] * has_hist,
                                      x_ref[0, pl.ds(t0, tc), ls]], axis=0)
            o_ref[0, pl.ds(t0, tc), ls] = _conv_taps(window, w_ref, b_ref, ls, tc)
        o_ref[0, pl.ds(t0, tc), :] = _ln_swish(o_ref[0, pl.ds(t0, tc), :], g_ref, beta_ref)
        return carry

    lax.fori_loop(0, t_len // tc, chunk, 0)


def _conv_sample_kernel(x_ref, st_ref, w_ref, b_ref, g_ref, beta_ref, o_ref):
    bb, tc, n_ch = x_ref.shape

    def one(b, carry):
        for lt in range(n_ch // LANES):
            ls = slice(lt * LANES, (lt + 1) * LANES)
            window = jnp.concatenate([st_ref[b, :, ls], x_ref[b, :, ls]], axis=0)
            o_ref[b, :, ls] = _conv_taps(window, w_ref, b_ref, ls, tc)
        o_ref[b] = _ln_swish(o_ref[b], g_ref, beta_ref)
        return carry

    lax.fori_loop(0, bb, one, 0)


def _conv_specs(conv_w, n_ch):
    const = lambda i: (0, 0)
    vec = pl.BlockSpec((1, n_ch), const)
    return [pl.BlockSpec(conv_w.shape, const), vec, vec, vec]


def _conv_prompt(glu, conv_w, conv_b, ln_g, ln_b, tc=128):
    n_b, t_len, n_ch = glu.shape
    blk = pl.BlockSpec((1, t_len, n_ch), lambda i: (i, 0, 0))
    return pl.pallas_call(
        functools.partial(_conv_prompt_kernel, tc=tc),
        grid=(n_b,),
        in_specs=[blk] + _conv_specs(conv_w, n_ch),
        out_specs=blk,
        out_shape=jax.ShapeDtypeStruct(glu.shape, f32),
        compiler_params=_cparams("parallel"),
        name="conv_prompt",
    )(glu, conv_w, conv_b[None, :], ln_g[None, :], ln_b[None, :])


def _conv_sample(glu, state, conv_w, conv_b, ln_g, ln_b, bb=32):
    n_b, t_len, n_ch = glu.shape
    hist = jnp.pad(state, ((0, 0), (CONV_PAD - state.shape[1], 0), (0, 0)))
    return pl.pallas_call(
        _conv_sample_kernel,
        grid=(n_b // bb,),
        in_specs=[pl.BlockSpec((bb, t_len, n_ch), lambda i: (i, 0, 0)),
                  pl.BlockSpec((bb, CONV_PAD, n_ch), lambda i: (i, 0, 0))] + _conv_specs(conv_w, n_ch),
        out_specs=pl.BlockSpec((bb, t_len, n_ch), lambda i: (i, 0, 0)),
        out_shape=jax.ShapeDtypeStruct(glu.shape, f32),
        compiler_params=_cparams("parallel"),
        name="conv_sample",
    )(glu, hist, conv_w, conv_b[None, :], ln_g[None, :], ln_b[None, :])


NT_DIMS = (((1,), (1,)), ((), ()))


def _attn_prompt_kernel(q_ref, k_ref, vt_ref, o_ref, *, tq, tk):
    qi = pl.program_id(2)
    sub = tq // tk
    n_h = q_ref.shape[2] // HEAD_PAD
    qs = [q_ref[0, :, h * HEAD_PAD:(h + 1) * HEAD_PAD] for h in range(n_h)]

    def step(kb, carry, masked):
        k0 = pl.multiple_of(kb * tk, tk)
        if masked:
            key = k0 + lax.broadcasted_iota(jnp.int32, (tk, tq), 0)
            qry = qi * tq + lax.broadcasted_iota(jnp.int32, (tk, tq), 1)
            visible = key <= qry
        ones = jnp.ones((ONES_ROWS, tk), bf16)
        sts = []
        for h in range(n_h):
            k_h = k_ref[0, pl.ds(k0, tk), h * HEAD_PAD:(h + 1) * HEAD_PAD]
            sts.append(lax.dot_general(k_h, qs[h], NT_DIMS, preferred_element_type=f32))
        out = []
        for h, (m, acc) in enumerate(carry):
            st = jnp.where(visible, sts[h], -jnp.inf) if masked else sts[h]
            m_new = jnp.maximum(m, jnp.max(st, axis=0, keepdims=True))
            alpha = jnp.exp2(m - m_new)
            p = jnp.exp2(st - m_new).astype(bf16)
            vt_h = jnp.concatenate([vt_ref[0, h * V_DIM:(h + 1) * V_DIM, pl.ds(k0, tk)], ones], axis=0)
            acc = alpha * acc + jnp.dot(vt_h, p, preferred_element_type=f32)
            out.append((m_new, acc))
        return tuple(out)

    carry = tuple((jnp.full((1, tq), -jnp.inf, f32), jnp.zeros((V_DIM + ONES_ROWS, tq), f32)) for _ in range(n_h))
    carry = lax.fori_loop(0, qi * sub, functools.partial(step, masked=False), carry)
    for d in range(sub):
        carry = step(qi * sub + d, carry, True)
    o_t = jnp.concatenate([acc[:V_DIM] / acc[V_DIM:V_DIM + 1] for (_, acc) in carry], axis=0)
    o_ref[0] = o_t.T


ONES_ROWS = 16


ATTN_HEAD_GROUP = 4


def _attn_prompt(q, k, vt, tq, tk):
    n_b, t_len, _ = q.shape
    g = ATTN_HEAD_GROUP
    return pl.pallas_call(
        functools.partial(_attn_prompt_kernel, tq=tq, tk=tk),
        grid=(n_b, N_HEADS // g, t_len // tq),
        in_specs=[pl.BlockSpec((1, tq, g * HEAD_PAD), lambda b, h, i: (b, i, h)),
                  pl.BlockSpec((1, t_len, g * HEAD_PAD), lambda b, h, i: (b, 0, h)),
                  pl.BlockSpec((1, g * V_DIM, t_len), lambda b, h, i: (b, h, 0))],
        out_specs=pl.BlockSpec((1, tq, g * V_DIM), lambda b, h, i: (b, i, h)),
        out_shape=jax.ShapeDtypeStruct((n_b, t_len, N_HEADS * V_DIM), f32),
        compiler_params=_cparams("parallel", "parallel", "arbitrary"),
        name="attn_prompt",
    )(q, k, vt)


N_TH = 64


def _q_absorb_kernel(q_ref, gk_ref, wukt_ref, sel_ref, qabs_ref, qr_ref):
    gk = gk_ref[...]
    kv_rank = wukt_ref.shape[-1]
    for h in range(N_HEADS):
        qg = (q_ref[:, h * HEAD_PAD:(h + 1) * HEAD_PAD].astype(f32) * gk).astype(bf16)
        qabs_ref[:, h * kv_rank:(h + 1) * kv_rank] = jnp.dot(
            qg, wukt_ref[h], preferred_element_type=f32).astype(qabs_ref.dtype)
        qr_ref[:, h * QK_ROPE_DIM:(h + 1) * QK_ROPE_DIM] = jnp.dot(
            qg, sel_ref[...], preferred_element_type=f32).astype(qr_ref.dtype)


def _q_absorb(q2d, g_qk_k, w_ukv):
    n = q2d.shape[0]
    kv_rank = w_ukv.shape[0]
    wukt = jnp.transpose(w_ukv[:, :, :QK_NOPE_DIM], (1, 2, 0))
    wukt = jnp.pad(wukt, ((0, 0), (0, HEAD_PAD - QK_NOPE_DIM), (0, 0))).astype(bf16)
    sel = jnp.zeros((HEAD_PAD, QK_ROPE_DIM), f32).at[
        QK_NOPE_DIM + jnp.arange(QK_ROPE_DIM), jnp.arange(QK_ROPE_DIM)].set(1.0).astype(bf16)
    gk = _head_gain(g_qk_k)
    full = lambda a: pl.BlockSpec(a.shape, lambda i: (0,) * a.ndim)
    return pl.pallas_call(
        _q_absorb_kernel,
        grid=(1,),
        in_specs=[full(q2d), full(gk), full(wukt), full(sel)],
        out_specs=(pl.BlockSpec((n, N_HEADS * kv_rank), lambda i: (0, 0)),
                   pl.BlockSpec((n, N_HEADS * QK_ROPE_DIM), lambda i: (0, 0))),
        out_shape=(jax.ShapeDtypeStruct((n, N_HEADS * kv_rank), bf16),
                   jax.ShapeDtypeStruct((n, N_HEADS * QK_ROPE_DIM), bf16)),
        compiler_params=_cparams("arbitrary"),
        name="q_absorb",
    )(q2d, gk, wukt, sel)


SUB_PAGES = 16


def _key_sumsq_partials(c_bf, wuk):
    kn = jnp.dot(c_bf, wuk, preferred_element_type=f32)
    sq = kn * kn
    t = sq[:, :LANES]
    for i in range(1, sq.shape[1] // LANES):
        t = t + sq[:, i * LANES:(i + 1) * LANES]
    return t.astype(bf16)


def _scores_t(c_bf, t_bf, krt, qabs, qr):
    row_head = lax.broadcasted_iota(jnp.int32, (N_TH, LANES), 0) % N_HEADS
    lane_head = lax.broadcasted_iota(jnp.int32, (N_TH, LANES), 1) % N_HEADS
    head_sum = jnp.where(row_head == lane_head, 1.0, 0.0).astype(bf16)
    ssq = lax.dot_general(head_sum, t_bf, NT_DIMS, preferred_element_type=f32)
    ssq = ssq + jnp.sum(krt * krt, axis=0, keepdims=True)
    st = lax.dot_general(qabs, c_bf, NT_DIMS, preferred_element_type=f32)
    st = st + jnp.dot(qr, krt.astype(bf16), preferred_element_type=f32)
    return st * lax.rsqrt(ssq * (1.0 / QK_DIM) + NORM_EPS)


def _softmax_update_t(st, c_bf, state, mask=None):
    m, l, acc = state
    if mask is not None:
        st = jnp.where(mask, st, -jnp.inf)
    m_new = jnp.maximum(m, jnp.max(st, axis=1, keepdims=True))
    alpha = jnp.exp(m - m_new)
    p = jnp.exp(st - m_new)
    l = alpha * l + jnp.sum(p, axis=1, keepdims=True)
    acc = alpha * acc + jnp.dot(p.astype(bf16), c_bf, preferred_element_type=f32)
    return m_new, l, acc


def _attn_sample_kernel(pt_ref, qabs_ref, qr_ref, cnew_ref, krtnew_ref, wuk_ref, cache_c, cache_krt, o_ref,
                            cbuf, krbuf, sems, m_s, l_s, acc_s, *, pp):
    s, j = pl.program_id(0), pl.program_id(1)
    n_j = pl.num_programs(1)
    step = s * n_j + j
    slot = step % 2

    def page_copies(step_idx, buf_slot):
        copies = []
        for i in range(pp):
            page = pt_ref[step_idx * pp + i]
            copies.append(pltpu.make_async_copy(cache_c.at[page], cbuf.at[buf_slot, i], sems.at[0, buf_slot]))
            copies.append(pltpu.make_async_copy(cache_krt.at[page], krbuf.at[buf_slot, i], sems.at[1, buf_slot]))
        return copies

    @pl.when(step == 0)
    def _():
        for cp in page_copies(0, 0):
            cp.start()

    @pl.when(step + 1 < pl.num_programs(0) * n_j)
    def _():
        for cp in page_copies(step + 1, 1 - slot):
            cp.start()

    @pl.when(j == 0)
    def _():
        m_s[...] = jnp.full(m_s.shape, -jnp.inf, f32)
        l_s[...] = jnp.zeros(l_s.shape, f32)
        acc_s[...] = jnp.zeros(acc_s.shape, f32)

    for cp in page_copies(step, slot):
        cp.wait()

    qabs, qr, wuk = qabs_ref[0], qr_ref[0], wuk_ref[...]
    sub = min(SUB_PAGES, pp)
    blocks = list(range(0, pp, sub))

    def load(b):
        c_bf = jnp.concatenate([cbuf[slot, i].astype(bf16) for i in range(b, b + sub)], axis=0)
        return c_bf, _key_sumsq_partials(c_bf, wuk)

    state = (m_s[...], l_s[...], acc_s[...])
    nxt = load(blocks[0])
    for n, b in enumerate(blocks):
        c_bf, t_bf = nxt
        if n + 1 < len(blocks):
            nxt = load(blocks[n + 1])
        krt = jnp.concatenate([krbuf[slot, i] for i in range(b, b + sub)], axis=1)
        state = _softmax_update_t(_scores_t(c_bf, t_bf, krt, qabs, qr), c_bf, state)
    m_s[...], l_s[...], acc_s[...] = state

    @pl.when(j == n_j - 1)
    def _():
        c_bf = cnew_ref[0].astype(bf16)
        n_pad = c_bf.shape[0]
        key_t = lax.broadcasted_iota(jnp.int32, (N_TH, n_pad), 1)
        q_t = lax.broadcasted_iota(jnp.int32, (N_TH, n_pad), 0) // N_HEADS
        st = _scores_t(c_bf, _key_sumsq_partials(c_bf, wuk), krtnew_ref[0], qabs, qr)
        m, l, acc = _softmax_update_t(st, c_bf, state, mask=key_t <= q_t)
        o_ref[0] = acc / l


def _attn_sample(page_table, qabs, qr, c_new, kr_new, cache_c, cache_kr, wuk_perm, pp):
    n_seq, n_pages = page_table.shape
    kv_rank = cache_c.shape[-1]
    t_new = c_new.shape[1]
    seq3 = lambda s, j, pt: (s, 0, 0)
    cache_kr_t = jnp.swapaxes(cache_kr, 1, 2)
    c_new_pad = jnp.pad(c_new, ((0, 0), (0, PAGE_SIZE - t_new), (0, 0)))
    krt_new_pad = jnp.pad(jnp.swapaxes(kr_new, 1, 2), ((0, 0), (0, 0), (0, PAGE_SIZE - t_new)))
    grid_spec = pltpu.PrefetchScalarGridSpec(
        num_scalar_prefetch=1,
        grid=(n_seq, n_pages // pp),
        in_specs=[pl.BlockSpec((1, N_TH, kv_rank), seq3),
                  pl.BlockSpec((1, N_TH, QK_ROPE_DIM), seq3),
                  pl.BlockSpec((1, PAGE_SIZE, kv_rank), seq3),
                  pl.BlockSpec((1, QK_ROPE_DIM, PAGE_SIZE), seq3),
                  pl.BlockSpec(wuk_perm.shape, lambda s, j, pt: (0, 0)),
                  pl.BlockSpec(memory_space=pl.ANY),
                  pl.BlockSpec(memory_space=pl.ANY)],
        out_specs=pl.BlockSpec((1, N_TH, kv_rank), seq3),
        scratch_shapes=[pltpu.VMEM((2, pp, PAGE_SIZE, kv_rank), f32),
                        pltpu.VMEM((2, pp, QK_ROPE_DIM, PAGE_SIZE), f32),
                        pltpu.SemaphoreType.DMA((2, 2)),
                        pltpu.VMEM((N_TH, 1), f32), pltpu.VMEM((N_TH, 1), f32),
                        pltpu.VMEM((N_TH, kv_rank), f32)],
    )
    return pl.pallas_call(
        functools.partial(_attn_sample_kernel, pp=pp),
        grid_spec=grid_spec,
        out_shape=jax.ShapeDtypeStruct((n_seq, N_TH, kv_rank), f32),
        compiler_params=_cparams("arbitrary", "arbitrary"),
        name="attn_sample",
    )(page_table.reshape(-1), qabs, qr, c_new_pad, krt_new_pad, wuk_perm, cache_c, cache_kr_t)


def _v_up_kernel(lat_ref, wuv_ref, o_ref):
    rows = lat_ref.shape[0]
    full = jnp.dot(lat_ref[...].astype(bf16), wuv_ref[...], preferred_element_type=f32)
    head_of_row = lax.broadcasted_iota(jnp.int32, full.shape, 0) % N_HEADS
    head_of_lane = lax.broadcasted_iota(jnp.int32, full.shape, 1) // V_DIM
    own = jnp.where(head_of_row == head_of_lane, full, 0.0)
    o_ref[...] = jnp.sum(own.reshape(rows // N_HEADS, N_HEADS, full.shape[1]), axis=1)


def _v_up(lat2d, wuv, rows):
    n, kv_rank = lat2d.shape
    d_attn = wuv.shape[1]
    return pl.pallas_call(
        _v_up_kernel,
        grid=(n // rows,),
        in_specs=[pl.BlockSpec((rows, kv_rank), lambda i: (i, 0)), pl.BlockSpec(wuv.shape, lambda i: (0, 0))],
        out_specs=pl.BlockSpec((rows // N_HEADS, d_attn), lambda i: (i, 0)),
        out_shape=jax.ShapeDtypeStruct((n // N_HEADS, d_attn), f32),
        compiler_params=_cparams("parallel"),
        name="v_up",
    )(lat2d, wuv)


def _attn_sample_all(q_s2d, c_new, kr_new, page_table, cache_c, cache_kr, w_ukv, g_qk_k, pp):
    n_seq, t_new, kv_rank = c_new.shape
    qabs, qr = _q_absorb(q_s2d, g_qk_k, w_ukv)
    qabs = qabs.reshape(n_seq, t_new * N_HEADS, kv_rank)
    qr = qr.reshape(n_seq, t_new * N_HEADS, QK_ROPE_DIM)
    wuk_perm = jnp.transpose(w_ukv[:, :, :QK_NOPE_DIM], (0, 2, 1)).reshape(kv_rank, -1).astype(bf16)
    lat = _attn_sample(page_table, qabs, qr, c_new, kr_new, cache_c, cache_kr, wuk_perm, pp)
    wuv = w_ukv[:, :, QK_NOPE_DIM:].reshape(kv_rank, N_HEADS * V_DIM).astype(bf16)
    rows = min(512, n_seq * N_TH)
    return _v_up(lat.reshape(n_seq * N_TH, kv_rank), wuv, rows)


def _rms(x, g):
    return x * lax.rsqrt(jnp.mean(x * x, axis=-1, keepdims=True) + NORM_EPS) * g


def _split_bf16(x):
    hi = x.astype(bf16)
    return hi, (x - hi.astype(f32)).astype(bf16)


PACK_W = 256
u32 = jnp.uint32


def _pack_bf16_pairs(x):
    bits = lax.bitcast_convert_type(x.astype(bf16).astype(f32), u32)
    pieces = []
    for p in range(x.shape[1] // (2 * PACK_W)):
        lo = bits[:, 2 * p * PACK_W:(2 * p + 1) * PACK_W]
        hi = bits[:, (2 * p + 1) * PACK_W:(2 * p + 2) * PACK_W]
        pieces.append((lo >> 16) | (hi & jnp.uint32(0xFFFF0000)))
    return pieces


def _unpack_bf16_pairs(words):
    return (lax.bitcast_convert_type(words << 16, f32),
            lax.bitcast_convert_type(words & jnp.uint32(0xFFFF0000), f32))


def _merge_kernel(xp_ref, cvp_ref, atp_ref, xs_ref, cvs_ref, ats_ref, gc_ref, ga_ref, woc_ref, woa_ref, gf_ref,
                  wrh_ref, wrl_ref, br_ref, h_ref, xna_ref, xnb_ref, exp_ref, gate_ref, rank_ref, cnt_ref, cnt_s,
                  *, n_prompt_tiles):
    i = pl.program_id(0)
    tm = xp_ref.shape[0]

    @pl.when(i == 0)
    def _():
        cnt_s[...] = jnp.zeros(cnt_s.shape, f32)

    is_p = i < n_prompt_tiles
    x = jnp.where(is_p, xp_ref[...], xs_ref[...])
    yc = _rms(jnp.where(is_p, cvp_ref[...], cvs_ref[...]), gc_ref[...]).astype(bf16)
    ya = _rms(jnp.where(is_p, atp_ref[...], ats_ref[...]), ga_ref[...]).astype(bf16)
    h = x + jnp.dot(yc, woc_ref[...], preferred_element_type=f32) \
        + jnp.dot(ya, woa_ref[...], preferred_element_type=f32)
    h_ref[...] = h
    xn = _rms(h, gf_ref[...])
    xh, xl = _split_bf16(xn)
    xna_ref[...], xnb_ref[...] = _pack_bf16_pairs(xn)
    logits = (jnp.dot(xh, wrh_ref[...], preferred_element_type=f32)
              + jnp.dot(xl, wrh_ref[...], preferred_element_type=f32)
              + jnp.dot(xh, wrl_ref[...], preferred_element_type=f32)) + br_ref[...]
    lane = lax.broadcasted_iota(jnp.int32, logits.shape, 1)
    tops, idxs = [], []
    cur = logits
    for _k in range(TOP_K):
        mk = jnp.max(cur, axis=-1, keepdims=True)
        ik = jnp.min(jnp.where(cur == mk, lane, LANES), axis=-1, keepdims=True)
        tops.append(mk)
        idxs.append(ik)
        cur = jnp.where(lane == ik, -jnp.inf, cur)
    es = [jnp.exp(t - tops[0]) for t in tops]
    den = es[0] + es[1] + es[2] + es[3]
    onehots = [(lane == ik).astype(f32) for ik in idxs]
    oh = onehots[0] + onehots[1] + onehots[2] + onehots[3]
    r_i = lax.broadcasted_iota(jnp.int32, (tm, tm), 0)
    c_i = lax.broadcasted_iota(jnp.int32, (tm, tm), 1)
    before = jnp.dot((c_i < r_i).astype(bf16), oh.astype(bf16), preferred_element_type=f32) + cnt_s[...]
    exp_out = jnp.zeros(logits.shape, jnp.int32)
    gate_out = jnp.zeros(logits.shape, f32)
    rank_out = jnp.zeros(logits.shape, jnp.int32)
    for k in range(TOP_K):
        rk = jnp.sum(onehots[k] * before, axis=-1, keepdims=True)
        exp_out = jnp.where(lane == k, idxs[k], exp_out)
        gate_out = jnp.where(lane == k, es[k] / den, gate_out)
        rank_out = jnp.where(lane == k, rk.astype(jnp.int32), rank_out)
    exp_ref[...] = exp_out
    gate_ref[...] = gate_out
    rank_ref[...] = rank_out
    cnt_s[...] = cnt_s[...] + jnp.sum(oh, axis=0, keepdims=True)
    cnt_ref[...] = cnt_s[...]


def _merge(x_p, cv_p, at_p, x_s, cv_s, at_s, g_out_conv, g_out_attn, w_out, g_ffn, w_router, b_router, tm):
    (n_p, d_model), n_s = x_p.shape, x_s.shape[0]
    n = n_p + n_s
    d_conv, d_attn = cv_p.shape[1], at_p.shape[1]
    woc, woa = w_out[:d_conv].astype(bf16), w_out[d_conv:].astype(bf16)
    wr = jnp.pad(w_router, ((0, 0), (0, LANES - N_EXPERTS)))
    wrh, wrl = _split_bf16(wr)
    br = jnp.concatenate([b_router.astype(f32), jnp.full((LANES - N_EXPERTS,), -jnp.inf, f32)])[None, :]
    gc, ga, gf = g_out_conv[None, :], g_out_attn[None, :], g_ffn[None, :]
    tiles_p, tiles_s = n_p // tm, n_s // tm
    full = lambda a: pl.BlockSpec(a.shape, lambda i: (0, 0))
    rows = lambda w: pl.BlockSpec((tm, w), lambda i: (i, 0))
    rows_p = lambda w: pl.BlockSpec((tm, w), lambda i: (jnp.minimum(i, tiles_p - 1), 0))
    rows_s = lambda w: pl.BlockSpec((tm, w), lambda i: (jnp.maximum(i - tiles_p, 0), 0))
    assert d_model == 4 * PACK_W
    out_shape = (jax.ShapeDtypeStruct((n, d_model), f32),
                 jax.ShapeDtypeStruct((n, PACK_W), u32), jax.ShapeDtypeStruct((n, PACK_W), u32),
                 jax.ShapeDtypeStruct((n, LANES), jnp.int32), jax.ShapeDtypeStruct((n, LANES), f32),
                 jax.ShapeDtypeStruct((n, LANES), jnp.int32), jax.ShapeDtypeStruct((1, LANES), f32))
    return pl.pallas_call(
        functools.partial(_merge_kernel, n_prompt_tiles=tiles_p),
        grid=(tiles_p + tiles_s,),
        in_specs=[rows_p(d_model), rows_p(d_conv), rows_p(d_attn), rows_s(d_model), rows_s(d_conv), rows_s(d_attn),
                  full(gc), full(ga), full(woc), full(woa), full(gf), full(wrh), full(wrl), full(br)],
        out_specs=(rows(d_model), rows(PACK_W), rows(PACK_W), rows(LANES), rows(LANES), rows(LANES),
                   pl.BlockSpec((1, LANES), lambda i: (0, 0))),
        out_shape=out_shape,
        scratch_shapes=[pltpu.VMEM((1, LANES), f32)],
        compiler_params=_cparams("arbitrary"),
        name="merge",
    )(x_p, cv_p, at_p, x_s, cv_s, at_s, gc, ga, woc, woa, gf, wrh, wrl, br)


def _experts_kernel(be_ref, br_ref, xa_ref, xb_ref, wgu_ref, bgu_ref, wd_ref, bd_ref, ya_ref, yb_ref, wgu_s, wd_s):
    i = pl.program_id(0)
    d_expert = wd_ref.shape[1]
    blk = xa_ref.shape[0]

    @pl.when((i == 0) | (be_ref[i] != be_ref[jnp.maximum(i - 1, 0)]))
    def _():
        wgu_s[...] = wgu_ref[0].astype(bf16)
        wd_s[...] = wd_ref[0].astype(bf16)

    @pl.when(br_ref[i] > 0)
    def _():
        x = jnp.concatenate(_unpack_bf16_pairs(xa_ref[...]) + _unpack_bf16_pairs(xb_ref[...]), axis=1)
        live = lax.broadcasted_iota(jnp.int32, (blk, 1), 0) < br_ref[i]
        x = jnp.where(live, x, 0.0).astype(bf16)
        gu = jnp.dot(x, wgu_s[...], preferred_element_type=f32) + bgu_ref[0]
        gate = jnp.minimum(gu[:, :d_expert], SWIGLU_LIMIT)
        up = jnp.clip(gu[:, d_expert:], -SWIGLU_LIMIT, SWIGLU_LIMIT)
        hid = (up + 1.0) * gate * jax.nn.sigmoid(SWIGLU_ALPHA * gate)
        y = jnp.dot(hid.astype(bf16), wd_s[...], preferred_element_type=f32) + bd_ref[0]
        ya_ref[...], yb_ref[...] = _pack_bf16_pairs(y)

    @pl.when(br_ref[i] <= 0)
    def _():
        ya_ref[...] = jnp.zeros(ya_ref.shape, ya_ref.dtype)
        yb_ref[...] = jnp.zeros(yb_ref.shape, yb_ref.dtype)


def _experts(block_expert, block_rows, xs_a, xs_b, w_gate_up, b_gate_up, w_down, b_down, blk):
    n_slots = xs_a.shape[0]
    n_exp, d_model, d_gu = w_gate_up.shape
    d_expert = w_down.shape[1]
    slots = pl.BlockSpec((blk, PACK_W), lambda i, be, br: (i, 0))
    grid_spec = pltpu.PrefetchScalarGridSpec(
        num_scalar_prefetch=2,
        grid=(n_slots // blk,),
        in_specs=[slots, slots,
                  pl.BlockSpec((1, d_model, d_gu), lambda i, be, br: (be[i], 0, 0)),
                  pl.BlockSpec((1, 1, d_gu), lambda i, be, br: (be[i], 0, 0)),
                  pl.BlockSpec((1, d_expert, d_model), lambda i, be, br: (be[i], 0, 0)),
                  pl.BlockSpec((1, 1, d_model), lambda i, be, br: (be[i], 0, 0))],
        out_specs=(slots, slots),
        scratch_shapes=[pltpu.VMEM((d_model, d_gu), bf16), pltpu.VMEM((d_expert, d_model), bf16)],
    )
    packed = jax.ShapeDtypeStruct((n_slots, PACK_W), u32)
    return pl.pallas_call(
        _experts_kernel,
        grid_spec=grid_spec,
        out_shape=(packed, packed),
        compiler_params=_cparams("arbitrary"),
        name="experts",
    )(block_expert, block_rows, xs_a, xs_b, w_gate_up, b_gate_up[:, None, :], w_down, b_down[:, None, :])


def _combine_kernel(h_ref, ya_ref, yb_ref, gate_ref, yp_ref, ys_ref, *, n_prompt_tiles):
    i = pl.program_id(0)
    g = gate_ref[...]
    acc = None
    for k in range(TOP_K):
        gk = g[:, k:k + 1]
        pieces = [p * gk for p in _unpack_bf16_pairs(ya_ref[k]) + _unpack_bf16_pairs(yb_ref[k])]
        acc = pieces if acc is None else [a + p for a, p in zip(acc, pieces)]
    y = h_ref[...] + jnp.concatenate(acc, axis=1)

    @pl.when(i < n_prompt_tiles)
    def _():
        yp_ref[...] = y

    @pl.when(i >= n_prompt_tiles)
    def _():
        ys_ref[...] = y


def _combine(h, ya, yb, gates, n_p, tm):
    n, d_model = h.shape
    tiles_p = n_p // tm
    packed = pl.BlockSpec((TOP_K, tm, PACK_W), lambda i: (0, i, 0))
    return pl.pallas_call(
        functools.partial(_combine_kernel, n_prompt_tiles=tiles_p),
        grid=(n // tm,),
        in_specs=[pl.BlockSpec((tm, d_model), lambda i: (i, 0)), packed, packed,
                  pl.BlockSpec((tm, LANES), lambda i: (i, 0))],
        out_specs=(pl.BlockSpec((tm, d_model), lambda i: (jnp.minimum(i, tiles_p - 1), 0)),
                   pl.BlockSpec((tm, d_model), lambda i: (jnp.maximum(i - tiles_p, 0), 0))),
        out_shape=(jax.ShapeDtypeStruct((n_p, d_model), f32), jax.ShapeDtypeStruct((n - n_p, d_model), f32)),
        compiler_params=_cparams("arbitrary"),
        name="combine",
    )(h, ya, yb, gates)


SC_WINDOW = 128


def _sc_mesh():
    return plsc.VectorSubcoreMesh(core_axis_name="core", subcore_axis_name="subcore")


def _sc_scatter_rows(x, idx, n_out):
    n, w = x.shape
    n_k = idx.shape[0]

    @pl.kernel(out_type=jax.ShapeDtypeStruct((n_out, w), x.dtype), mesh=_sc_mesh(), scratch_types=[])
    def scatter(x_hbm, i_hbm, o_hbm):
        def body(x_vmem, *i_vmem):
            for iv in i_vmem:
                pltpu.sync_copy(x_vmem, o_hbm.at[iv.at[0]])

        pltpu.emit_pipeline(
            body,
            grid=(n // SC_WINDOW,),
            in_specs=[pl.BlockSpec((SC_WINDOW, w), lambda i: (i, 0))]
            + [pl.BlockSpec((1, SC_WINDOW), functools.partial(lambda k, i: (k, i), k)) for k in range(n_k)],
            out_specs=[],
            core_axis_name=("core", "subcore"),
            dimension_semantics=(pltpu.PARALLEL,),
        )(x_hbm, *([i_hbm] * n_k))

    return scatter(x, idx)


def _sc_gather_rows(x, idx):
    n_idx, w = idx.shape[0], x.shape[1]

    @pl.kernel(out_type=jax.ShapeDtypeStruct((n_idx, w), x.dtype), mesh=_sc_mesh(), scratch_types=[])
    def gather(x_hbm, i_hbm, o_hbm):
        def body(i_vmem, o_vmem):
            pltpu.sync_copy(x_hbm.at[i_vmem.at[0]], o_vmem)

        pltpu.emit_pipeline(
            body,
            grid=(n_idx // SC_WINDOW,),
            in_specs=[pl.BlockSpec((1, SC_WINDOW), lambda i: (0, i))],
            out_specs=[pl.BlockSpec((SC_WINDOW, w), lambda i: (i, 0))],
            core_axis_name=("core", "subcore"),
            dimension_semantics=(pltpu.PARALLEL,),
        )(i_hbm, o_hbm)

    return gather(x, idx.reshape(1, n_idx))


EXPERT_BLK = 512


def _moe(h, xn_a, xn_b, expert, gates, rank, counts, n_p, w_gate_up, b_gate_up, w_down, b_down):
    n = h.shape[0]
    blk = EXPERT_BLK
    padded = (counts + blk - 1) // blk * blk
    padded_end = jnp.cumsum(padded)
    start_padded = padded_end - padded
    one_hot = (expert[:, :, None] == jnp.arange(N_EXPERTS)[None, None, :]).astype(jnp.int32)
    slot_t = (jnp.sum(one_hot * start_padded[None, None, :], axis=-1) + rank).T
    n_blocks = -(-(n * TOP_K + N_EXPERTS * (blk - 1)) // blk)
    n_slots = n_blocks * blk
    block_start = jnp.arange(n_blocks, dtype=jnp.int32) * blk
    block_expert = jnp.minimum(jnp.sum((block_start[:, None] >= padded_end[None, :]).astype(jnp.int32), axis=1),
                               N_EXPERTS - 1)
    block_rows = jnp.clip(counts[block_expert] - (block_start - start_padded[block_expert]), 0, blk)
    xs_a = _sc_scatter_rows(xn_a, slot_t, n_slots)
    xs_b = _sc_scatter_rows(xn_b, slot_t, n_slots)
    ys_a, ys_b = _experts(block_expert, block_rows, xs_a, xs_b, w_gate_up, b_gate_up, w_down, b_down, blk)
    flat = slot_t.reshape(-1)
    ya = _sc_gather_rows(ys_a, flat).reshape(TOP_K, n, PACK_W)
    yb = _sc_gather_rows(ys_b, flat).reshape(TOP_K, n, PACK_W)
    return _combine(h, ya, yb, gates, n_p, 512)


def _layer(x_p, x_s, cache_c, cache_kr, state, page_table, norm_mix, w_in, norm_q_lat, w_uq, norm_kv_lat, w_ukv,
           norm_qk_q, norm_qk_k, conv_w, conv_b, conv_ln_g, conv_ln_b, norm_out_conv, norm_out_attn, w_out,
           norm_ffn, w_router, b_router, w_gate_up, b_gate_up, w_down, b_down):
    n_b, t_p, d_model = x_p.shape
    n_seq, t_s, _ = x_s.shape
    d_conv = conv_w.shape[1]
    q_rank, kv_rank = norm_q_lat.shape[0], norm_kv_lat.shape[0]
    n_past = page_table.shape[1] * PAGE_SIZE
    win, wuq, wk, wvt = _mix_in_weights(w_in, w_uq, w_ukv, d_conv, q_rank, kv_rank)
    mix = functools.partial(_mix_in, g_mix=norm_mix, win=win, g_q_lat=norm_q_lat, wuq=wuq, g_kv_lat=norm_kv_lat,
                            g_qk_q=norm_qk_q, wk=wk, wvt=wvt, g_qk_k=norm_qk_k,
                            d_conv=d_conv, q_rank=q_rank, kv_rank=kv_rank)
    x_p2, x_s2 = x_p.reshape(n_b * t_p, d_model), x_s.reshape(n_seq * t_s, d_model)
    tab_p = _rope_head_tables(jnp.arange(t_p))
    tab_s = _rope_head_tables(n_past + jnp.arange(n_seq * t_s) % t_s)
    glu_p, q_p, c_p, kr_p, k_p, vt_p = mix(x_p2, tab_p, 512, seq_len=t_p, q_scale=ATTN_SCALE * LOG2_E)
    glu_s, q_s, c_s, kr_s, _, _ = mix(x_s2, tab_s, 512, seq_len=n_seq * t_s, q_scale=ATTN_SCALE)

    glu_p3, glu_s3 = glu_p.reshape(n_b, t_p, d_conv), glu_s.reshape(n_seq, t_s, d_conv)
    cv_p = _conv_prompt(glu_p3, conv_w, conv_b, conv_ln_g, conv_ln_b)
    cv_s = _conv_sample(glu_s3, state, conv_w, conv_b, conv_ln_g, conv_ln_b)

    hp = N_HEADS * HEAD_PAD
    at_p = _attn_prompt(q_p.reshape(n_b, t_p, hp), k_p.reshape(n_b, t_p, hp), vt_p, 512, 512)
    c_s3, kr_s3 = c_s.reshape(n_seq, t_s, kv_rank), kr_s.reshape(n_seq, t_s, QK_ROPE_DIM)
    at_s = _attn_sample_all(q_s, c_s3, kr_s3, page_table, cache_c, cache_kr, w_ukv, norm_qk_k, 32)

    h, xn_a, xn_b, expert, gates, rank, cnt = _merge(
        x_p2, cv_p.reshape(n_b * t_p, d_conv), at_p.reshape(n_b * t_p, -1),
        x_s2, cv_s.reshape(n_seq * t_s, d_conv), at_s,
        norm_out_conv, norm_out_attn, w_out, norm_ffn, w_router, b_router, 512)
    counts = cnt[0, :N_EXPERTS].astype(jnp.int32)
    y_p, y_s = _moe(h, xn_a, xn_b, expert[:, :TOP_K], gates, rank[:, :TOP_K], counts, n_b * t_p,
                    w_gate_up, b_gate_up, w_down, b_down)
    n_hist = CONV_WIDTH - 1
    conv_state_p = jnp.concatenate([jnp.zeros((n_b, max(n_hist - t_p, 0), d_conv), f32),
                                    glu_p3[:, max(t_p - n_hist, 0):]], axis=1)
    conv_state_s = jnp.concatenate([state, glu_s3], axis=1)[:, -n_hist:]
    return (y_p.reshape(x_p.shape), y_s.reshape(x_s.shape), c_p.reshape(n_b, t_p, kv_rank),
            kr_p.reshape(n_b, t_p, QK_ROPE_DIM), conv_state_p, c_s3, kr_s3, conv_state_s)


def kernel(x_prompt, x_sample, cache_kv_latent, cache_k_rope, state_conv, page_table, norm_mix, w_in, norm_q_lat,
           w_uq, norm_kv_lat, w_ukv, norm_qk_q, norm_qk_k, conv_w, conv_b, conv_ln_g, conv_ln_b, norm_out_conv,
           norm_out_attn, w_out, norm_ffn, w_router, b_router, w_gate_up, b_gate_up, w_down, b_down):
    h_p, h_s = x_prompt, x_sample
    per_layer = []
    for l in range(w_in.shape[0]):
        outs = _layer(h_p, h_s, cache_kv_latent[l], cache_k_rope[l], state_conv[l], page_table, norm_mix[l],
                      w_in[l], norm_q_lat[l], w_uq[l], norm_kv_lat[l], w_ukv[l], norm_qk_q[l], norm_qk_k[l],
                      conv_w[l], conv_b[l], conv_ln_g[l], conv_ln_b[l], norm_out_conv[l], norm_out_attn[l],
                      w_out[l], norm_ffn[l], w_router[l], b_router[l], w_gate_up[l], b_gate_up[l], w_down[l],
                      b_down[l])
        h_p, h_s = outs[0], outs[1]
        per_layer.append(outs[2:])
    stacked = [jnp.stack([pl_[i] for pl_ in per_layer]) for i in range(6)]
    return (h_p, h_s, *stacked)
```

```python
import functools

import jax
import jax.numpy as jnp
import numpy as np
from jax import lax
from jax.experimental import pallas as pl
from jax.experimental.pallas import tpu as pltpu
from jax.experimental.pallas import tpu_sc as plsc

N_HEADS = 8
QK_NOPE_DIM = 64
QK_ROPE_DIM = 32
ROPE_HALF = QK_ROPE_DIM // 2
QK_DIM = QK_NOPE_DIM + QK_ROPE_DIM
V_DIM = 64
CONV_WIDTH = 31
N_EXPERTS = 32
TOP_K = 4
SWIGLU_LIMIT = 7.0
SWIGLU_ALPHA = 1.702
ROPE_THETA = 10000.0
NORM_EPS = 1e-6
ATTN_SCALE = QK_DIM ** -0.5
LOG2_E = 1.4426950408889634
PAGE_SIZE = 128

LANES = 128
HEAD_PAD = LANES
VMEM_LIMIT = 56 * 1024 * 1024

f32 = jnp.float32
bf16 = jnp.bfloat16


def _cparams(*sem):
    return pltpu.CompilerParams(dimension_semantics=sem, vmem_limit_bytes=VMEM_LIMIT)


def _rope_head_tables(pos):
    inv_freq = ROPE_THETA ** (-jnp.arange(ROPE_HALF, dtype=f32) / ROPE_HALF)
    ang = pos.astype(f32)[:, None] * inv_freq[None, :]
    cos, sin = jnp.cos(ang), jnp.sin(ang)
    n = pos.shape[0]
    z = lambda w: jnp.zeros((n, w), f32)
    c = jnp.concatenate([jnp.ones((n, QK_NOPE_DIM), f32), cos, cos, z(32)], axis=1)
    s1 = jnp.concatenate([z(QK_NOPE_DIM + ROPE_HALF), sin, z(32)], axis=1)
    s2 = jnp.concatenate([z(QK_NOPE_DIM), -sin, z(ROPE_HALF + 32)], axis=1)
    return jnp.concatenate([c, s1, s2], axis=1)


def _rope_head(x, tab):
    c, s1, s2 = tab[:, :LANES], tab[:, LANES:2 * LANES], tab[:, 2 * LANES:]
    return x * c + pltpu.roll(x, ROPE_HALF, 1) * s1 + pltpu.roll(x, LANES - ROPE_HALF, 1) * s2


def _mix_in_kernel(x_ref, tab_ref, gmix_ref, win_ref, gq_ref, wuq_ref, gkv_ref, gqq_ref,
                   wk_ref, wvt_ref, gqk_ref,
                   glu_ref, q_ref, c_ref, kr_ref, k_ref, vt_ref, *, d_conv, q_rank, kv_rank, q_scale):
    tm = x_ref.shape[0]
    rows = tm // MIX_SUB_TILES
    gqq = gqq_ref[...] * q_scale
    gqk = gqk_ref[...]

    def project(r):
        x = x_ref[r, :]
        xn = x * lax.rsqrt(jnp.mean(x * x, axis=-1, keepdims=True) + NORM_EPS) * gmix_ref[...]
        return jnp.dot(xn.astype(bf16), win_ref[...], preferred_element_type=f32)

    def latents(r, proj):
        o = 0
        a = proj[:, o:o + d_conv]; o += d_conv
        gt = proj[:, o:o + d_conv]; o += d_conv
        q_lat = proj[:, o:o + q_rank]; o += q_rank
        kv_lat = proj[:, o:o + kv_rank]; o += kv_rank
        kr_raw = proj[:, o:o + LANES]
        glu_ref[r, :] = a * jax.nn.sigmoid(gt)
        tab = tab_ref[r, :]
        qn = q_lat * lax.rsqrt(jnp.mean(q_lat * q_lat, axis=-1, keepdims=True) + NORM_EPS) * gq_ref[...]
        q = jnp.dot(qn.astype(bf16), wuq_ref[...], preferred_element_type=f32)
        c_kv = kv_lat * lax.rsqrt(jnp.mean(kv_lat * kv_lat, axis=-1, keepdims=True) + NORM_EPS) * gkv_ref[...]
        c_ref[r, :] = c_kv
        kr = _rope_head(kr_raw, tab)
        kr_ref[r, :] = kr[:, QK_NOPE_DIM:QK_NOPE_DIM + QK_ROPE_DIM]
        c_bf = c_kv.astype(bf16)
        kn = jnp.dot(c_bf, wk_ref[...], preferred_element_type=f32)
        vt_ref[0, :, r] = lax.dot_general(wvt_ref[...], c_bf, NT_DIMS,
                                          preferred_element_type=f32).astype(vt_ref.dtype)
        return q, kn, kr, tab

    def heads(r, q, kn, kr, tab):
        for h in range(N_HEADS):
            sl = slice(h * HEAD_PAD, (h + 1) * HEAD_PAD)
            qh = _rope_head(q[:, sl], tab)
            qh = qh * lax.rsqrt(jnp.sum(qh * qh, axis=-1, keepdims=True) * (1.0 / QK_DIM) + NORM_EPS) * gqq
            q_ref[r, sl] = qh.astype(q_ref.dtype)
            kh = kn[:, sl] + kr
            kh = kh * lax.rsqrt(jnp.sum(kh * kh, axis=-1, keepdims=True) * (1.0 / QK_DIM) + NORM_EPS) * gqk
            k_ref[r, sl] = kh.astype(k_ref.dtype)

    slices = [pl.ds(i * rows, rows) for i in range(MIX_SUB_TILES)]
    proj = project(slices[0])
    for i, r in enumerate(slices):
        nxt = project(slices[i + 1]) if i + 1 < len(slices) else None
        heads(r, *latents(r, proj))
        proj = nxt


MIX_TILE = 1024
MIX_SUB_TILES = 4


def _head_gain(g_pairs):
    g_rot = g_pairs[QK_NOPE_DIM:]
    return jnp.concatenate([g_pairs[:QK_NOPE_DIM], g_rot, g_rot, jnp.zeros((32,), f32)])[None, :]


def _mix_in_weights(w_in, w_uq, w_ukv, d_conv, q_rank, kv_rank):
    d_model = w_in.shape[0]
    base = 2 * d_conv + q_rank + kv_rank
    w_kr = w_in[:, base:base + QK_ROPE_DIM]
    w_kr_pad = jnp.concatenate([jnp.zeros((d_model, QK_NOPE_DIM), f32), w_kr,
                                jnp.zeros((d_model, 32), f32)], axis=1)
    win = jnp.concatenate([w_in[:, :base], w_kr_pad], axis=1).astype(bf16)
    wuq = jnp.pad(w_uq, ((0, 0), (0, 0), (0, HEAD_PAD - QK_DIM))).reshape(q_rank, N_HEADS * HEAD_PAD).astype(bf16)
    wk = jnp.pad(w_ukv[:, :, :QK_NOPE_DIM], ((0, 0), (0, 0), (0, HEAD_PAD - QK_NOPE_DIM)))
    wk = wk.reshape(kv_rank, N_HEADS * HEAD_PAD).astype(bf16)
    wvt = w_ukv[:, :, QK_NOPE_DIM:].reshape(kv_rank, N_HEADS * V_DIM).T.astype(bf16)
    return win, wuq, wk, wvt


def _mix_in(x2d, tab, tm, g_mix, win, g_q_lat, wuq, g_kv_lat, g_qk_q, wk, wvt, g_qk_k, d_conv, q_rank, kv_rank,
            seq_len, q_scale):
    n, d_model = x2d.shape
    n_tab_blocks = tab.shape[0] // tm
    seq_tiles = seq_len // tm
    row = lambda i: (i, 0)
    const = lambda i: (0, 0)
    full = lambda a: pl.BlockSpec(a.shape, const)
    gm, gq, gkv = g_mix[None, :], g_q_lat[None, :], g_kv_lat[None, :]
    gqq, gqk = _head_gain(g_qk_q), _head_gain(g_qk_k)
    hp = N_HEADS * HEAD_PAD
    out_shape = (
        jax.ShapeDtypeStruct((n, d_conv), f32),
        jax.ShapeDtypeStruct((n, hp), bf16),
        jax.ShapeDtypeStruct((n, kv_rank), f32),
        jax.ShapeDtypeStruct((n, QK_ROPE_DIM), f32),
        jax.ShapeDtypeStruct((n, hp), bf16),
        jax.ShapeDtypeStruct((n // seq_len, N_HEADS * V_DIM, seq_len), bf16),
    )
    out_specs = [pl.BlockSpec((tm, s.shape[1]), row) for s in out_shape[:-1]]
    out_specs.append(pl.BlockSpec((1, N_HEADS * V_DIM, tm), lambda i: (i // seq_tiles, 0, i % seq_tiles)))
    return pl.pallas_call(
        functools.partial(_mix_in_kernel, d_conv=d_conv, q_rank=q_rank, kv_rank=kv_rank, q_scale=q_scale),
        grid=(n // tm,),
        in_specs=[pl.BlockSpec((tm, d_model), row),
                  pl.BlockSpec((tm, tab.shape[1]), lambda i: (i % n_tab_blocks, 0)),
                  full(gm), full(win), full(gq), full(wuq), full(gkv), full(gqq), full(wk), full(wvt), full(gqk)],
        out_specs=tuple(out_specs),
        out_shape=out_shape,
        compiler_params=_cparams("parallel"),
        name="mix_in",
    )(x2d, tab, gm, win, gq, wuq, gkv, gqq, wk, wvt, gqk)


CONV_PAD = 32


SUBLANES = 8


def _conv_taps(window, w_ref, b_ref, ls, tc):
    lead = CONV_PAD - (CONV_WIDTH - 1)
    acc = jnp.zeros((tc, LANES), f32) + b_ref[:, ls]
    for r in range(SUBLANES):
        taps = [(a, a * SUBLANES + r - lead) for a in range(CONV_PAD // SUBLANES + 1)
                if 0 <= a * SUBLANES + r - lead < CONV_WIDTH]
        shifted = window if r == 0 else pltpu.roll(window, window.shape[0] - r, 0)
        for a, j in taps:
            acc = acc + shifted[a * SUBLANES:a * SUBLANES + tc] * w_ref[j:j + 1, ls]
    return acc


def _ln_swish(y, g_ref, beta_ref):
    mu = jnp.mean(y, axis=-1, keepdims=True)
    yc = y - mu
    var = jnp.mean(yc * yc, axis=-1, keepdims=True)
    z = yc * lax.rsqrt(var + NORM_EPS) * g_ref[...] + beta_ref[...]
    return z * jax.nn.sigmoid(z)


def _conv_prompt_kernel(x_ref, w_ref, b_ref, g_ref, beta_ref, o_ref, *, tc):
    t_len, n_ch = x_ref.shape[1], x_ref.shape[2]

    def chunk(idx, carry):
        t0 = pl.multiple_of(idx * tc, tc)
        h0 = pl.multiple_of(jnp.maximum(t0 - CONV_PAD, 0), SUBLANES)
        has_hist = jnp.where(idx > 0, 1.0, 0.0)
        for lt in range(n_ch // LANES):
            ls = slice(lt * LANES, (lt + 1) * LANES)
            window = jnp.concatenate([x_ref[0, pl.ds(h0, CONV_PAD), ls] * has_hist,
                                      x_ref[0, pl.ds(t0, tc), ls]], axis=0)
            o_ref[0, pl.ds(t0, tc), ls] = _conv_taps(window, w_ref, b_ref, ls, tc)
        o_ref[0, pl.ds(t0, tc), :] = _ln_swish(o_ref[0, pl.ds(t0, tc), :], g_ref, beta_ref)
        return carry

    lax.fori_loop(0, t_len // tc, chunk, 0)


def _conv_sample_kernel(x_ref, st_ref, w_ref, b_ref, g_ref, beta_ref, o_ref):
    bb, tc, n_ch = x_ref.shape

    def one(b, carry):
        for lt in range(n_ch // LANES):
            ls = slice(lt * LANES, (lt + 1) * LANES)
            window = jnp.concatenate([st_ref[b, :, ls], x_ref[b, :, ls]], axis=0)
            o_ref[b, :, ls] = _conv_taps(window, w_ref, b_ref, ls, tc)
        o_ref[b] = _ln_swish(o_ref[b], g_ref, beta_ref)
        return carry

    lax.fori_loop(0, bb, one, 0)


def _conv_specs(conv_w, n_ch):
    const = lambda i: (0, 0)
    vec = pl.BlockSpec((1, n_ch), const)
    return [pl.BlockSpec(conv_w.shape, const), vec, vec, vec]


def _conv_prompt(glu, conv_w, conv_b, ln_g, ln_b, tc=128):
    n_b, t_len, n_ch = glu.shape
    blk = pl.BlockSpec((1, t_len, n_ch), lambda i: (i, 0, 0))
    return pl.pallas_call(
        functools.partial(_conv_prompt_kernel, tc=tc),
        grid=(n_b,),
        in_specs=[blk] + _conv_specs(conv_w, n_ch),
        out_specs=blk,
        out_shape=jax.ShapeDtypeStruct(glu.shape, f32),
        compiler_params=_cparams("parallel"),
        name="conv_prompt",
    )(glu, conv_w, conv_b[None, :], ln_g[None, :], ln_b[None, :])


def _conv_sample(glu, state, conv_w, conv_b, ln_g, ln_b, bb=32):
    n_b, t_len, n_ch = glu.shape
    hist = jnp.pad(state, ((0, 0), (CONV_PAD - state.shape[1], 0), (0, 0)))
    return pl.pallas_call(
        _conv_sample_kernel,
        grid=(n_b // bb,),
        in_specs=[pl.BlockSpec((bb, t_len, n_ch), lambda i: (i, 0, 0)),
                  pl.BlockSpec((bb, CONV_PAD, n_ch), lambda i: (i, 0, 0))] + _conv_specs(conv_w, n_ch),
        out_specs=pl.BlockSpec((bb, t_len, n_ch), lambda i: (i, 0, 0)),
        out_shape=jax.ShapeDtypeStruct(glu.shape, f32),
        compiler_params=_cparams("parallel"),
        name="conv_sample",
    )(glu, hist, conv_w, conv_b[None, :], ln_g[None, :], ln_b[None, :])


NT_DIMS = (((1,), (1,)), ((), ()))


def _attn_prompt_kernel(q_ref, k_ref, vt_ref, o_ref, *, tq, tk):
    qi = pl.program_id(2)
    sub = tq // tk
    n_h = q_ref.shape[2] // HEAD_PAD
    qs = [q_ref[0, :, h * HEAD_PAD:(h + 1) * HEAD_PAD] for h in range(n_h)]

    def step(kb, carry, masked):
        k0 = pl.multiple_of(kb * tk, tk)
        if masked:
            key = k0 + lax.broadcasted_iota(jnp.int32, (tk, tq), 0)
            qry = qi * tq + lax.broadcasted_iota(jnp.int32, (tk, tq), 1)
            visible = key <= qry
        ones = jnp.ones((ONES_ROWS, tk), bf16)
        sts = []
        for h in range(n_h):
            k_h = k_ref[0, pl.ds(k0, tk), h * HEAD_PAD:(h + 1) * HEAD_PAD]
            sts.append(lax.dot_general(k_h, qs[h], NT_DIMS, preferred_element_type=f32))
        out = []
        for h, (m, acc) in enumerate(carry):
            st = jnp.where(visible, sts[h], -jnp.inf) if masked else sts[h]
            m_new = jnp.maximum(m, jnp.max(st, axis=0, keepdims=True))
            alpha = jnp.exp2(m - m_new)
            p = jnp.exp2(st - m_new).astype(bf16)
            vt_h = jnp.concatenate([vt_ref[0, h * V_DIM:(h + 1) * V_DIM, pl.ds(k0, tk)], ones], axis=0)
            acc = alpha * acc + jnp.dot(vt_h, p, preferred_element_type=f32)
            out.append((m_new, acc))
        return tuple(out)

    carry = tuple((jnp.full((1, tq), -jnp.inf, f32), jnp.zeros((V_DIM + ONES_ROWS, tq), f32)) for _ in range(n_h))
    carry = lax.fori_loop(0, qi * sub, functools.partial(step, masked=False), carry)
    for d in range(sub):
        carry = step(qi * sub + d, carry, True)
    o_t = jnp.concatenate([acc[:V_DIM] / acc[V_DIM:V_DIM + 1] for (_, acc) in carry], axis=0)
    o_ref[0] = o_t.T


ONES_ROWS = 16


ATTN_HEAD_GROUP = 4


def _attn_prompt(q, k, vt, tq, tk):
    n_b, t_len, _ = q.shape
    g = ATTN_HEAD_GROUP
    return pl.pallas_call(
        functools.partial(_attn_prompt_kernel, tq=tq, tk=tk),
        grid=(n_b, N_HEADS // g, t_len // tq),
        in_specs=[pl.BlockSpec((1, tq, g * HEAD_PAD), lambda b, h, i: (b, i, h)),
                  pl.BlockSpec((1, t_len, g * HEAD_PAD), lambda b, h, i: (b, 0, h)),
                  pl.BlockSpec((1, g * V_DIM, t_len), lambda b, h, i: (b, h, 0))],
        out_specs=pl.BlockSpec((1, tq, g * V_DIM), lambda b, h, i: (b, i, h)),
        out_shape=jax.ShapeDtypeStruct((n_b, t_len, N_HEADS * V_DIM), f32),
        compiler_params=_cparams("parallel", "parallel", "arbitrary"),
        name="attn_prompt",
    )(q, k, vt)


N_TH = 64


def _q_absorb_kernel(q_ref, gk_ref, wukt_ref, sel_ref, qabs_ref, qr_ref):
    gk = gk_ref[...]
    kv_rank = wukt_ref.shape[-1]
    for h in range(N_HEADS):
        qg = (q_ref[:, h * HEAD_PAD:(h + 1) * HEAD_PAD].astype(f32) * gk).astype(bf16)
        qabs_ref[:, h * kv_rank:(h + 1) * kv_rank] = jnp.dot(
            qg, wukt_ref[h], preferred_element_type=f32).astype(qabs_ref.dtype)
        qr_ref[:, h * QK_ROPE_DIM:(h + 1) * QK_ROPE_DIM] = jnp.dot(
            qg, sel_ref[...], preferred_element_type=f32).astype(qr_ref.dtype)


def _q_absorb(q2d, g_qk_k, w_ukv):
    n = q2d.shape[0]
    kv_rank = w_ukv.shape[0]
    wukt = jnp.transpose(w_ukv[:, :, :QK_NOPE_DIM], (1, 2, 0))
    wukt = jnp.pad(wukt, ((0, 0), (0, HEAD_PAD - QK_NOPE_DIM), (0, 0))).astype(bf16)
    sel = jnp.zeros((HEAD_PAD, QK_ROPE_DIM), f32).at[
        QK_NOPE_DIM + jnp.arange(QK_ROPE_DIM), jnp.arange(QK_ROPE_DIM)].set(1.0).astype(bf16)
    gk = _head_gain(g_qk_k)
    full = lambda a: pl.BlockSpec(a.shape, lambda i: (0,) * a.ndim)
    return pl.pallas_call(
        _q_absorb_kernel,
        grid=(1,),
        in_specs=[full(q2d), full(gk), full(wukt), full(sel)],
        out_specs=(pl.BlockSpec((n, N_HEADS * kv_rank), lambda i: (0, 0)),
                   pl.BlockSpec((n, N_HEADS * QK_ROPE_DIM), lambda i: (0, 0))),
        out_shape=(jax.ShapeDtypeStruct((n, N_HEADS * kv_rank), bf16),
                   jax.ShapeDtypeStruct((n, N_HEADS * QK_ROPE_DIM), bf16)),
        compiler_params=_cparams("arbitrary"),
        name="q_absorb",
    )(q2d, gk, wukt, sel)


SUB_PAGES = 16


def _key_sumsq_partials(c_bf, wuk):
    kn = jnp.dot(c_bf, wuk, preferred_element_type=f32)
    sq = kn * kn
    t = sq[:, :LANES]
    for i in range(1, sq.shape[1] // LANES):
        t = t + sq[:, i * LANES:(i + 1) * LANES]
    return t.astype(bf16)


def _scores_t(c_bf, t_bf, krt, qabs, qr):
    row_head = lax.broadcasted_iota(jnp.int32, (N_TH, LANES), 0) % N_HEADS
    lane_head = lax.broadcasted_iota(jnp.int32, (N_TH, LANES), 1) % N_HEADS
    head_sum = jnp.where(row_head == lane_head, 1.0, 0.0).astype(bf16)
    ssq = lax.dot_general(head_sum, t_bf, NT_DIMS, preferred_element_type=f32)
    ssq = ssq + jnp.sum(krt * krt, axis=0, keepdims=True)
    st = lax.dot_general(qabs, c_bf, NT_DIMS, preferred_element_type=f32)
    st = st + jnp.dot(qr, krt.astype(bf16), preferred_element_type=f32)
    return st * lax.rsqrt(ssq * (1.0 / QK_DIM) + NORM_EPS)


def _softmax_update_t(st, c_bf, state, mask=None):
    m, l, acc = state
    if mask is not None:
        st = jnp.where(mask, st, -jnp.inf)
    m_new = jnp.maximum(m, jnp.max(st, axis=1, keepdims=True))
    alpha = jnp.exp(m - m_new)
    p = jnp.exp(st - m_new)
    l = alpha * l + jnp.sum(p, axis=1, keepdims=True)
    acc = alpha * acc + jnp.dot(p.astype(bf16), c_bf, preferred_element_type=f32)
    return m_new, l, acc


def _attn_sample_kernel(pt_ref, qabs_ref, qr_ref, cnew_ref, krtnew_ref, wuk_ref, cache_c, cache_krt, o_ref,
                            cbuf, krbuf, sems, m_s, l_s, acc_s, *, pp):
    s, j = pl.program_id(0), pl.program_id(1)
    n_j = pl.num_programs(1)
    step = s * n_j + j
    slot = step % 2

    def page_copies(step_idx, buf_slot):
        copies = []
        for i in range(pp):
            page = pt_ref[step_idx * pp + i]
            copies.append(pltpu.make_async_copy(cache_c.at[page], cbuf.at[buf_slot, i], sems.at[0, buf_slot]))
            copies.append(pltpu.make_async_copy(cache_krt.at[page], krbuf.at[buf_slot, i], sems.at[1, buf_slot]))
        return copies

    @pl.when(step == 0)
    def _():
        for cp in page_copies(0, 0):
            cp.start()

    @pl.when(step + 1 < pl.num_programs(0) * n_j)
    def _():
        for cp in page_copies(step + 1, 1 - slot):
            cp.start()

    @pl.when(j == 0)
    def _():
        m_s[...] = jnp.full(m_s.shape, -jnp.inf, f32)
        l_s[...] = jnp.zeros(l_s.shape, f32)
        acc_s[...] = jnp.zeros(acc_s.shape, f32)

    for cp in page_copies(step, slot):
        cp.wait()

    qabs, qr, wuk = qabs_ref[0], qr_ref[0], wuk_ref[...]
    sub = min(SUB_PAGES, pp)
    blocks = list(range(0, pp, sub))

    def load(b):
        c_bf = jnp.concatenate([cbuf[slot, i].astype(bf16) for i in range(b, b + sub)], axis=0)
        return c_bf, _key_sumsq_partials(c_bf, wuk)

    state = (m_s[...], l_s[...], acc_s[...])
    nxt = load(blocks[0])
    for n, b in enumerate(blocks):
        c_bf, t_bf = nxt
        if n + 1 < len(blocks):
            nxt = load(blocks[n + 1])
        krt = jnp.concatenate([krbuf[slot, i] for i in range(b, b + sub)], axis=1)
        state = _softmax_update_t(_scores_t(c_bf, t_bf, krt, qabs, qr), c_bf, state)
    m_s[...], l_s[...], acc_s[...] = state

    @pl.when(j == n_j - 1)
    def _():
        c_bf = cnew_ref[0].astype(bf16)
        n_pad = c_bf.shape[0]
        key_t = lax.broadcasted_iota(jnp.int32, (N_TH, n_pad), 1)
        q_t = lax.broadcasted_iota(jnp.int32, (N_TH, n_pad), 0) // N_HEADS
        st = _scores_t(c_bf, _key_sumsq_partials(c_bf, wuk), krtnew_ref[0], qabs, qr)
        m, l, acc = _softmax_update_t(st, c_bf, state, mask=key_t <= q_t)
        o_ref[0] = acc / l


def _attn_sample(page_table, qabs, qr, c_new, kr_new, cache_c, cache_kr, wuk_perm, pp):
    n_seq, n_pages = page_table.shape
    kv_rank = cache_c.shape[-1]
    t_new = c_new.shape[1]
    seq3 = lambda s, j, pt: (s, 0, 0)
    cache_kr_t = jnp.swapaxes(cache_kr, 1, 2)
    c_new_pad = jnp.pad(c_new, ((0, 0), (0, PAGE_SIZE - t_new), (0, 0)))
    krt_new_pad = jnp.pad(jnp.swapaxes(kr_new, 1, 2), ((0, 0), (0, 0), (0, PAGE_SIZE - t_new)))
    grid_spec = pltpu.PrefetchScalarGridSpec(
        num_scalar_prefetch=1,
        grid=(n_seq, n_pages // pp),
        in_specs=[pl.BlockSpec((1, N_TH, kv_rank), seq3),
                  pl.BlockSpec((1, N_TH, QK_ROPE_DIM), seq3),
                  pl.BlockSpec((1, PAGE_SIZE, kv_rank), seq3),
                  pl.BlockSpec((1, QK_ROPE_DIM, PAGE_SIZE), seq3),
                  pl.BlockSpec(wuk_perm.shape, lambda s, j, pt: (0, 0)),
                  pl.BlockSpec(memory_space=pl.ANY),
                  pl.BlockSpec(memory_space=pl.ANY)],
        out_specs=pl.BlockSpec((1, N_TH, kv_rank), seq3),
        scratch_shapes=[pltpu.VMEM((2, pp, PAGE_SIZE, kv_rank), f32),
                        pltpu.VMEM((2, pp, QK_ROPE_DIM, PAGE_SIZE), f32),
                        pltpu.SemaphoreType.DMA((2, 2)),
                        pltpu.VMEM((N_TH, 1), f32), pltpu.VMEM((N_TH, 1), f32),
                        pltpu.VMEM((N_TH, kv_rank), f32)],
    )
    return pl.pallas_call(
        functools.partial(_attn_sample_kernel, pp=pp),
        grid_spec=grid_spec,
        out_shape=jax.ShapeDtypeStruct((n_seq, N_TH, kv_rank), f32),
        compiler_params=_cparams("arbitrary", "arbitrary"),
        name="attn_sample",
    )(page_table.reshape(-1), qabs, qr, c_new_pad, krt_new_pad, wuk_perm, cache_c, cache_kr_t)


def _v_up_kernel(lat_ref, wuv_ref, o_ref):
    rows = lat_ref.shape[0]
    full = jnp.dot(lat_ref[...].astype(bf16), wuv_ref[...], preferred_element_type=f32)
    head_of_row = lax.broadcasted_iota(jnp.int32, full.shape, 0) % N_HEADS
    head_of_lane = lax.broadcasted_iota(jnp.int32, full.shape, 1) // V_DIM
    own = jnp.where(head_of_row == head_of_lane, full, 0.0)
    o_ref[...] = jnp.sum(own.reshape(rows // N_HEADS, N_HEADS, full.shape[1]), axis=1)


def _v_up(lat2d, wuv, rows):
    n, kv_rank = lat2d.shape
    d_attn = wuv.shape[1]
    return pl.pallas_call(
        _v_up_kernel,
        grid=(n // rows,),
        in_specs=[pl.BlockSpec((rows, kv_rank), lambda i: (i, 0)), pl.BlockSpec(wuv.shape, lambda i: (0, 0))],
        out_specs=pl.BlockSpec((rows // N_HEADS, d_attn), lambda i: (i, 0)),
        out_shape=jax.ShapeDtypeStruct((n // N_HEADS, d_attn), f32),
        compiler_params=_cparams("parallel"),
        name="v_up",
    )(lat2d, wuv)


def _attn_sample_all(q_s2d, c_new, kr_new, page_table, cache_c, cache_kr, w_ukv, g_qk_k, pp):
    n_seq, t_new, kv_rank = c_new.shape
    qabs, qr = _q_absorb(q_s2d, g_qk_k, w_ukv)
    qabs = qabs.reshape(n_seq, t_new * N_HEADS, kv_rank)
    qr = qr.reshape(n_seq, t_new * N_HEADS, QK_ROPE_DIM)
    wuk_perm = jnp.transpose(w_ukv[:, :, :QK_NOPE_DIM], (0, 2, 1)).reshape(kv_rank, -1).astype(bf16)
    lat = _attn_sample(page_table, qabs, qr, c_new, kr_new, cache_c, cache_kr, wuk_perm, pp)
    wuv = w_ukv[:, :, QK_NOPE_DIM:].reshape(kv_rank, N_HEADS * V_DIM).astype(bf16)
    rows = min(512, n_seq * N_TH)
    return _v_up(lat.reshape(n_seq * N_TH, kv_rank), wuv, rows)


def _rms(x, g):
    return x * lax.rsqrt(jnp.mean(x * x, axis=-1, keepdims=True) + NORM_EPS) * g


def _split_bf16(x):
    hi = x.astype(bf16)
    return hi, (x - hi.astype(f32)).astype(bf16)


PACK_W = 256
u32 = jnp.uint32


def _pack_bf16_pairs(x):
    bits = lax.bitcast_convert_type(x.astype(bf16).astype(f32), u32)
    pieces = []
    for p in range(x.shape[1] // (2 * PACK_W)):
        lo = bits[:, 2 * p * PACK_W:(2 * p + 1) * PACK_W]
        hi = bits[:, (2 * p + 1) * PACK_W:(2 * p + 2) * PACK_W]
        pieces.append((lo >> 16) | (hi & jnp.uint32(0xFFFF0000)))
    return pieces


def _unpack_bf16_pairs(words):
    return (lax.bitcast_convert_type(words << 16, f32),
            lax.bitcast_convert_type(words & jnp.uint32(0xFFFF0000), f32))


def _merge_kernel(xp_ref, cvp_ref, atp_ref, xs_ref, cvs_ref, ats_ref, gc_ref, ga_ref, woc_ref, woa_ref, gf_ref,
                  wrh_ref, wrl_ref, br_ref, h_ref, xna_ref, xnb_ref, exp_ref, gate_ref, rank_ref, cnt_ref, cnt_s,
                  *, n_prompt_tiles):
    i = pl.program_id(0)
    tm = xp_ref.shape[0]

    @pl.when(i == 0)
    def _():
        cnt_s[...] = jnp.zeros(cnt_s.shape, f32)

    is_p = i < n_prompt_tiles
    rows = tm

    def project(r):
        x = jnp.where(is_p, xp_ref[r, :], xs_ref[r, :])
        yc = _rms(jnp.where(is_p, cvp_ref[r, :], cvs_ref[r, :]), gc_ref[...]).astype(bf16)
        ya = _rms(jnp.where(is_p, atp_ref[r, :], ats_ref[r, :]), ga_ref[...]).astype(bf16)
        h = x + jnp.dot(yc, woc_ref[...], preferred_element_type=f32) \
            + jnp.dot(ya, woa_ref[...], preferred_element_type=f32)
        h_ref[r, :] = h
        xn = _rms(h, gf_ref[...])
        xh, xl = _split_bf16(xn)
        xna_ref[r, :], xnb_ref[r, :] = _pack_bf16_pairs(xn)
        return (jnp.dot(xh, wrh_ref[...], preferred_element_type=f32)
                + jnp.dot(xl, wrh_ref[...], preferred_element_type=f32)
                + jnp.dot(xh, wrl_ref[...], preferred_element_type=f32)) + br_ref[...]

    def route(r, logits, cnt):
        lane = lax.broadcasted_iota(jnp.int32, logits.shape, 1)
        tops, idxs = [], []
        cur = logits
        for _k in range(TOP_K):
            mk = jnp.max(cur, axis=-1, keepdims=True)
            ik = jnp.min(jnp.where(cur == mk, lane, LANES), axis=-1, keepdims=True)
            tops.append(mk)
            idxs.append(ik)
            cur = jnp.where(lane == ik, -jnp.inf, cur)
        es = [jnp.exp(t - tops[0]) for t in tops]
        den = es[0] + es[1] + es[2] + es[3]
        onehots = [(lane == ik).astype(f32) for ik in idxs]
        oh = onehots[0] + onehots[1] + onehots[2] + onehots[3]
        r_i = lax.broadcasted_iota(jnp.int32, (rows, rows), 0)
        c_i = lax.broadcasted_iota(jnp.int32, (rows, rows), 1)
        before = jnp.dot((c_i < r_i).astype(bf16), oh.astype(bf16), preferred_element_type=f32) + cnt
        exp_out = jnp.zeros(logits.shape, jnp.int32)
        gate_out = jnp.zeros(logits.shape, f32)
        rank_out = jnp.zeros(logits.shape, jnp.int32)
        for k in range(TOP_K):
            rk = jnp.sum(onehots[k] * before, axis=-1, keepdims=True)
            exp_out = jnp.where(lane == k, idxs[k], exp_out)
            gate_out = jnp.where(lane == k, es[k] / den, gate_out)
            rank_out = jnp.where(lane == k, rk.astype(jnp.int32), rank_out)
        exp_ref[r, :] = exp_out
        gate_ref[r, :] = gate_out
        rank_ref[r, :] = rank_out
        return cnt + jnp.sum(oh, axis=0, keepdims=True)

    whole = pl.ds(0, tm)
    cnt = route(whole, project(whole), cnt_s[...])
    cnt_s[...] = cnt
    cnt_ref[...] = cnt


def _merge(x_p, cv_p, at_p, x_s, cv_s, at_s, g_out_conv, g_out_attn, w_out, g_ffn, w_router, b_router, tm):
    (n_p, d_model), n_s = x_p.shape, x_s.shape[0]
    n = n_p + n_s
    d_conv, d_attn = cv_p.shape[1], at_p.shape[1]
    woc, woa = w_out[:d_conv].astype(bf16), w_out[d_conv:].astype(bf16)
    wr = jnp.pad(w_router, ((0, 0), (0, LANES - N_EXPERTS)))
    wrh, wrl = _split_bf16(wr)
    br = jnp.concatenate([b_router.astype(f32), jnp.full((LANES - N_EXPERTS,), -jnp.inf, f32)])[None, :]
    gc, ga, gf = g_out_conv[None, :], g_out_attn[None, :], g_ffn[None, :]
    tiles_p, tiles_s = n_p // tm, n_s // tm
    full = lambda a: pl.BlockSpec(a.shape, lambda i: (0, 0))
    rows = lambda w: pl.BlockSpec((tm, w), lambda i: (i, 0))
    rows_p = lambda w: pl.BlockSpec((tm, w), lambda i: (jnp.minimum(i, tiles_p - 1), 0))
    rows_s = lambda w: pl.BlockSpec((tm, w), lambda i: (jnp.maximum(i - tiles_p, 0), 0))
    assert d_model == 4 * PACK_W
    out_shape = (jax.ShapeDtypeStruct((n, d_model), f32),
                 jax.ShapeDtypeStruct((n, PACK_W), u32), jax.ShapeDtypeStruct((n, PACK_W), u32),
                 jax.ShapeDtypeStruct((n, LANES), jnp.int32), jax.ShapeDtypeStruct((n, LANES), f32),
                 jax.ShapeDtypeStruct((n, LANES), jnp.int32), jax.ShapeDtypeStruct((1, LANES), f32))
    return pl.pallas_call(
        functools.partial(_merge_kernel, n_prompt_tiles=tiles_p),
        grid=(tiles_p + tiles_s,),
        in_specs=[rows_p(d_model), rows_p(d_conv), rows_p(d_attn), rows_s(d_model), rows_s(d_conv), rows_s(d_attn),
                  full(gc), full(ga), full(woc), full(woa), full(gf), full(wrh), full(wrl), full(br)],
        out_specs=(rows(d_model), rows(PACK_W), rows(PACK_W), rows(LANES), rows(LANES), rows(LANES),
                   pl.BlockSpec((1, LANES), lambda i: (0, 0))),
        out_shape=out_shape,
        scratch_shapes=[pltpu.VMEM((1, LANES), f32)],
        compiler_params=_cparams("arbitrary"),
        name="merge",
    )(x_p, cv_p, at_p, x_s, cv_s, at_s, gc, ga, woc, woa, gf, wrh, wrl, br)


def _experts_kernel(be_ref, br_ref, xa_ref, xb_ref, wgu_ref, bgu_ref, wd_ref, bd_ref, ya_ref, yb_ref, wgu_s, wd_s):
    i = pl.program_id(0)
    d_expert = wd_ref.shape[1]
    blk = xa_ref.shape[0]

    @pl.when((i == 0) | (be_ref[i] != be_ref[jnp.maximum(i - 1, 0)]))
    def _():
        wgu_s[...] = wgu_ref[0].astype(bf16)
        wd_s[...] = wd_ref[0].astype(bf16)

    @pl.when(br_ref[i] > 0)
    def _():
        x = jnp.concatenate(_unpack_bf16_pairs(xa_ref[...]) + _unpack_bf16_pairs(xb_ref[...]), axis=1)
        live = lax.broadcasted_iota(jnp.int32, (blk, 1), 0) < br_ref[i]
        x = jnp.where(live, x, 0.0).astype(bf16)
        gu = jnp.dot(x, wgu_s[...], preferred_element_type=f32) + bgu_ref[0]
        gate = jnp.minimum(gu[:, :d_expert], SWIGLU_LIMIT)
        up = jnp.clip(gu[:, d_expert:], -SWIGLU_LIMIT, SWIGLU_LIMIT)
        hid = (up + 1.0) * gate * jax.nn.sigmoid(SWIGLU_ALPHA * gate)
        y = jnp.dot(hid.astype(bf16), wd_s[...], preferred_element_type=f32) + bd_ref[0]
        ya_ref[...], yb_ref[...] = _pack_bf16_pairs(y)

    @pl.when(br_ref[i] <= 0)
    def _():
        ya_ref[...] = jnp.zeros(ya_ref.shape, ya_ref.dtype)
        yb_ref[...] = jnp.zeros(yb_ref.shape, yb_ref.dtype)


def _experts(block_expert, block_rows, xs_a, xs_b, w_gate_up, b_gate_up, w_down, b_down, blk):
    n_slots = xs_a.shape[0]
    n_exp, d_model, d_gu = w_gate_up.shape
    d_expert = w_down.shape[1]
    slots = pl.BlockSpec((blk, PACK_W), lambda i, be, br: (i, 0))
    grid_spec = pltpu.PrefetchScalarGridSpec(
        num_scalar_prefetch=2,
        grid=(n_slots // blk,),
        in_specs=[slots, slots,
                  pl.BlockSpec((1, d_model, d_gu), lambda i, be, br: (be[i], 0, 0)),
                  pl.BlockSpec((1, 1, d_gu), lambda i, be, br: (be[i], 0, 0)),
                  pl.BlockSpec((1, d_expert, d_model), lambda i, be, br: (be[i], 0, 0)),
                  pl.BlockSpec((1, 1, d_model), lambda i, be, br: (be[i], 0, 0))],
        out_specs=(slots, slots),
        scratch_shapes=[pltpu.VMEM((d_model, d_gu), bf16), pltpu.VMEM((d_expert, d_model), bf16)],
    )
    packed = jax.ShapeDtypeStruct((n_slots, PACK_W), u32)
    return pl.pallas_call(
        _experts_kernel,
        grid_spec=grid_spec,
        out_shape=(packed, packed),
        compiler_params=_cparams("arbitrary"),
        name="experts",
    )(block_expert, block_rows, xs_a, xs_b, w_gate_up, b_gate_up[:, None, :], w_down, b_down[:, None, :])


def _combine_kernel(h_ref, ya_ref, yb_ref, gate_ref, yp_ref, ys_ref, *, n_prompt_tiles):
    i = pl.program_id(0)
    g = gate_ref[...]
    acc = None
    for k in range(TOP_K):
        gk = g[:, k:k + 1]
        pieces = [p * gk for p in _unpack_bf16_pairs(ya_ref[k]) + _unpack_bf16_pairs(yb_ref[k])]
        acc = pieces if acc is None else [a + p for a, p in zip(acc, pieces)]
    y = h_ref[...] + jnp.concatenate(acc, axis=1)

    @pl.when(i < n_prompt_tiles)
    def _():
        yp_ref[...] = y

    @pl.when(i >= n_prompt_tiles)
    def _():
        ys_ref[...] = y


def _combine(h, ya, yb, gates, n_p, tm):
    n, d_model = h.shape
    tiles_p = n_p // tm
    packed = pl.BlockSpec((TOP_K, tm, PACK_W), lambda i: (0, i, 0))
    return pl.pallas_call(
        functools.partial(_combine_kernel, n_prompt_tiles=tiles_p),
        grid=(n // tm,),
        in_specs=[pl.BlockSpec((tm, d_model), lambda i: (i, 0)), packed, packed,
                  pl.BlockSpec((tm, LANES), lambda i: (i, 0))],
        out_specs=(pl.BlockSpec((tm, d_model), lambda i: (jnp.minimum(i, tiles_p - 1), 0)),
                   pl.BlockSpec((tm, d_model), lambda i: (jnp.maximum(i - tiles_p, 0), 0))),
        out_shape=(jax.ShapeDtypeStruct((n_p, d_model), f32), jax.ShapeDtypeStruct((n - n_p, d_model), f32)),
        compiler_params=_cparams("arbitrary"),
        name="combine",
    )(h, ya, yb, gates)


SC_WINDOW = 128


def _sc_mesh():
    return plsc.VectorSubcoreMesh(core_axis_name="core", subcore_axis_name="subcore")


def _sc_scatter_rows(x, idx, n_out):
    n, w = x.shape
    n_k = idx.shape[0]

    @pl.kernel(out_type=jax.ShapeDtypeStruct((n_out, w), x.dtype), mesh=_sc_mesh(), scratch_types=[])
    def scatter(x_hbm, i_hbm, o_hbm):
        def body(x_vmem, *i_vmem):
            for iv in i_vmem:
                pltpu.sync_copy(x_vmem, o_hbm.at[iv.at[0]])

        pltpu.emit_pipeline(
            body,
            grid=(n // SC_WINDOW,),
            in_specs=[pl.BlockSpec((SC_WINDOW, w), lambda i: (i, 0))]
            + [pl.BlockSpec((1, SC_WINDOW), functools.partial(lambda k, i: (k, i), k)) for k in range(n_k)],
            out_specs=[],
            core_axis_name=("core", "subcore"),
            dimension_semantics=(pltpu.PARALLEL,),
        )(x_hbm, *([i_hbm] * n_k))

    return scatter(x, idx)


def _sc_gather_rows(x, idx):
    n_idx, w = idx.shape[0], x.shape[1]

    @pl.kernel(out_type=jax.ShapeDtypeStruct((n_idx, w), x.dtype), mesh=_sc_mesh(), scratch_types=[])
    def gather(x_hbm, i_hbm, o_hbm):
        def body(i_vmem, o_vmem):
            pltpu.sync_copy(x_hbm.at[i_vmem.at[0]], o_vmem)

        pltpu.emit_pipeline(
            body,
            grid=(n_idx // SC_WINDOW,),
            in_specs=[pl.BlockSpec((1, SC_WINDOW), lambda i: (0, i))],
            out_specs=[pl.BlockSpec((SC_WINDOW, w), lambda i: (i, 0))],
            core_axis_name=("core", "subcore"),
            dimension_semantics=(pltpu.PARALLEL,),
        )(i_hbm, o_hbm)

    return gather(x, idx.reshape(1, n_idx))


EXPERT_BLK = 512


def _moe(h, xn_a, xn_b, expert, gates, rank, counts, n_p, w_gate_up, b_gate_up, w_down, b_down):
    n = h.shape[0]
    blk = EXPERT_BLK
    padded = (counts + blk - 1) // blk * blk
    padded_end = jnp.cumsum(padded)
    start_padded = padded_end - padded
    one_hot = (expert[:, :, None] == jnp.arange(N_EXPERTS)[None, None, :]).astype(jnp.int32)
    slot_t = (jnp.sum(one_hot * start_padded[None, None, :], axis=-1) + rank).T
    n_blocks = -(-(n * TOP_K + N_EXPERTS * (blk - 1)) // blk)
    n_slots = n_blocks * blk
    block_start = jnp.arange(n_blocks, dtype=jnp.int32) * blk
    block_expert = jnp.minimum(jnp.sum((block_start[:, None] >= padded_end[None, :]).astype(jnp.int32), axis=1),
                               N_EXPERTS - 1)
    block_rows = jnp.clip(counts[block_expert] - (block_start - start_padded[block_expert]), 0, blk)
    xs_a = _sc_scatter_rows(xn_a, slot_t, n_slots)
    xs_b = _sc_scatter_rows(xn_b, slot_t, n_slots)
    ys_a, ys_b = _experts(block_expert, block_rows, xs_a, xs_b, w_gate_up, b_gate_up, w_down, b_down, blk)
    flat = slot_t.reshape(-1)
    ya = _sc_gather_rows(ys_a, flat).reshape(TOP_K, n, PACK_W)
    yb = _sc_gather_rows(ys_b, flat).reshape(TOP_K, n, PACK_W)
    return _combine(h, ya, yb, gates, n_p, 512)


def _layer(x_p, x_s, cache_c, cache_kr, state, page_table, norm_mix, w_in, norm_q_lat, w_uq, norm_kv_lat, w_ukv,
           norm_qk_q, norm_qk_k, conv_w, conv_b, conv_ln_g, conv_ln_b, norm_out_conv, norm_out_attn, w_out,
           norm_ffn, w_router, b_router, w_gate_up, b_gate_up, w_down, b_down):
    n_b, t_p, d_model = x_p.shape
    n_seq, t_s, _ = x_s.shape
    d_conv = conv_w.shape[1]
    q_rank, kv_rank = norm_q_lat.shape[0], norm_kv_lat.shape[0]
    n_past = page_table.shape[1] * PAGE_SIZE
    win, wuq, wk, wvt = _mix_in_weights(w_in, w_uq, w_ukv, d_conv, q_rank, kv_rank)
    mix = functools.partial(_mix_in, g_mix=norm_mix, win=win, g_q_lat=norm_q_lat, wuq=wuq, g_kv_lat=norm_kv_lat,
                            g_qk_q=norm_qk_q, wk=wk, wvt=wvt, g_qk_k=norm_qk_k,
                            d_conv=d_conv, q_rank=q_rank, kv_rank=kv_rank)
    x_p2, x_s2 = x_p.reshape(n_b * t_p, d_model), x_s.reshape(n_seq * t_s, d_model)
    tab_p = _rope_head_tables(jnp.arange(t_p))
    tab_s = _rope_head_tables(n_past + jnp.arange(n_seq * t_s) % t_s)
    glu_p, q_p, c_p, kr_p, k_p, vt_p = mix(x_p2, tab_p, MIX_TILE, seq_len=t_p, q_scale=ATTN_SCALE * LOG2_E)
    glu_s, q_s, c_s, kr_s, _, _ = mix(x_s2, tab_s, MIX_TILE, seq_len=n_seq * t_s, q_scale=ATTN_SCALE)

    glu_p3, glu_s3 = glu_p.reshape(n_b, t_p, d_conv), glu_s.reshape(n_seq, t_s, d_conv)
    cv_p = _conv_prompt(glu_p3, conv_w, conv_b, conv_ln_g, conv_ln_b)
    cv_s = _conv_sample(glu_s3, state, conv_w, conv_b, conv_ln_g, conv_ln_b)

    hp = N_HEADS * HEAD_PAD
    at_p = _attn_prompt(q_p.reshape(n_b, t_p, hp), k_p.reshape(n_b, t_p, hp), vt_p, 512, 512)
    c_s3, kr_s3 = c_s.reshape(n_seq, t_s, kv_rank), kr_s.reshape(n_seq, t_s, QK_ROPE_DIM)
    at_s = _attn_sample_all(q_s, c_s3, kr_s3, page_table, cache_c, cache_kr, w_ukv, norm_qk_k, 32)

    h, xn_a, xn_b, expert, gates, rank, cnt = _merge(
        x_p2, cv_p.reshape(n_b * t_p, d_conv), at_p.reshape(n_b * t_p, -1),
        x_s2, cv_s.reshape(n_seq * t_s, d_conv), at_s,
        norm_out_conv, norm_out_attn, w_out, norm_ffn, w_router, b_router, 512)
    counts = cnt[0, :N_EXPERTS].astype(jnp.int32)
    y_p, y_s = _moe(h, xn_a, xn_b, expert[:, :TOP_K], gates, rank[:, :TOP_K], counts, n_b * t_p,
                    w_gate_up, b_gate_up, w_down, b_down)
    n_hist = CONV_WIDTH - 1
    conv_state_p = jnp.concatenate([jnp.zeros((n_b, max(n_hist - t_p, 0), d_conv), f32),
                                    glu_p3[:, max(t_p - n_hist, 0):]], axis=1)
    conv_state_s = jnp.concatenate([state, glu_s3], axis=1)[:, -n_hist:]
    return (y_p.reshape(x_p.shape), y_s.reshape(x_s.shape), c_p.reshape(n_b, t_p, kv_rank),
            kr_p.reshape(n_b, t_p, QK_ROPE_DIM), conv_state_p, c_s3, kr_s3, conv_state_s)


def kernel(x_prompt, x_sample, cache_kv_latent, cache_k_rope, state_conv, page_table, norm_mix, w_in, norm_q_lat,
           w_uq, norm_kv_lat, w_ukv, norm_qk_q, norm_qk_k, conv_w, conv_b, conv_ln_g, conv_ln_b, norm_out_conv,
           norm_out_attn, w_out, norm_ffn, w_router, b_router, w_gate_up, b_gate_up, w_down, b_down):
    h_p, h_s = x_prompt, x_sample
    per_layer = []
    for l in range(w_in.shape[0]):
        outs = _layer(h_p, h_s, cache_kv_latent[l], cache_k_rope[l], state_conv[l], page_table, norm_mix[l],
                      w_in[l], norm_q_lat[l], w_uq[l], norm_kv_lat[l], w_ukv[l], norm_qk_q[l], norm_qk_k[l],
                      conv_w[l], conv_b[l], conv_ln_g[l], conv_ln_b[l], norm_out_conv[l], norm_out_attn[l],
                      w_out[l], norm_ffn[l], w_router[l], b_router[l], w_gate_up[l], b_gate_up[l], w_down[l],
                      b_down[l])
        h_p, h_s = outs[0], outs[1]
        per_layer.append(outs[2:])
    stacked = [jnp.stack([pl_[i] for pl_ in per_layer]) for i in range(6)]
    return (h_p, h_s, *stacked)
```

```python
import functools

import jax
import jax.numpy as jnp
import numpy as np
from jax import lax
from jax.experimental import pallas as pl
from jax.experimental.pallas import tpu as pltpu
from jax.experimental.pallas import tpu_sc as plsc

N_HEADS = 8
QK_NOPE_DIM = 64
QK_ROPE_DIM = 32
ROPE_HALF = QK_ROPE_DIM // 2
QK_DIM = QK_NOPE_DIM + QK_ROPE_DIM
V_DIM = 64
CONV_WIDTH = 31
N_EXPERTS = 32
TOP_K = 4
SWIGLU_LIMIT = 7.0
SWIGLU_ALPHA = 1.702
ROPE_THETA = 10000.0
NORM_EPS = 1e-6
ATTN_SCALE = QK_DIM ** -0.5
LOG2_E = 1.4426950408889634
PAGE_SIZE = 128

LANES = 128
HEAD_PAD = LANES
VMEM_LIMIT = 56 * 1024 * 1024

f32 = jnp.float32
bf16 = jnp.bfloat16


def _cparams(*sem):
    return pltpu.CompilerParams(dimension_semantics=sem, vmem_limit_bytes=VMEM_LIMIT)


def _rope_head_tables(pos):
    inv_freq = ROPE_THETA ** (-jnp.arange(ROPE_HALF, dtype=f32) / ROPE_HALF)
    ang = pos.astype(f32)[:, None] * inv_freq[None, :]
    cos, sin = jnp.cos(ang), jnp.sin(ang)
    n = pos.shape[0]
    z = lambda w: jnp.zeros((n, w), f32)
    c = jnp.concatenate([jnp.ones((n, QK_NOPE_DIM), f32), cos, cos, z(32)], axis=1)
    s1 = jnp.concatenate([z(QK_NOPE_DIM + ROPE_HALF), sin, z(32)], axis=1)
    s2 = jnp.concatenate([z(QK_NOPE_DIM), -sin, z(ROPE_HALF + 32)], axis=1)
    return jnp.concatenate([c, s1, s2], axis=1)


def _rope_head(x, tab):
    c, s1, s2 = tab[:, :LANES], tab[:, LANES:2 * LANES], tab[:, 2 * LANES:]
    return x * c + pltpu.roll(x, ROPE_HALF, 1) * s1 + pltpu.roll(x, LANES - ROPE_HALF, 1) * s2


def _mix_in_kernel(x_ref, tab_ref, gmix_ref, win_ref, gq_ref, wuq_ref, gkv_ref, gqq_ref,
                   wk_ref, wvt_ref, gqk_ref,
                   glu_ref, q_ref, c_ref, kr_ref, k_ref, vt_ref, *, d_conv, q_rank, kv_rank, q_scale):
    tm = x_ref.shape[0]
    rows = tm // MIX_SUB_TILES
    gqq = gqq_ref[...] * q_scale
    gqk = gqk_ref[...]

    def project(r):
        x = x_ref[r, :]
        xn = x * lax.rsqrt(jnp.mean(x * x, axis=-1, keepdims=True) + NORM_EPS) * gmix_ref[...]
        return jnp.dot(xn.astype(bf16), win_ref[...], preferred_element_type=f32)

    def latents(r, proj):
        o = 0
        a = proj[:, o:o + d_conv]; o += d_conv
        gt = proj[:, o:o + d_conv]; o += d_conv
        q_lat = proj[:, o:o + q_rank]; o += q_rank
        kv_lat = proj[:, o:o + kv_rank]; o += kv_rank
        kr_raw = proj[:, o:o + LANES]
        glu_ref[r, :] = a * jax.nn.sigmoid(gt)
        tab = tab_ref[r, :]
        qn = q_lat * lax.rsqrt(jnp.mean(q_lat * q_lat, axis=-1, keepdims=True) + NORM_EPS) * gq_ref[...]
        q = jnp.dot(qn.astype(bf16), wuq_ref[...], preferred_element_type=f32)
        c_kv = kv_lat * lax.rsqrt(jnp.mean(kv_lat * kv_lat, axis=-1, keepdims=True) + NORM_EPS) * gkv_ref[...]
        c_ref[r, :] = c_kv
        kr = _rope_head(kr_raw, tab)
        kr_ref[r, :] = kr[:, QK_NOPE_DIM:QK_NOPE_DIM + QK_ROPE_DIM]
        c_bf = c_kv.astype(bf16)
        kn = jnp.dot(c_bf, wk_ref[...], preferred_element_type=f32)
        vt_ref[0, :, r] = lax.dot_general(wvt_ref[...], c_bf, NT_DIMS,
                                          preferred_element_type=f32).astype(vt_ref.dtype)
        return q, kn, kr, tab

    def heads(r, q, kn, kr, tab):
        for h in range(N_HEADS):
            sl = slice(h * HEAD_PAD, (h + 1) * HEAD_PAD)
            qh = _rope_head(q[:, sl], tab)
            qh = qh * lax.rsqrt(jnp.sum(qh * qh, axis=-1, keepdims=True) * (1.0 / QK_DIM) + NORM_EPS) * gqq
            q_ref[r, sl] = qh.astype(q_ref.dtype)
            kh = kn[:, sl] + kr
            kh = kh * lax.rsqrt(jnp.sum(kh * kh, axis=-1, keepdims=True) * (1.0 / QK_DIM) + NORM_EPS) * gqk
            k_ref[r, sl] = kh.astype(k_ref.dtype)

    slices = [pl.ds(i * rows, rows) for i in range(MIX_SUB_TILES)]
    proj = project(slices[0])
    for i, r in enumerate(slices):
        nxt = project(slices[i + 1]) if i + 1 < len(slices) else None
        heads(r, *latents(r, proj))
        proj = nxt


MIX_TILE = 1024
MIX_SUB_TILES = 4


def _head_gain(g_pairs):
    g_rot = g_pairs[QK_NOPE_DIM:]
    return jnp.concatenate([g_pairs[:QK_NOPE_DIM], g_rot, g_rot, jnp.zeros((32,), f32)])[None, :]


def _mix_in_weights(w_in, w_uq, w_ukv, d_conv, q_rank, kv_rank):
    d_model = w_in.shape[0]
    base = 2 * d_conv + q_rank + kv_rank
    w_kr = w_in[:, base:base + QK_ROPE_DIM]
    w_kr_pad = jnp.concatenate([jnp.zeros((d_model, QK_NOPE_DIM), f32), w_kr,
                                jnp.zeros((d_model, 32), f32)], axis=1)
    win = jnp.concatenate([w_in[:, :base], w_kr_pad], axis=1).astype(bf16)
    wuq = jnp.pad(w_uq, ((0, 0), (0, 0), (0, HEAD_PAD - QK_DIM))).reshape(q_rank, N_HEADS * HEAD_PAD).astype(bf16)
    wk = jnp.pad(w_ukv[:, :, :QK_NOPE_DIM], ((0, 0), (0, 0), (0, HEAD_PAD - QK_NOPE_DIM)))
    wk = wk.reshape(kv_rank, N_HEADS * HEAD_PAD).astype(bf16)
    wvt = w_ukv[:, :, QK_NOPE_DIM:].reshape(kv_rank, N_HEADS * V_DIM).T.astype(bf16)
    return win, wuq, wk, wvt


def _mix_in(x2d, tab, tm, g_mix, win, g_q_lat, wuq, g_kv_lat, g_qk_q, wk, wvt, g_qk_k, d_conv, q_rank, kv_rank,
            seq_len, q_scale):
    n, d_model = x2d.shape
    n_tab_blocks = tab.shape[0] // tm
    seq_tiles = seq_len // tm
    row = lambda i: (i, 0)
    const = lambda i: (0, 0)
    full = lambda a: pl.BlockSpec(a.shape, const)
    gm, gq, gkv = g_mix[None, :], g_q_lat[None, :], g_kv_lat[None, :]
    gqq, gqk = _head_gain(g_qk_q), _head_gain(g_qk_k)
    hp = N_HEADS * HEAD_PAD
    out_shape = (
        jax.ShapeDtypeStruct((n, d_conv), f32),
        jax.ShapeDtypeStruct((n, hp), bf16),
        jax.ShapeDtypeStruct((n, kv_rank), f32),
        jax.ShapeDtypeStruct((n, QK_ROPE_DIM), f32),
        jax.ShapeDtypeStruct((n, hp), bf16),
        jax.ShapeDtypeStruct((n // seq_len, N_HEADS * V_DIM, seq_len), bf16),
    )
    out_specs = [pl.BlockSpec((tm, s.shape[1]), row) for s in out_shape[:-1]]
    out_specs.append(pl.BlockSpec((1, N_HEADS * V_DIM, tm), lambda i: (i // seq_tiles, 0, i % seq_tiles)))
    return pl.pallas_call(
        functools.partial(_mix_in_kernel, d_conv=d_conv, q_rank=q_rank, kv_rank=kv_rank, q_scale=q_scale),
        grid=(n // tm,),
        in_specs=[pl.BlockSpec((tm, d_model), row),
                  pl.BlockSpec((tm, tab.shape[1]), lambda i: (i % n_tab_blocks, 0)),
                  full(gm), full(win), full(gq), full(wuq), full(gkv), full(gqq), full(wk), full(wvt), full(gqk)],
        out_specs=tuple(out_specs),
        out_shape=out_shape,
        compiler_params=_cparams("parallel"),
        name="mix_in",
    )(x2d, tab, gm, win, gq, wuq, gkv, gqq, wk, wvt, gqk)


CONV_PAD = 32


SUBLANES = 8


def _conv_taps(window, w_ref, b_ref, ls, tc):
    lead = CONV_PAD - (CONV_WIDTH - 1)
    acc = jnp.zeros((tc, LANES), f32) + b_ref[:, ls]
    for r in range(SUBLANES):
        taps = [(a, a * SUBLANES + r - lead) for a in range(CONV_PAD // SUBLANES + 1)
                if 0 <= a * SUBLANES + r - lead < CONV_WIDTH]
        shifted = window if r == 0 else pltpu.roll(window, window.shape[0] - r, 0)
        for a, j in taps:
            acc = acc + shifted[a * SUBLANES:a * SUBLANES + tc] * w_ref[j:j + 1, ls]
    return acc


def _ln_swish(y, g_ref, beta_ref):
    mu = jnp.mean(y, axis=-1, keepdims=True)
    yc = y - mu
    var = jnp.mean(yc * yc, axis=-1, keepdims=True)
    z = yc * lax.rsqrt(var + NORM_EPS) * g_ref[...] + beta_ref[...]
    return z * jax.nn.sigmoid(z)


def _conv_prompt_kernel(x_ref, w_ref, b_ref, g_ref, beta_ref, o_ref, *, tc):
    t_len, n_ch = x_ref.shape[1], x_ref.shape[2]

    def chunk(idx, carry):
        t0 = pl.multiple_of(idx * tc, tc)
        h0 = pl.multiple_of(jnp.maximum(t0 - CONV_PAD, 0), SUBLANES)
        has_hist = jnp.where(idx > 0, 1.0, 0.0)
        for lt in range(n_ch // LANES):
            ls = slice(lt * LANES, (lt + 1) * LANES)
            window = jnp.concatenate([x_ref[0, pl.ds(h0, CONV_PAD), ls] * has_hist,
                                      x_ref[0, pl.ds(t0, tc), ls]], axis=0)
            o_ref[0, pl.ds(t0, tc), ls] = _conv_taps(window, w_ref, b_ref, ls, tc)
        o_ref[0, pl.ds(t0, tc), :] = _ln_swish(o_ref[0, pl.ds(t0, tc), :], g_ref, beta_ref)
        return carry

    lax.fori_loop(0, t_len // tc, chunk, 0)


def _conv_sample_kernel(x_ref, st_ref, w_ref, b_ref, g_ref, beta_ref, o_ref):
    bb, tc, n_ch = x_ref.shape

    def one(b, carry):
        for lt in range(n_ch // LANES):
            ls = slice(lt * LANES, (lt + 1) * LANES)
            window = jnp.concatenate([st_ref[b, :, ls], x_ref[b, :, ls]], axis=0)
            o_ref[b, :, ls] = _conv_taps(window, w_ref, b_ref, ls, tc)
        o_ref[b] = _ln_swish(o_ref[b], g_ref, beta_ref)
        return carry

    lax.fori_loop(0, bb, one, 0)


def _conv_specs(conv_w, n_ch):
    const = lambda i: (0, 0)
    vec = pl.BlockSpec((1, n_ch), const)
    return [pl.BlockSpec(conv_w.shape, const), vec, vec, vec]


def _conv_prompt(glu, conv_w, conv_b, ln_g, ln_b, tc=128):
    n_b, t_len, n_ch = glu.shape
    blk = pl.BlockSpec((1, t_len, n_ch), lambda i: (i, 0, 0))
    return pl.pallas_call(
        functools.partial(_conv_prompt_kernel, tc=tc),
        grid=(n_b,),
        in_specs=[blk] + _conv_specs(conv_w, n_ch),
        out_specs=blk,
        out_shape=jax.ShapeDtypeStruct(glu.shape, f32),
        compiler_params=_cparams("parallel"),
        name="conv_prompt",
    )(glu, conv_w, conv_b[None, :], ln_g[None, :], ln_b[None, :])


def _conv_sample(glu, state, conv_w, conv_b, ln_g, ln_b, bb=32):
    n_b, t_len, n_ch = glu.shape
    hist = jnp.pad(state, ((0, 0), (CONV_PAD - state.shape[1], 0), (0, 0)))
    return pl.pallas_call(
        _conv_sample_kernel,
        grid=(n_b // bb,),
        in_specs=[pl.BlockSpec((bb, t_len, n_ch), lambda i: (i, 0, 0)),
                  pl.BlockSpec((bb, CONV_PAD, n_ch), lambda i: (i, 0, 0))] + _conv_specs(conv_w, n_ch),
        out_specs=pl.BlockSpec((bb, t_len, n_ch), lambda i: (i, 0, 0)),
        out_shape=jax.ShapeDtypeStruct(glu.shape, f32),
        compiler_params=_cparams("parallel"),
        name="conv_sample",
    )(glu, hist, conv_w, conv_b[None, :], ln_g[None, :], ln_b[None, :])


NT_DIMS = (((1,), (1,)), ((), ()))


def _attn_prompt_kernel(q_ref, k_ref, vt_ref, o_ref, *, tq, tk):
    assert tq == tk
    qi = pl.program_id(2)
    n_h = q_ref.shape[2] // HEAD_PAD
    qs = [q_ref[0, :, h * HEAD_PAD:(h + 1) * HEAD_PAD] for h in range(n_h)]

    def block(k0, nk, q_lo, carry, masked):
        nq = tq - q_lo
        if masked:
            key = k0 + lax.broadcasted_iota(jnp.int32, (nk, nq), 0)
            qry = qi * tq + q_lo + lax.broadcasted_iota(jnp.int32, (nk, nq), 1)
            visible = key <= qry
        ones = jnp.ones((ONES_ROWS, nk), bf16)
        sts = []
        for h in range(n_h):
            k_h = k_ref[0, pl.ds(k0, nk), h * HEAD_PAD:(h + 1) * HEAD_PAD]
            sts.append(lax.dot_general(k_h, qs[h][q_lo:], NT_DIMS, preferred_element_type=f32))
        out = []
        for h, (m_all, acc_all) in enumerate(carry):
            m, acc = m_all[:, q_lo:], acc_all[:, q_lo:]
            st = jnp.where(visible, sts[h], -jnp.inf) if masked else sts[h]
            m_new = jnp.maximum(m, jnp.max(st, axis=0, keepdims=True))
            alpha = jnp.exp2(m - m_new)
            p = jnp.exp2(st - m_new).astype(bf16)
            vt_h = jnp.concatenate([vt_ref[0, h * V_DIM:(h + 1) * V_DIM, pl.ds(k0, nk)], ones], axis=0)
            acc = alpha * acc + jnp.dot(vt_h, p, preferred_element_type=f32)
            if q_lo:
                m_new = jnp.concatenate([m_all[:, :q_lo], m_new], axis=1)
                acc = jnp.concatenate([acc_all[:, :q_lo], acc], axis=1)
            out.append((m_new, acc))
        return tuple(out)

    carry = tuple((jnp.full((1, tq), -jnp.inf, f32), jnp.zeros((V_DIM + ONES_ROWS, tq), f32)) for _ in range(n_h))
    carry = lax.fori_loop(
        0, qi, lambda kb, c: block(pl.multiple_of(kb * tk, tk), tk, 0, c, False), carry)
    half = tk // 2
    d0 = pl.multiple_of(qi * tq, tq)
    carry = block(d0, half, 0, carry, True)
    carry = block(pl.multiple_of(d0 + half, half), half, half, carry, True)
    o_t = jnp.concatenate([acc[:V_DIM] / acc[V_DIM:V_DIM + 1] for (_, acc) in carry], axis=0)
    o_ref[0] = o_t.T


ONES_ROWS = 16


ATTN_HEAD_GROUP = 4


def _attn_prompt(q, k, vt, tq, tk):
    n_b, t_len, _ = q.shape
    g = ATTN_HEAD_GROUP
    return pl.pallas_call(
        functools.partial(_attn_prompt_kernel, tq=tq, tk=tk),
        grid=(n_b, N_HEADS // g, t_len // tq),
        in_specs=[pl.BlockSpec((1, tq, g * HEAD_PAD), lambda b, h, i: (b, i, h)),
                  pl.BlockSpec((1, t_len, g * HEAD_PAD), lambda b, h, i: (b, 0, h)),
                  pl.BlockSpec((1, g * V_DIM, t_len), lambda b, h, i: (b, h, 0))],
        out_specs=pl.BlockSpec((1, tq, g * V_DIM), lambda b, h, i: (b, i, h)),
        out_shape=jax.ShapeDtypeStruct((n_b, t_len, N_HEADS * V_DIM), f32),
        compiler_params=_cparams("parallel", "parallel", "arbitrary"),
        name="attn_prompt",
    )(q, k, vt)


N_TH = 64


def _q_absorb_kernel(q_ref, gk_ref, wukt_ref, sel_ref, qabs_ref, qr_ref):
    gk = gk_ref[...]
    kv_rank = wukt_ref.shape[-1]
    for h in range(N_HEADS):
        qg = (q_ref[:, h * HEAD_PAD:(h + 1) * HEAD_PAD].astype(f32) * gk).astype(bf16)
        qabs_ref[:, h * kv_rank:(h + 1) * kv_rank] = jnp.dot(
            qg, wukt_ref[h], preferred_element_type=f32).astype(qabs_ref.dtype)
        qr_ref[:, h * QK_ROPE_DIM:(h + 1) * QK_ROPE_DIM] = jnp.dot(
            qg, sel_ref[...], preferred_element_type=f32).astype(qr_ref.dtype)


def _q_absorb(q2d, g_qk_k, w_ukv):
    n = q2d.shape[0]
    kv_rank = w_ukv.shape[0]
    wukt = jnp.transpose(w_ukv[:, :, :QK_NOPE_DIM], (1, 2, 0))
    wukt = jnp.pad(wukt, ((0, 0), (0, HEAD_PAD - QK_NOPE_DIM), (0, 0))).astype(bf16)
    sel = jnp.zeros((HEAD_PAD, QK_ROPE_DIM), f32).at[
        QK_NOPE_DIM + jnp.arange(QK_ROPE_DIM), jnp.arange(QK_ROPE_DIM)].set(1.0).astype(bf16)
    gk = _head_gain(g_qk_k)
    full = lambda a: pl.BlockSpec(a.shape, lambda i: (0,) * a.ndim)
    return pl.pallas_call(
        _q_absorb_kernel,
        grid=(1,),
        in_specs=[full(q2d), full(gk), full(wukt), full(sel)],
        out_specs=(pl.BlockSpec((n, N_HEADS * kv_rank), lambda i: (0, 0)),
                   pl.BlockSpec((n, N_HEADS * QK_ROPE_DIM), lambda i: (0, 0))),
        out_shape=(jax.ShapeDtypeStruct((n, N_HEADS * kv_rank), bf16),
                   jax.ShapeDtypeStruct((n, N_HEADS * QK_ROPE_DIM), bf16)),
        compiler_params=_cparams("arbitrary"),
        name="q_absorb",
    )(q2d, gk, wukt, sel)


SUB_PAGES = 16


def _latent_products_t(lhs, c_bf):
    n = c_bf.shape[0]
    full = lax.dot_general(lhs, c_bf, NT_DIMS, preferred_element_type=f32)
    knt = full[N_TH:]
    ssq = jnp.sum((knt * knt).reshape(QK_NOPE_DIM, N_HEADS, n), axis=0)
    return full[:N_TH], ssq


def _scores_t(st_lat, ssq_nope, krt, qr):
    n = krt.shape[1]
    ssq = ssq_nope + jnp.sum(krt * krt, axis=0, keepdims=True)
    rinv = lax.rsqrt(ssq * (1.0 / QK_DIM) + NORM_EPS)
    st = st_lat + jnp.dot(qr, krt.astype(bf16), preferred_element_type=f32)
    return (st.reshape(N_TH // N_HEADS, N_HEADS, n) * rinv[None]).reshape(N_TH, n)


def _softmax_update_t(st, c_bf, state, mask=None):
    m, l, acc = state
    if mask is not None:
        st = jnp.where(mask, st, -jnp.inf)
    m_new = jnp.maximum(m, jnp.max(st, axis=1, keepdims=True))
    alpha = jnp.exp(m - m_new)
    p = jnp.exp(st - m_new)
    l = alpha * l + jnp.sum(p, axis=1, keepdims=True)
    acc = alpha * acc + jnp.dot(p.astype(bf16), c_bf, preferred_element_type=f32)
    return m_new, l, acc


def _attn_sample_kernel(pt_ref, qabs_ref, qr_ref, cnew_ref, krtnew_ref, wukt_ref, cache_c, cache_krt, o_ref,
                            cbuf, krbuf, sems, m_s, l_s, acc_s, *, pp):
    s, j = pl.program_id(0), pl.program_id(1)
    n_j = pl.num_programs(1)
    step = s * n_j + j
    slot = step % 2

    def page_copies(step_idx, buf_slot):
        copies = []
        for i in range(pp):
            page = pt_ref[step_idx * pp + i]
            copies.append(pltpu.make_async_copy(cache_c.at[page], cbuf.at[buf_slot, i], sems.at[0, buf_slot]))
            copies.append(pltpu.make_async_copy(cache_krt.at[page], krbuf.at[buf_slot, i], sems.at[1, buf_slot]))
        return copies

    @pl.when(step == 0)
    def _():
        for cp in page_copies(0, 0):
            cp.start()

    @pl.when(step + 1 < pl.num_programs(0) * n_j)
    def _():
        for cp in page_copies(step + 1, 1 - slot):
            cp.start()

    @pl.when(j == 0)
    def _():
        m_s[...] = jnp.full(m_s.shape, -jnp.inf, f32)
        l_s[...] = jnp.zeros(l_s.shape, f32)
        acc_s[...] = jnp.zeros(acc_s.shape, f32)

    for cp in page_copies(step, slot):
        cp.wait()

    qr = qr_ref[0]
    lhs = jnp.concatenate([qabs_ref[0], wukt_ref[...]], axis=0)
    sub = min(SUB_PAGES, pp)
    blocks = list(range(0, pp, sub))

    def load(b):
        c_bf = jnp.concatenate([cbuf[slot, i].astype(bf16) for i in range(b, b + sub)], axis=0)
        return (c_bf,) + _latent_products_t(lhs, c_bf)

    state = (m_s[...], l_s[...], acc_s[...])
    nxt = load(blocks[0])
    for n, b in enumerate(blocks):
        c_bf, st_lat, ssq = nxt
        if n + 1 < len(blocks):
            nxt = load(blocks[n + 1])
        krt = jnp.concatenate([krbuf[slot, i] for i in range(b, b + sub)], axis=1)
        state = _softmax_update_t(_scores_t(st_lat, ssq, krt, qr), c_bf, state)
    m_s[...], l_s[...], acc_s[...] = state

    @pl.when(j == n_j - 1)
    def _():
        c_bf = cnew_ref[0].astype(bf16)
        n_pad = c_bf.shape[0]
        key_t = lax.broadcasted_iota(jnp.int32, (N_TH, n_pad), 1)
        q_t = lax.broadcasted_iota(jnp.int32, (N_TH, n_pad), 0) // N_HEADS
        st = _scores_t(*_latent_products_t(lhs, c_bf), krtnew_ref[0], qr)
        m, l, acc = _softmax_update_t(st, c_bf, state, mask=key_t <= q_t)
        o_ref[0] = acc / l


def _attn_sample(page_table, qabs, qr, c_new, kr_new, cache_c, cache_kr, wuk_perm, pp):
    n_seq, n_pages = page_table.shape
    kv_rank = cache_c.shape[-1]
    t_new = c_new.shape[1]
    seq3 = lambda s, j, pt: (s, 0, 0)
    cache_kr_t = jnp.swapaxes(cache_kr, 1, 2)
    c_new_pad = jnp.pad(c_new, ((0, 0), (0, PAGE_SIZE - t_new), (0, 0)))
    krt_new_pad = jnp.pad(jnp.swapaxes(kr_new, 1, 2), ((0, 0), (0, 0), (0, PAGE_SIZE - t_new)))
    grid_spec = pltpu.PrefetchScalarGridSpec(
        num_scalar_prefetch=1,
        grid=(n_seq, n_pages // pp),
        in_specs=[pl.BlockSpec((1, N_TH, kv_rank), seq3),
                  pl.BlockSpec((1, N_TH, QK_ROPE_DIM), seq3),
                  pl.BlockSpec((1, PAGE_SIZE, kv_rank), seq3),
                  pl.BlockSpec((1, QK_ROPE_DIM, PAGE_SIZE), seq3),
                  pl.BlockSpec(wuk_perm.shape, lambda s, j, pt: (0, 0)),
                  pl.BlockSpec(memory_space=pl.ANY),
                  pl.BlockSpec(memory_space=pl.ANY)],
        out_specs=pl.BlockSpec((1, N_TH, kv_rank), seq3),
        scratch_shapes=[pltpu.VMEM((2, pp, PAGE_SIZE, kv_rank), f32),
                        pltpu.VMEM((2, pp, QK_ROPE_DIM, PAGE_SIZE), f32),
                        pltpu.SemaphoreType.DMA((2, 2)),
                        pltpu.VMEM((N_TH, 1), f32), pltpu.VMEM((N_TH, 1), f32),
                        pltpu.VMEM((N_TH, kv_rank), f32)],
    )
    return pl.pallas_call(
        functools.partial(_attn_sample_kernel, pp=pp),
        grid_spec=grid_spec,
        out_shape=jax.ShapeDtypeStruct((n_seq, N_TH, kv_rank), f32),
        compiler_params=_cparams("arbitrary", "arbitrary"),
        name="attn_sample",
    )(page_table.reshape(-1), qabs, qr, c_new_pad, krt_new_pad, wuk_perm, cache_c, cache_kr_t)


def _v_up_kernel(lat_ref, wuv_ref, o_ref):
    rows = lat_ref.shape[0]
    full = jnp.dot(lat_ref[...].astype(bf16), wuv_ref[...], preferred_element_type=f32)
    head_of_row = lax.broadcasted_iota(jnp.int32, full.shape, 0) % N_HEADS
    head_of_lane = lax.broadcasted_iota(jnp.int32, full.shape, 1) // V_DIM
    own = jnp.where(head_of_row == head_of_lane, full, 0.0)
    o_ref[...] = jnp.sum(own.reshape(rows // N_HEADS, N_HEADS, full.shape[1]), axis=1)


def _v_up(lat2d, wuv, rows):
    n, kv_rank = lat2d.shape
    d_attn = wuv.shape[1]
    return pl.pallas_call(
        _v_up_kernel,
        grid=(n // rows,),
        in_specs=[pl.BlockSpec((rows, kv_rank), lambda i: (i, 0)), pl.BlockSpec(wuv.shape, lambda i: (0, 0))],
        out_specs=pl.BlockSpec((rows // N_HEADS, d_attn), lambda i: (i, 0)),
        out_shape=jax.ShapeDtypeStruct((n // N_HEADS, d_attn), f32),
        compiler_params=_cparams("parallel"),
        name="v_up",
    )(lat2d, wuv)


def _attn_sample_all(q_s2d, c_new, kr_new, page_table, cache_c, cache_kr, w_ukv, g_qk_k, pp):
    n_seq, t_new, kv_rank = c_new.shape
    qabs, qr = _q_absorb(q_s2d, g_qk_k, w_ukv)
    qabs = qabs.reshape(n_seq, t_new * N_HEADS, kv_rank)
    qr = qr.reshape(n_seq, t_new * N_HEADS, QK_ROPE_DIM)
    wuk_perm = jnp.transpose(w_ukv[:, :, :QK_NOPE_DIM], (2, 1, 0)).reshape(-1, kv_rank).astype(bf16)
    lat = _attn_sample(page_table, qabs, qr, c_new, kr_new, cache_c, cache_kr, wuk_perm, pp)
    wuv = w_ukv[:, :, QK_NOPE_DIM:].reshape(kv_rank, N_HEADS * V_DIM).astype(bf16)
    rows = min(512, n_seq * N_TH)
    return _v_up(lat.reshape(n_seq * N_TH, kv_rank), wuv, rows)


def _rms(x, g):
    return x * lax.rsqrt(jnp.mean(x * x, axis=-1, keepdims=True) + NORM_EPS) * g


def _split_bf16(x):
    hi = x.astype(bf16)
    return hi, (x - hi.astype(f32)).astype(bf16)


PACK_W = 256
u32 = jnp.uint32


def _pack_bf16_pairs(x):
    bits = lax.bitcast_convert_type(x.astype(bf16).astype(f32), u32)
    pieces = []
    for p in range(x.shape[1] // (2 * PACK_W)):
        lo = bits[:, 2 * p * PACK_W:(2 * p + 1) * PACK_W]
        hi = bits[:, (2 * p + 1) * PACK_W:(2 * p + 2) * PACK_W]
        pieces.append((lo >> 16) | (hi & jnp.uint32(0xFFFF0000)))
    return pieces


def _unpack_bf16_pairs(words):
    return (lax.bitcast_convert_type(words << 16, f32),
            lax.bitcast_convert_type(words & jnp.uint32(0xFFFF0000), f32))


def _merge_kernel(xp_ref, cvp_ref, atp_ref, xs_ref, cvs_ref, ats_ref, gc_ref, ga_ref, woc_ref, woa_ref, gf_ref,
                  wrh_ref, wrl_ref, br_ref, h_ref, xna_ref, xnb_ref, exp_ref, gate_ref, rank_ref, cnt_ref, cnt_s,
                  *, n_prompt_tiles):
    i = pl.program_id(0)
    tm = xp_ref.shape[0]

    @pl.when(i == 0)
    def _():
        cnt_s[...] = jnp.zeros(cnt_s.shape, f32)

    is_p = i < n_prompt_tiles
    rows = tm

    def project(r):
        x = jnp.where(is_p, xp_ref[r, :], xs_ref[r, :])
        yc = _rms(jnp.where(is_p, cvp_ref[r, :], cvs_ref[r, :]), gc_ref[...]).astype(bf16)
        ya = _rms(jnp.where(is_p, atp_ref[r, :], ats_ref[r, :]), ga_ref[...]).astype(bf16)
        h = x + jnp.dot(yc, woc_ref[...], preferred_element_type=f32) \
            + jnp.dot(ya, woa_ref[...], preferred_element_type=f32)
        h_ref[r, :] = h
        xn = _rms(h, gf_ref[...])
        xh, xl = _split_bf16(xn)
        xna_ref[r, :], xnb_ref[r, :] = _pack_bf16_pairs(xn)
        return (jnp.dot(xh, wrh_ref[...], preferred_element_type=f32)
                + jnp.dot(xl, wrh_ref[...], preferred_element_type=f32)
                + jnp.dot(xh, wrl_ref[...], preferred_element_type=f32)) + br_ref[...]

    def route(r, logits, cnt):
        lane = lax.broadcasted_iota(jnp.int32, logits.shape, 1)
        tops, idxs = [], []
        cur = logits
        for _k in range(TOP_K):
            mk = jnp.max(cur, axis=-1, keepdims=True)
            ik = jnp.min(jnp.where(cur == mk, lane, LANES), axis=-1, keepdims=True)
            tops.append(mk)
            idxs.append(ik)
            cur = jnp.where(lane == ik, -jnp.inf, cur)
        es = [jnp.exp(t - tops[0]) for t in tops]
        den = es[0] + es[1] + es[2] + es[3]
        onehots = [(lane == ik).astype(f32) for ik in idxs]
        oh = onehots[0] + onehots[1] + onehots[2] + onehots[3]
        r_i = lax.broadcasted_iota(jnp.int32, (rows, rows), 0)
        c_i = lax.broadcasted_iota(jnp.int32, (rows, rows), 1)
        before = jnp.dot((c_i < r_i).astype(bf16), oh.astype(bf16), preferred_element_type=f32) + cnt
        exp_out = jnp.zeros(logits.shape, jnp.int32)
        gate_out = jnp.zeros(logits.shape, f32)
        rank_out = jnp.zeros(logits.shape, jnp.int32)
        for k in range(TOP_K):
            rk = jnp.sum(onehots[k] * before, axis=-1, keepdims=True)
            exp_out = jnp.where(lane == k, idxs[k], exp_out)
            gate_out = jnp.where(lane == k, es[k] / den, gate_out)
            rank_out = jnp.where(lane == k, rk.astype(jnp.int32), rank_out)
        exp_ref[r, :] = exp_out
        gate_ref[r, :] = gate_out
        rank_ref[r, :] = rank_out
        return cnt + jnp.sum(oh, axis=0, keepdims=True)

    whole = pl.ds(0, tm)
    cnt = route(whole, project(whole), cnt_s[...])
    cnt_s[...] = cnt
    cnt_ref[...] = cnt


def _merge(x_p, cv_p, at_p, x_s, cv_s, at_s, g_out_conv, g_out_attn, w_out, g_ffn, w_router, b_router, tm):
    (n_p, d_model), n_s = x_p.shape, x_s.shape[0]
    n = n_p + n_s
    d_conv, d_attn = cv_p.shape[1], at_p.shape[1]
    woc, woa = w_out[:d_conv].astype(bf16), w_out[d_conv:].astype(bf16)
    wr = jnp.pad(w_router, ((0, 0), (0, LANES - N_EXPERTS)))
    wrh, wrl = _split_bf16(wr)
    br = jnp.concatenate([b_router.astype(f32), jnp.full((LANES - N_EXPERTS,), -jnp.inf, f32)])[None, :]
    gc, ga, gf = g_out_conv[None, :], g_out_attn[None, :], g_ffn[None, :]
    tiles_p, tiles_s = n_p // tm, n_s // tm
    full = lambda a: pl.BlockSpec(a.shape, lambda i: (0, 0))
    rows = lambda w: pl.BlockSpec((tm, w), lambda i: (i, 0))
    rows_p = lambda w: pl.BlockSpec((tm, w), lambda i: (jnp.minimum(i, tiles_p - 1), 0))
    rows_s = lambda w: pl.BlockSpec((tm, w), lambda i: (jnp.maximum(i - tiles_p, 0), 0))
    assert d_model == 4 * PACK_W
    out_shape = (jax.ShapeDtypeStruct((n, d_model), f32),
                 jax.ShapeDtypeStruct((n, PACK_W), u32), jax.ShapeDtypeStruct((n, PACK_W), u32),
                 jax.ShapeDtypeStruct((n, LANES), jnp.int32), jax.ShapeDtypeStruct((n, LANES), f32),
                 jax.ShapeDtypeStruct((n, LANES), jnp.int32), jax.ShapeDtypeStruct((1, LANES), f32))
    return pl.pallas_call(
        functools.partial(_merge_kernel, n_prompt_tiles=tiles_p),
        grid=(tiles_p + tiles_s,),
        in_specs=[rows_p(d_model), rows_p(d_conv), rows_p(d_attn), rows_s(d_model), rows_s(d_conv), rows_s(d_attn),
                  full(gc), full(ga), full(woc), full(woa), full(gf), full(wrh), full(wrl), full(br)],
        out_specs=(rows(d_model), rows(PACK_W), rows(PACK_W), rows(LANES), rows(LANES), rows(LANES),
                   pl.BlockSpec((1, LANES), lambda i: (0, 0))),
        out_shape=out_shape,
        scratch_shapes=[pltpu.VMEM((1, LANES), f32)],
        compiler_params=_cparams("arbitrary"),
        name="merge",
    )(x_p, cv_p, at_p, x_s, cv_s, at_s, gc, ga, woc, woa, gf, wrh, wrl, br)


def _experts_kernel(be_ref, br_ref, xa_ref, xb_ref, wgu_ref, bgu_ref, wd_ref, bd_ref, ya_ref, yb_ref, wgu_s, wd_s):
    i = pl.program_id(0)
    d_expert = wd_ref.shape[1]
    blk = xa_ref.shape[0]

    @pl.when((i == 0) | (be_ref[i] != be_ref[jnp.maximum(i - 1, 0)]))
    def _():
        wgu_s[...] = wgu_ref[0].astype(bf16)
        wd_s[...] = wd_ref[0].astype(bf16)

    @pl.when(br_ref[i] > 0)
    def _():
        x = jnp.concatenate(_unpack_bf16_pairs(xa_ref[...]) + _unpack_bf16_pairs(xb_ref[...]), axis=1)
        live = lax.broadcasted_iota(jnp.int32, (blk, 1), 0) < br_ref[i]
        x = jnp.where(live, x, 0.0).astype(bf16)
        gu = jnp.dot(x, wgu_s[...], preferred_element_type=f32) + bgu_ref[0]
        gate = jnp.minimum(gu[:, :d_expert], SWIGLU_LIMIT)
        up = jnp.clip(gu[:, d_expert:], -SWIGLU_LIMIT, SWIGLU_LIMIT)
        hid = (up + 1.0) * gate * jax.nn.sigmoid(SWIGLU_ALPHA * gate)
        y = jnp.dot(hid.astype(bf16), wd_s[...], preferred_element_type=f32) + bd_ref[0]
        ya_ref[...], yb_ref[...] = _pack_bf16_pairs(y)

    @pl.when(br_ref[i] <= 0)
    def _():
        ya_ref[...] = jnp.zeros(ya_ref.shape, ya_ref.dtype)
        yb_ref[...] = jnp.zeros(yb_ref.shape, yb_ref.dtype)


def _experts(block_expert, block_rows, xs_a, xs_b, w_gate_up, b_gate_up, w_down, b_down, blk):
    n_slots = xs_a.shape[0]
    n_exp, d_model, d_gu = w_gate_up.shape
    d_expert = w_down.shape[1]
    slots = pl.BlockSpec((blk, PACK_W), lambda i, be, br: (i, 0))
    grid_spec = pltpu.PrefetchScalarGridSpec(
        num_scalar_prefetch=2,
        grid=(n_slots // blk,),
        in_specs=[slots, slots,
                  pl.BlockSpec((1, d_model, d_gu), lambda i, be, br: (be[i], 0, 0)),
                  pl.BlockSpec((1, 1, d_gu), lambda i, be, br: (be[i], 0, 0)),
                  pl.BlockSpec((1, d_expert, d_model), lambda i, be, br: (be[i], 0, 0)),
                  pl.BlockSpec((1, 1, d_model), lambda i, be, br: (be[i], 0, 0))],
        out_specs=(slots, slots),
        scratch_shapes=[pltpu.VMEM((d_model, d_gu), bf16), pltpu.VMEM((d_expert, d_model), bf16)],
    )
    packed = jax.ShapeDtypeStruct((n_slots, PACK_W), u32)
    return pl.pallas_call(
        _experts_kernel,
        grid_spec=grid_spec,
        out_shape=(packed, packed),
        compiler_params=_cparams("arbitrary"),
        name="experts",
    )(block_expert, block_rows, xs_a, xs_b, w_gate_up, b_gate_up[:, None, :], w_down, b_down[:, None, :])


def _combine_kernel(h_ref, ya_ref, yb_ref, gate_ref, yp_ref, ys_ref, *, n_prompt_tiles):
    i = pl.program_id(0)
    g = gate_ref[...]
    acc = None
    for k in range(TOP_K):
        gk = g[:, k:k + 1]
        pieces = [p * gk for p in _unpack_bf16_pairs(ya_ref[k]) + _unpack_bf16_pairs(yb_ref[k])]
        acc = pieces if acc is None else [a + p for a, p in zip(acc, pieces)]
    y = h_ref[...] + jnp.concatenate(acc, axis=1)

    @pl.when(i < n_prompt_tiles)
    def _():
        yp_ref[...] = y

    @pl.when(i >= n_prompt_tiles)
    def _():
        ys_ref[...] = y


def _combine(h, ya, yb, gates, n_p, tm):
    n, d_model = h.shape
    tiles_p = n_p // tm
    packed = pl.BlockSpec((TOP_K, tm, PACK_W), lambda i: (0, i, 0))
    return pl.pallas_call(
        functools.partial(_combine_kernel, n_prompt_tiles=tiles_p),
        grid=(n // tm,),
        in_specs=[pl.BlockSpec((tm, d_model), lambda i: (i, 0)), packed, packed,
                  pl.BlockSpec((tm, LANES), lambda i: (i, 0))],
        out_specs=(pl.BlockSpec((tm, d_model), lambda i: (jnp.minimum(i, tiles_p - 1), 0)),
                   pl.BlockSpec((tm, d_model), lambda i: (jnp.maximum(i - tiles_p, 0), 0))),
        out_shape=(jax.ShapeDtypeStruct((n_p, d_model), f32), jax.ShapeDtypeStruct((n - n_p, d_model), f32)),
        compiler_params=_cparams("arbitrary"),
        name="combine",
    )(h, ya, yb, gates)


SC_WINDOW = 128


def _sc_mesh():
    return plsc.VectorSubcoreMesh(core_axis_name="core", subcore_axis_name="subcore")


def _sc_scatter_rows(x, idx, n_out):
    n, w = x.shape
    n_k = idx.shape[0]

    @pl.kernel(out_type=jax.ShapeDtypeStruct((n_out, w), x.dtype), mesh=_sc_mesh(), scratch_types=[])
    def scatter(x_hbm, i_hbm, o_hbm):
        def body(x_vmem, *i_vmem):
            for iv in i_vmem:
                pltpu.sync_copy(x_vmem, o_hbm.at[iv.at[0]])

        pltpu.emit_pipeline(
            body,
            grid=(n // SC_WINDOW,),
            in_specs=[pl.BlockSpec((SC_WINDOW, w), lambda i: (i, 0))]
            + [pl.BlockSpec((1, SC_WINDOW), functools.partial(lambda k, i: (k, i), k)) for k in range(n_k)],
            out_specs=[],
            core_axis_name=("core", "subcore"),
            dimension_semantics=(pltpu.PARALLEL,),
        )(x_hbm, *([i_hbm] * n_k))

    return scatter(x, idx)


def _sc_gather_rows(x, idx):
    n_idx, w = idx.shape[0], x.shape[1]

    @pl.kernel(out_type=jax.ShapeDtypeStruct((n_idx, w), x.dtype), mesh=_sc_mesh(), scratch_types=[])
    def gather(x_hbm, i_hbm, o_hbm):
        def body(i_vmem, o_vmem):
            pltpu.sync_copy(x_hbm.at[i_vmem.at[0]], o_vmem)

        pltpu.emit_pipeline(
            body,
            grid=(n_idx // SC_WINDOW,),
            in_specs=[pl.BlockSpec((1, SC_WINDOW), lambda i: (0, i))],
            out_specs=[pl.BlockSpec((SC_WINDOW, w), lambda i: (i, 0))],
            core_axis_name=("core", "subcore"),
            dimension_semantics=(pltpu.PARALLEL,),
        )(i_hbm, o_hbm)

    return gather(x, idx.reshape(1, n_idx))


EXPERT_BLK = 512


def _moe(h, xn_a, xn_b, expert, gates, rank, counts, n_p, w_gate_up, b_gate_up, w_down, b_down):
    n = h.shape[0]
    blk = EXPERT_BLK
    padded = (counts + blk - 1) // blk * blk
    padded_end = jnp.cumsum(padded)
    start_padded = padded_end - padded
    expert_t, slot_t = expert.T, rank.T
    for e in range(N_EXPERTS):
        slot_t = slot_t + jnp.where(expert_t == e, start_padded[e], 0)
    n_blocks = -(-(n * TOP_K + N_EXPERTS * (blk - 1)) // blk)
    n_slots = n_blocks * blk
    block_start = jnp.arange(n_blocks, dtype=jnp.int32) * blk
    block_expert = jnp.minimum(jnp.sum((block_start[:, None] >= padded_end[None, :]).astype(jnp.int32), axis=1),
                               N_EXPERTS - 1)
    block_rows = jnp.clip(counts[block_expert] - (block_start - start_padded[block_expert]), 0, blk)
    xs_a = _sc_scatter_rows(xn_a, slot_t, n_slots)
    xs_b = _sc_scatter_rows(xn_b, slot_t, n_slots)
    ys_a, ys_b = _experts(block_expert, block_rows, xs_a, xs_b, w_gate_up, b_gate_up, w_down, b_down, blk)
    flat = slot_t.reshape(-1)
    ya = _sc_gather_rows(ys_a, flat).reshape(TOP_K, n, PACK_W)
    yb = _sc_gather_rows(ys_b, flat).reshape(TOP_K, n, PACK_W)
    return _combine(h, ya, yb, gates, n_p, 512)


def _layer(x_p, x_s, cache_c, cache_kr, state, page_table, norm_mix, w_in, norm_q_lat, w_uq, norm_kv_lat, w_ukv,
           norm_qk_q, norm_qk_k, conv_w, conv_b, conv_ln_g, conv_ln_b, norm_out_conv, norm_out_attn, w_out,
           norm_ffn, w_router, b_router, w_gate_up, b_gate_up, w_down, b_down):
    n_b, t_p, d_model = x_p.shape
    n_seq, t_s, _ = x_s.shape
    d_conv = conv_w.shape[1]
    q_rank, kv_rank = norm_q_lat.shape[0], norm_kv_lat.shape[0]
    n_past = page_table.shape[1] * PAGE_SIZE
    win, wuq, wk, wvt = _mix_in_weights(w_in, w_uq, w_ukv, d_conv, q_rank, kv_rank)
    mix = functools.partial(_mix_in, g_mix=norm_mix, win=win, g_q_lat=norm_q_lat, wuq=wuq, g_kv_lat=norm_kv_lat,
                            g_qk_q=norm_qk_q, wk=wk, wvt=wvt, g_qk_k=norm_qk_k,
                            d_conv=d_conv, q_rank=q_rank, kv_rank=kv_rank)
    x_p2, x_s2 = x_p.reshape(n_b * t_p, d_model), x_s.reshape(n_seq * t_s, d_model)
    tab_p = _rope_head_tables(jnp.arange(t_p))
    tab_s = _rope_head_tables(n_past + jnp.arange(n_seq * t_s) % t_s)
    glu_p, q_p, c_p, kr_p, k_p, vt_p = mix(x_p2, tab_p, MIX_TILE, seq_len=t_p, q_scale=ATTN_SCALE * LOG2_E)
    glu_s, q_s, c_s, kr_s, _, _ = mix(x_s2, tab_s, MIX_TILE, seq_len=n_seq * t_s, q_scale=ATTN_SCALE)

    glu_p3, glu_s3 = glu_p.reshape(n_b, t_p, d_conv), glu_s.reshape(n_seq, t_s, d_conv)
    cv_p = _conv_prompt(glu_p3, conv_w, conv_b, conv_ln_g, conv_ln_b)
    cv_s = _conv_sample(glu_s3, state, conv_w, conv_b, conv_ln_g, conv_ln_b)

    hp = N_HEADS * HEAD_PAD
    at_p = _attn_prompt(q_p.reshape(n_b, t_p, hp), k_p.reshape(n_b, t_p, hp), vt_p, 512, 512)
    c_s3, kr_s3 = c_s.reshape(n_seq, t_s, kv_rank), kr_s.reshape(n_seq, t_s, QK_ROPE_DIM)
    at_s = _attn_sample_all(q_s, c_s3, kr_s3, page_table, cache_c, cache_kr, w_ukv, norm_qk_k, 32)

    h, xn_a, xn_b, expert, gates, rank, cnt = _merge(
        x_p2, cv_p.reshape(n_b * t_p, d_conv), at_p.reshape(n_b * t_p, -1),
        x_s2, cv_s.reshape(n_seq * t_s, d_conv), at_s,
        norm_out_conv, norm_out_attn, w_out, norm_ffn, w_router, b_router, 512)
    counts = cnt[0, :N_EXPERTS].astype(jnp.int32)
    y_p, y_s = _moe(h, xn_a, xn_b, expert[:, :TOP_K], gates, rank[:, :TOP_K], counts, n_b * t_p,
                    w_gate_up, b_gate_up, w_down, b_down)
    n_hist = CONV_WIDTH - 1
    conv_state_p = jnp.concatenate([jnp.zeros((n_b, max(n_hist - t_p, 0), d_conv), f32),
                                    glu_p3[:, max(t_p - n_hist, 0):]], axis=1)
    conv_state_s = jnp.concatenate([state, glu_s3], axis=1)[:, -n_hist:]
    return (y_p.reshape(x_p.shape), y_s.reshape(x_s.shape), c_p.reshape(n_b, t_p, kv_rank),
            kr_p.reshape(n_b, t_p, QK_ROPE_DIM), conv_state_p, c_s3, kr_s3, conv_state_s)


def kernel(x_prompt, x_sample, cache_kv_latent, cache_k_rope, state_conv, page_table, norm_mix, w_in, norm_q_lat,
           w_uq, norm_kv_lat, w_ukv, norm_qk_q, norm_qk_k, conv_w, conv_b, conv_ln_g, conv_ln_b, norm_out_conv,
           norm_out_attn, w_out, norm_ffn, w_router, b_router, w_gate_up, b_gate_up, w_down, b_down):
    h_p, h_s = x_prompt, x_sample
    per_layer = []
    for l in range(w_in.shape[0]):
        outs = _layer(h_p, h_s, cache_kv_latent[l], cache_k_rope[l], state_conv[l], page_table, norm_mix[l],
                      w_in[l], norm_q_lat[l], w_uq[l], norm_kv_lat[l], w_ukv[l], norm_qk_q[l], norm_qk_k[l],
                      conv_w[l], conv_b[l], conv_ln_g[l], conv_ln_b[l], norm_out_conv[l], norm_out_attn[l],
                      w_out[l], norm_ffn[l], w_router[l], b_router[l], w_gate_up[l], b_gate_up[l], w_down[l],
                      b_down[l])
        h_p, h_s = outs[0], outs[1]
        per_layer.append(outs[2:])
    stacked = [jnp.stack([pl_[i] for pl_ in per_layer]) for i in range(6)]
    return (h_p, h_s, *stacked)
```

```python
import functools

import jax
import jax.numpy as jnp
import numpy as np
from jax import lax
from jax.experimental import pallas as pl
from jax.experimental.pallas import tpu as pltpu
from jax.experimental.pallas import tpu_sc as plsc

N_HEADS = 8
QK_NOPE_DIM = 64
QK_ROPE_DIM = 32
ROPE_HALF = QK_ROPE_DIM // 2
QK_DIM = QK_NOPE_DIM + QK_ROPE_DIM
V_DIM = 64
CONV_WIDTH = 31
N_EXPERTS = 32
TOP_K = 4
SWIGLU_LIMIT = 7.0
SWIGLU_ALPHA = 1.702
ROPE_THETA = 10000.0
NORM_EPS = 1e-6
ATTN_SCALE = QK_DIM ** -0.5
LOG2_E = 1.4426950408889634
PAGE_SIZE = 128

LANES = 128
HEAD_PAD = LANES
VMEM_LIMIT = 56 * 1024 * 1024

f32 = jnp.float32
bf16 = jnp.bfloat16


def _cparams(*sem):
    return pltpu.CompilerParams(dimension_semantics=sem, vmem_limit_bytes=VMEM_LIMIT)


def _rope_head_tables(pos):
    inv_freq = ROPE_THETA ** (-jnp.arange(ROPE_HALF, dtype=f32) / ROPE_HALF)
    ang = pos.astype(f32)[:, None] * inv_freq[None, :]
    cos, sin = jnp.cos(ang), jnp.sin(ang)
    n = pos.shape[0]
    z = lambda w: jnp.zeros((n, w), f32)
    c = jnp.concatenate([jnp.ones((n, QK_NOPE_DIM), f32), cos, cos, z(32)], axis=1)
    s1 = jnp.concatenate([z(QK_NOPE_DIM + ROPE_HALF), sin, z(32)], axis=1)
    s2 = jnp.concatenate([z(QK_NOPE_DIM), -sin, z(ROPE_HALF + 32)], axis=1)
    return jnp.concatenate([c, s1, s2], axis=1)


def _rope_head(x, tab):
    c, s1, s2 = tab[:, :LANES], tab[:, LANES:2 * LANES], tab[:, 2 * LANES:]
    return x * c + pltpu.roll(x, ROPE_HALF, 1) * s1 + pltpu.roll(x, LANES - ROPE_HALF, 1) * s2


def _mix_in_kernel(x_ref, tab_ref, gmix_ref, win_ref, gq_ref, wuq_ref, gkv_ref, gqq_ref,
                   wk_ref, wvt_ref, gqk_ref,
                   glu_ref, q_ref, c_ref, kr_ref, k_ref, vt_ref, *, d_conv, q_rank, kv_rank, q_scale):
    tm = x_ref.shape[0]
    rows = tm // MIX_SUB_TILES
    gqq = gqq_ref[...] * q_scale
    gqk = gqk_ref[...]

    def project(r):
        x = x_ref[r, :]
        xn = x * lax.rsqrt(jnp.mean(x * x, axis=-1, keepdims=True) + NORM_EPS) * gmix_ref[...]
        return jnp.dot(xn.astype(bf16), win_ref[...], preferred_element_type=f32)

    def latents(r, proj):
        o = 0
        a = proj[:, o:o + d_conv]; o += d_conv
        gt = proj[:, o:o + d_conv]; o += d_conv
        q_lat = proj[:, o:o + q_rank]; o += q_rank
        kv_lat = proj[:, o:o + kv_rank]; o += kv_rank
        kr_raw = proj[:, o:o + LANES]
        glu_ref[r, :] = a * jax.nn.sigmoid(gt)
        tab = tab_ref[r, :]
        qn = q_lat * lax.rsqrt(jnp.mean(q_lat * q_lat, axis=-1, keepdims=True) + NORM_EPS) * gq_ref[...]
        q = jnp.dot(qn.astype(bf16), wuq_ref[...], preferred_element_type=f32)
        c_kv = kv_lat * lax.rsqrt(jnp.mean(kv_lat * kv_lat, axis=-1, keepdims=True) + NORM_EPS) * gkv_ref[...]
        c_ref[r, :] = c_kv
        kr = _rope_head(kr_raw, tab)
        kr_ref[r, :] = kr[:, QK_NOPE_DIM:QK_NOPE_DIM + QK_ROPE_DIM]
        c_bf = c_kv.astype(bf16)
        kn = jnp.dot(c_bf, wk_ref[...], preferred_element_type=f32)
        vt_ref[0, :, r] = lax.dot_general(wvt_ref[...], c_bf, NT_DIMS,
                                          preferred_element_type=f32).astype(vt_ref.dtype)
        return q, kn, kr, tab

    def heads(r, q, kn, kr, tab):
        for h in range(N_HEADS):
            sl = slice(h * HEAD_PAD, (h + 1) * HEAD_PAD)
            qh = _rope_head(q[:, sl], tab)
            qh = qh * lax.rsqrt(jnp.sum(qh * qh, axis=-1, keepdims=True) * (1.0 / QK_DIM) + NORM_EPS) * gqq
            q_ref[r, sl] = qh.astype(q_ref.dtype)
            kh = kn[:, sl] + kr
            kh = kh * lax.rsqrt(jnp.sum(kh * kh, axis=-1, keepdims=True) * (1.0 / QK_DIM) + NORM_EPS) * gqk
            k_ref[r, sl] = kh.astype(k_ref.dtype)

    slices = [pl.ds(i * rows, rows) for i in range(MIX_SUB_TILES)]
    proj = project(slices[0])
    for i, r in enumerate(slices):
        nxt = project(slices[i + 1]) if i + 1 < len(slices) else None
        heads(r, *latents(r, proj))
        proj = nxt


MIX_TILE = 1024
MIX_SUB_TILES = 4


def _head_gain(g_pairs):
    g_rot = g_pairs[QK_NOPE_DIM:]
    return jnp.concatenate([g_pairs[:QK_NOPE_DIM], g_rot, g_rot, jnp.zeros((32,), f32)])[None, :]


def _mix_in_weights(w_in, w_uq, w_ukv, d_conv, q_rank, kv_rank):
    d_model = w_in.shape[0]
    base = 2 * d_conv + q_rank + kv_rank
    w_kr = w_in[:, base:base + QK_ROPE_DIM]
    w_kr_pad = jnp.concatenate([jnp.zeros((d_model, QK_NOPE_DIM), f32), w_kr,
                                jnp.zeros((d_model, 32), f32)], axis=1)
    win = jnp.concatenate([w_in[:, :base], w_kr_pad], axis=1).astype(bf16)
    wuq = jnp.pad(w_uq, ((0, 0), (0, 0), (0, HEAD_PAD - QK_DIM))).reshape(q_rank, N_HEADS * HEAD_PAD).astype(bf16)
    wk = jnp.pad(w_ukv[:, :, :QK_NOPE_DIM], ((0, 0), (0, 0), (0, HEAD_PAD - QK_NOPE_DIM)))
    wk = wk.reshape(kv_rank, N_HEADS * HEAD_PAD).astype(bf16)
    wvt = w_ukv[:, :, QK_NOPE_DIM:].reshape(kv_rank, N_HEADS * V_DIM).T.astype(bf16)
    return win, wuq, wk, wvt


def _mix_in(x2d, tab, tm, g_mix, win, g_q_lat, wuq, g_kv_lat, g_qk_q, wk, wvt, g_qk_k, d_conv, q_rank, kv_rank,
            seq_len, q_scale):
    n, d_model = x2d.shape
    n_tab_blocks = tab.shape[0] // tm
    seq_tiles = seq_len // tm
    row = lambda i: (i, 0)
    const = lambda i: (0, 0)
    full = lambda a: pl.BlockSpec(a.shape, const)
    gm, gq, gkv = g_mix[None, :], g_q_lat[None, :], g_kv_lat[None, :]
    gqq, gqk = _head_gain(g_qk_q), _head_gain(g_qk_k)
    hp = N_HEADS * HEAD_PAD
    out_shape = (
        jax.ShapeDtypeStruct((n, d_conv), f32),
        jax.ShapeDtypeStruct((n, hp), bf16),
        jax.ShapeDtypeStruct((n, kv_rank), f32),
        jax.ShapeDtypeStruct((n, QK_ROPE_DIM), f32),
        jax.ShapeDtypeStruct((n, hp), bf16),
        jax.ShapeDtypeStruct((n // seq_len, N_HEADS * V_DIM, seq_len), bf16),
    )
    out_specs = [pl.BlockSpec((tm, s.shape[1]), row) for s in out_shape[:-1]]
    out_specs.append(pl.BlockSpec((1, N_HEADS * V_DIM, tm), lambda i: (i // seq_tiles, 0, i % seq_tiles)))
    return pl.pallas_call(
        functools.partial(_mix_in_kernel, d_conv=d_conv, q_rank=q_rank, kv_rank=kv_rank, q_scale=q_scale),
        grid=(n // tm,),
        in_specs=[pl.BlockSpec((tm, d_model), row),
                  pl.BlockSpec((tm, tab.shape[1]), lambda i: (i % n_tab_blocks, 0)),
                  full(gm), full(win), full(gq), full(wuq), full(gkv), full(gqq), full(wk), full(wvt), full(gqk)],
        out_specs=tuple(out_specs),
        out_shape=out_shape,
        compiler_params=_cparams("parallel"),
        name="mix_in",
    )(x2d, tab, gm, win, gq, wuq, gkv, gqq, wk, wvt, gqk)


CONV_PAD = 32


SUBLANES = 8


def _conv_taps(window, w_ref, b_ref, ls, tc):
    lead = CONV_PAD - (CONV_WIDTH - 1)
    acc = jnp.zeros((tc, LANES), f32) + b_ref[:, ls]
    for r in range(SUBLANES):
        taps = [(a, a * SUBLANES + r - lead) for a in range(CONV_PAD // SUBLANES + 1)
                if 0 <= a * SUBLANES + r - lead < CONV_WIDTH]
        shifted = window if r == 0 else pltpu.roll(window, window.shape[0] - r, 0)
        for a, j in taps:
            acc = acc + shifted[a * SUBLANES:a * SUBLANES + tc] * w_ref[j:j + 1, ls]
    return acc


def _ln_swish(y, g_ref, beta_ref):
    mu = jnp.mean(y, axis=-1, keepdims=True)
    yc = y - mu
    var = jnp.mean(yc * yc, axis=-1, keepdims=True)
    z = yc * lax.rsqrt(var + NORM_EPS) * g_ref[...] + beta_ref[...]
    return z * jax.nn.sigmoid(z)


def _conv_prompt_kernel(x_ref, w_ref, b_ref, g_ref, beta_ref, o_ref, *, tc):
    t_len, n_ch = x_ref.shape[1], x_ref.shape[2]

    def chunk(idx, carry):
        t0 = pl.multiple_of(idx * tc, tc)
        h0 = pl.multiple_of(jnp.maximum(t0 - CONV_PAD, 0), SUBLANES)
        has_hist = jnp.where(idx > 0, 1.0, 0.0)
        for lt in range(n_ch // LANES):
            ls = slice(lt * LANES, (lt + 1) * LANES)
            window = jnp.concatenate([x_ref[0, pl.ds(h0, CONV_PAD), ls] * has_hist,
                                      x_ref[0, pl.ds(t0, tc), ls]], axis=0)
            o_ref[0, pl.ds(t0, tc), ls] = _conv_taps(window, w_ref, b_ref, ls, tc)
        o_ref[0, pl.ds(t0, tc), :] = _ln_swish(o_ref[0, pl.ds(t0, tc), :], g_ref, beta_ref)
        return carry

    lax.fori_loop(0, t_len // tc, chunk, 0)


def _conv_sample_kernel(x_ref, st_ref, w_ref, b_ref, g_ref, beta_ref, o_ref):
    bb, tc, n_ch = x_ref.shape

    def one(b, carry):
        for lt in range(n_ch // LANES):
            ls = slice(lt * LANES, (lt + 1) * LANES)
            window = jnp.concatenate([st_ref[b, :, ls], x_ref[b, :, ls]], axis=0)
            o_ref[b, :, ls] = _conv_taps(window, w_ref, b_ref, ls, tc)
        o_ref[b] = _ln_swish(o_ref[b], g_ref, beta_ref)
        return carry

    lax.fori_loop(0, bb, one, 0)


def _conv_specs(conv_w, n_ch):
    const = lambda i: (0, 0)
    vec = pl.BlockSpec((1, n_ch), const)
    return [pl.BlockSpec(conv_w.shape, const), vec, vec, vec]


def _conv_prompt(glu, conv_w, conv_b, ln_g, ln_b, tc=128):
    n_b, t_len, n_ch = glu.shape
    blk = pl.BlockSpec((1, t_len, n_ch), lambda i: (i, 0, 0))
    return pl.pallas_call(
        functools.partial(_conv_prompt_kernel, tc=tc),
        grid=(n_b,),
        in_specs=[blk] + _conv_specs(conv_w, n_ch),
        out_specs=blk,
        out_shape=jax.ShapeDtypeStruct(glu.shape, f32),
        compiler_params=_cparams("parallel"),
        name="conv_prompt",
    )(glu, conv_w, conv_b[None, :], ln_g[None, :], ln_b[None, :])


def _conv_sample(glu, state, conv_w, conv_b, ln_g, ln_b, bb=32):
    n_b, t_len, n_ch = glu.shape
    hist = jnp.pad(state, ((0, 0), (CONV_PAD - state.shape[1], 0), (0, 0)))
    return pl.pallas_call(
        _conv_sample_kernel,
        grid=(n_b // bb,),
        in_specs=[pl.BlockSpec((bb, t_len, n_ch), lambda i: (i, 0, 0)),
                  pl.BlockSpec((bb, CONV_PAD, n_ch), lambda i: (i, 0, 0))] + _conv_specs(conv_w, n_ch),
        out_specs=pl.BlockSpec((bb, t_len, n_ch), lambda i: (i, 0, 0)),
        out_shape=jax.ShapeDtypeStruct(glu.shape, f32),
        compiler_params=_cparams("parallel"),
        name="conv_sample",
    )(glu, hist, conv_w, conv_b[None, :], ln_g[None, :], ln_b[None, :])


NT_DIMS = (((1,), (1,)), ((), ()))


def _attn_prompt_kernel(q_ref, k_ref, vt_ref, o_ref, *, tq, tk):
    assert tq == tk
    qi = pl.program_id(2)
    n_h = q_ref.shape[2] // HEAD_PAD
    qs = [q_ref[0, :, h * HEAD_PAD:(h + 1) * HEAD_PAD] for h in range(n_h)]

    def block(k0, nk, q_lo, carry, masked):
        nq = tq - q_lo
        if masked:
            key = k0 + lax.broadcasted_iota(jnp.int32, (nk, nq), 0)
            qry = qi * tq + q_lo + lax.broadcasted_iota(jnp.int32, (nk, nq), 1)
            visible = key <= qry
        ones = jnp.ones((ONES_ROWS, nk), bf16)
        sts = []
        for h in range(n_h):
            k_h = k_ref[0, pl.ds(k0, nk), h * HEAD_PAD:(h + 1) * HEAD_PAD]
            sts.append(lax.dot_general(k_h, qs[h][q_lo:], NT_DIMS, preferred_element_type=f32))
        out = []
        for h, (m_all, acc_all) in enumerate(carry):
            m, acc = m_all[:, q_lo:], acc_all[:, q_lo:]
            st = jnp.where(visible, sts[h], -jnp.inf) if masked else sts[h]
            m_new = jnp.maximum(m, jnp.max(st, axis=0, keepdims=True))
            alpha = jnp.exp2(m - m_new)
            p = jnp.exp2(st - m_new).astype(bf16)
            vt_h = jnp.concatenate([vt_ref[0, h * V_DIM:(h + 1) * V_DIM, pl.ds(k0, nk)], ones], axis=0)
            acc = alpha * acc + jnp.dot(vt_h, p, preferred_element_type=f32)
            if q_lo:
                m_new = jnp.concatenate([m_all[:, :q_lo], m_new], axis=1)
                acc = jnp.concatenate([acc_all[:, :q_lo], acc], axis=1)
            out.append((m_new, acc))
        return tuple(out)

    carry = tuple((jnp.full((1, tq), -jnp.inf, f32), jnp.zeros((V_DIM + ONES_ROWS, tq), f32)) for _ in range(n_h))
    carry = lax.fori_loop(
        0, qi, lambda kb, c: block(pl.multiple_of(kb * tk, tk), tk, 0, c, False), carry)
    half = tk // 2
    d0 = pl.multiple_of(qi * tq, tq)
    carry = block(d0, half, 0, carry, True)
    carry = block(pl.multiple_of(d0 + half, half), half, half, carry, True)
    o_t = jnp.concatenate([acc[:V_DIM] / acc[V_DIM:V_DIM + 1] for (_, acc) in carry], axis=0)
    o_ref[0] = o_t.T


ONES_ROWS = 16


ATTN_HEAD_GROUP = 4


def _attn_prompt(q, k, vt, tq, tk):
    n_b, t_len, _ = q.shape
    g = ATTN_HEAD_GROUP
    return pl.pallas_call(
        functools.partial(_attn_prompt_kernel, tq=tq, tk=tk),
        grid=(n_b, N_HEADS // g, t_len // tq),
        in_specs=[pl.BlockSpec((1, tq, g * HEAD_PAD), lambda b, h, i: (b, i, h)),
                  pl.BlockSpec((1, t_len, g * HEAD_PAD), lambda b, h, i: (b, 0, h)),
                  pl.BlockSpec((1, g * V_DIM, t_len), lambda b, h, i: (b, h, 0))],
        out_specs=pl.BlockSpec((1, tq, g * V_DIM), lambda b, h, i: (b, i, h)),
        out_shape=jax.ShapeDtypeStruct((n_b, t_len, N_HEADS * V_DIM), f32),
        compiler_params=_cparams("parallel", "parallel", "arbitrary"),
        name="attn_prompt",
    )(q, k, vt)


N_TH = 64


def _q_absorb_kernel(q_ref, gk_ref, wukt_ref, sel_ref, qabs_ref, qr_ref):
    gk = gk_ref[...]
    kv_rank = wukt_ref.shape[-1]
    for h in range(N_HEADS):
        qg = (q_ref[:, h * HEAD_PAD:(h + 1) * HEAD_PAD].astype(f32) * gk).astype(bf16)
        qabs_ref[:, h * kv_rank:(h + 1) * kv_rank] = jnp.dot(
            qg, wukt_ref[h], preferred_element_type=f32).astype(qabs_ref.dtype)
        qr_ref[:, h * QK_ROPE_DIM:(h + 1) * QK_ROPE_DIM] = jnp.dot(
            qg, sel_ref[...], preferred_element_type=f32).astype(qr_ref.dtype)


def _q_absorb(q2d, g_qk_k, w_ukv):
    n = q2d.shape[0]
    kv_rank = w_ukv.shape[0]
    wukt = jnp.transpose(w_ukv[:, :, :QK_NOPE_DIM], (1, 2, 0))
    wukt = jnp.pad(wukt, ((0, 0), (0, HEAD_PAD - QK_NOPE_DIM), (0, 0))).astype(bf16)
    sel = jnp.zeros((HEAD_PAD, QK_ROPE_DIM), f32).at[
        QK_NOPE_DIM + jnp.arange(QK_ROPE_DIM), jnp.arange(QK_ROPE_DIM)].set(1.0).astype(bf16)
    gk = _head_gain(g_qk_k)
    full = lambda a: pl.BlockSpec(a.shape, lambda i: (0,) * a.ndim)
    return pl.pallas_call(
        _q_absorb_kernel,
        grid=(1,),
        in_specs=[full(q2d), full(gk), full(wukt), full(sel)],
        out_specs=(pl.BlockSpec((n, N_HEADS * kv_rank), lambda i: (0, 0)),
                   pl.BlockSpec((n, N_HEADS * QK_ROPE_DIM), lambda i: (0, 0))),
        out_shape=(jax.ShapeDtypeStruct((n, N_HEADS * kv_rank), bf16),
                   jax.ShapeDtypeStruct((n, N_HEADS * QK_ROPE_DIM), bf16)),
        compiler_params=_cparams("arbitrary"),
        name="q_absorb",
    )(q2d, gk, wukt, sel)


SUB_PAGES = 16


def _latent_products_t(lhs, c_bf):
    n = c_bf.shape[0]
    full = lax.dot_general(lhs, c_bf, NT_DIMS, preferred_element_type=f32)
    knt = full[N_TH:]
    ssq = jnp.sum((knt * knt).reshape(QK_NOPE_DIM, N_HEADS, n), axis=0)
    return full[:N_TH], ssq


def _scores_t(st_lat, ssq_nope, krt, qr):
    n = krt.shape[1]
    ssq = ssq_nope + jnp.sum(krt * krt, axis=0, keepdims=True)
    rinv = lax.rsqrt(ssq * (1.0 / QK_DIM) + NORM_EPS)
    st = st_lat + jnp.dot(qr, krt.astype(bf16), preferred_element_type=f32)
    return (st.reshape(N_TH // N_HEADS, N_HEADS, n) * rinv[None]).reshape(N_TH, n)


def _softmax_update_t(st, c_bf, state, mask=None):
    m, l, acc = state
    if mask is not None:
        st = jnp.where(mask, st, -jnp.inf)
    m_new = jnp.maximum(m, jnp.max(st, axis=1, keepdims=True))
    alpha = jnp.exp(m - m_new)
    p = jnp.exp(st - m_new)
    l = alpha * l + jnp.sum(p, axis=1, keepdims=True)
    acc = alpha * acc + jnp.dot(p.astype(bf16), c_bf, preferred_element_type=f32)
    return m_new, l, acc


def _attn_sample_kernel(pt_ref, qabs_ref, qr_ref, cnew_ref, krtnew_ref, wukt_ref, cache_c, cache_krt, o_ref,
                            cbuf, krbuf, sems, m_s, l_s, acc_s, *, pp):
    s, j = pl.program_id(0), pl.program_id(1)
    n_j = pl.num_programs(1)
    step = s * n_j + j
    slot = step % 2

    def page_copies(step_idx, buf_slot):
        copies = []
        for i in range(pp):
            page = pt_ref[step_idx * pp + i]
            copies.append(pltpu.make_async_copy(cache_c.at[page], cbuf.at[buf_slot, i], sems.at[0, buf_slot]))
            copies.append(pltpu.make_async_copy(cache_krt.at[page], krbuf.at[buf_slot, i], sems.at[1, buf_slot]))
        return copies

    @pl.when(step == 0)
    def _():
        for cp in page_copies(0, 0):
            cp.start()

    @pl.when(step + 1 < pl.num_programs(0) * n_j)
    def _():
        for cp in page_copies(step + 1, 1 - slot):
            cp.start()

    @pl.when(j == 0)
    def _():
        m_s[...] = jnp.full(m_s.shape, -jnp.inf, f32)
        l_s[...] = jnp.zeros(l_s.shape, f32)
        acc_s[...] = jnp.zeros(acc_s.shape, f32)

    for cp in page_copies(step, slot):
        cp.wait()

    qr = qr_ref[0]
    lhs = jnp.concatenate([qabs_ref[0], wukt_ref[...]], axis=0)
    sub = min(SUB_PAGES, pp)
    blocks = list(range(0, pp, sub))

    def load(b):
        c_bf = jnp.concatenate([cbuf[slot, i].astype(bf16) for i in range(b, b + sub)], axis=0)
        return (c_bf,) + _latent_products_t(lhs, c_bf)

    state = (m_s[...], l_s[...], acc_s[...])
    nxt = load(blocks[0])
    for n, b in enumerate(blocks):
        c_bf, st_lat, ssq = nxt
        if n + 1 < len(blocks):
            nxt = load(blocks[n + 1])
        krt = jnp.concatenate([krbuf[slot, i] for i in range(b, b + sub)], axis=1)
        state = _softmax_update_t(_scores_t(st_lat, ssq, krt, qr), c_bf, state)
    m_s[...], l_s[...], acc_s[...] = state

    @pl.when(j == n_j - 1)
    def _():
        c_bf = cnew_ref[0].astype(bf16)
        n_pad = c_bf.shape[0]
        key_t = lax.broadcasted_iota(jnp.int32, (N_TH, n_pad), 1)
        q_t = lax.broadcasted_iota(jnp.int32, (N_TH, n_pad), 0) // N_HEADS
        st = _scores_t(*_latent_products_t(lhs, c_bf), krtnew_ref[0], qr)
        m, l, acc = _softmax_update_t(st, c_bf, state, mask=key_t <= q_t)
        o_ref[0] = acc / l


def _attn_sample(page_table, qabs, qr, c_new, kr_new, cache_c, cache_kr, wuk_perm, pp):
    n_seq, n_pages = page_table.shape
    kv_rank = cache_c.shape[-1]
    t_new = c_new.shape[1]
    seq3 = lambda s, j, pt: (s, 0, 0)
    cache_kr_t = jnp.swapaxes(cache_kr, 1, 2)
    c_new_pad = jnp.pad(c_new, ((0, 0), (0, PAGE_SIZE - t_new), (0, 0)))
    krt_new_pad = jnp.pad(jnp.swapaxes(kr_new, 1, 2), ((0, 0), (0, 0), (0, PAGE_SIZE - t_new)))
    grid_spec = pltpu.PrefetchScalarGridSpec(
        num_scalar_prefetch=1,
        grid=(n_seq, n_pages // pp),
        in_specs=[pl.BlockSpec((1, N_TH, kv_rank), seq3),
                  pl.BlockSpec((1, N_TH, QK_ROPE_DIM), seq3),
                  pl.BlockSpec((1, PAGE_SIZE, kv_rank), seq3),
                  pl.BlockSpec((1, QK_ROPE_DIM, PAGE_SIZE), seq3),
                  pl.BlockSpec(wuk_perm.shape, lambda s, j, pt: (0, 0)),
                  pl.BlockSpec(memory_space=pl.ANY),
                  pl.BlockSpec(memory_space=pl.ANY)],
        out_specs=pl.BlockSpec((1, N_TH, kv_rank), seq3),
        scratch_shapes=[pltpu.VMEM((2, pp, PAGE_SIZE, kv_rank), f32),
                        pltpu.VMEM((2, pp, QK_ROPE_DIM, PAGE_SIZE), f32),
                        pltpu.SemaphoreType.DMA((2, 2)),
                        pltpu.VMEM((N_TH, 1), f32), pltpu.VMEM((N_TH, 1), f32),
                        pltpu.VMEM((N_TH, kv_rank), f32)],
    )
    return pl.pallas_call(
        functools.partial(_attn_sample_kernel, pp=pp),
        grid_spec=grid_spec,
        out_shape=jax.ShapeDtypeStruct((n_seq, N_TH, kv_rank), f32),
        compiler_params=_cparams("arbitrary", "arbitrary"),
        name="attn_sample",
    )(page_table.reshape(-1), qabs, qr, c_new_pad, krt_new_pad, wuk_perm, cache_c, cache_kr_t)


def _v_up_kernel(lat_ref, wuv_ref, o_ref):
    rows = lat_ref.shape[0]
    full = jnp.dot(lat_ref[...].astype(bf16), wuv_ref[...], preferred_element_type=f32)
    head_of_row = lax.broadcasted_iota(jnp.int32, full.shape, 0) % N_HEADS
    head_of_lane = lax.broadcasted_iota(jnp.int32, full.shape, 1) // V_DIM
    own = jnp.where(head_of_row == head_of_lane, full, 0.0)
    o_ref[...] = jnp.sum(own.reshape(rows // N_HEADS, N_HEADS, full.shape[1]), axis=1)


def _v_up(lat2d, wuv, rows):
    n, kv_rank = lat2d.shape
    d_attn = wuv.shape[1]
    return pl.pallas_call(
        _v_up_kernel,
        grid=(n // rows,),
        in_specs=[pl.BlockSpec((rows, kv_rank), lambda i: (i, 0)), pl.BlockSpec(wuv.shape, lambda i: (0, 0))],
        out_specs=pl.BlockSpec((rows // N_HEADS, d_attn), lambda i: (i, 0)),
        out_shape=jax.ShapeDtypeStruct((n // N_HEADS, d_attn), f32),
        compiler_params=_cparams("parallel"),
        name="v_up",
    )(lat2d, wuv)


def _attn_sample_all(q_s2d, c_new, kr_new, page_table, cache_c, cache_kr, w_ukv, g_qk_k, pp):
    n_seq, t_new, kv_rank = c_new.shape
    qabs, qr = _q_absorb(q_s2d, g_qk_k, w_ukv)
    qabs = qabs.reshape(n_seq, t_new * N_HEADS, kv_rank)
    qr = qr.reshape(n_seq, t_new * N_HEADS, QK_ROPE_DIM)
    wuk_perm = jnp.transpose(w_ukv[:, :, :QK_NOPE_DIM], (2, 1, 0)).reshape(-1, kv_rank).astype(bf16)
    lat = _attn_sample(page_table, qabs, qr, c_new, kr_new, cache_c, cache_kr, wuk_perm, pp)
    wuv = w_ukv[:, :, QK_NOPE_DIM:].reshape(kv_rank, N_HEADS * V_DIM).astype(bf16)
    rows = min(512, n_seq * N_TH)
    return _v_up(lat.reshape(n_seq * N_TH, kv_rank), wuv, rows)


def _rms(x, g):
    return x * lax.rsqrt(jnp.mean(x * x, axis=-1, keepdims=True) + NORM_EPS) * g


def _split_bf16(x):
    hi = x.astype(bf16)
    return hi, (x - hi.astype(f32)).astype(bf16)


PACK_W = 256
u32 = jnp.uint32


def _pack_bf16_pairs(x):
    bits = lax.bitcast_convert_type(x.astype(bf16).astype(f32), u32)
    pieces = []
    for p in range(x.shape[1] // (2 * PACK_W)):
        lo = bits[:, 2 * p * PACK_W:(2 * p + 1) * PACK_W]
        hi = bits[:, (2 * p + 1) * PACK_W:(2 * p + 2) * PACK_W]
        pieces.append((lo >> 16) | (hi & jnp.uint32(0xFFFF0000)))
    return pieces


def _unpack_bf16_pairs(words):
    return (lax.bitcast_convert_type(words << 16, f32),
            lax.bitcast_convert_type(words & jnp.uint32(0xFFFF0000), f32))


def _merge_kernel(xp_ref, cvp_ref, atp_ref, xs_ref, cvs_ref, ats_ref, gc_ref, ga_ref, woc_ref, woa_ref, gf_ref,
                  wrh_ref, wrl_ref, br_ref, h_ref, xna_ref, xnb_ref, exp_ref, gate_ref, rank_ref, cnt_ref, cnt_s,
                  *, n_prompt_tiles):
    i = pl.program_id(0)
    tm = xp_ref.shape[0]

    @pl.when(i == 0)
    def _():
        cnt_s[...] = jnp.zeros(cnt_s.shape, f32)

    is_p = i < n_prompt_tiles
    rows = tm

    def project(r):
        x = jnp.where(is_p, xp_ref[r, :], xs_ref[r, :])
        yc = _rms(jnp.where(is_p, cvp_ref[r, :], cvs_ref[r, :]), gc_ref[...]).astype(bf16)
        ya = _rms(jnp.where(is_p, atp_ref[r, :], ats_ref[r, :]), ga_ref[...]).astype(bf16)
        h = x + jnp.dot(yc, woc_ref[...], preferred_element_type=f32) \
            + jnp.dot(ya, woa_ref[...], preferred_element_type=f32)
        h_ref[r, :] = h
        xn = _rms(h, gf_ref[...])
        xh, xl = _split_bf16(xn)
        xna_ref[r, :], xnb_ref[r, :] = _pack_bf16_pairs(xn)
        return (jnp.dot(xh, wrh_ref[...], preferred_element_type=f32)
                + jnp.dot(xl, wrh_ref[...], preferred_element_type=f32)
                + jnp.dot(xh, wrl_ref[...], preferred_element_type=f32)) + br_ref[...]

    def route(logits, cnt):
        lt = logits.T[:N_EXPERTS]
        eidx = lax.broadcasted_iota(jnp.int32, lt.shape, 0)
        tops, idxs = [], []
        cur = lt
        for _k in range(TOP_K):
            mk = jnp.max(cur, axis=0, keepdims=True)
            ik = jnp.min(jnp.where(cur == mk, eidx, N_EXPERTS), axis=0, keepdims=True)
            tops.append(mk)
            idxs.append(ik)
            cur = jnp.where(eidx == ik, -jnp.inf, cur)
        es = [jnp.exp(t - tops[0]) for t in tops]
        den = es[0] + es[1] + es[2] + es[3]
        onehots = [(eidx == ik).astype(f32) for ik in idxs]
        oh = onehots[0] + onehots[1] + onehots[2] + onehots[3]
        s_i = lax.broadcasted_iota(jnp.int32, (rows, rows), 0)
        t_i = lax.broadcasted_iota(jnp.int32, (rows, rows), 1)
        before = jnp.dot(oh.astype(bf16), (s_i < t_i).astype(bf16), preferred_element_type=f32) + cnt
        ranks = [jnp.sum(oh_k * before, axis=0, keepdims=True).astype(jnp.int32) for oh_k in onehots]
        pad_i = jnp.zeros((ROUTE_ROWS - TOP_K, rows), jnp.int32)
        pad_f = jnp.zeros((ROUTE_ROWS - TOP_K, rows), f32)
        exp_ref[...] = jnp.concatenate(idxs + [pad_i], axis=0)
        gate_ref[...] = jnp.concatenate([e / den for e in es] + [pad_f], axis=0)
        rank_ref[...] = jnp.concatenate(ranks + [pad_i], axis=0)
        return cnt + jnp.sum(oh, axis=1, keepdims=True)

    cnt = route(project(pl.ds(0, tm)), cnt_s[...])
    cnt_s[...] = cnt
    cnt_ref[...] = cnt


ROUTE_ROWS = 8


def _merge(x_p, cv_p, at_p, x_s, cv_s, at_s, g_out_conv, g_out_attn, w_out, g_ffn, w_router, b_router, tm):
    (n_p, d_model), n_s = x_p.shape, x_s.shape[0]
    n = n_p + n_s
    d_conv, d_attn = cv_p.shape[1], at_p.shape[1]
    woc, woa = w_out[:d_conv].astype(bf16), w_out[d_conv:].astype(bf16)
    wr = jnp.pad(w_router, ((0, 0), (0, LANES - N_EXPERTS)))
    wrh, wrl = _split_bf16(wr)
    br = jnp.concatenate([b_router.astype(f32), jnp.full((LANES - N_EXPERTS,), -jnp.inf, f32)])[None, :]
    gc, ga, gf = g_out_conv[None, :], g_out_attn[None, :], g_ffn[None, :]
    tiles_p, tiles_s = n_p // tm, n_s // tm
    full = lambda a: pl.BlockSpec(a.shape, lambda i: (0, 0))
    rows = lambda w: pl.BlockSpec((tm, w), lambda i: (i, 0))
    rows_p = lambda w: pl.BlockSpec((tm, w), lambda i: (jnp.minimum(i, tiles_p - 1), 0))
    rows_s = lambda w: pl.BlockSpec((tm, w), lambda i: (jnp.maximum(i - tiles_p, 0), 0))
    assert d_model == 4 * PACK_W
    route = lambda dt: jax.ShapeDtypeStruct((ROUTE_ROWS, n), dt)
    route_blk = pl.BlockSpec((ROUTE_ROWS, tm), lambda i: (0, i))
    out_shape = (jax.ShapeDtypeStruct((n, d_model), f32),
                 jax.ShapeDtypeStruct((n, PACK_W), u32), jax.ShapeDtypeStruct((n, PACK_W), u32),
                 route(jnp.int32), route(f32), route(jnp.int32), jax.ShapeDtypeStruct((N_EXPERTS, 1), f32))
    return pl.pallas_call(
        functools.partial(_merge_kernel, n_prompt_tiles=tiles_p),
        grid=(tiles_p + tiles_s,),
        in_specs=[rows_p(d_model), rows_p(d_conv), rows_p(d_attn), rows_s(d_model), rows_s(d_conv), rows_s(d_attn),
                  full(gc), full(ga), full(woc), full(woa), full(gf), full(wrh), full(wrl), full(br)],
        out_specs=(rows(d_model), rows(PACK_W), rows(PACK_W), route_blk, route_blk, route_blk,
                   pl.BlockSpec((N_EXPERTS, 1), lambda i: (0, 0))),
        out_shape=out_shape,
        scratch_shapes=[pltpu.VMEM((N_EXPERTS, 1), f32)],
        compiler_params=_cparams("arbitrary"),
        name="merge",
    )(x_p, cv_p, at_p, x_s, cv_s, at_s, gc, ga, woc, woa, gf, wrh, wrl, br)


def _experts_kernel(be_ref, br_ref, xa_ref, xb_ref, wgu_ref, bgu_ref, wd_ref, bd_ref, ya_ref, yb_ref, wgu_s, wd_s):
    i = pl.program_id(0)
    d_expert = wd_ref.shape[1]
    blk = xa_ref.shape[0]

    @pl.when((i == 0) | (be_ref[i] != be_ref[jnp.maximum(i - 1, 0)]))
    def _():
        wgu_s[...] = wgu_ref[0].astype(bf16)
        wd_s[...] = wd_ref[0].astype(bf16)

    @pl.when(br_ref[i] > 0)
    def _():
        x = jnp.concatenate(_unpack_bf16_pairs(xa_ref[...]) + _unpack_bf16_pairs(xb_ref[...]), axis=1)
        live = lax.broadcasted_iota(jnp.int32, (blk, 1), 0) < br_ref[i]
        x = jnp.where(live, x, 0.0).astype(bf16)
        gu = jnp.dot(x, wgu_s[...], preferred_element_type=f32) + bgu_ref[0]
        gate = jnp.minimum(gu[:, :d_expert], SWIGLU_LIMIT)
        up = jnp.clip(gu[:, d_expert:], -SWIGLU_LIMIT, SWIGLU_LIMIT)
        hid = (up + 1.0) * gate * jax.nn.sigmoid(SWIGLU_ALPHA * gate)
        y = jnp.dot(hid.astype(bf16), wd_s[...], preferred_element_type=f32) + bd_ref[0]
        ya_ref[...], yb_ref[...] = _pack_bf16_pairs(y)

    @pl.when(br_ref[i] <= 0)
    def _():
        ya_ref[...] = jnp.zeros(ya_ref.shape, ya_ref.dtype)
        yb_ref[...] = jnp.zeros(yb_ref.shape, yb_ref.dtype)


def _experts(block_expert, block_rows, xs_a, xs_b, w_gate_up, b_gate_up, w_down, b_down, blk):
    n_slots = xs_a.shape[0]
    n_exp, d_model, d_gu = w_gate_up.shape
    d_expert = w_down.shape[1]
    slots = pl.BlockSpec((blk, PACK_W), lambda i, be, br: (i, 0))
    grid_spec = pltpu.PrefetchScalarGridSpec(
        num_scalar_prefetch=2,
        grid=(n_slots // blk,),
        in_specs=[slots, slots,
                  pl.BlockSpec((1, d_model, d_gu), lambda i, be, br: (be[i], 0, 0)),
                  pl.BlockSpec((1, 1, d_gu), lambda i, be, br: (be[i], 0, 0)),
                  pl.BlockSpec((1, d_expert, d_model), lambda i, be, br: (be[i], 0, 0)),
                  pl.BlockSpec((1, 1, d_model), lambda i, be, br: (be[i], 0, 0))],
        out_specs=(slots, slots),
        scratch_shapes=[pltpu.VMEM((d_model, d_gu), bf16), pltpu.VMEM((d_expert, d_model), bf16)],
    )
    packed = jax.ShapeDtypeStruct((n_slots, PACK_W), u32)
    return pl.pallas_call(
        _experts_kernel,
        grid_spec=grid_spec,
        out_shape=(packed, packed),
        compiler_params=_cparams("arbitrary"),
        name="experts",
    )(block_expert, block_rows, xs_a, xs_b, w_gate_up, b_gate_up[:, None, :], w_down, b_down[:, None, :])


def _combine_kernel(h_ref, ya_ref, yb_ref, gate_ref, yp_ref, ys_ref, *, n_prompt_tiles):
    i = pl.program_id(0)
    g = gate_ref[...].T
    acc = None
    for k in range(TOP_K):
        gk = g[:, k:k + 1]
        pieces = [p * gk for p in _unpack_bf16_pairs(ya_ref[k]) + _unpack_bf16_pairs(yb_ref[k])]
        acc = pieces if acc is None else [a + p for a, p in zip(acc, pieces)]
    y = h_ref[...] + jnp.concatenate(acc, axis=1)

    @pl.when(i < n_prompt_tiles)
    def _():
        yp_ref[...] = y

    @pl.when(i >= n_prompt_tiles)
    def _():
        ys_ref[...] = y


def _combine(h, ya, yb, gates, n_p, tm):
    n, d_model = h.shape
    tiles_p = n_p // tm
    packed = pl.BlockSpec((TOP_K, tm, PACK_W), lambda i: (0, i, 0))
    return pl.pallas_call(
        functools.partial(_combine_kernel, n_prompt_tiles=tiles_p),
        grid=(n // tm,),
        in_specs=[pl.BlockSpec((tm, d_model), lambda i: (i, 0)), packed, packed,
                  pl.BlockSpec((ROUTE_ROWS, tm), lambda i: (0, i))],
        out_specs=(pl.BlockSpec((tm, d_model), lambda i: (jnp.minimum(i, tiles_p - 1), 0)),
                   pl.BlockSpec((tm, d_model), lambda i: (jnp.maximum(i - tiles_p, 0), 0))),
        out_shape=(jax.ShapeDtypeStruct((n_p, d_model), f32), jax.ShapeDtypeStruct((n - n_p, d_model), f32)),
        compiler_params=_cparams("arbitrary"),
        name="combine",
    )(h, ya, yb, gates)


SC_WINDOW = 128


def _sc_mesh():
    return plsc.VectorSubcoreMesh(core_axis_name="core", subcore_axis_name="subcore")


def _sc_scatter_rows(x, idx, n_out):
    n, w = x.shape
    n_k = idx.shape[0]

    @pl.kernel(out_type=jax.ShapeDtypeStruct((n_out, w), x.dtype), mesh=_sc_mesh(), scratch_types=[])
    def scatter(x_hbm, i_hbm, o_hbm):
        def body(x_vmem, *i_vmem):
            for iv in i_vmem:
                pltpu.sync_copy(x_vmem, o_hbm.at[iv.at[0]])

        pltpu.emit_pipeline(
            body,
            grid=(n // SC_WINDOW,),
            in_specs=[pl.BlockSpec((SC_WINDOW, w), lambda i: (i, 0))]
            + [pl.BlockSpec((1, SC_WINDOW), functools.partial(lambda k, i: (k, i), k)) for k in range(n_k)],
            out_specs=[],
            core_axis_name=("core", "subcore"),
            dimension_semantics=(pltpu.PARALLEL,),
        )(x_hbm, *([i_hbm] * n_k))

    return scatter(x, idx)


def _sc_gather_rows(x, idx):
    n_idx, w = idx.shape[0], x.shape[1]

    @pl.kernel(out_type=jax.ShapeDtypeStruct((n_idx, w), x.dtype), mesh=_sc_mesh(), scratch_types=[])
    def gather(x_hbm, i_hbm, o_hbm):
        def body(i_vmem, o_vmem):
            pltpu.sync_copy(x_hbm.at[i_vmem.at[0]], o_vmem)

        pltpu.emit_pipeline(
            body,
            grid=(n_idx // SC_WINDOW,),
            in_specs=[pl.BlockSpec((1, SC_WINDOW), lambda i: (0, i))],
            out_specs=[pl.BlockSpec((SC_WINDOW, w), lambda i: (i, 0))],
            core_axis_name=("core", "subcore"),
            dimension_semantics=(pltpu.PARALLEL,),
        )(i_hbm, o_hbm)

    return gather(x, idx.reshape(1, n_idx))


EXPERT_BLK = 512


def _moe(h, xn_a, xn_b, expert, gates, rank, counts, n_p, w_gate_up, b_gate_up, w_down, b_down):
    n = h.shape[0]
    blk = EXPERT_BLK
    padded = (counts + blk - 1) // blk * blk
    padded_end = jnp.cumsum(padded)
    start_padded = padded_end - padded
    slot_t = rank
    for e in range(N_EXPERTS):
        slot_t = slot_t + jnp.where(expert == e, start_padded[e], 0)
    n_blocks = -(-(n * TOP_K + N_EXPERTS * (blk - 1)) // blk)
    n_slots = n_blocks * blk
    block_start = jnp.arange(n_blocks, dtype=jnp.int32) * blk
    block_expert = jnp.minimum(jnp.sum((block_start[:, None] >= padded_end[None, :]).astype(jnp.int32), axis=1),
                               N_EXPERTS - 1)
    block_rows = jnp.clip(counts[block_expert] - (block_start - start_padded[block_expert]), 0, blk)
    xs_a = _sc_scatter_rows(xn_a, slot_t, n_slots)
    xs_b = _sc_scatter_rows(xn_b, slot_t, n_slots)
    ys_a, ys_b = _experts(block_expert, block_rows, xs_a, xs_b, w_gate_up, b_gate_up, w_down, b_down, blk)
    flat = slot_t.reshape(-1)
    ya = _sc_gather_rows(ys_a, flat).reshape(TOP_K, n, PACK_W)
    yb = _sc_gather_rows(ys_b, flat).reshape(TOP_K, n, PACK_W)
    return _combine(h, ya, yb, gates, n_p, 512)


def _layer(x_p, x_s, cache_c, cache_kr, state, page_table, norm_mix, w_in, norm_q_lat, w_uq, norm_kv_lat, w_ukv,
           norm_qk_q, norm_qk_k, conv_w, conv_b, conv_ln_g, conv_ln_b, norm_out_conv, norm_out_attn, w_out,
           norm_ffn, w_router, b_router, w_gate_up, b_gate_up, w_down, b_down):
    n_b, t_p, d_model = x_p.shape
    n_seq, t_s, _ = x_s.shape
    d_conv = conv_w.shape[1]
    q_rank, kv_rank = norm_q_lat.shape[0], norm_kv_lat.shape[0]
    n_past = page_table.shape[1] * PAGE_SIZE
    win, wuq, wk, wvt = _mix_in_weights(w_in, w_uq, w_ukv, d_conv, q_rank, kv_rank)
    mix = functools.partial(_mix_in, g_mix=norm_mix, win=win, g_q_lat=norm_q_lat, wuq=wuq, g_kv_lat=norm_kv_lat,
                            g_qk_q=norm_qk_q, wk=wk, wvt=wvt, g_qk_k=norm_qk_k,
                            d_conv=d_conv, q_rank=q_rank, kv_rank=kv_rank)
    x_p2, x_s2 = x_p.reshape(n_b * t_p, d_model), x_s.reshape(n_seq * t_s, d_model)
    tab_p = _rope_head_tables(jnp.arange(t_p))
    tab_s = _rope_head_tables(n_past + jnp.arange(n_seq * t_s) % t_s)
    glu_p, q_p, c_p, kr_p, k_p, vt_p = mix(x_p2, tab_p, MIX_TILE, seq_len=t_p, q_scale=ATTN_SCALE * LOG2_E)
    glu_s, q_s, c_s, kr_s, _, _ = mix(x_s2, tab_s, MIX_TILE, seq_len=n_seq * t_s, q_scale=ATTN_SCALE)

    glu_p3, glu_s3 = glu_p.reshape(n_b, t_p, d_conv), glu_s.reshape(n_seq, t_s, d_conv)
    cv_p = _conv_prompt(glu_p3, conv_w, conv_b, conv_ln_g, conv_ln_b)
    cv_s = _conv_sample(glu_s3, state, conv_w, conv_b, conv_ln_g, conv_ln_b)

    hp = N_HEADS * HEAD_PAD
    at_p = _attn_prompt(q_p.reshape(n_b, t_p, hp), k_p.reshape(n_b, t_p, hp), vt_p, 512, 512)
    c_s3, kr_s3 = c_s.reshape(n_seq, t_s, kv_rank), kr_s.reshape(n_seq, t_s, QK_ROPE_DIM)
    at_s = _attn_sample_all(q_s, c_s3, kr_s3, page_table, cache_c, cache_kr, w_ukv, norm_qk_k, 32)

    h, xn_a, xn_b, expert, gates, rank, cnt = _merge(
        x_p2, cv_p.reshape(n_b * t_p, d_conv), at_p.reshape(n_b * t_p, -1),
        x_s2, cv_s.reshape(n_seq * t_s, d_conv), at_s,
        norm_out_conv, norm_out_attn, w_out, norm_ffn, w_router, b_router, 512)
    counts = cnt[:, 0].astype(jnp.int32)
    y_p, y_s = _moe(h, xn_a, xn_b, expert[:TOP_K], gates, rank[:TOP_K], counts, n_b * t_p,
                    w_gate_up, b_gate_up, w_down, b_down)
    n_hist = CONV_WIDTH - 1
    conv_state_p = jnp.concatenate([jnp.zeros((n_b, max(n_hist - t_p, 0), d_conv), f32),
                                    glu_p3[:, max(t_p - n_hist, 0):]], axis=1)
    conv_state_s = jnp.concatenate([state, glu_s3], axis=1)[:, -n_hist:]
    return (y_p.reshape(x_p.shape), y_s.reshape(x_s.shape), c_p.reshape(n_b, t_p, kv_rank),
            kr_p.reshape(n_b, t_p, QK_ROPE_DIM), conv_state_p, c_s3, kr_s3, conv_state_s)


def kernel(x_prompt, x_sample, cache_kv_latent, cache_k_rope, state_conv, page_table, norm_mix, w_in, norm_q_lat,
           w_uq, norm_kv_lat, w_ukv, norm_qk_q, norm_qk_k, conv_w, conv_b, conv_ln_g, conv_ln_b, norm_out_conv,
           norm_out_attn, w_out, norm_ffn, w_router, b_router, w_gate_up, b_gate_up, w_down, b_down):
    h_p, h_s = x_prompt, x_sample
    per_layer = []
    for l in range(w_in.shape[0]):
        outs = _layer(h_p, h_s, cache_kv_latent[l], cache_k_rope[l], state_conv[l], page_table, norm_mix[l],
                      w_in[l], norm_q_lat[l], w_uq[l], norm_kv_lat[l], w_ukv[l], norm_qk_q[l], norm_qk_k[l],
                      conv_w[l], conv_b[l], conv_ln_g[l], conv_ln_b[l], norm_out_conv[l], norm_out_attn[l],
                      w_out[l], norm_ffn[l], w_router[l], b_router[l], w_gate_up[l], b_gate_up[l], w_down[l],
                      b_down[l])
        h_p, h_s = outs[0], outs[1]
        per_layer.append(outs[2:])
    stacked = [jnp.stack([pl_[i] for pl_ in per_layer]) for i in range(6)]
    return (h_p, h_s, *stacked)
```

```python
import functools

import jax
import jax.numpy as jnp
import numpy as np
from jax import lax
from jax.experimental import pallas as pl
from jax.experimental.pallas import tpu as pltpu
from jax.experimental.pallas import tpu_sc as plsc

N_HEADS = 8
QK_NOPE_DIM = 64
QK_ROPE_DIM = 32
ROPE_HALF = QK_ROPE_DIM // 2
QK_DIM = QK_NOPE_DIM + QK_ROPE_DIM
V_DIM = 64
CONV_WIDTH = 31
N_EXPERTS = 32
TOP_K = 4
SWIGLU_LIMIT = 7.0
SWIGLU_ALPHA = 1.702
ROPE_THETA = 10000.0
NORM_EPS = 1e-6
ATTN_SCALE = QK_DIM ** -0.5
LOG2_E = 1.4426950408889634
PAGE_SIZE = 128

LANES = 128
HEAD_PAD = LANES
VMEM_LIMIT = 56 * 1024 * 1024

f32 = jnp.float32
bf16 = jnp.bfloat16


def _cparams(*sem):
    return pltpu.CompilerParams(dimension_semantics=sem, vmem_limit_bytes=VMEM_LIMIT)


def _rope_head_tables(pos):
    inv_freq = ROPE_THETA ** (-jnp.arange(ROPE_HALF, dtype=f32) / ROPE_HALF)
    ang = pos.astype(f32)[:, None] * inv_freq[None, :]
    cos, sin = jnp.cos(ang), jnp.sin(ang)
    n = pos.shape[0]
    z = lambda w: jnp.zeros((n, w), f32)
    c = jnp.concatenate([jnp.ones((n, QK_NOPE_DIM), f32), cos, cos, z(32)], axis=1)
    s1 = jnp.concatenate([z(QK_NOPE_DIM + ROPE_HALF), sin, z(32)], axis=1)
    s2 = jnp.concatenate([z(QK_NOPE_DIM), -sin, z(ROPE_HALF + 32)], axis=1)
    return jnp.concatenate([c, s1, s2], axis=1)


def _rope_head(x, tab):
    c, s1, s2 = tab[:, :LANES], tab[:, LANES:2 * LANES], tab[:, 2 * LANES:]
    return x * c + pltpu.roll(x, ROPE_HALF, 1) * s1 + pltpu.roll(x, LANES - ROPE_HALF, 1) * s2


def _mix_in_kernel(x_ref, tab_ref, gmix_ref, win_ref, gq_ref, wuq_ref, gkv_ref, gqq_ref,
                   wk_ref, wvt_ref, gqk_ref,
                   glu_ref, q_ref, c_ref, kr_ref, k_ref, vt_ref, *, d_conv, q_rank, kv_rank, q_scale):
    tm = x_ref.shape[0]
    rows = tm // MIX_SUB_TILES
    gqq = gqq_ref[...] * q_scale
    gqk = gqk_ref[...]

    def project(r):
        x = x_ref[r, :]
        xn = x * lax.rsqrt(jnp.mean(x * x, axis=-1, keepdims=True) + NORM_EPS) * gmix_ref[...]
        return jnp.dot(xn.astype(bf16), win_ref[...], preferred_element_type=f32)

    def latents(r, proj):
        o = 0
        a = proj[:, o:o + d_conv]; o += d_conv
        gt = proj[:, o:o + d_conv]; o += d_conv
        q_lat = proj[:, o:o + q_rank]; o += q_rank
        kv_lat = proj[:, o:o + kv_rank]; o += kv_rank
        kr_raw = proj[:, o:o + LANES]
        glu_ref[r, :] = a * jax.nn.sigmoid(gt)
        tab = tab_ref[r, :]
        qn = q_lat * lax.rsqrt(jnp.mean(q_lat * q_lat, axis=-1, keepdims=True) + NORM_EPS) * gq_ref[...]
        q = jnp.dot(qn.astype(bf16), wuq_ref[...], preferred_element_type=f32)
        c_kv = kv_lat * lax.rsqrt(jnp.mean(kv_lat * kv_lat, axis=-1, keepdims=True) + NORM_EPS) * gkv_ref[...]
        c_ref[r, :] = c_kv
        kr = _rope_head(kr_raw, tab)
        kr_ref[r, :] = kr[:, QK_NOPE_DIM:QK_NOPE_DIM + QK_ROPE_DIM]
        c_bf = c_kv.astype(bf16)
        kn = jnp.dot(c_bf, wk_ref[...], preferred_element_type=f32)
        vt_ref[0, :, r] = lax.dot_general(wvt_ref[...], c_bf, NT_DIMS,
                                          preferred_element_type=f32).astype(vt_ref.dtype)
        return q, kn, kr, tab

    def heads(r, q, kn, kr, tab):
        for h in range(N_HEADS):
            sl = slice(h * HEAD_PAD, (h + 1) * HEAD_PAD)
            qh = _rope_head(q[:, sl], tab)
            qh = qh * lax.rsqrt(jnp.sum(qh * qh, axis=-1, keepdims=True) * (1.0 / QK_DIM) + NORM_EPS) * gqq
            q_ref[r, sl] = qh.astype(q_ref.dtype)
            kh = kn[:, sl] + kr
            kh = kh * lax.rsqrt(jnp.sum(kh * kh, axis=-1, keepdims=True) * (1.0 / QK_DIM) + NORM_EPS) * gqk
            k_ref[r, sl] = kh.astype(k_ref.dtype)

    slices = [pl.ds(i * rows, rows) for i in range(MIX_SUB_TILES)]
    proj = project(slices[0])
    for i, r in enumerate(slices):
        nxt = project(slices[i + 1]) if i + 1 < len(slices) else None
        heads(r, *latents(r, proj))
        proj = nxt


MIX_TILE = 1024
MIX_SUB_TILES = 4


def _head_gain(g_pairs):
    g_rot = g_pairs[QK_NOPE_DIM:]
    return jnp.concatenate([g_pairs[:QK_NOPE_DIM], g_rot, g_rot, jnp.zeros((32,), f32)])[None, :]


def _mix_in_weights(w_in, w_uq, w_ukv, d_conv, q_rank, kv_rank):
    d_model = w_in.shape[0]
    base = 2 * d_conv + q_rank + kv_rank
    w_kr = w_in[:, base:base + QK_ROPE_DIM]
    w_kr_pad = jnp.concatenate([jnp.zeros((d_model, QK_NOPE_DIM), f32), w_kr,
                                jnp.zeros((d_model, 32), f32)], axis=1)
    win = jnp.concatenate([w_in[:, :base], w_kr_pad], axis=1).astype(bf16)
    wuq = jnp.pad(w_uq, ((0, 0), (0, 0), (0, HEAD_PAD - QK_DIM))).reshape(q_rank, N_HEADS * HEAD_PAD).astype(bf16)
    wk = jnp.pad(w_ukv[:, :, :QK_NOPE_DIM], ((0, 0), (0, 0), (0, HEAD_PAD - QK_NOPE_DIM)))
    wk = wk.reshape(kv_rank, N_HEADS * HEAD_PAD).astype(bf16)
    wvt = w_ukv[:, :, QK_NOPE_DIM:].reshape(kv_rank, N_HEADS * V_DIM).T.astype(bf16)
    return win, wuq, wk, wvt


def _mix_in(x2d, tab, tm, g_mix, win, g_q_lat, wuq, g_kv_lat, g_qk_q, wk, wvt, g_qk_k, d_conv, q_rank, kv_rank,
            seq_len, q_scale):
    n, d_model = x2d.shape
    n_tab_blocks = tab.shape[0] // tm
    seq_tiles = seq_len // tm
    row = lambda i: (i, 0)
    const = lambda i: (0, 0)
    full = lambda a: pl.BlockSpec(a.shape, const)
    gm, gq, gkv = g_mix[None, :], g_q_lat[None, :], g_kv_lat[None, :]
    gqq, gqk = _head_gain(g_qk_q), _head_gain(g_qk_k)
    hp = N_HEADS * HEAD_PAD
    out_shape = (
        jax.ShapeDtypeStruct((n, d_conv), f32),
        jax.ShapeDtypeStruct((n, hp), bf16),
        jax.ShapeDtypeStruct((n, kv_rank), f32),
        jax.ShapeDtypeStruct((n, QK_ROPE_DIM), f32),
        jax.ShapeDtypeStruct((n, hp), bf16),
        jax.ShapeDtypeStruct((n // seq_len, N_HEADS * V_DIM, seq_len), bf16),
    )
    out_specs = [pl.BlockSpec((tm, s.shape[1]), row) for s in out_shape[:-1]]
    out_specs.append(pl.BlockSpec((1, N_HEADS * V_DIM, tm), lambda i: (i // seq_tiles, 0, i % seq_tiles)))
    return pl.pallas_call(
        functools.partial(_mix_in_kernel, d_conv=d_conv, q_rank=q_rank, kv_rank=kv_rank, q_scale=q_scale),
        grid=(n // tm,),
        in_specs=[pl.BlockSpec((tm, d_model), row),
                  pl.BlockSpec((tm, tab.shape[1]), lambda i: (i % n_tab_blocks, 0)),
                  full(gm), full(win), full(gq), full(wuq), full(gkv), full(gqq), full(wk), full(wvt), full(gqk)],
        out_specs=tuple(out_specs),
        out_shape=out_shape,
        compiler_params=_cparams("parallel"),
        name="mix_in",
    )(x2d, tab, gm, win, gq, wuq, gkv, gqq, wk, wvt, gqk)


CONV_PAD = 32


SUBLANES = 8


def _conv_taps(window, w_ref, b_ref, ls, tc):
    lead = CONV_PAD - (CONV_WIDTH - 1)
    acc = jnp.zeros((tc, LANES), f32) + b_ref[:, ls]
    for r in range(SUBLANES):
        taps = [(a, a * SUBLANES + r - lead) for a in range(CONV_PAD // SUBLANES + 1)
                if 0 <= a * SUBLANES + r - lead < CONV_WIDTH]
        shifted = window if r == 0 else pltpu.roll(window, window.shape[0] - r, 0)
        for a, j in taps:
            acc = acc + shifted[a * SUBLANES:a * SUBLANES + tc] * w_ref[j:j + 1, ls]
    return acc


def _ln_swish(y, g_ref, beta_ref):
    mu = jnp.mean(y, axis=-1, keepdims=True)
    yc = y - mu
    var = jnp.mean(yc * yc, axis=-1, keepdims=True)
    z = yc * lax.rsqrt(var + NORM_EPS) * g_ref[...] + beta_ref[...]
    return z * jax.nn.sigmoid(z)


def _conv_prompt_kernel(x_ref, w_ref, b_ref, g_ref, beta_ref, o_ref, *, tc):
    t_len, n_ch = x_ref.shape[1], x_ref.shape[2]

    def chunk(idx, carry):
        t0 = pl.multiple_of(idx * tc, tc)
        h0 = pl.multiple_of(jnp.maximum(t0 - CONV_PAD, 0), SUBLANES)
        has_hist = jnp.where(idx > 0, 1.0, 0.0)
        for lt in range(n_ch // LANES):
            ls = slice(lt * LANES, (lt + 1) * LANES)
            window = jnp.concatenate([x_ref[0, pl.ds(h0, CONV_PAD), ls] * has_hist,
                                      x_ref[0, pl.ds(t0, tc), ls]], axis=0)
            o_ref[0, pl.ds(t0, tc), ls] = _conv_taps(window, w_ref, b_ref, ls, tc)
        o_ref[0, pl.ds(t0, tc), :] = _ln_swish(o_ref[0, pl.ds(t0, tc), :], g_ref, beta_ref)
        return carry

    lax.fori_loop(0, t_len // tc, chunk, 0)


def _conv_sample_kernel(x_ref, st_ref, w_ref, b_ref, g_ref, beta_ref, o_ref):
    bb, tc, n_ch = x_ref.shape

    def one(b, carry):
        for lt in range(n_ch // LANES):
            ls = slice(lt * LANES, (lt + 1) * LANES)
            window = jnp.concatenate([st_ref[b, :, ls], x_ref[b, :, ls]], axis=0)
            o_ref[b, :, ls] = _conv_taps(window, w_ref, b_ref, ls, tc)
        o_ref[b] = _ln_swish(o_ref[b], g_ref, beta_ref)
        return carry

    lax.fori_loop(0, bb, one, 0)


def _conv_specs(conv_w, n_ch):
    const = lambda i: (0, 0)
    vec = pl.BlockSpec((1, n_ch), const)
    return [pl.BlockSpec(conv_w.shape, const), vec, vec, vec]


def _conv_prompt(glu, conv_w, conv_b, ln_g, ln_b, tc=128):
    n_b, t_len, n_ch = glu.shape
    blk = pl.BlockSpec((1, t_len, n_ch), lambda i: (i, 0, 0))
    return pl.pallas_call(
        functools.partial(_conv_prompt_kernel, tc=tc),
        grid=(n_b,),
        in_specs=[blk] + _conv_specs(conv_w, n_ch),
        out_specs=blk,
        out_shape=jax.ShapeDtypeStruct(glu.shape, f32),
        compiler_params=_cparams("parallel"),
        name="conv_prompt",
    )(glu, conv_w, conv_b[None, :], ln_g[None, :], ln_b[None, :])


def _conv_sample(glu, state, conv_w, conv_b, ln_g, ln_b, bb=32):
    n_b, t_len, n_ch = glu.shape
    hist = jnp.pad(state, ((0, 0), (CONV_PAD - state.shape[1], 0), (0, 0)))
    return pl.pallas_call(
        _conv_sample_kernel,
        grid=(n_b // bb,),
        in_specs=[pl.BlockSpec((bb, t_len, n_ch), lambda i: (i, 0, 0)),
                  pl.BlockSpec((bb, CONV_PAD, n_ch), lambda i: (i, 0, 0))] + _conv_specs(conv_w, n_ch),
        out_specs=pl.BlockSpec((bb, t_len, n_ch), lambda i: (i, 0, 0)),
        out_shape=jax.ShapeDtypeStruct(glu.shape, f32),
        compiler_params=_cparams("parallel"),
        name="conv_sample",
    )(glu, hist, conv_w, conv_b[None, :], ln_g[None, :], ln_b[None, :])


NT_DIMS = (((1,), (1,)), ((), ()))


def _attn_prompt_kernel(q_ref, k_ref, vt_ref, o_ref, *, tq, tk):
    assert tq == tk
    qi = pl.program_id(2)
    n_h = q_ref.shape[2] // HEAD_PAD
    qs = [q_ref[0, :, h * HEAD_PAD:(h + 1) * HEAD_PAD] for h in range(n_h)]

    def block(k0, nk, q_lo, carry, masked):
        nq = tq - q_lo
        if masked:
            key = k0 + lax.broadcasted_iota(jnp.int32, (nk, nq), 0)
            qry = qi * tq + q_lo + lax.broadcasted_iota(jnp.int32, (nk, nq), 1)
            visible = key <= qry
        ones = jnp.ones((ONES_ROWS, nk), bf16)
        sts = []
        for h in range(n_h):
            k_h = k_ref[0, pl.ds(k0, nk), h * HEAD_PAD:(h + 1) * HEAD_PAD]
            sts.append(lax.dot_general(k_h, qs[h][q_lo:], NT_DIMS, preferred_element_type=f32))
        out = []
        for h, (m_all, acc_all) in enumerate(carry):
            m, acc = m_all[:, q_lo:], acc_all[:, q_lo:]
            st = jnp.where(visible, sts[h], -jnp.inf) if masked else sts[h]
            m_new = jnp.maximum(m, jnp.max(st, axis=0, keepdims=True))
            alpha = jnp.exp2(m - m_new)
            p = jnp.exp2(st - m_new).astype(bf16)
            vt_h = jnp.concatenate([vt_ref[0, h * V_DIM:(h + 1) * V_DIM, pl.ds(k0, nk)], ones], axis=0)
            acc = alpha * acc + jnp.dot(vt_h, p, preferred_element_type=f32)
            if q_lo:
                m_new = jnp.concatenate([m_all[:, :q_lo], m_new], axis=1)
                acc = jnp.concatenate([acc_all[:, :q_lo], acc], axis=1)
            out.append((m_new, acc))
        return tuple(out)

    carry = tuple((jnp.full((1, tq), -jnp.inf, f32), jnp.zeros((V_DIM + ONES_ROWS, tq), f32)) for _ in range(n_h))
    carry = lax.fori_loop(
        0, qi, lambda kb, c: block(pl.multiple_of(kb * tk, tk), tk, 0, c, False), carry)
    half = tk // 2
    d0 = pl.multiple_of(qi * tq, tq)
    carry = block(d0, half, 0, carry, True)
    carry = block(pl.multiple_of(d0 + half, half), half, half, carry, True)
    o_t = jnp.concatenate([acc[:V_DIM] / acc[V_DIM:V_DIM + 1] for (_, acc) in carry], axis=0)
    o_ref[0] = o_t.T


ONES_ROWS = 16


ATTN_HEAD_GROUP = 4


def _attn_prompt(q, k, vt, tq, tk):
    n_b, t_len, _ = q.shape
    g = ATTN_HEAD_GROUP
    return pl.pallas_call(
        functools.partial(_attn_prompt_kernel, tq=tq, tk=tk),
        grid=(n_b, N_HEADS // g, t_len // tq),
        in_specs=[pl.BlockSpec((1, tq, g * HEAD_PAD), lambda b, h, i: (b, i, h)),
                  pl.BlockSpec((1, t_len, g * HEAD_PAD), lambda b, h, i: (b, 0, h)),
                  pl.BlockSpec((1, g * V_DIM, t_len), lambda b, h, i: (b, h, 0))],
        out_specs=pl.BlockSpec((1, tq, g * V_DIM), lambda b, h, i: (b, i, h)),
        out_shape=jax.ShapeDtypeStruct((n_b, t_len, N_HEADS * V_DIM), f32),
        compiler_params=_cparams("parallel", "parallel", "arbitrary"),
        name="attn_prompt",
    )(q, k, vt)


N_TH = 64


def _q_absorb_kernel(q_ref, gk_ref, wukt_ref, sel_ref, qabs_ref, qr_ref):
    gk = gk_ref[...]
    kv_rank = wukt_ref.shape[-1]
    for h in range(N_HEADS):
        qg = (q_ref[:, h * HEAD_PAD:(h + 1) * HEAD_PAD].astype(f32) * gk).astype(bf16)
        qabs_ref[:, h * kv_rank:(h + 1) * kv_rank] = jnp.dot(
            qg, wukt_ref[h], preferred_element_type=f32).astype(qabs_ref.dtype)
        qr_ref[:, h * QK_ROPE_DIM:(h + 1) * QK_ROPE_DIM] = jnp.dot(
            qg, sel_ref[...], preferred_element_type=f32).astype(qr_ref.dtype)


def _q_absorb(q2d, g_qk_k, w_ukv):
    n = q2d.shape[0]
    kv_rank = w_ukv.shape[0]
    wukt = jnp.transpose(w_ukv[:, :, :QK_NOPE_DIM], (1, 2, 0))
    wukt = jnp.pad(wukt, ((0, 0), (0, HEAD_PAD - QK_NOPE_DIM), (0, 0))).astype(bf16)
    sel = jnp.zeros((HEAD_PAD, QK_ROPE_DIM), f32).at[
        QK_NOPE_DIM + jnp.arange(QK_ROPE_DIM), jnp.arange(QK_ROPE_DIM)].set(1.0).astype(bf16)
    gk = _head_gain(g_qk_k)
    full = lambda a: pl.BlockSpec(a.shape, lambda i: (0,) * a.ndim)
    return pl.pallas_call(
        _q_absorb_kernel,
        grid=(1,),
        in_specs=[full(q2d), full(gk), full(wukt), full(sel)],
        out_specs=(pl.BlockSpec((n, N_HEADS * kv_rank), lambda i: (0, 0)),
                   pl.BlockSpec((n, N_HEADS * QK_ROPE_DIM), lambda i: (0, 0))),
        out_shape=(jax.ShapeDtypeStruct((n, N_HEADS * kv_rank), bf16),
                   jax.ShapeDtypeStruct((n, N_HEADS * QK_ROPE_DIM), bf16)),
        compiler_params=_cparams("arbitrary"),
        name="q_absorb",
    )(q2d, gk, wukt, sel)


SUB_PAGES = 16


def _latent_products_t(lhs, c_bf):
    n = c_bf.shape[0]
    full = lax.dot_general(lhs, c_bf, NT_DIMS, preferred_element_type=f32)
    knt = full[N_TH:]
    ssq = jnp.sum((knt * knt).reshape(QK_NOPE_DIM, N_HEADS, n), axis=0)
    return full[:N_TH], ssq


def _scores_t(st_lat, ssq_nope, krt, qr):
    n = krt.shape[1]
    ssq = ssq_nope + jnp.sum(krt * krt, axis=0, keepdims=True)
    rinv = lax.rsqrt(ssq * (1.0 / QK_DIM) + NORM_EPS)
    st = st_lat + jnp.dot(qr, krt.astype(bf16), preferred_element_type=f32)
    return (st.reshape(N_TH // N_HEADS, N_HEADS, n) * rinv[None]).reshape(N_TH, n)


def _softmax_update_t(st, c_bf, state, mask=None):
    m, l, acc = state
    if mask is not None:
        st = jnp.where(mask, st, -jnp.inf)
    m_new = jnp.maximum(m, jnp.max(st, axis=1, keepdims=True))
    alpha = jnp.exp(m - m_new)
    p = jnp.exp(st - m_new)
    l = alpha * l + jnp.sum(p, axis=1, keepdims=True)
    acc = alpha * acc + jnp.dot(p.astype(bf16), c_bf, preferred_element_type=f32)
    return m_new, l, acc


def _attn_sample_kernel(pt_ref, qabs_ref, qr_ref, cnew_ref, krtnew_ref, wukt_ref, cache_c, cache_krt, o_ref,
                            cbuf, krbuf, sems, m_s, l_s, acc_s, *, pp):
    s, j = pl.program_id(0), pl.program_id(1)
    n_j = pl.num_programs(1)
    step = s * n_j + j
    slot = step % 2

    def page_copies(step_idx, buf_slot):
        copies = []
        for i in range(pp):
            page = pt_ref[step_idx * pp + i]
            copies.append(pltpu.make_async_copy(cache_c.at[page], cbuf.at[buf_slot, i], sems.at[0, buf_slot]))
            copies.append(pltpu.make_async_copy(cache_krt.at[page], krbuf.at[buf_slot, i], sems.at[1, buf_slot]))
        return copies

    @pl.when(step == 0)
    def _():
        for cp in page_copies(0, 0):
            cp.start()

    @pl.when(step + 1 < pl.num_programs(0) * n_j)
    def _():
        for cp in page_copies(step + 1, 1 - slot):
            cp.start()

    @pl.when(j == 0)
    def _():
        m_s[...] = jnp.full(m_s.shape, -jnp.inf, f32)
        l_s[...] = jnp.zeros(l_s.shape, f32)
        acc_s[...] = jnp.zeros(acc_s.shape, f32)

    for cp in page_copies(step, slot):
        cp.wait()

    qr = qr_ref[0]
    lhs = jnp.concatenate([qabs_ref[0], wukt_ref[...]], axis=0)
    sub = min(SUB_PAGES, pp)
    blocks = list(range(0, pp, sub))

    def load(b):
        c_bf = jnp.concatenate([cbuf[slot, i].astype(bf16) for i in range(b, b + sub)], axis=0)
        return (c_bf,) + _latent_products_t(lhs, c_bf)

    state = (m_s[...], l_s[...], acc_s[...])
    nxt = load(blocks[0])
    for n, b in enumerate(blocks):
        c_bf, st_lat, ssq = nxt
        if n + 1 < len(blocks):
            nxt = load(blocks[n + 1])
        krt = jnp.concatenate([krbuf[slot, i] for i in range(b, b + sub)], axis=1)
        state = _softmax_update_t(_scores_t(st_lat, ssq, krt, qr), c_bf, state)
    m_s[...], l_s[...], acc_s[...] = state

    @pl.when(j == n_j - 1)
    def _():
        c_bf = cnew_ref[0].astype(bf16)
        n_pad = c_bf.shape[0]
        key_t = lax.broadcasted_iota(jnp.int32, (N_TH, n_pad), 1)
        q_t = lax.broadcasted_iota(jnp.int32, (N_TH, n_pad), 0) // N_HEADS
        st = _scores_t(*_latent_products_t(lhs, c_bf), krtnew_ref[0], qr)
        m, l, acc = _softmax_update_t(st, c_bf, state, mask=key_t <= q_t)
        o_ref[0] = acc / l


def _attn_sample(page_table, qabs, qr, c_new, kr_new, cache_c, cache_kr, wuk_perm, pp):
    n_seq, n_pages = page_table.shape
    kv_rank = cache_c.shape[-1]
    t_new = c_new.shape[1]
    seq3 = lambda s, j, pt: (s, 0, 0)
    cache_kr_t = jnp.swapaxes(cache_kr, 1, 2)
    c_new_pad = jnp.pad(c_new, ((0, 0), (0, PAGE_SIZE - t_new), (0, 0)))
    krt_new_pad = jnp.pad(jnp.swapaxes(kr_new, 1, 2), ((0, 0), (0, 0), (0, PAGE_SIZE - t_new)))
    grid_spec = pltpu.PrefetchScalarGridSpec(
        num_scalar_prefetch=1,
        grid=(n_seq, n_pages // pp),
        in_specs=[pl.BlockSpec((1, N_TH, kv_rank), seq3),
                  pl.BlockSpec((1, N_TH, QK_ROPE_DIM), seq3),
                  pl.BlockSpec((1, PAGE_SIZE, kv_rank), seq3),
                  pl.BlockSpec((1, QK_ROPE_DIM, PAGE_SIZE), seq3),
                  pl.BlockSpec(wuk_perm.shape, lambda s, j, pt: (0, 0)),
                  pl.BlockSpec(memory_space=pl.ANY),
                  pl.BlockSpec(memory_space=pl.ANY)],
        out_specs=pl.BlockSpec((1, N_TH, kv_rank), seq3),
        scratch_shapes=[pltpu.VMEM((2, pp, PAGE_SIZE, kv_rank), f32),
                        pltpu.VMEM((2, pp, QK_ROPE_DIM, PAGE_SIZE), f32),
                        pltpu.SemaphoreType.DMA((2, 2)),
                        pltpu.VMEM((N_TH, 1), f32), pltpu.VMEM((N_TH, 1), f32),
                        pltpu.VMEM((N_TH, kv_rank), f32)],
    )
    return pl.pallas_call(
        functools.partial(_attn_sample_kernel, pp=pp),
        grid_spec=grid_spec,
        out_shape=jax.ShapeDtypeStruct((n_seq, N_TH, kv_rank), f32),
        compiler_params=_cparams("arbitrary", "arbitrary"),
        name="attn_sample",
    )(page_table.reshape(-1), qabs, qr, c_new_pad, krt_new_pad, wuk_perm, cache_c, cache_kr_t)


def _v_up_kernel(lat_ref, wuv_ref, o_ref):
    rows = lat_ref.shape[0]
    full = jnp.dot(lat_ref[...].astype(bf16), wuv_ref[...], preferred_element_type=f32)
    head_of_row = lax.broadcasted_iota(jnp.int32, full.shape, 0) % N_HEADS
    head_of_lane = lax.broadcasted_iota(jnp.int32, full.shape, 1) // V_DIM
    own = jnp.where(head_of_row == head_of_lane, full, 0.0)
    o_ref[...] = jnp.sum(own.reshape(rows // N_HEADS, N_HEADS, full.shape[1]), axis=1)


def _v_up(lat2d, wuv, rows):
    n, kv_rank = lat2d.shape
    d_attn = wuv.shape[1]
    return pl.pallas_call(
        _v_up_kernel,
        grid=(n // rows,),
        in_specs=[pl.BlockSpec((rows, kv_rank), lambda i: (i, 0)), pl.BlockSpec(wuv.shape, lambda i: (0, 0))],
        out_specs=pl.BlockSpec((rows // N_HEADS, d_attn), lambda i: (i, 0)),
        out_shape=jax.ShapeDtypeStruct((n // N_HEADS, d_attn), f32),
        compiler_params=_cparams("parallel"),
        name="v_up",
    )(lat2d, wuv)


def _attn_sample_all(q_s2d, c_new, kr_new, page_table, cache_c, cache_kr, w_ukv, g_qk_k, pp):
    n_seq, t_new, kv_rank = c_new.shape
    qabs, qr = _q_absorb(q_s2d, g_qk_k, w_ukv)
    qabs = qabs.reshape(n_seq, t_new * N_HEADS, kv_rank)
    qr = qr.reshape(n_seq, t_new * N_HEADS, QK_ROPE_DIM)
    wuk_perm = jnp.transpose(w_ukv[:, :, :QK_NOPE_DIM], (2, 1, 0)).reshape(-1, kv_rank).astype(bf16)
    lat = _attn_sample(page_table, qabs, qr, c_new, kr_new, cache_c, cache_kr, wuk_perm, pp)
    wuv = w_ukv[:, :, QK_NOPE_DIM:].reshape(kv_rank, N_HEADS * V_DIM).astype(bf16)
    rows = min(512, n_seq * N_TH)
    return _v_up(lat.reshape(n_seq * N_TH, kv_rank), wuv, rows)


def _rms(x, g):
    return x * lax.rsqrt(jnp.mean(x * x, axis=-1, keepdims=True) + NORM_EPS) * g


def _split_bf16(x):
    hi = x.astype(bf16)
    return hi, (x - hi.astype(f32)).astype(bf16)


PACK_W = 256
u32 = jnp.uint32


def _pack_bf16_pairs(x):
    bits = lax.bitcast_convert_type(x.astype(bf16).astype(f32), u32)
    pieces = []
    for p in range(x.shape[1] // (2 * PACK_W)):
        lo = bits[:, 2 * p * PACK_W:(2 * p + 1) * PACK_W]
        hi = bits[:, (2 * p + 1) * PACK_W:(2 * p + 2) * PACK_W]
        pieces.append((lo >> 16) | (hi & jnp.uint32(0xFFFF0000)))
    return pieces


def _unpack_bf16_pairs(words):
    return (lax.bitcast_convert_type(words << 16, f32),
            lax.bitcast_convert_type(words & jnp.uint32(0xFFFF0000), f32))


def _merge_kernel(xp_ref, cvp_ref, atp_ref, xs_ref, cvs_ref, ats_ref, gc_ref, ga_ref, woc_ref, woa_ref, gf_ref,
                  wrh_ref, wrl_ref, br_ref, h_ref, xna_ref, xnb_ref, exp_ref, gate_ref, rank_ref, cnt_ref, cnt_s,
                  *, n_prompt_tiles):
    i = pl.program_id(0)
    tm = xp_ref.shape[0]

    @pl.when(i == 0)
    def _():
        cnt_s[...] = jnp.zeros(cnt_s.shape, f32)

    is_p = i < n_prompt_tiles
    rows = tm

    def project(r):
        x = jnp.where(is_p, xp_ref[r, :], xs_ref[r, :])
        yc = _rms(jnp.where(is_p, cvp_ref[r, :], cvs_ref[r, :]), gc_ref[...]).astype(bf16)
        ya = _rms(jnp.where(is_p, atp_ref[r, :], ats_ref[r, :]), ga_ref[...]).astype(bf16)
        h = x + jnp.dot(yc, woc_ref[...], preferred_element_type=f32) \
            + jnp.dot(ya, woa_ref[...], preferred_element_type=f32)
        h_ref[r, :] = h
        xn = _rms(h, gf_ref[...])
        xh, xl = _split_bf16(xn)
        xna_ref[r, :], xnb_ref[r, :] = _pack_bf16_pairs(xn)
        return (jnp.dot(xh, wrh_ref[...], preferred_element_type=f32)
                + jnp.dot(xl, wrh_ref[...], preferred_element_type=f32)
                + jnp.dot(xh, wrl_ref[...], preferred_element_type=f32)) + br_ref[...]

    def route(logits, cnt):
        lt = logits.T[:N_EXPERTS]
        eidx = lax.broadcasted_iota(jnp.int32, lt.shape, 0)
        tops, idxs = [], []
        cur = lt
        for _k in range(TOP_K):
            mk = jnp.max(cur, axis=0, keepdims=True)
            ik = jnp.min(jnp.where(cur == mk, eidx, N_EXPERTS), axis=0, keepdims=True)
            tops.append(mk)
            idxs.append(ik)
            cur = jnp.where(eidx == ik, -jnp.inf, cur)
        es = [jnp.exp(t - tops[0]) for t in tops]
        den = es[0] + es[1] + es[2] + es[3]
        onehots = [(eidx == ik).astype(f32) for ik in idxs]
        oh = onehots[0] + onehots[1] + onehots[2] + onehots[3]
        s_i = lax.broadcasted_iota(jnp.int32, (rows, rows), 0)
        t_i = lax.broadcasted_iota(jnp.int32, (rows, rows), 1)
        before = jnp.dot(oh.astype(bf16), (s_i < t_i).astype(bf16), preferred_element_type=f32) + cnt
        ranks = [jnp.sum(oh_k * before, axis=0, keepdims=True).astype(jnp.int32) for oh_k in onehots]
        pad_i = jnp.zeros((ROUTE_ROWS - TOP_K, rows), jnp.int32)
        pad_f = jnp.zeros((ROUTE_ROWS - TOP_K, rows), f32)
        exp_ref[...] = jnp.concatenate(idxs + [pad_i], axis=0)
        gate_ref[...] = jnp.concatenate([e / den for e in es] + [pad_f], axis=0)
        rank_ref[...] = jnp.concatenate(ranks + [pad_i], axis=0)
        return cnt + jnp.sum(oh, axis=1, keepdims=True)

    cnt = route(project(pl.ds(0, tm)), cnt_s[...])
    cnt_s[...] = cnt
    cnt_ref[...] = cnt


ROUTE_ROWS = 8


def _merge(x_p, cv_p, at_p, x_s, cv_s, at_s, g_out_conv, g_out_attn, w_out, g_ffn, w_router, b_router, tm):
    (n_p, d_model), n_s = x_p.shape, x_s.shape[0]
    n = n_p + n_s
    d_conv, d_attn = cv_p.shape[1], at_p.shape[1]
    woc, woa = w_out[:d_conv].astype(bf16), w_out[d_conv:].astype(bf16)
    wr = jnp.pad(w_router, ((0, 0), (0, LANES - N_EXPERTS)))
    wrh, wrl = _split_bf16(wr)
    br = jnp.concatenate([b_router.astype(f32), jnp.full((LANES - N_EXPERTS,), -jnp.inf, f32)])[None, :]
    gc, ga, gf = g_out_conv[None, :], g_out_attn[None, :], g_ffn[None, :]
    tiles_p, tiles_s = n_p // tm, n_s // tm
    full = lambda a: pl.BlockSpec(a.shape, lambda i: (0, 0))
    rows = lambda w: pl.BlockSpec((tm, w), lambda i: (i, 0))
    rows_p = lambda w: pl.BlockSpec((tm, w), lambda i: (jnp.minimum(i, tiles_p - 1), 0))
    rows_s = lambda w: pl.BlockSpec((tm, w), lambda i: (jnp.maximum(i - tiles_p, 0), 0))
    assert d_model == 4 * PACK_W
    route = lambda dt: jax.ShapeDtypeStruct((ROUTE_ROWS, n), dt)
    route_blk = pl.BlockSpec((ROUTE_ROWS, tm), lambda i: (0, i))
    out_shape = (jax.ShapeDtypeStruct((n, d_model), f32),
                 jax.ShapeDtypeStruct((n, PACK_W), u32), jax.ShapeDtypeStruct((n, PACK_W), u32),
                 route(jnp.int32), route(f32), route(jnp.int32), jax.ShapeDtypeStruct((N_EXPERTS, 1), f32))
    return pl.pallas_call(
        functools.partial(_merge_kernel, n_prompt_tiles=tiles_p),
        grid=(tiles_p + tiles_s,),
        in_specs=[rows_p(d_model), rows_p(d_conv), rows_p(d_attn), rows_s(d_model), rows_s(d_conv), rows_s(d_attn),
                  full(gc), full(ga), full(woc), full(woa), full(gf), full(wrh), full(wrl), full(br)],
        out_specs=(rows(d_model), rows(PACK_W), rows(PACK_W), route_blk, route_blk, route_blk,
                   pl.BlockSpec((N_EXPERTS, 1), lambda i: (0, 0))),
        out_shape=out_shape,
        scratch_shapes=[pltpu.VMEM((N_EXPERTS, 1), f32)],
        compiler_params=_cparams("arbitrary"),
        name="merge",
    )(x_p, cv_p, at_p, x_s, cv_s, at_s, gc, ga, woc, woa, gf, wrh, wrl, br)


def _experts_kernel(be_ref, br_ref, xa_ref, xb_ref, wgu_ref, bgu_ref, wd_ref, bd_ref, ya_ref, yb_ref, wgu_s, wd_s):
    i = pl.program_id(0)
    d_expert = wd_ref.shape[1]
    blk = xa_ref.shape[0]

    @pl.when((i == 0) | (be_ref[i] != be_ref[jnp.maximum(i - 1, 0)]))
    def _():
        wgu_s[...] = wgu_ref[0].astype(bf16)
        wd_s[...] = wd_ref[0].astype(bf16)

    @pl.when(br_ref[i] > 0)
    def _():
        x = jnp.concatenate(_unpack_bf16_pairs(xa_ref[...]) + _unpack_bf16_pairs(xb_ref[...]), axis=1)
        live = lax.broadcasted_iota(jnp.int32, (blk, 1), 0) < br_ref[i]
        x = jnp.where(live, x, 0.0).astype(bf16)
        gu = jnp.dot(x, wgu_s[...], preferred_element_type=f32) + bgu_ref[0]
        gate = jnp.minimum(gu[:, :d_expert], SWIGLU_LIMIT)
        up = jnp.clip(gu[:, d_expert:], -SWIGLU_LIMIT, SWIGLU_LIMIT)
        hid = (up + 1.0) * gate * jax.nn.sigmoid(SWIGLU_ALPHA * gate)
        y = jnp.dot(hid.astype(bf16), wd_s[...], preferred_element_type=f32) + bd_ref[0]
        ya_ref[...], yb_ref[...] = _pack_bf16_pairs(y)

    @pl.when(br_ref[i] <= 0)
    def _():
        ya_ref[...] = jnp.zeros(ya_ref.shape, ya_ref.dtype)
        yb_ref[...] = jnp.zeros(yb_ref.shape, yb_ref.dtype)


def _experts(block_expert, block_rows, xs_a, xs_b, w_gate_up, b_gate_up, w_down, b_down, blk):
    n_slots = xs_a.shape[0]
    n_exp, d_model, d_gu = w_gate_up.shape
    d_expert = w_down.shape[1]
    slots = pl.BlockSpec((blk, PACK_W), lambda i, be, br: (i, 0))
    grid_spec = pltpu.PrefetchScalarGridSpec(
        num_scalar_prefetch=2,
        grid=(n_slots // blk,),
        in_specs=[slots, slots,
                  pl.BlockSpec((1, d_model, d_gu), lambda i, be, br: (be[i], 0, 0)),
                  pl.BlockSpec((1, 1, d_gu), lambda i, be, br: (be[i], 0, 0)),
                  pl.BlockSpec((1, d_expert, d_model), lambda i, be, br: (be[i], 0, 0)),
                  pl.BlockSpec((1, 1, d_model), lambda i, be, br: (be[i], 0, 0))],
        out_specs=(slots, slots),
        scratch_shapes=[pltpu.VMEM((d_model, d_gu), bf16), pltpu.VMEM((d_expert, d_model), bf16)],
    )
    packed = jax.ShapeDtypeStruct((n_slots, PACK_W), u32)
    return pl.pallas_call(
        _experts_kernel,
        grid_spec=grid_spec,
        out_shape=(packed, packed),
        compiler_params=_cparams("arbitrary"),
        name="experts",
    )(block_expert, block_rows, xs_a, xs_b, w_gate_up, b_gate_up[:, None, :], w_down, b_down[:, None, :])


def _combine_kernel(h_ref, ya_ref, yb_ref, gate_ref, yp_ref, ys_ref, *, n_prompt_tiles):
    i = pl.program_id(0)
    g = gate_ref[...].T
    acc = None
    for k in range(TOP_K):
        gk = g[:, k:k + 1]
        pieces = [p * gk for p in _unpack_bf16_pairs(ya_ref[k]) + _unpack_bf16_pairs(yb_ref[k])]
        acc = pieces if acc is None else [a + p for a, p in zip(acc, pieces)]
    y = h_ref[...] + jnp.concatenate(acc, axis=1)

    @pl.when(i < n_prompt_tiles)
    def _():
        yp_ref[...] = y

    @pl.when(i >= n_prompt_tiles)
    def _():
        ys_ref[...] = y


def _combine(h, ya, yb, gates, n_p, tm):
    n, d_model = h.shape
    tiles_p = n_p // tm
    packed = pl.BlockSpec((TOP_K, tm, PACK_W), lambda i: (0, i, 0))
    return pl.pallas_call(
        functools.partial(_combine_kernel, n_prompt_tiles=tiles_p),
        grid=(n // tm,),
        in_specs=[pl.BlockSpec((tm, d_model), lambda i: (i, 0)), packed, packed,
                  pl.BlockSpec((ROUTE_ROWS, tm), lambda i: (0, i))],
        out_specs=(pl.BlockSpec((tm, d_model), lambda i: (jnp.minimum(i, tiles_p - 1), 0)),
                   pl.BlockSpec((tm, d_model), lambda i: (jnp.maximum(i - tiles_p, 0), 0))),
        out_shape=(jax.ShapeDtypeStruct((n_p, d_model), f32), jax.ShapeDtypeStruct((n - n_p, d_model), f32)),
        compiler_params=_cparams("arbitrary"),
        name="combine",
    )(h, ya, yb, gates)


SC_WINDOW = 128


def _sc_mesh():
    return plsc.VectorSubcoreMesh(core_axis_name="core", subcore_axis_name="subcore")


def _sc_scatter_rows(x, idx, n_k, n_out):
    n, w = x.shape

    @pl.kernel(out_type=jax.ShapeDtypeStruct((n_out, w), x.dtype), mesh=_sc_mesh(), scratch_types=[])
    def scatter(x_hbm, i_hbm, o_hbm):
        def body(x_vmem, *i_vmem):
            for iv in i_vmem:
                pltpu.sync_copy(x_vmem, o_hbm.at[iv.at[0]])

        pltpu.emit_pipeline(
            body,
            grid=(n // SC_WINDOW,),
            in_specs=[pl.BlockSpec((SC_WINDOW, w), lambda i: (i, 0))]
            + [pl.BlockSpec((1, SC_WINDOW), functools.partial(lambda k, i: (k, i), k)) for k in range(n_k)],
            out_specs=[],
            core_axis_name=("core", "subcore"),
            dimension_semantics=(pltpu.PARALLEL,),
        )(x_hbm, *([i_hbm] * n_k))

    return scatter(x, idx)


def _sc_gather_rows(x, idx):
    n_idx, w = idx.shape[0], x.shape[1]

    @pl.kernel(out_type=jax.ShapeDtypeStruct((n_idx, w), x.dtype), mesh=_sc_mesh(), scratch_types=[])
    def gather(x_hbm, i_hbm, o_hbm):
        def body(i_vmem, o_vmem):
            pltpu.sync_copy(x_hbm.at[i_vmem.at[0]], o_vmem)

        pltpu.emit_pipeline(
            body,
            grid=(n_idx // SC_WINDOW,),
            in_specs=[pl.BlockSpec((1, SC_WINDOW), lambda i: (0, i))],
            out_specs=[pl.BlockSpec((SC_WINDOW, w), lambda i: (i, 0))],
            core_axis_name=("core", "subcore"),
            dimension_semantics=(pltpu.PARALLEL,),
        )(i_hbm, o_hbm)

    return gather(x, idx.reshape(1, n_idx))


EXPERT_BLK = 512
SAMPLE_PAGES_PER_STEP = 64


def _slots_kernel(sp_ref, exp_ref, rank_ref, slot_ref):
    expert = exp_ref[...]
    slot = rank_ref[...]
    for e in range(N_EXPERTS):
        slot = slot + jnp.where(expert == e, sp_ref[e], 0)
    slot_ref[...] = slot


def _slots(start_padded, expert, rank):
    whole = pl.BlockSpec(expert.shape, lambda i, sp: (0, 0))
    return pl.pallas_call(
        _slots_kernel,
        grid_spec=pltpu.PrefetchScalarGridSpec(num_scalar_prefetch=1, grid=(1,), in_specs=[whole, whole],
                                               out_specs=whole),
        out_shape=jax.ShapeDtypeStruct(expert.shape, jnp.int32),
        compiler_params=_cparams("arbitrary"),
        name="slots",
    )(start_padded, expert, rank)


def _moe(h, xn_a, xn_b, expert, gates, rank, counts, n_p, w_gate_up, b_gate_up, w_down, b_down):
    n = h.shape[0]
    blk = EXPERT_BLK
    padded = (counts + blk - 1) // blk * blk
    padded_end = jnp.cumsum(padded)
    start_padded = padded_end - padded
    slot_t = _slots(start_padded, expert, rank)
    n_blocks = -(-(n * TOP_K + N_EXPERTS * (blk - 1)) // blk)
    n_slots = n_blocks * blk
    block_start = jnp.arange(n_blocks, dtype=jnp.int32) * blk
    block_expert = jnp.minimum(jnp.sum((block_start[:, None] >= padded_end[None, :]).astype(jnp.int32), axis=1),
                               N_EXPERTS - 1)
    of_block = block_expert[:, None] == jnp.arange(N_EXPERTS, dtype=jnp.int32)[None, :]
    block_rows = jnp.clip(jnp.sum(jnp.where(of_block, (counts + start_padded)[None, :], 0), axis=1) - block_start,
                          0, blk)
    xs_a = _sc_scatter_rows(xn_a, slot_t, TOP_K, n_slots)
    xs_b = _sc_scatter_rows(xn_b, slot_t, TOP_K, n_slots)
    ys_a, ys_b = _experts(block_expert, block_rows, xs_a, xs_b, w_gate_up, b_gate_up, w_down, b_down, blk)
    flat = slot_t[:TOP_K].reshape(-1)
    ya = _sc_gather_rows(ys_a, flat).reshape(TOP_K, n, PACK_W)
    yb = _sc_gather_rows(ys_b, flat).reshape(TOP_K, n, PACK_W)
    return _combine(h, ya, yb, gates, n_p, 512)


def _layer(x_p, x_s, cache_c, cache_kr, state, page_table, norm_mix, w_in, norm_q_lat, w_uq, norm_kv_lat, w_ukv,
           norm_qk_q, norm_qk_k, conv_w, conv_b, conv_ln_g, conv_ln_b, norm_out_conv, norm_out_attn, w_out,
           norm_ffn, w_router, b_router, w_gate_up, b_gate_up, w_down, b_down):
    n_b, t_p, d_model = x_p.shape
    n_seq, t_s, _ = x_s.shape
    d_conv = conv_w.shape[1]
    q_rank, kv_rank = norm_q_lat.shape[0], norm_kv_lat.shape[0]
    n_past = page_table.shape[1] * PAGE_SIZE
    win, wuq, wk, wvt = _mix_in_weights(w_in, w_uq, w_ukv, d_conv, q_rank, kv_rank)
    mix = functools.partial(_mix_in, g_mix=norm_mix, win=win, g_q_lat=norm_q_lat, wuq=wuq, g_kv_lat=norm_kv_lat,
                            g_qk_q=norm_qk_q, wk=wk, wvt=wvt, g_qk_k=norm_qk_k,
                            d_conv=d_conv, q_rank=q_rank, kv_rank=kv_rank)
    x_p2, x_s2 = x_p.reshape(n_b * t_p, d_model), x_s.reshape(n_seq * t_s, d_model)
    tab_p = _rope_head_tables(jnp.arange(t_p))
    tab_s = _rope_head_tables(n_past + jnp.arange(n_seq * t_s) % t_s)
    glu_p, q_p, c_p, kr_p, k_p, vt_p = mix(x_p2, tab_p, MIX_TILE, seq_len=t_p, q_scale=ATTN_SCALE * LOG2_E)
    glu_s, q_s, c_s, kr_s, _, _ = mix(x_s2, tab_s, MIX_TILE, seq_len=n_seq * t_s, q_scale=ATTN_SCALE)

    glu_p3, glu_s3 = glu_p.reshape(n_b, t_p, d_conv), glu_s.reshape(n_seq, t_s, d_conv)
    cv_p = _conv_prompt(glu_p3, conv_w, conv_b, conv_ln_g, conv_ln_b)
    cv_s = _conv_sample(glu_s3, state, conv_w, conv_b, conv_ln_g, conv_ln_b)

    hp = N_HEADS * HEAD_PAD
    at_p = _attn_prompt(q_p.reshape(n_b, t_p, hp), k_p.reshape(n_b, t_p, hp), vt_p, 512, 512)
    c_s3, kr_s3 = c_s.reshape(n_seq, t_s, kv_rank), kr_s.reshape(n_seq, t_s, QK_ROPE_DIM)
    at_s = _attn_sample_all(q_s, c_s3, kr_s3, page_table, cache_c, cache_kr, w_ukv, norm_qk_k, SAMPLE_PAGES_PER_STEP)

    h, xn_a, xn_b, expert, gates, rank, cnt = _merge(
        x_p2, cv_p.reshape(n_b * t_p, d_conv), at_p.reshape(n_b * t_p, -1),
        x_s2, cv_s.reshape(n_seq * t_s, d_conv), at_s,
        norm_out_conv, norm_out_attn, w_out, norm_ffn, w_router, b_router, 512)
    counts = cnt[:, 0].astype(jnp.int32)
    y_p, y_s = _moe(h, xn_a, xn_b, expert, gates, rank, counts, n_b * t_p,
                    w_gate_up, b_gate_up, w_down, b_down)
    n_hist = CONV_WIDTH - 1
    conv_state_p = jnp.concatenate([jnp.zeros((n_b, max(n_hist - t_p, 0), d_conv), f32),
                                    glu_p3[:, max(t_p - n_hist, 0):]], axis=1)
    conv_state_s = jnp.concatenate([state, glu_s3], axis=1)[:, -n_hist:]
    return (y_p.reshape(x_p.shape), y_s.reshape(x_s.shape), c_p.reshape(n_b, t_p, kv_rank),
            kr_p.reshape(n_b, t_p, QK_ROPE_DIM), conv_state_p, c_s3, kr_s3, conv_state_s)


def kernel(x_prompt, x_sample, cache_kv_latent, cache_k_rope, state_conv, page_table, norm_mix, w_in, norm_q_lat,
           w_uq, norm_kv_lat, w_ukv, norm_qk_q, norm_qk_k, conv_w, conv_b, conv_ln_g, conv_ln_b, norm_out_conv,
           norm_out_attn, w_out, norm_ffn, w_router, b_router, w_gate_up, b_gate_up, w_down, b_down):
    h_p, h_s = x_prompt, x_sample
    per_layer = []
    for l in range(w_in.shape[0]):
        outs = _layer(h_p, h_s, cache_kv_latent[l], cache_k_rope[l], state_conv[l], page_table, norm_mix[l],
                      w_in[l], norm_q_lat[l], w_uq[l], norm_kv_lat[l], w_ukv[l], norm_qk_q[l], norm_qk_k[l],
                      conv_w[l], conv_b[l], conv_ln_g[l], conv_ln_b[l], norm_out_conv[l], norm_out_attn[l],
                      w_out[l], norm_ffn[l], w_router[l], b_router[l], w_gate_up[l], b_gate_up[l], w_down[l],
                      b_down[l])
        h_p, h_s = outs[0], outs[1]
        per_layer.append(outs[2:])
    stacked = [jnp.stack([pl_[i] for pl_ in per_layer]) for i in range(6)]
    return (h_p, h_s, *stacked)
```

```python
import functools

import jax
import jax.numpy as jnp
import numpy as np
from jax import lax
from jax.experimental import pallas as pl
from jax.experimental.pallas import tpu as pltpu
from jax.experimental.pallas import tpu_sc as plsc

N_HEADS = 8
QK_NOPE_DIM = 64
QK_ROPE_DIM = 32
ROPE_HALF = QK_ROPE_DIM // 2
QK_DIM = QK_NOPE_DIM + QK_ROPE_DIM
V_DIM = 64
CONV_WIDTH = 31
N_EXPERTS = 32
TOP_K = 4
SWIGLU_LIMIT = 7.0
SWIGLU_ALPHA = 1.702
ROPE_THETA = 10000.0
NORM_EPS = 1e-6
ATTN_SCALE = QK_DIM ** -0.5
LOG2_E = 1.4426950408889634
PAGE_SIZE = 128

LANES = 128
HEAD_PAD = LANES
VMEM_LIMIT = 56 * 1024 * 1024

f32 = jnp.float32
bf16 = jnp.bfloat16


def _cparams(*sem):
    return pltpu.CompilerParams(dimension_semantics=sem, vmem_limit_bytes=VMEM_LIMIT)


def _rope_head_tables(pos):
    inv_freq = ROPE_THETA ** (-jnp.arange(ROPE_HALF, dtype=f32) / ROPE_HALF)
    ang = pos.astype(f32)[:, None] * inv_freq[None, :]
    cos, sin = jnp.cos(ang), jnp.sin(ang)
    n = pos.shape[0]
    z = lambda w: jnp.zeros((n, w), f32)
    c = jnp.concatenate([jnp.ones((n, QK_NOPE_DIM), f32), cos, cos, z(32)], axis=1)
    s1 = jnp.concatenate([z(QK_NOPE_DIM + ROPE_HALF), sin, z(32)], axis=1)
    s2 = jnp.concatenate([z(QK_NOPE_DIM), -sin, z(ROPE_HALF + 32)], axis=1)
    return jnp.concatenate([c, s1, s2], axis=1)


def _rope_head(x, tab):
    c, s1, s2 = tab[:, :LANES], tab[:, LANES:2 * LANES], tab[:, 2 * LANES:]
    return x * c + pltpu.roll(x, ROPE_HALF, 1) * s1 + pltpu.roll(x, LANES - ROPE_HALF, 1) * s2


def _mix_in_kernel(x_ref, tab_ref, gmix_ref, win_ref, gq_ref, wuq_ref, gkv_ref, gqq_ref,
                   wk_ref, wvt_ref, gqk_ref, *rest, d_conv, q_rank, kv_rank, q_scale, conv_seq_tiles):
    if conv_seq_tiles:
        cw_ref, cb_ref, lg_ref, lb_ref, glu_ref, q_ref, c_ref, kr_ref, k_ref, vt_ref, cv_ref, hist_s = rest
    else:
        glu_ref, q_ref, c_ref, kr_ref, k_ref, vt_ref = rest
    tm = x_ref.shape[0]
    rows = tm // MIX_SUB_TILES
    gqq = gqq_ref[...] * q_scale
    gqk = gqk_ref[...]

    if conv_seq_tiles:
        @pl.when(pl.program_id(0) % conv_seq_tiles == 0)
        def _():
            hist_s[...] = jnp.zeros(hist_s.shape, f32)

    def conv(r0):
        tc = min(rows, CONV_CHUNK)
        for t0 in range(r0, r0 + rows, tc):
            for lt in range(d_conv // LANES):
                ls = slice(lt * LANES, (lt + 1) * LANES)
                hist = hist_s[:, ls] if t0 == 0 else glu_ref[t0 - CONV_PAD:t0, ls]
                window = jnp.concatenate([hist, glu_ref[t0:t0 + tc, ls]], axis=0)
                cv_ref[t0:t0 + tc, ls] = _conv_taps(window, cw_ref, cb_ref, ls, tc)
            cv_ref[t0:t0 + tc, :] = _ln_swish(cv_ref[t0:t0 + tc, :], lg_ref, lb_ref)

    def project(r):
        x = x_ref[r, :]
        xn = x * lax.rsqrt(jnp.mean(x * x, axis=-1, keepdims=True) + NORM_EPS) * gmix_ref[...]
        return jnp.dot(xn.astype(bf16), win_ref[...], preferred_element_type=f32)

    def latents(r, proj):
        o = 0
        a = proj[:, o:o + d_conv]; o += d_conv
        gt = proj[:, o:o + d_conv]; o += d_conv
        q_lat = proj[:, o:o + q_rank]; o += q_rank
        kv_lat = proj[:, o:o + kv_rank]; o += kv_rank
        kr_raw = proj[:, o:o + LANES]
        glu_ref[r, :] = a * jax.nn.sigmoid(gt)
        tab = tab_ref[r, :]
        qn = q_lat * lax.rsqrt(jnp.mean(q_lat * q_lat, axis=-1, keepdims=True) + NORM_EPS) * gq_ref[...]
        q = jnp.dot(qn.astype(bf16), wuq_ref[...], preferred_element_type=f32)
        c_kv = kv_lat * lax.rsqrt(jnp.mean(kv_lat * kv_lat, axis=-1, keepdims=True) + NORM_EPS) * gkv_ref[...]
        c_ref[r, :] = c_kv
        kr = _rope_head(kr_raw, tab)
        kr_ref[r, :] = kr[:, QK_NOPE_DIM:QK_NOPE_DIM + QK_ROPE_DIM]
        c_bf = c_kv.astype(bf16)
        kn = jnp.dot(c_bf, wk_ref[...], preferred_element_type=f32)
        vt_ref[0, :, r] = lax.dot_general(wvt_ref[...], c_bf, NT_DIMS,
                                          preferred_element_type=f32).astype(vt_ref.dtype)
        return q, kn, kr, tab

    def heads(r, q, kn, kr, tab):
        for h in range(N_HEADS):
            sl = slice(h * HEAD_PAD, (h + 1) * HEAD_PAD)
            qh = _rope_head(q[:, sl], tab)
            qh = qh * lax.rsqrt(jnp.sum(qh * qh, axis=-1, keepdims=True) * (1.0 / QK_DIM) + NORM_EPS) * gqq
            q_ref[r, sl] = qh.astype(q_ref.dtype)
            kh = kn[:, sl] + kr
            kh = kh * lax.rsqrt(jnp.sum(kh * kh, axis=-1, keepdims=True) * (1.0 / QK_DIM) + NORM_EPS) * gqk
            k_ref[r, sl] = kh.astype(k_ref.dtype)

    slices = [pl.ds(i * rows, rows) for i in range(MIX_SUB_TILES)]
    proj = project(slices[0])
    for i, r in enumerate(slices):
        nxt = project(slices[i + 1]) if i + 1 < len(slices) else None
        heads(r, *latents(r, proj))
        if conv_seq_tiles:
            conv(i * rows)
        proj = nxt
    if conv_seq_tiles:
        hist_s[...] = glu_ref[tm - CONV_PAD:tm, :]


MIX_TILE = 1024
MIX_SUB_TILES = 4
CONV_CHUNK = 128


def _head_gain(g_pairs):
    g_rot = g_pairs[QK_NOPE_DIM:]
    return jnp.concatenate([g_pairs[:QK_NOPE_DIM], g_rot, g_rot, jnp.zeros((32,), f32)])[None, :]


def _mix_in_weights(w_in, w_uq, w_ukv, d_conv, q_rank, kv_rank):
    d_model = w_in.shape[0]
    base = 2 * d_conv + q_rank + kv_rank
    w_kr = w_in[:, base:base + QK_ROPE_DIM]
    w_kr_pad = jnp.concatenate([jnp.zeros((d_model, QK_NOPE_DIM), f32), w_kr,
                                jnp.zeros((d_model, 32), f32)], axis=1)
    win = jnp.concatenate([w_in[:, :base], w_kr_pad], axis=1).astype(bf16)
    wuq = jnp.pad(w_uq, ((0, 0), (0, 0), (0, HEAD_PAD - QK_DIM))).reshape(q_rank, N_HEADS * HEAD_PAD).astype(bf16)
    wk = jnp.pad(w_ukv[:, :, :QK_NOPE_DIM], ((0, 0), (0, 0), (0, HEAD_PAD - QK_NOPE_DIM)))
    wk = wk.reshape(kv_rank, N_HEADS * HEAD_PAD).astype(bf16)
    wvt = w_ukv[:, :, QK_NOPE_DIM:].reshape(kv_rank, N_HEADS * V_DIM).T.astype(bf16)
    return win, wuq, wk, wvt


def _mix_in(x2d, tab, tm, g_mix, win, g_q_lat, wuq, g_kv_lat, g_qk_q, wk, wvt, g_qk_k, d_conv, q_rank, kv_rank,
            seq_len, q_scale, conv=None):
    n, d_model = x2d.shape
    n_tab_blocks = tab.shape[0] // tm
    seq_tiles = seq_len // tm
    row = lambda i: (i, 0)
    const = lambda i: (0, 0)
    full = lambda a: pl.BlockSpec(a.shape, const)
    gm, gq, gkv = g_mix[None, :], g_q_lat[None, :], g_kv_lat[None, :]
    gqq, gqk = _head_gain(g_qk_q), _head_gain(g_qk_k)
    hp = N_HEADS * HEAD_PAD
    out_shape = [
        jax.ShapeDtypeStruct((n, d_conv), f32),
        jax.ShapeDtypeStruct((n, hp), bf16),
        jax.ShapeDtypeStruct((n, kv_rank), f32),
        jax.ShapeDtypeStruct((n, QK_ROPE_DIM), f32),
        jax.ShapeDtypeStruct((n, hp), bf16),
        jax.ShapeDtypeStruct((n // seq_len, N_HEADS * V_DIM, seq_len), bf16),
    ]
    out_specs = [pl.BlockSpec((tm, s.shape[1]), row) for s in out_shape[:-1]]
    out_specs.append(pl.BlockSpec((1, N_HEADS * V_DIM, tm), lambda i: (i // seq_tiles, 0, i % seq_tiles)))
    operands = [x2d, tab, gm, win, gq, wuq, gkv, gqq, wk, wvt, gqk]
    in_specs = [pl.BlockSpec((tm, d_model), row),
                pl.BlockSpec((tm, tab.shape[1]), lambda i: (i % n_tab_blocks, 0)),
                full(gm), full(win), full(gq), full(wuq), full(gkv), full(gqq), full(wk), full(wvt), full(gqk)]
    scratch = []
    if conv is not None:
        conv_w, conv_b, ln_g, ln_b = conv
        extra = [conv_w, conv_b[None, :], ln_g[None, :], ln_b[None, :]]
        operands += extra
        in_specs += [full(a) for a in extra]
        out_shape.append(jax.ShapeDtypeStruct((n, d_conv), f32))
        out_specs.append(pl.BlockSpec((tm, d_conv), row))
        scratch = [pltpu.VMEM((CONV_PAD, d_conv), f32)]
    return pl.pallas_call(
        functools.partial(_mix_in_kernel, d_conv=d_conv, q_rank=q_rank, kv_rank=kv_rank, q_scale=q_scale,
                          conv_seq_tiles=seq_tiles if conv is not None else 0),
        grid=(n // tm,),
        in_specs=in_specs,
        out_specs=tuple(out_specs),
        out_shape=tuple(out_shape),
        scratch_shapes=scratch,
        compiler_params=_cparams("arbitrary"),
        name="mix_in",
    )(*operands)


CONV_PAD = 32


SUBLANES = 8


def _conv_taps(window, w_ref, b_ref, ls, tc):
    lead = CONV_PAD - (CONV_WIDTH - 1)
    acc = jnp.zeros((tc, LANES), f32) + b_ref[:, ls]
    for r in range(SUBLANES):
        taps = [(a, a * SUBLANES + r - lead) for a in range(CONV_PAD // SUBLANES + 1)
                if 0 <= a * SUBLANES + r - lead < CONV_WIDTH]
        shifted = window if r == 0 else pltpu.roll(window, window.shape[0] - r, 0)
        for a, j in taps:
            acc = acc + shifted[a * SUBLANES:a * SUBLANES + tc] * w_ref[j:j + 1, ls]
    return acc


def _ln_swish(y, g_ref, beta_ref):
    mu = jnp.mean(y, axis=-1, keepdims=True)
    yc = y - mu
    var = jnp.mean(yc * yc, axis=-1, keepdims=True)
    z = yc * lax.rsqrt(var + NORM_EPS) * g_ref[...] + beta_ref[...]
    return z * jax.nn.sigmoid(z)


def _conv_sample_kernel(x_ref, st_ref, w_ref, b_ref, g_ref, beta_ref, o_ref):
    bb, tc, n_ch = x_ref.shape

    def one(b, carry):
        for lt in range(n_ch // LANES):
            ls = slice(lt * LANES, (lt + 1) * LANES)
            window = jnp.concatenate([st_ref[b, :, ls], x_ref[b, :, ls]], axis=0)
            o_ref[b, :, ls] = _conv_taps(window, w_ref, b_ref, ls, tc)
        o_ref[b] = _ln_swish(o_ref[b], g_ref, beta_ref)
        return carry

    lax.fori_loop(0, bb, one, 0)


def _conv_specs(conv_w, n_ch):
    const = lambda i: (0, 0)
    vec = pl.BlockSpec((1, n_ch), const)
    return [pl.BlockSpec(conv_w.shape, const), vec, vec, vec]


def _conv_sample(glu, state, conv_w, conv_b, ln_g, ln_b, bb=32):
    n_b, t_len, n_ch = glu.shape
    hist = jnp.pad(state, ((0, 0), (CONV_PAD - state.shape[1], 0), (0, 0)))
    return pl.pallas_call(
        _conv_sample_kernel,
        grid=(n_b // bb,),
        in_specs=[pl.BlockSpec((bb, t_len, n_ch), lambda i: (i, 0, 0)),
                  pl.BlockSpec((bb, CONV_PAD, n_ch), lambda i: (i, 0, 0))] + _conv_specs(conv_w, n_ch),
        out_specs=pl.BlockSpec((bb, t_len, n_ch), lambda i: (i, 0, 0)),
        out_shape=jax.ShapeDtypeStruct(glu.shape, f32),
        compiler_params=_cparams("parallel"),
        name="conv_sample",
    )(glu, hist, conv_w, conv_b[None, :], ln_g[None, :], ln_b[None, :])


NT_DIMS = (((1,), (1,)), ((), ()))


def _attn_prompt_kernel(q_ref, k_ref, vt_ref, o_ref, *, tq, tk):
    assert tq == tk
    qi = pl.program_id(2)
    n_h = q_ref.shape[2] // HEAD_PAD
    qs = [q_ref[0, :, h * HEAD_PAD:(h + 1) * HEAD_PAD] for h in range(n_h)]

    def block(k0, nk, q_lo, carry, masked):
        nq = tq - q_lo
        if masked:
            key = k0 + lax.broadcasted_iota(jnp.int32, (nk, nq), 0)
            qry = qi * tq + q_lo + lax.broadcasted_iota(jnp.int32, (nk, nq), 1)
            visible = key <= qry
        ones = jnp.ones((ONES_ROWS, nk), bf16)
        sts = []
        for h in range(n_h):
            k_h = k_ref[0, pl.ds(k0, nk), h * HEAD_PAD:(h + 1) * HEAD_PAD]
            sts.append(lax.dot_general(k_h, qs[h][q_lo:], NT_DIMS, preferred_element_type=f32))
        out = []
        for h, (m_all, acc_all) in enumerate(carry):
            m, acc = m_all[:, q_lo:], acc_all[:, q_lo:]
            st = jnp.where(visible, sts[h], -jnp.inf) if masked else sts[h]
            m_new = jnp.maximum(m, jnp.max(st, axis=0, keepdims=True))
            alpha = jnp.exp2(m - m_new)
            p = jnp.exp2(st - m_new).astype(bf16)
            vt_h = jnp.concatenate([vt_ref[0, h * V_DIM:(h + 1) * V_DIM, pl.ds(k0, nk)], ones], axis=0)
            acc = alpha * acc + jnp.dot(vt_h, p, preferred_element_type=f32)
            if q_lo:
                m_new = jnp.concatenate([m_all[:, :q_lo], m_new], axis=1)
                acc = jnp.concatenate([acc_all[:, :q_lo], acc], axis=1)
            out.append((m_new, acc))
        return tuple(out)

    carry = tuple((jnp.full((1, tq), -jnp.inf, f32), jnp.zeros((V_DIM + ONES_ROWS, tq), f32)) for _ in range(n_h))
    carry = lax.fori_loop(
        0, qi, lambda kb, c: block(pl.multiple_of(kb * tk, tk), tk, 0, c, False), carry)
    half = tk // 2
    d0 = pl.multiple_of(qi * tq, tq)
    carry = block(d0, half, 0, carry, True)
    carry = block(pl.multiple_of(d0 + half, half), half, half, carry, True)
    o_t = jnp.concatenate([acc[:V_DIM] / acc[V_DIM:V_DIM + 1] for (_, acc) in carry], axis=0)
    o_ref[0] = o_t.T


ONES_ROWS = 16


ATTN_HEAD_GROUP = 4


def _attn_prompt(q, k, vt, tq, tk):
    n_b, t_len, _ = q.shape
    g = ATTN_HEAD_GROUP
    return pl.pallas_call(
        functools.partial(_attn_prompt_kernel, tq=tq, tk=tk),
        grid=(n_b, N_HEADS // g, t_len // tq),
        in_specs=[pl.BlockSpec((1, tq, g * HEAD_PAD), lambda b, h, i: (b, i, h)),
                  pl.BlockSpec((1, t_len, g * HEAD_PAD), lambda b, h, i: (b, 0, h)),
                  pl.BlockSpec((1, g * V_DIM, t_len), lambda b, h, i: (b, h, 0))],
        out_specs=pl.BlockSpec((1, tq, g * V_DIM), lambda b, h, i: (b, i, h)),
        out_shape=jax.ShapeDtypeStruct((n_b, t_len, N_HEADS * V_DIM), f32),
        compiler_params=_cparams("parallel", "parallel", "arbitrary"),
        name="attn_prompt",
    )(q, k, vt)


N_TH = 64


def _q_absorb_kernel(q_ref, gk_ref, wukt_ref, sel_ref, qabs_ref, qr_ref):
    gk = gk_ref[...]
    kv_rank = wukt_ref.shape[-1]
    for h in range(N_HEADS):
        qg = (q_ref[:, h * HEAD_PAD:(h + 1) * HEAD_PAD].astype(f32) * gk).astype(bf16)
        qabs_ref[:, h * kv_rank:(h + 1) * kv_rank] = jnp.dot(
            qg, wukt_ref[h], preferred_element_type=f32).astype(qabs_ref.dtype)
        qr_ref[:, h * QK_ROPE_DIM:(h + 1) * QK_ROPE_DIM] = jnp.dot(
            qg, sel_ref[...], preferred_element_type=f32).astype(qr_ref.dtype)


def _q_absorb(q2d, g_qk_k, w_ukv):
    n = q2d.shape[0]
    kv_rank = w_ukv.shape[0]
    wukt = jnp.transpose(w_ukv[:, :, :QK_NOPE_DIM], (1, 2, 0))
    wukt = jnp.pad(wukt, ((0, 0), (0, HEAD_PAD - QK_NOPE_DIM), (0, 0))).astype(bf16)
    sel = jnp.zeros((HEAD_PAD, QK_ROPE_DIM), f32).at[
        QK_NOPE_DIM + jnp.arange(QK_ROPE_DIM), jnp.arange(QK_ROPE_DIM)].set(1.0).astype(bf16)
    gk = _head_gain(g_qk_k)
    full = lambda a: pl.BlockSpec(a.shape, lambda i: (0,) * a.ndim)
    return pl.pallas_call(
        _q_absorb_kernel,
        grid=(1,),
        in_specs=[full(q2d), full(gk), full(wukt), full(sel)],
        out_specs=(pl.BlockSpec((n, N_HEADS * kv_rank), lambda i: (0, 0)),
                   pl.BlockSpec((n, N_HEADS * QK_ROPE_DIM), lambda i: (0, 0))),
        out_shape=(jax.ShapeDtypeStruct((n, N_HEADS * kv_rank), bf16),
                   jax.ShapeDtypeStruct((n, N_HEADS * QK_ROPE_DIM), bf16)),
        compiler_params=_cparams("arbitrary"),
        name="q_absorb",
    )(q2d, gk, wukt, sel)


SUB_PAGES = 32


def _latent_products_t(lhs, c_bf):
    n = c_bf.shape[0]
    full = lax.dot_general(lhs, c_bf, NT_DIMS, preferred_element_type=f32)
    knt = full[N_TH:]
    ssq = jnp.sum((knt * knt).reshape(QK_NOPE_DIM, N_HEADS, n), axis=0)
    return full[:N_TH], ssq


def _scores_t(st_lat, ssq_nope, krt, qr):
    n = krt.shape[1]
    ssq = ssq_nope + jnp.sum(krt * krt, axis=0, keepdims=True)
    rinv = lax.rsqrt(ssq * (1.0 / QK_DIM) + NORM_EPS)
    st = st_lat + jnp.dot(qr, krt.astype(bf16), preferred_element_type=f32)
    return (st.reshape(N_TH // N_HEADS, N_HEADS, n) * rinv[None]).reshape(N_TH, n)


def _softmax_update_t(st, c_bf, state, mask=None):
    m, l, acc = state
    if mask is not None:
        st = jnp.where(mask, st, -jnp.inf)
    m_new = jnp.maximum(m, jnp.max(st, axis=1, keepdims=True))
    alpha = jnp.exp(m - m_new)
    p = jnp.exp(st - m_new)
    l = alpha * l + jnp.sum(p, axis=1, keepdims=True)
    acc = alpha * acc + jnp.dot(p.astype(bf16), c_bf, preferred_element_type=f32)
    return m_new, l, acc


def _attn_sample_kernel(pt_ref, qabs_ref, qr_ref, cnew_ref, krtnew_ref, wukt_ref, cache_c, cache_krt, o_ref,
                            cbuf, krbuf, sems, m_s, l_s, acc_s, *, pp):
    s, j = pl.program_id(0), pl.program_id(1)
    n_j = pl.num_programs(1)
    step = s * n_j + j
    slot = step % 2

    def page_copies(step_idx, buf_slot):
        copies = []
        for i in range(pp):
            page = pt_ref[step_idx * pp + i]
            copies.append(pltpu.make_async_copy(cache_c.at[page], cbuf.at[buf_slot, i], sems.at[0, buf_slot]))
            copies.append(pltpu.make_async_copy(cache_krt.at[page], krbuf.at[buf_slot, i], sems.at[1, buf_slot]))
        return copies

    @pl.when(step == 0)
    def _():
        for cp in page_copies(0, 0):
            cp.start()

    @pl.when(step + 1 < pl.num_programs(0) * n_j)
    def _():
        for cp in page_copies(step + 1, 1 - slot):
            cp.start()

    @pl.when(j == 0)
    def _():
        m_s[...] = jnp.full(m_s.shape, -jnp.inf, f32)
        l_s[...] = jnp.zeros(l_s.shape, f32)
        acc_s[...] = jnp.zeros(acc_s.shape, f32)

    for cp in page_copies(step, slot):
        cp.wait()

    qr = qr_ref[0]
    lhs = jnp.concatenate([qabs_ref[0], wukt_ref[...]], axis=0)
    sub = min(SUB_PAGES, pp)
    blocks = list(range(0, pp, sub))

    def load(b):
        c_bf = jnp.concatenate([cbuf[slot, i].astype(bf16) for i in range(b, b + sub)], axis=0)
        return (c_bf,) + _latent_products_t(lhs, c_bf)

    state = (m_s[...], l_s[...], acc_s[...])
    nxt = load(blocks[0])
    for n, b in enumerate(blocks):
        c_bf, st_lat, ssq = nxt
        if n + 1 < len(blocks):
            nxt = load(blocks[n + 1])
        krt = jnp.concatenate([krbuf[slot, i] for i in range(b, b + sub)], axis=1)
        state = _softmax_update_t(_scores_t(st_lat, ssq, krt, qr), c_bf, state)
    m_s[...], l_s[...], acc_s[...] = state

    @pl.when(j == n_j - 1)
    def _():
        c_bf = cnew_ref[0].astype(bf16)
        n_pad = c_bf.shape[0]
        key_t = lax.broadcasted_iota(jnp.int32, (N_TH, n_pad), 1)
        q_t = lax.broadcasted_iota(jnp.int32, (N_TH, n_pad), 0) // N_HEADS
        st = _scores_t(*_latent_products_t(lhs, c_bf), krtnew_ref[0], qr)
        m, l, acc = _softmax_update_t(st, c_bf, state, mask=key_t <= q_t)
        o_ref[0] = acc / l


def _attn_sample(page_table, qabs, qr, c_new, kr_new, cache_c, cache_kr, wuk_perm, pp):
    n_seq, n_pages = page_table.shape
    kv_rank = cache_c.shape[-1]
    t_new = c_new.shape[1]
    seq3 = lambda s, j, pt: (s, 0, 0)
    cache_kr_t = jnp.swapaxes(cache_kr, 1, 2)
    c_new_pad = jnp.pad(c_new, ((0, 0), (0, PAGE_SIZE - t_new), (0, 0)))
    krt_new_pad = jnp.pad(jnp.swapaxes(kr_new, 1, 2), ((0, 0), (0, 0), (0, PAGE_SIZE - t_new)))
    grid_spec = pltpu.PrefetchScalarGridSpec(
        num_scalar_prefetch=1,
        grid=(n_seq, n_pages // pp),
        in_specs=[pl.BlockSpec((1, N_TH, kv_rank), seq3),
                  pl.BlockSpec((1, N_TH, QK_ROPE_DIM), seq3),
                  pl.BlockSpec((1, PAGE_SIZE, kv_rank), seq3),
                  pl.BlockSpec((1, QK_ROPE_DIM, PAGE_SIZE), seq3),
                  pl.BlockSpec(wuk_perm.shape, lambda s, j, pt: (0, 0)),
                  pl.BlockSpec(memory_space=pl.ANY),
                  pl.BlockSpec(memory_space=pl.ANY)],
        out_specs=pl.BlockSpec((1, N_TH, kv_rank), seq3),
        scratch_shapes=[pltpu.VMEM((2, pp, PAGE_SIZE, kv_rank), f32),
                        pltpu.VMEM((2, pp, QK_ROPE_DIM, PAGE_SIZE), f32),
                        pltpu.SemaphoreType.DMA((2, 2)),
                        pltpu.VMEM((N_TH, 1), f32), pltpu.VMEM((N_TH, 1), f32),
                        pltpu.VMEM((N_TH, kv_rank), f32)],
    )
    return pl.pallas_call(
        functools.partial(_attn_sample_kernel, pp=pp),
        grid_spec=grid_spec,
        out_shape=jax.ShapeDtypeStruct((n_seq, N_TH, kv_rank), f32),
        compiler_params=_cparams("arbitrary", "arbitrary"),
        name="attn_sample",
    )(page_table.reshape(-1), qabs, qr, c_new_pad, krt_new_pad, wuk_perm, cache_c, cache_kr_t)


def _v_up_kernel(lat_ref, wuv_ref, o_ref):
    rows = lat_ref.shape[0]
    full = jnp.dot(lat_ref[...].astype(bf16), wuv_ref[...], preferred_element_type=f32)
    head_of_row = lax.broadcasted_iota(jnp.int32, full.shape, 0) % N_HEADS
    head_of_lane = lax.broadcasted_iota(jnp.int32, full.shape, 1) // V_DIM
    own = jnp.where(head_of_row == head_of_lane, full, 0.0)
    o_ref[...] = jnp.sum(own.reshape(rows // N_HEADS, N_HEADS, full.shape[1]), axis=1)


def _v_up(lat2d, wuv, rows):
    n, kv_rank = lat2d.shape
    d_attn = wuv.shape[1]
    return pl.pallas_call(
        _v_up_kernel,
        grid=(n // rows,),
        in_specs=[pl.BlockSpec((rows, kv_rank), lambda i: (i, 0)), pl.BlockSpec(wuv.shape, lambda i: (0, 0))],
        out_specs=pl.BlockSpec((rows // N_HEADS, d_attn), lambda i: (i, 0)),
        out_shape=jax.ShapeDtypeStruct((n // N_HEADS, d_attn), f32),
        compiler_params=_cparams("parallel"),
        name="v_up",
    )(lat2d, wuv)


def _attn_sample_all(q_s2d, c_new, kr_new, page_table, cache_c, cache_kr, w_ukv, g_qk_k, pp):
    n_seq, t_new, kv_rank = c_new.shape
    qabs, qr = _q_absorb(q_s2d, g_qk_k, w_ukv)
    qabs = qabs.reshape(n_seq, t_new * N_HEADS, kv_rank)
    qr = qr.reshape(n_seq, t_new * N_HEADS, QK_ROPE_DIM)
    wuk_perm = jnp.transpose(w_ukv[:, :, :QK_NOPE_DIM], (2, 1, 0)).reshape(-1, kv_rank).astype(bf16)
    lat = _attn_sample(page_table, qabs, qr, c_new, kr_new, cache_c, cache_kr, wuk_perm, pp)
    wuv = w_ukv[:, :, QK_NOPE_DIM:].reshape(kv_rank, N_HEADS * V_DIM).astype(bf16)
    rows = min(512, n_seq * N_TH)
    return _v_up(lat.reshape(n_seq * N_TH, kv_rank), wuv, rows)


def _rms(x, g):
    return x * lax.rsqrt(jnp.mean(x * x, axis=-1, keepdims=True) + NORM_EPS) * g


def _split_bf16(x):
    hi = x.astype(bf16)
    return hi, (x - hi.astype(f32)).astype(bf16)


PACK_W = 256
u32 = jnp.uint32


def _pack_bf16_pairs(x):
    bits = lax.bitcast_convert_type(x.astype(bf16).astype(f32), u32)
    pieces = []
    for p in range(x.shape[1] // (2 * PACK_W)):
        lo = bits[:, 2 * p * PACK_W:(2 * p + 1) * PACK_W]
        hi = bits[:, (2 * p + 1) * PACK_W:(2 * p + 2) * PACK_W]
        pieces.append((lo >> 16) | (hi & jnp.uint32(0xFFFF0000)))
    return pieces


def _unpack_bf16_pairs(words):
    return (lax.bitcast_convert_type(words << 16, f32),
            lax.bitcast_convert_type(words & jnp.uint32(0xFFFF0000), f32))


def _merge_kernel(xp_ref, cvp_ref, atp_ref, xs_ref, cvs_ref, ats_ref, gc_ref, ga_ref, woc_ref, woa_ref, gf_ref,
                  wrh_ref, wrl_ref, br_ref, h_ref, xna_ref, xnb_ref, exp_ref, gate_ref, rank_ref, cnt_ref, cnt_s,
                  *, n_prompt_tiles):
    i = pl.program_id(0)
    tm = xp_ref.shape[0]

    @pl.when(i == 0)
    def _():
        cnt_s[...] = jnp.zeros(cnt_s.shape, f32)

    is_p = i < n_prompt_tiles
    rows = tm

    def project(r):
        x = jnp.where(is_p, xp_ref[r, :], xs_ref[r, :])
        yc = _rms(jnp.where(is_p, cvp_ref[r, :], cvs_ref[r, :]), gc_ref[...]).astype(bf16)
        ya = _rms(jnp.where(is_p, atp_ref[r, :], ats_ref[r, :]), ga_ref[...]).astype(bf16)
        h = x + jnp.dot(yc, woc_ref[...], preferred_element_type=f32) \
            + jnp.dot(ya, woa_ref[...], preferred_element_type=f32)
        h_ref[r, :] = h
        xn = _rms(h, gf_ref[...])
        xh, xl = _split_bf16(xn)
        xna_ref[r, :], xnb_ref[r, :] = _pack_bf16_pairs(xn)
        return (jnp.dot(xh, wrh_ref[...], preferred_element_type=f32)
                + jnp.dot(xl, wrh_ref[...], preferred_element_type=f32)
                + jnp.dot(xh, wrl_ref[...], preferred_element_type=f32)) + br_ref[...]

    def route(logits, cnt):
        lt = logits.T[:N_EXPERTS]
        eidx = lax.broadcasted_iota(jnp.int32, lt.shape, 0)
        tops, idxs = [], []
        cur = lt
        for _k in range(TOP_K):
            mk = jnp.max(cur, axis=0, keepdims=True)
            ik = jnp.min(jnp.where(cur == mk, eidx, N_EXPERTS), axis=0, keepdims=True)
            tops.append(mk)
            idxs.append(ik)
            cur = jnp.where(eidx == ik, -jnp.inf, cur)
        es = [jnp.exp(t - tops[0]) for t in tops]
        den = es[0] + es[1] + es[2] + es[3]
        onehots = [(eidx == ik).astype(f32) for ik in idxs]
        oh = onehots[0] + onehots[1] + onehots[2] + onehots[3]
        s_i = lax.broadcasted_iota(jnp.int32, (rows, rows), 0)
        t_i = lax.broadcasted_iota(jnp.int32, (rows, rows), 1)
        before = jnp.dot(oh.astype(bf16), (s_i < t_i).astype(bf16), preferred_element_type=f32) + cnt
        ranks = [jnp.sum(oh_k * before, axis=0, keepdims=True).astype(jnp.int32) for oh_k in onehots]
        pad_i = jnp.zeros((ROUTE_ROWS - TOP_K, rows), jnp.int32)
        pad_f = jnp.zeros((ROUTE_ROWS - TOP_K, rows), f32)
        exp_ref[...] = jnp.concatenate(idxs + [pad_i], axis=0)
        gate_ref[...] = jnp.concatenate([e / den for e in es] + [pad_f], axis=0)
        rank_ref[...] = jnp.concatenate(ranks + [pad_i], axis=0)
        return cnt + jnp.sum(oh, axis=1, keepdims=True)

    cnt = route(project(pl.ds(0, tm)), cnt_s[...])
    cnt_s[...] = cnt
    cnt_ref[...] = cnt


ROUTE_ROWS = 8


def _merge(x_p, cv_p, at_p, x_s, cv_s, at_s, g_out_conv, g_out_attn, w_out, g_ffn, w_router, b_router, tm):
    (n_p, d_model), n_s = x_p.shape, x_s.shape[0]
    n = n_p + n_s
    d_conv, d_attn = cv_p.shape[1], at_p.shape[1]
    woc, woa = w_out[:d_conv].astype(bf16), w_out[d_conv:].astype(bf16)
    wr = jnp.pad(w_router, ((0, 0), (0, LANES - N_EXPERTS)))
    wrh, wrl = _split_bf16(wr)
    br = jnp.concatenate([b_router.astype(f32), jnp.full((LANES - N_EXPERTS,), -jnp.inf, f32)])[None, :]
    gc, ga, gf = g_out_conv[None, :], g_out_attn[None, :], g_ffn[None, :]
    tiles_p, tiles_s = n_p // tm, n_s // tm
    full = lambda a: pl.BlockSpec(a.shape, lambda i: (0, 0))
    rows = lambda w: pl.BlockSpec((tm, w), lambda i: (i, 0))
    rows_p = lambda w: pl.BlockSpec((tm, w), lambda i: (jnp.minimum(i, tiles_p - 1), 0))
    rows_s = lambda w: pl.BlockSpec((tm, w), lambda i: (jnp.maximum(i - tiles_p, 0), 0))
    assert d_model == 4 * PACK_W
    route = lambda dt: jax.ShapeDtypeStruct((ROUTE_ROWS, n), dt)
    route_blk = pl.BlockSpec((ROUTE_ROWS, tm), lambda i: (0, i))
    out_shape = (jax.ShapeDtypeStruct((n, d_model), f32),
                 jax.ShapeDtypeStruct((n, PACK_W), u32), jax.ShapeDtypeStruct((n, PACK_W), u32),
                 route(jnp.int32), route(f32), route(jnp.int32), jax.ShapeDtypeStruct((N_EXPERTS, 1), f32))
    return pl.pallas_call(
        functools.partial(_merge_kernel, n_prompt_tiles=tiles_p),
        grid=(tiles_p + tiles_s,),
        in_specs=[rows_p(d_model), rows_p(d_conv), rows_p(d_attn), rows_s(d_model), rows_s(d_conv), rows_s(d_attn),
                  full(gc), full(ga), full(woc), full(woa), full(gf), full(wrh), full(wrl), full(br)],
        out_specs=(rows(d_model), rows(PACK_W), rows(PACK_W), route_blk, route_blk, route_blk,
                   pl.BlockSpec((N_EXPERTS, 1), lambda i: (0, 0))),
        out_shape=out_shape,
        scratch_shapes=[pltpu.VMEM((N_EXPERTS, 1), f32)],
        compiler_params=_cparams("arbitrary"),
        name="merge",
    )(x_p, cv_p, at_p, x_s, cv_s, at_s, gc, ga, woc, woa, gf, wrh, wrl, br)


def _experts_kernel(be_ref, br_ref, xa_ref, xb_ref, wgu_ref, bgu_ref, wd_ref, bd_ref, ya_ref, yb_ref, wgu_s, wd_s):
    i = pl.program_id(0)
    d_expert = wd_ref.shape[1]
    blk = xa_ref.shape[0]

    @pl.when((i == 0) | (be_ref[i] != be_ref[jnp.maximum(i - 1, 0)]))
    def _():
        wgu_s[...] = wgu_ref[0].astype(bf16)
        wd_s[...] = wd_ref[0].astype(bf16)

    @pl.when(br_ref[i] > 0)
    def _():
        x = jnp.concatenate(_unpack_bf16_pairs(xa_ref[...]) + _unpack_bf16_pairs(xb_ref[...]), axis=1)
        live = lax.broadcasted_iota(jnp.int32, (blk, 1), 0) < br_ref[i]
        x = jnp.where(live, x, 0.0).astype(bf16)
        gu = jnp.dot(x, wgu_s[...], preferred_element_type=f32) + bgu_ref[0]
        gate = jnp.minimum(gu[:, :d_expert], SWIGLU_LIMIT)
        up = jnp.clip(gu[:, d_expert:], -SWIGLU_LIMIT, SWIGLU_LIMIT)
        hid = (up + 1.0) * gate * jax.nn.sigmoid(SWIGLU_ALPHA * gate)
        y = jnp.dot(hid.astype(bf16), wd_s[...], preferred_element_type=f32) + bd_ref[0]
        ya_ref[...], yb_ref[...] = _pack_bf16_pairs(y)

    @pl.when(br_ref[i] <= 0)
    def _():
        ya_ref[...] = jnp.zeros(ya_ref.shape, ya_ref.dtype)
        yb_ref[...] = jnp.zeros(yb_ref.shape, yb_ref.dtype)


def _experts(block_expert, block_rows, xs_a, xs_b, w_gate_up, b_gate_up, w_down, b_down, blk):
    n_slots = xs_a.shape[0]
    n_exp, d_model, d_gu = w_gate_up.shape
    d_expert = w_down.shape[1]
    slots = pl.BlockSpec((blk, PACK_W), lambda i, be, br: (i, 0))
    grid_spec = pltpu.PrefetchScalarGridSpec(
        num_scalar_prefetch=2,
        grid=(n_slots // blk,),
        in_specs=[slots, slots,
                  pl.BlockSpec((1, d_model, d_gu), lambda i, be, br: (be[i], 0, 0)),
                  pl.BlockSpec((1, 1, d_gu), lambda i, be, br: (be[i], 0, 0)),
                  pl.BlockSpec((1, d_expert, d_model), lambda i, be, br: (be[i], 0, 0)),
                  pl.BlockSpec((1, 1, d_model), lambda i, be, br: (be[i], 0, 0))],
        out_specs=(slots, slots),
        scratch_shapes=[pltpu.VMEM((d_model, d_gu), bf16), pltpu.VMEM((d_expert, d_model), bf16)],
    )
    packed = jax.ShapeDtypeStruct((n_slots, PACK_W), u32)
    return pl.pallas_call(
        _experts_kernel,
        grid_spec=grid_spec,
        out_shape=(packed, packed),
        compiler_params=_cparams("arbitrary"),
        name="experts",
    )(block_expert, block_rows, xs_a, xs_b, w_gate_up, b_gate_up[:, None, :], w_down, b_down[:, None, :])


def _combine_kernel(h_ref, ya_ref, yb_ref, gate_ref, yp_ref, ys_ref, *, n_prompt_tiles):
    i = pl.program_id(0)
    g = gate_ref[...].T
    acc = None
    for k in range(TOP_K):
        gk = g[:, k:k + 1]
        pieces = [p * gk for p in _unpack_bf16_pairs(ya_ref[k]) + _unpack_bf16_pairs(yb_ref[k])]
        acc = pieces if acc is None else [a + p for a, p in zip(acc, pieces)]
    y = h_ref[...] + jnp.concatenate(acc, axis=1)

    @pl.when(i < n_prompt_tiles)
    def _():
        yp_ref[...] = y

    @pl.when(i >= n_prompt_tiles)
    def _():
        ys_ref[...] = y


def _combine(h, ya, yb, gates, n_p, tm):
    n, d_model = h.shape
    tiles_p = n_p // tm
    packed = pl.BlockSpec((TOP_K, tm, PACK_W), lambda i: (0, i, 0))
    return pl.pallas_call(
        functools.partial(_combine_kernel, n_prompt_tiles=tiles_p),
        grid=(n // tm,),
        in_specs=[pl.BlockSpec((tm, d_model), lambda i: (i, 0)), packed, packed,
                  pl.BlockSpec((ROUTE_ROWS, tm), lambda i: (0, i))],
        out_specs=(pl.BlockSpec((tm, d_model), lambda i: (jnp.minimum(i, tiles_p - 1), 0)),
                   pl.BlockSpec((tm, d_model), lambda i: (jnp.maximum(i - tiles_p, 0), 0))),
        out_shape=(jax.ShapeDtypeStruct((n_p, d_model), f32), jax.ShapeDtypeStruct((n - n_p, d_model), f32)),
        compiler_params=_cparams("arbitrary"),
        name="combine",
    )(h, ya, yb, gates)


SC_WINDOW = 128


def _sc_mesh():
    return plsc.VectorSubcoreMesh(core_axis_name="core", subcore_axis_name="subcore")


def _sc_scatter_rows(x, idx, n_k, n_out):
    n, w = x.shape

    @pl.kernel(out_type=jax.ShapeDtypeStruct((n_out, w), x.dtype), mesh=_sc_mesh(), scratch_types=[])
    def scatter(x_hbm, i_hbm, o_hbm):
        def body(x_vmem, *i_vmem):
            for iv in i_vmem:
                pltpu.sync_copy(x_vmem, o_hbm.at[iv.at[0]])

        pltpu.emit_pipeline(
            body,
            grid=(n // SC_WINDOW,),
            in_specs=[pl.BlockSpec((SC_WINDOW, w), lambda i: (i, 0))]
            + [pl.BlockSpec((1, SC_WINDOW), functools.partial(lambda k, i: (k, i), k)) for k in range(n_k)],
            out_specs=[],
            core_axis_name=("core", "subcore"),
            dimension_semantics=(pltpu.PARALLEL,),
        )(x_hbm, *([i_hbm] * n_k))

    return scatter(x, idx)


def _sc_gather_rows(x, idx):
    n_idx, w = idx.shape[0], x.shape[1]

    @pl.kernel(out_type=jax.ShapeDtypeStruct((n_idx, w), x.dtype), mesh=_sc_mesh(), scratch_types=[])
    def gather(x_hbm, i_hbm, o_hbm):
        def body(i_vmem, o_vmem):
            pltpu.sync_copy(x_hbm.at[i_vmem.at[0]], o_vmem)

        pltpu.emit_pipeline(
            body,
            grid=(n_idx // SC_WINDOW,),
            in_specs=[pl.BlockSpec((1, SC_WINDOW), lambda i: (0, i))],
            out_specs=[pl.BlockSpec((SC_WINDOW, w), lambda i: (i, 0))],
            core_axis_name=("core", "subcore"),
            dimension_semantics=(pltpu.PARALLEL,),
        )(i_hbm, o_hbm)

    return gather(x, idx.reshape(1, n_idx))


EXPERT_BLK = 512
SAMPLE_PAGES_PER_STEP = 64


def _slots_kernel(sp_ref, exp_ref, rank_ref, slot_ref):
    expert = exp_ref[...]
    slot = rank_ref[...]
    for e in range(N_EXPERTS):
        slot = slot + jnp.where(expert == e, sp_ref[e], 0)
    slot_ref[...] = slot


def _slots(start_padded, expert, rank):
    whole = pl.BlockSpec(expert.shape, lambda i, sp: (0, 0))
    return pl.pallas_call(
        _slots_kernel,
        grid_spec=pltpu.PrefetchScalarGridSpec(num_scalar_prefetch=1, grid=(1,), in_specs=[whole, whole],
                                               out_specs=whole),
        out_shape=jax.ShapeDtypeStruct(expert.shape, jnp.int32),
        compiler_params=_cparams("arbitrary"),
        name="slots",
    )(start_padded, expert, rank)


def _moe(h, xn_a, xn_b, expert, gates, rank, counts, n_p, w_gate_up, b_gate_up, w_down, b_down):
    n = h.shape[0]
    blk = EXPERT_BLK
    padded = (counts + blk - 1) // blk * blk
    padded_end = jnp.cumsum(padded)
    start_padded = padded_end - padded
    slot_t = _slots(start_padded, expert, rank)
    n_blocks = -(-(n * TOP_K + N_EXPERTS * (blk - 1)) // blk)
    n_slots = n_blocks * blk
    block_start = jnp.arange(n_blocks, dtype=jnp.int32) * blk
    block_expert = jnp.minimum(jnp.sum((block_start[:, None] >= padded_end[None, :]).astype(jnp.int32), axis=1),
                               N_EXPERTS - 1)
    of_block = block_expert[:, None] == jnp.arange(N_EXPERTS, dtype=jnp.int32)[None, :]
    block_rows = jnp.clip(jnp.sum(jnp.where(of_block, (counts + start_padded)[None, :], 0), axis=1) - block_start,
                          0, blk)
    xs_a = _sc_scatter_rows(xn_a, slot_t, TOP_K, n_slots)
    xs_b = _sc_scatter_rows(xn_b, slot_t, TOP_K, n_slots)
    ys_a, ys_b = _experts(block_expert, block_rows, xs_a, xs_b, w_gate_up, b_gate_up, w_down, b_down, blk)
    flat = slot_t[:TOP_K].reshape(-1)
    ya = _sc_gather_rows(ys_a, flat).reshape(TOP_K, n, PACK_W)
    yb = _sc_gather_rows(ys_b, flat).reshape(TOP_K, n, PACK_W)
    return _combine(h, ya, yb, gates, n_p, 512)


def _layer(x_p, x_s, cache_c, cache_kr, state, page_table, norm_mix, w_in, norm_q_lat, w_uq, norm_kv_lat, w_ukv,
           norm_qk_q, norm_qk_k, conv_w, conv_b, conv_ln_g, conv_ln_b, norm_out_conv, norm_out_attn, w_out,
           norm_ffn, w_router, b_router, w_gate_up, b_gate_up, w_down, b_down):
    n_b, t_p, d_model = x_p.shape
    n_seq, t_s, _ = x_s.shape
    d_conv = conv_w.shape[1]
    q_rank, kv_rank = norm_q_lat.shape[0], norm_kv_lat.shape[0]
    n_past = page_table.shape[1] * PAGE_SIZE
    win, wuq, wk, wvt = _mix_in_weights(w_in, w_uq, w_ukv, d_conv, q_rank, kv_rank)
    mix = functools.partial(_mix_in, g_mix=norm_mix, win=win, g_q_lat=norm_q_lat, wuq=wuq, g_kv_lat=norm_kv_lat,
                            g_qk_q=norm_qk_q, wk=wk, wvt=wvt, g_qk_k=norm_qk_k,
                            d_conv=d_conv, q_rank=q_rank, kv_rank=kv_rank)
    x_p2, x_s2 = x_p.reshape(n_b * t_p, d_model), x_s.reshape(n_seq * t_s, d_model)
    tab_p = _rope_head_tables(jnp.arange(t_p))
    tab_s = _rope_head_tables(n_past + jnp.arange(n_seq * t_s) % t_s)
    glu_p, q_p, c_p, kr_p, k_p, vt_p, cv_p = mix(x_p2, tab_p, MIX_TILE, seq_len=t_p, q_scale=ATTN_SCALE * LOG2_E,
                                                 conv=(conv_w, conv_b, conv_ln_g, conv_ln_b))
    glu_s, q_s, c_s, kr_s, _, _ = mix(x_s2, tab_s, MIX_TILE, seq_len=n_seq * t_s, q_scale=ATTN_SCALE)

    glu_p3, glu_s3 = glu_p.reshape(n_b, t_p, d_conv), glu_s.reshape(n_seq, t_s, d_conv)
    cv_s = _conv_sample(glu_s3, state, conv_w, conv_b, conv_ln_g, conv_ln_b)

    hp = N_HEADS * HEAD_PAD
    at_p = _attn_prompt(q_p.reshape(n_b, t_p, hp), k_p.reshape(n_b, t_p, hp), vt_p, 512, 512)
    c_s3, kr_s3 = c_s.reshape(n_seq, t_s, kv_rank), kr_s.reshape(n_seq, t_s, QK_ROPE_DIM)
    at_s = _attn_sample_all(q_s, c_s3, kr_s3, page_table, cache_c, cache_kr, w_ukv, norm_qk_k, SAMPLE_PAGES_PER_STEP)

    h, xn_a, xn_b, expert, gates, rank, cnt = _merge(
        x_p2, cv_p.reshape(n_b * t_p, d_conv), at_p.reshape(n_b * t_p, -1),
        x_s2, cv_s.reshape(n_seq * t_s, d_conv), at_s,
        norm_out_conv, norm_out_attn, w_out, norm_ffn, w_router, b_router, 512)
    counts = cnt[:, 0].astype(jnp.int32)
    y_p, y_s = _moe(h, xn_a, xn_b, expert, gates, rank, counts, n_b * t_p,
                    w_gate_up, b_gate_up, w_down, b_down)
    n_hist = CONV_WIDTH - 1
    conv_state_p = jnp.concatenate([jnp.zeros((n_b, max(n_hist - t_p, 0), d_conv), f32),
                                    glu_p3[:, max(t_p - n_hist, 0):]], axis=1)
    conv_state_s = jnp.concatenate([state, glu_s3], axis=1)[:, -n_hist:]
    return (y_p.reshape(x_p.shape), y_s.reshape(x_s.shape), c_p.reshape(n_b, t_p, kv_rank),
            kr_p.reshape(n_b, t_p, QK_ROPE_DIM), conv_state_p, c_s3, kr_s3, conv_state_s)


def kernel(x_prompt, x_sample, cache_kv_latent, cache_k_rope, state_conv, page_table, norm_mix, w_in, norm_q_lat,
           w_uq, norm_kv_lat, w_ukv, norm_qk_q, norm_qk_k, conv_w, conv_b, conv_ln_g, conv_ln_b, norm_out_conv,
           norm_out_attn, w_out, norm_ffn, w_router, b_router, w_gate_up, b_gate_up, w_down, b_down):
    h_p, h_s = x_prompt, x_sample
    per_layer = []
    for l in range(w_in.shape[0]):
        outs = _layer(h_p, h_s, cache_kv_latent[l], cache_k_rope[l], state_conv[l], page_table, norm_mix[l],
                      w_in[l], norm_q_lat[l], w_uq[l], norm_kv_lat[l], w_ukv[l], norm_qk_q[l], norm_qk_k[l],
                      conv_w[l], conv_b[l], conv_ln_g[l], conv_ln_b[l], norm_out_conv[l], norm_out_attn[l],
                      w_out[l], norm_ffn[l], w_router[l], b_router[l], w_gate_up[l], b_gate_up[l], w_down[l],
                      b_down[l])
        h_p, h_s = outs[0], outs[1]
        per_layer.append(outs[2:])
    stacked = [jnp.stack([pl_[i] for pl_ in per_layer]) for i in range(6)]
    return (h_p, h_s, *stacked)
```

```python
import functools

import jax
import jax.numpy as jnp
import numpy as np
from jax import lax
from jax.experimental import pallas as pl
from jax.experimental.pallas import tpu as pltpu
from jax.experimental.pallas import tpu_sc as plsc

N_HEADS = 8
QK_NOPE_DIM = 64
QK_ROPE_DIM = 32
ROPE_HALF = QK_ROPE_DIM // 2
QK_DIM = QK_NOPE_DIM + QK_ROPE_DIM
V_DIM = 64
CONV_WIDTH = 31
N_EXPERTS = 32
TOP_K = 4
SWIGLU_LIMIT = 7.0
SWIGLU_ALPHA = 1.702
ROPE_THETA = 10000.0
NORM_EPS = 1e-6
ATTN_SCALE = QK_DIM ** -0.5
LOG2_E = 1.4426950408889634
PAGE_SIZE = 128

LANES = 128
HEAD_PAD = LANES
VMEM_LIMIT = 56 * 1024 * 1024

f32 = jnp.float32
bf16 = jnp.bfloat16


def _cparams(*sem):
    return pltpu.CompilerParams(dimension_semantics=sem, vmem_limit_bytes=VMEM_LIMIT)


def _rope_head_tables(pos):
    inv_freq = ROPE_THETA ** (-jnp.arange(ROPE_HALF, dtype=f32) / ROPE_HALF)
    ang = pos.astype(f32)[:, None] * inv_freq[None, :]
    cos, sin = jnp.cos(ang), jnp.sin(ang)
    n = pos.shape[0]
    z = lambda w: jnp.zeros((n, w), f32)
    c = jnp.concatenate([jnp.ones((n, QK_NOPE_DIM), f32), cos, cos, z(32)], axis=1)
    s1 = jnp.concatenate([z(QK_NOPE_DIM + ROPE_HALF), sin, z(32)], axis=1)
    s2 = jnp.concatenate([z(QK_NOPE_DIM), -sin, z(ROPE_HALF + 32)], axis=1)
    return jnp.concatenate([c, s1, s2], axis=1)


def _rope_head(x, tab):
    c, s1, s2 = tab[:, :LANES], tab[:, LANES:2 * LANES], tab[:, 2 * LANES:]
    return x * c + pltpu.roll(x, ROPE_HALF, 1) * s1 + pltpu.roll(x, LANES - ROPE_HALF, 1) * s2


def _mix_in_kernel(x_ref, tab_ref, gmix_ref, win_ref, gq_ref, wuq_ref, gkv_ref, gqq_ref,
                   wk_ref, wvt_ref, gqk_ref, *rest, d_conv, q_rank, kv_rank, q_scale, conv_seq_tiles):
    if conv_seq_tiles:
        cw_ref, cb_ref, lg_ref, lb_ref, glu_ref, q_ref, c_ref, kr_ref, k_ref, vt_ref, cv_ref, hist_s = rest
    else:
        glu_ref, q_ref, c_ref, kr_ref, k_ref, vt_ref = rest
    tm = x_ref.shape[0]
    rows = tm // MIX_SUB_TILES
    gqq = gqq_ref[...] * q_scale
    gqk = gqk_ref[...]

    if conv_seq_tiles:
        @pl.when(pl.program_id(0) % conv_seq_tiles == 0)
        def _():
            hist_s[...] = jnp.zeros(hist_s.shape, f32)

    def conv(r0):
        tc = min(rows, CONV_CHUNK)
        for t0 in range(r0, r0 + rows, tc):
            for lt in range(d_conv // LANES):
                ls = slice(lt * LANES, (lt + 1) * LANES)
                hist = hist_s[:, ls] if t0 == 0 else glu_ref[t0 - CONV_PAD:t0, ls]
                window = jnp.concatenate([hist, glu_ref[t0:t0 + tc, ls]], axis=0)
                cv_ref[t0:t0 + tc, ls] = _conv_taps(window, cw_ref, cb_ref, ls, tc)
            cv_ref[t0:t0 + tc, :] = _ln_swish(cv_ref[t0:t0 + tc, :], lg_ref, lb_ref)

    def project(r):
        x = x_ref[r, :]
        xn = x * lax.rsqrt(jnp.mean(x * x, axis=-1, keepdims=True) + NORM_EPS) * gmix_ref[...]
        return jnp.dot(xn.astype(bf16), win_ref[...], preferred_element_type=f32)

    def latents(r, proj):
        o = 0
        a = proj[:, o:o + d_conv]; o += d_conv
        gt = proj[:, o:o + d_conv]; o += d_conv
        q_lat = proj[:, o:o + q_rank]; o += q_rank
        kv_lat = proj[:, o:o + kv_rank]; o += kv_rank
        kr_raw = proj[:, o:o + LANES]
        glu_ref[r, :] = a * jax.nn.sigmoid(gt)
        tab = tab_ref[r, :]
        qn = q_lat * lax.rsqrt(jnp.mean(q_lat * q_lat, axis=-1, keepdims=True) + NORM_EPS) * gq_ref[...]
        q = jnp.dot(qn.astype(bf16), wuq_ref[...], preferred_element_type=f32)
        c_kv = kv_lat * lax.rsqrt(jnp.mean(kv_lat * kv_lat, axis=-1, keepdims=True) + NORM_EPS) * gkv_ref[...]
        c_ref[r, :] = c_kv
        kr = _rope_head(kr_raw, tab)
        kr_ref[r, :] = kr[:, QK_NOPE_DIM:QK_NOPE_DIM + QK_ROPE_DIM]
        c_bf = c_kv.astype(bf16)
        kn = jnp.dot(c_bf, wk_ref[...], preferred_element_type=f32)
        vt_ref[0, :, r] = lax.dot_general(wvt_ref[...], c_bf, NT_DIMS,
                                          preferred_element_type=f32).astype(vt_ref.dtype)
        return q, kn, kr, tab

    def heads(r, q, kn, kr, tab):
        for h in range(N_HEADS):
            sl = slice(h * HEAD_PAD, (h + 1) * HEAD_PAD)
            qh = _rope_head(q[:, sl], tab)
            qh = qh * lax.rsqrt(jnp.sum(qh * qh, axis=-1, keepdims=True) * (1.0 / QK_DIM) + NORM_EPS) * gqq
            q_ref[r, sl] = qh.astype(q_ref.dtype)
            kh = kn[:, sl] + kr
            kh = kh * lax.rsqrt(jnp.sum(kh * kh, axis=-1, keepdims=True) * (1.0 / QK_DIM) + NORM_EPS) * gqk
            k_ref[r, sl] = kh.astype(k_ref.dtype)

    slices = [pl.ds(i * rows, rows) for i in range(MIX_SUB_TILES)]
    proj = project(slices[0])
    for i, r in enumerate(slices):
        nxt = project(slices[i + 1]) if i + 1 < len(slices) else None
        heads(r, *latents(r, proj))
        if conv_seq_tiles:
            conv(i * rows)
        proj = nxt
    if conv_seq_tiles:
        hist_s[...] = glu_ref[tm - CONV_PAD:tm, :]


MIX_TILE = 1024
MIX_SUB_TILES = 4
CONV_CHUNK = 128


def _head_gain(g_pairs):
    g_rot = g_pairs[QK_NOPE_DIM:]
    return jnp.concatenate([g_pairs[:QK_NOPE_DIM], g_rot, g_rot, jnp.zeros((32,), f32)])[None, :]


def _mix_in_weights(w_in, w_uq, w_ukv, d_conv, q_rank, kv_rank):
    d_model = w_in.shape[0]
    base = 2 * d_conv + q_rank + kv_rank
    w_kr = w_in[:, base:base + QK_ROPE_DIM]
    w_kr_pad = jnp.concatenate([jnp.zeros((d_model, QK_NOPE_DIM), f32), w_kr,
                                jnp.zeros((d_model, 32), f32)], axis=1)
    win = jnp.concatenate([w_in[:, :base], w_kr_pad], axis=1).astype(bf16)
    wuq = jnp.pad(w_uq, ((0, 0), (0, 0), (0, HEAD_PAD - QK_DIM))).reshape(q_rank, N_HEADS * HEAD_PAD).astype(bf16)
    wk = jnp.pad(w_ukv[:, :, :QK_NOPE_DIM], ((0, 0), (0, 0), (0, HEAD_PAD - QK_NOPE_DIM)))
    wk = wk.reshape(kv_rank, N_HEADS * HEAD_PAD).astype(bf16)
    wvt = w_ukv[:, :, QK_NOPE_DIM:].reshape(kv_rank, N_HEADS * V_DIM).T.astype(bf16)
    return win, wuq, wk, wvt


def _mix_in(x2d, tab, tm, g_mix, win, g_q_lat, wuq, g_kv_lat, g_qk_q, wk, wvt, g_qk_k, d_conv, q_rank, kv_rank,
            seq_len, q_scale, conv=None):
    n, d_model = x2d.shape
    n_tab_blocks = tab.shape[0] // tm
    seq_tiles = seq_len // tm
    row = lambda i: (i, 0)
    const = lambda i: (0, 0)
    full = lambda a: pl.BlockSpec(a.shape, const)
    gm, gq, gkv = g_mix[None, :], g_q_lat[None, :], g_kv_lat[None, :]
    gqq, gqk = _head_gain(g_qk_q), _head_gain(g_qk_k)
    hp = N_HEADS * HEAD_PAD
    out_shape = [
        jax.ShapeDtypeStruct((n, d_conv), f32),
        jax.ShapeDtypeStruct((n, hp), bf16),
        jax.ShapeDtypeStruct((n, kv_rank), f32),
        jax.ShapeDtypeStruct((n, QK_ROPE_DIM), f32),
        jax.ShapeDtypeStruct((n, hp), bf16),
        jax.ShapeDtypeStruct((n // seq_len, N_HEADS * V_DIM, seq_len), bf16),
    ]
    out_specs = [pl.BlockSpec((tm, s.shape[1]), row) for s in out_shape[:-1]]
    out_specs.append(pl.BlockSpec((1, N_HEADS * V_DIM, tm), lambda i: (i // seq_tiles, 0, i % seq_tiles)))
    operands = [x2d, tab, gm, win, gq, wuq, gkv, gqq, wk, wvt, gqk]
    in_specs = [pl.BlockSpec((tm, d_model), row),
                pl.BlockSpec((tm, tab.shape[1]), lambda i: (i % n_tab_blocks, 0)),
                full(gm), full(win), full(gq), full(wuq), full(gkv), full(gqq), full(wk), full(wvt), full(gqk)]
    scratch = []
    if conv is not None:
        conv_w, conv_b, ln_g, ln_b = conv
        extra = [conv_w, conv_b[None, :], ln_g[None, :], ln_b[None, :]]
        operands += extra
        in_specs += [full(a) for a in extra]
        out_shape.append(jax.ShapeDtypeStruct((n, d_conv), f32))
        out_specs.append(pl.BlockSpec((tm, d_conv), row))
        scratch = [pltpu.VMEM((CONV_PAD, d_conv), f32)]
    return pl.pallas_call(
        functools.partial(_mix_in_kernel, d_conv=d_conv, q_rank=q_rank, kv_rank=kv_rank, q_scale=q_scale,
                          conv_seq_tiles=seq_tiles if conv is not None else 0),
        grid=(n // tm,),
        in_specs=in_specs,
        out_specs=tuple(out_specs),
        out_shape=tuple(out_shape),
        scratch_shapes=scratch,
        compiler_params=_cparams("arbitrary"),
        name="mix_in",
    )(*operands)


CONV_PAD = 32


SUBLANES = 8


def _conv_taps(window, w_ref, b_ref, ls, tc):
    lead = CONV_PAD - (CONV_WIDTH - 1)
    acc = jnp.zeros((tc, LANES), f32) + b_ref[:, ls]
    for r in range(SUBLANES):
        taps = [(a, a * SUBLANES + r - lead) for a in range(CONV_PAD // SUBLANES + 1)
                if 0 <= a * SUBLANES + r - lead < CONV_WIDTH]
        shifted = window if r == 0 else pltpu.roll(window, window.shape[0] - r, 0)
        for a, j in taps:
            acc = acc + shifted[a * SUBLANES:a * SUBLANES + tc] * w_ref[j:j + 1, ls]
    return acc


def _ln_swish(y, g_ref, beta_ref):
    mu = jnp.mean(y, axis=-1, keepdims=True)
    yc = y - mu
    var = jnp.mean(yc * yc, axis=-1, keepdims=True)
    z = yc * lax.rsqrt(var + NORM_EPS) * g_ref[...] + beta_ref[...]
    return z * jax.nn.sigmoid(z)


def _conv_sample_kernel(x_ref, st_ref, w_ref, b_ref, g_ref, beta_ref, o_ref):
    bb, tc, n_ch = x_ref.shape

    def one(b, carry):
        for lt in range(n_ch // LANES):
            ls = slice(lt * LANES, (lt + 1) * LANES)
            window = jnp.concatenate([st_ref[b, :, ls], x_ref[b, :, ls]], axis=0)
            o_ref[b, :, ls] = _conv_taps(window, w_ref, b_ref, ls, tc)
        o_ref[b] = _ln_swish(o_ref[b], g_ref, beta_ref)
        return carry

    lax.fori_loop(0, bb, one, 0)


def _conv_specs(conv_w, n_ch):
    const = lambda i: (0, 0)
    vec = pl.BlockSpec((1, n_ch), const)
    return [pl.BlockSpec(conv_w.shape, const), vec, vec, vec]


def _conv_sample(glu, state, conv_w, conv_b, ln_g, ln_b, bb=32):
    n_b, t_len, n_ch = glu.shape
    hist = jnp.pad(state, ((0, 0), (CONV_PAD - state.shape[1], 0), (0, 0)))
    return pl.pallas_call(
        _conv_sample_kernel,
        grid=(n_b // bb,),
        in_specs=[pl.BlockSpec((bb, t_len, n_ch), lambda i: (i, 0, 0)),
                  pl.BlockSpec((bb, CONV_PAD, n_ch), lambda i: (i, 0, 0))] + _conv_specs(conv_w, n_ch),
        out_specs=pl.BlockSpec((bb, t_len, n_ch), lambda i: (i, 0, 0)),
        out_shape=jax.ShapeDtypeStruct(glu.shape, f32),
        compiler_params=_cparams("parallel"),
        name="conv_sample",
    )(glu, hist, conv_w, conv_b[None, :], ln_g[None, :], ln_b[None, :])


NT_DIMS = (((1,), (1,)), ((), ()))


def _attn_prompt_kernel(q_ref, k_ref, vt_ref, o_ref, *, tq, tk):
    assert tq == tk
    qi = pl.program_id(2)
    n_h = q_ref.shape[2] // HEAD_PAD
    qs = [q_ref[0, :, h * HEAD_PAD:(h + 1) * HEAD_PAD] for h in range(n_h)]

    def block(k0, nk, q_lo, carry, masked):
        nq = tq - q_lo
        if masked:
            key = k0 + lax.broadcasted_iota(jnp.int32, (nk, nq), 0)
            qry = qi * tq + q_lo + lax.broadcasted_iota(jnp.int32, (nk, nq), 1)
            visible = key <= qry
        ones = jnp.ones((ONES_ROWS, nk), bf16)
        sts = []
        for h in range(n_h):
            k_h = k_ref[0, pl.ds(k0, nk), h * HEAD_PAD:(h + 1) * HEAD_PAD]
            sts.append(lax.dot_general(k_h, qs[h][q_lo:], NT_DIMS, preferred_element_type=f32))
        out = []
        for h, (m_all, acc_all) in enumerate(carry):
            m, acc = m_all[:, q_lo:], acc_all[:, q_lo:]
            st = jnp.where(visible, sts[h], -jnp.inf) if masked else sts[h]
            m_new = jnp.maximum(m, jnp.max(st, axis=0, keepdims=True))
            alpha = jnp.exp2(m - m_new)
            p = jnp.exp2(st - m_new).astype(bf16)
            vt_h = jnp.concatenate([vt_ref[0, h * V_DIM:(h + 1) * V_DIM, pl.ds(k0, nk)], ones], axis=0)
            acc = alpha * acc + jnp.dot(vt_h, p, preferred_element_type=f32)
            if q_lo:
                m_new = jnp.concatenate([m_all[:, :q_lo], m_new], axis=1)
                acc = jnp.concatenate([acc_all[:, :q_lo], acc], axis=1)
            out.append((m_new, acc))
        return tuple(out)

    carry = tuple((jnp.full((1, tq), -jnp.inf, f32), jnp.zeros((V_DIM + ONES_ROWS, tq), f32)) for _ in range(n_h))
    carry = lax.fori_loop(
        0, qi, lambda kb, c: block(pl.multiple_of(kb * tk, tk), tk, 0, c, False), carry)
    half = tk // 2
    d0 = pl.multiple_of(qi * tq, tq)
    carry = block(d0, half, 0, carry, True)
    carry = block(pl.multiple_of(d0 + half, half), half, half, carry, True)
    o_t = jnp.concatenate([acc[:V_DIM] / acc[V_DIM:V_DIM + 1] for (_, acc) in carry], axis=0)
    o_ref[0] = o_t.T


ONES_ROWS = 16


ATTN_HEAD_GROUP = 4


def _attn_prompt(q, k, vt, tq, tk):
    n_b, t_len, _ = q.shape
    g = ATTN_HEAD_GROUP
    return pl.pallas_call(
        functools.partial(_attn_prompt_kernel, tq=tq, tk=tk),
        grid=(n_b, N_HEADS // g, t_len // tq),
        in_specs=[pl.BlockSpec((1, tq, g * HEAD_PAD), lambda b, h, i: (b, i, h)),
                  pl.BlockSpec((1, t_len, g * HEAD_PAD), lambda b, h, i: (b, 0, h)),
                  pl.BlockSpec((1, g * V_DIM, t_len), lambda b, h, i: (b, h, 0))],
        out_specs=pl.BlockSpec((1, tq, g * V_DIM), lambda b, h, i: (b, i, h)),
        out_shape=jax.ShapeDtypeStruct((n_b, t_len, N_HEADS * V_DIM), f32),
        compiler_params=_cparams("parallel", "parallel", "arbitrary"),
        name="attn_prompt",
    )(q, k, vt)


N_TH = 64


def _q_absorb_kernel(q_ref, gk_ref, wukt_ref, sel_ref, qabs_ref, qr_ref):
    gk = gk_ref[...]
    kv_rank = wukt_ref.shape[-1]
    for h in range(N_HEADS):
        qg = (q_ref[:, h * HEAD_PAD:(h + 1) * HEAD_PAD].astype(f32) * gk).astype(bf16)
        qabs_ref[:, h * kv_rank:(h + 1) * kv_rank] = jnp.dot(
            qg, wukt_ref[h], preferred_element_type=f32).astype(qabs_ref.dtype)
        qr_ref[:, h * QK_ROPE_DIM:(h + 1) * QK_ROPE_DIM] = jnp.dot(
            qg, sel_ref[...], preferred_element_type=f32).astype(qr_ref.dtype)


def _q_absorb(q2d, g_qk_k, w_ukv):
    n = q2d.shape[0]
    kv_rank = w_ukv.shape[0]
    wukt = jnp.transpose(w_ukv[:, :, :QK_NOPE_DIM], (1, 2, 0))
    wukt = jnp.pad(wukt, ((0, 0), (0, HEAD_PAD - QK_NOPE_DIM), (0, 0))).astype(bf16)
    sel = jnp.zeros((HEAD_PAD, QK_ROPE_DIM), f32).at[
        QK_NOPE_DIM + jnp.arange(QK_ROPE_DIM), jnp.arange(QK_ROPE_DIM)].set(1.0).astype(bf16)
    gk = _head_gain(g_qk_k)
    full = lambda a: pl.BlockSpec(a.shape, lambda i: (0,) * a.ndim)
    return pl.pallas_call(
        _q_absorb_kernel,
        grid=(1,),
        in_specs=[full(q2d), full(gk), full(wukt), full(sel)],
        out_specs=(pl.BlockSpec((n, N_HEADS * kv_rank), lambda i: (0, 0)),
                   pl.BlockSpec((n, N_HEADS * QK_ROPE_DIM), lambda i: (0, 0))),
        out_shape=(jax.ShapeDtypeStruct((n, N_HEADS * kv_rank), bf16),
                   jax.ShapeDtypeStruct((n, N_HEADS * QK_ROPE_DIM), bf16)),
        compiler_params=_cparams("arbitrary"),
        name="q_absorb",
    )(q2d, gk, wukt, sel)


SUB_PAGES = 32


def _latent_products_t(lhs, c_bf):
    n = c_bf.shape[0]
    full = lax.dot_general(lhs, c_bf, NT_DIMS, preferred_element_type=f32)
    knt = full[N_TH:]
    ssq = jnp.sum((knt * knt).reshape(QK_NOPE_DIM, N_HEADS, n), axis=0)
    return full[:N_TH], ssq


def _scores_t(st_lat, ssq_nope, krt, qr):
    n = krt.shape[1]
    ssq = ssq_nope + jnp.sum(krt * krt, axis=0, keepdims=True)
    rinv = lax.rsqrt(ssq * (1.0 / QK_DIM) + NORM_EPS)
    st = st_lat + jnp.dot(qr, krt.astype(bf16), preferred_element_type=f32)
    return (st.reshape(N_TH // N_HEADS, N_HEADS, n) * rinv[None]).reshape(N_TH, n)


def _softmax_update_t(st, c_bf, state, mask=None):
    m, l, acc = state
    if mask is not None:
        st = jnp.where(mask, st, -jnp.inf)
    m_new = jnp.maximum(m, jnp.max(st, axis=1, keepdims=True))
    alpha = jnp.exp(m - m_new)
    p = jnp.exp(st - m_new)
    l = alpha * l + jnp.sum(p, axis=1, keepdims=True)
    acc = alpha * acc + jnp.dot(p.astype(bf16), c_bf, preferred_element_type=f32)
    return m_new, l, acc


def _attn_sample_kernel(pt_ref, qabs_ref, qr_ref, cnew_ref, krtnew_ref, wukt_ref, cache_c, cache_krt, o_ref,
                            cbuf, krbuf, sems, m_s, l_s, acc_s, *, pp):
    s, j = pl.program_id(0), pl.program_id(1)
    n_j = pl.num_programs(1)
    step = s * n_j + j
    slot = step % 2

    def page_copies(step_idx, buf_slot):
        copies = []
        for i in range(pp):
            page = pt_ref[step_idx * pp + i]
            copies.append(pltpu.make_async_copy(cache_c.at[page], cbuf.at[buf_slot, i], sems.at[0, buf_slot]))
            copies.append(pltpu.make_async_copy(cache_krt.at[page], krbuf.at[buf_slot, i], sems.at[1, buf_slot]))
        return copies

    @pl.when(step == 0)
    def _():
        for cp in page_copies(0, 0):
            cp.start()

    @pl.when(step + 1 < pl.num_programs(0) * n_j)
    def _():
        for cp in page_copies(step + 1, 1 - slot):
            cp.start()

    @pl.when(j == 0)
    def _():
        m_s[...] = jnp.full(m_s.shape, -jnp.inf, f32)
        l_s[...] = jnp.zeros(l_s.shape, f32)
        acc_s[...] = jnp.zeros(acc_s.shape, f32)

    for cp in page_copies(step, slot):
        cp.wait()

    qr = qr_ref[0]
    lhs = jnp.concatenate([qabs_ref[0], wukt_ref[...]], axis=0)
    sub = min(SUB_PAGES, pp)
    blocks = list(range(0, pp, sub))

    def load(b):
        c_bf = jnp.concatenate([cbuf[slot, i].astype(bf16) for i in range(b, b + sub)], axis=0)
        return (c_bf,) + _latent_products_t(lhs, c_bf)

    state = (m_s[...], l_s[...], acc_s[...])
    nxt = load(blocks[0])
    for n, b in enumerate(blocks):
        c_bf, st_lat, ssq = nxt
        if n + 1 < len(blocks):
            nxt = load(blocks[n + 1])
        krt = jnp.concatenate([krbuf[slot, i] for i in range(b, b + sub)], axis=1)
        state = _softmax_update_t(_scores_t(st_lat, ssq, krt, qr), c_bf, state)
    m_s[...], l_s[...], acc_s[...] = state

    @pl.when(j == n_j - 1)
    def _():
        c_bf = cnew_ref[0].astype(bf16)
        n_pad = c_bf.shape[0]
        key_t = lax.broadcasted_iota(jnp.int32, (N_TH, n_pad), 1)
        q_t = lax.broadcasted_iota(jnp.int32, (N_TH, n_pad), 0) // N_HEADS
        st = _scores_t(*_latent_products_t(lhs, c_bf), krtnew_ref[0], qr)
        m, l, acc = _softmax_update_t(st, c_bf, state, mask=key_t <= q_t)
        o_ref[0] = acc / l


def _attn_sample(page_table, qabs, qr, c_new, kr_new, cache_c, cache_kr, wuk_perm, pp):
    n_seq, n_pages = page_table.shape
    kv_rank = cache_c.shape[-1]
    t_new = c_new.shape[1]
    seq3 = lambda s, j, pt: (s, 0, 0)
    cache_kr_t = jnp.swapaxes(cache_kr, 1, 2)
    c_new_pad = jnp.pad(c_new, ((0, 0), (0, PAGE_SIZE - t_new), (0, 0)))
    krt_new_pad = jnp.pad(jnp.swapaxes(kr_new, 1, 2), ((0, 0), (0, 0), (0, PAGE_SIZE - t_new)))
    grid_spec = pltpu.PrefetchScalarGridSpec(
        num_scalar_prefetch=1,
        grid=(n_seq, n_pages // pp),
        in_specs=[pl.BlockSpec((1, N_TH, kv_rank), seq3),
                  pl.BlockSpec((1, N_TH, QK_ROPE_DIM), seq3),
                  pl.BlockSpec((1, PAGE_SIZE, kv_rank), seq3),
                  pl.BlockSpec((1, QK_ROPE_DIM, PAGE_SIZE), seq3),
                  pl.BlockSpec(wuk_perm.shape, lambda s, j, pt: (0, 0)),
                  pl.BlockSpec(memory_space=pl.ANY),
                  pl.BlockSpec(memory_space=pl.ANY)],
        out_specs=pl.BlockSpec((1, N_TH, kv_rank), seq3),
        scratch_shapes=[pltpu.VMEM((2, pp, PAGE_SIZE, kv_rank), f32),
                        pltpu.VMEM((2, pp, QK_ROPE_DIM, PAGE_SIZE), f32),
                        pltpu.SemaphoreType.DMA((2, 2)),
                        pltpu.VMEM((N_TH, 1), f32), pltpu.VMEM((N_TH, 1), f32),
                        pltpu.VMEM((N_TH, kv_rank), f32)],
    )
    return pl.pallas_call(
        functools.partial(_attn_sample_kernel, pp=pp),
        grid_spec=grid_spec,
        out_shape=jax.ShapeDtypeStruct((n_seq, N_TH, kv_rank), f32),
        compiler_params=_cparams("arbitrary", "arbitrary"),
        name="attn_sample",
    )(page_table.reshape(-1), qabs, qr, c_new_pad, krt_new_pad, wuk_perm, cache_c, cache_kr_t)


def _v_up_kernel(lat_ref, wuv_ref, o_ref):
    rows = lat_ref.shape[0]
    full = jnp.dot(lat_ref[...].astype(bf16), wuv_ref[...], preferred_element_type=f32)
    head_of_row = lax.broadcasted_iota(jnp.int32, full.shape, 0) % N_HEADS
    head_of_lane = lax.broadcasted_iota(jnp.int32, full.shape, 1) // V_DIM
    own = jnp.where(head_of_row == head_of_lane, full, 0.0)
    o_ref[...] = jnp.sum(own.reshape(rows // N_HEADS, N_HEADS, full.shape[1]), axis=1)


def _v_up(lat2d, wuv, rows):
    n, kv_rank = lat2d.shape
    d_attn = wuv.shape[1]
    return pl.pallas_call(
        _v_up_kernel,
        grid=(n // rows,),
        in_specs=[pl.BlockSpec((rows, kv_rank), lambda i: (i, 0)), pl.BlockSpec(wuv.shape, lambda i: (0, 0))],
        out_specs=pl.BlockSpec((rows // N_HEADS, d_attn), lambda i: (i, 0)),
        out_shape=jax.ShapeDtypeStruct((n // N_HEADS, d_attn), f32),
        compiler_params=_cparams("parallel"),
        name="v_up",
    )(lat2d, wuv)


def _attn_sample_all(q_s2d, c_new, kr_new, page_table, cache_c, cache_kr, w_ukv, g_qk_k, pp):
    n_seq, t_new, kv_rank = c_new.shape
    qabs, qr = _q_absorb(q_s2d, g_qk_k, w_ukv)
    qabs = qabs.reshape(n_seq, t_new * N_HEADS, kv_rank)
    qr = qr.reshape(n_seq, t_new * N_HEADS, QK_ROPE_DIM)
    wuk_perm = jnp.transpose(w_ukv[:, :, :QK_NOPE_DIM], (2, 1, 0)).reshape(-1, kv_rank).astype(bf16)
    lat = _attn_sample(page_table, qabs, qr, c_new, kr_new, cache_c, cache_kr, wuk_perm, pp)
    wuv = w_ukv[:, :, QK_NOPE_DIM:].reshape(kv_rank, N_HEADS * V_DIM).astype(bf16)
    rows = min(512, n_seq * N_TH)
    return _v_up(lat.reshape(n_seq * N_TH, kv_rank), wuv, rows)


def _rms(x, g):
    return x * lax.rsqrt(jnp.mean(x * x, axis=-1, keepdims=True) + NORM_EPS) * g


def _split_bf16(x):
    hi = x.astype(bf16)
    return hi, (x - hi.astype(f32)).astype(bf16)


PACK_W = 256
u32 = jnp.uint32


def _pack_bf16_pairs(x):
    bits = lax.bitcast_convert_type(x.astype(bf16).astype(f32), u32)
    pieces = []
    for p in range(x.shape[1] // (2 * PACK_W)):
        lo = bits[:, 2 * p * PACK_W:(2 * p + 1) * PACK_W]
        hi = bits[:, (2 * p + 1) * PACK_W:(2 * p + 2) * PACK_W]
        pieces.append((lo >> 16) | (hi & jnp.uint32(0xFFFF0000)))
    return pieces


def _unpack_bf16_pairs(words):
    return (lax.bitcast_convert_type(words << 16, f32),
            lax.bitcast_convert_type(words & jnp.uint32(0xFFFF0000), f32))


def _merge_kernel(xp_ref, cvp_ref, atp_ref, xs_ref, cvs_ref, ats_ref, gc_ref, ga_ref, woc_ref, woa_ref, gf_ref,
                  wrh_ref, wrl_ref, br_ref, h_ref, xna_ref, xnb_ref, exp_ref, gate_ref, rank_ref, cnt_ref, cnt_s,
                  *, n_prompt_tiles):
    i = pl.program_id(0)
    tm = xp_ref.shape[0]

    @pl.when(i == 0)
    def _():
        cnt_s[...] = jnp.zeros(cnt_s.shape, f32)

    is_p = i < n_prompt_tiles
    rows = tm

    def project(r):
        x = jnp.where(is_p, xp_ref[r, :], xs_ref[r, :])
        yc = _rms(jnp.where(is_p, cvp_ref[r, :], cvs_ref[r, :]), gc_ref[...]).astype(bf16)
        ya = _rms(jnp.where(is_p, atp_ref[r, :], ats_ref[r, :]), ga_ref[...]).astype(bf16)
        h = x + jnp.dot(yc, woc_ref[...], preferred_element_type=f32) \
            + jnp.dot(ya, woa_ref[...], preferred_element_type=f32)
        h_ref[r, :] = h
        xn = _rms(h, gf_ref[...])
        xh, xl = _split_bf16(xn)
        xna_ref[r, :], xnb_ref[r, :] = _pack_bf16_pairs(xn)
        return (jnp.dot(xh, wrh_ref[...], preferred_element_type=f32)
                + jnp.dot(xl, wrh_ref[...], preferred_element_type=f32)
                + jnp.dot(xh, wrl_ref[...], preferred_element_type=f32)) + br_ref[...]

    def route(logits, cnt):
        lt = logits.T[:N_EXPERTS]
        eidx = lax.broadcasted_iota(jnp.int32, lt.shape, 0)
        tops, idxs = [], []
        cur = lt
        for _k in range(TOP_K):
            mk = jnp.max(cur, axis=0, keepdims=True)
            ik = jnp.min(jnp.where(cur == mk, eidx, N_EXPERTS), axis=0, keepdims=True)
            tops.append(mk)
            idxs.append(ik)
            cur = jnp.where(eidx == ik, -jnp.inf, cur)
        es = [jnp.exp(t - tops[0]) for t in tops]
        den = es[0] + es[1] + es[2] + es[3]
        onehots = [(eidx == ik).astype(f32) for ik in idxs]
        oh = onehots[0] + onehots[1] + onehots[2] + onehots[3]
        s_i = lax.broadcasted_iota(jnp.int32, (rows, rows), 0)
        t_i = lax.broadcasted_iota(jnp.int32, (rows, rows), 1)
        before = jnp.dot(oh.astype(bf16), (s_i < t_i).astype(bf16), preferred_element_type=f32) + cnt
        ranks = [jnp.sum(oh_k * before, axis=0, keepdims=True).astype(jnp.int32) for oh_k in onehots]
        pad_i = jnp.zeros((ROUTE_ROWS - TOP_K, rows), jnp.int32)
        pad_f = jnp.zeros((ROUTE_ROWS - TOP_K, rows), f32)
        exp_ref[...] = jnp.concatenate(idxs + [pad_i], axis=0)
        gate_ref[...] = jnp.concatenate([e / den for e in es] + [pad_f], axis=0)
        rank_ref[...] = jnp.concatenate(ranks + [pad_i], axis=0)
        return cnt + jnp.sum(oh, axis=1, keepdims=True)

    cnt = route(project(pl.ds(0, tm)), cnt_s[...])
    cnt_s[...] = cnt
    cnt_ref[...] = cnt


ROUTE_ROWS = 8


def _merge(x_p, cv_p, at_p, x_s, cv_s, at_s, g_out_conv, g_out_attn, w_out, g_ffn, w_router, b_router, tm):
    (n_p, d_model), n_s = x_p.shape, x_s.shape[0]
    n = n_p + n_s
    d_conv, d_attn = cv_p.shape[1], at_p.shape[1]
    woc, woa = w_out[:d_conv].astype(bf16), w_out[d_conv:].astype(bf16)
    wr = jnp.pad(w_router, ((0, 0), (0, LANES - N_EXPERTS)))
    wrh, wrl = _split_bf16(wr)
    br = jnp.concatenate([b_router.astype(f32), jnp.full((LANES - N_EXPERTS,), -jnp.inf, f32)])[None, :]
    gc, ga, gf = g_out_conv[None, :], g_out_attn[None, :], g_ffn[None, :]
    tiles_p, tiles_s = n_p // tm, n_s // tm
    full = lambda a: pl.BlockSpec(a.shape, lambda i: (0, 0))
    rows = lambda w: pl.BlockSpec((tm, w), lambda i: (i, 0))
    rows_p = lambda w: pl.BlockSpec((tm, w), lambda i: (jnp.minimum(i, tiles_p - 1), 0))
    rows_s = lambda w: pl.BlockSpec((tm, w), lambda i: (jnp.maximum(i - tiles_p, 0), 0))
    assert d_model == 4 * PACK_W
    route = lambda dt: jax.ShapeDtypeStruct((ROUTE_ROWS, n), dt)
    route_blk = pl.BlockSpec((ROUTE_ROWS, tm), lambda i: (0, i))
    out_shape = (jax.ShapeDtypeStruct((n, d_model), f32),
                 jax.ShapeDtypeStruct((n, PACK_W), u32), jax.ShapeDtypeStruct((n, PACK_W), u32),
                 route(jnp.int32), route(f32), route(jnp.int32), jax.ShapeDtypeStruct((N_EXPERTS, 1), f32))
    return pl.pallas_call(
        functools.partial(_merge_kernel, n_prompt_tiles=tiles_p),
        grid=(tiles_p + tiles_s,),
        in_specs=[rows_p(d_model), rows_p(d_conv), rows_p(d_attn), rows_s(d_model), rows_s(d_conv), rows_s(d_attn),
                  full(gc), full(ga), full(woc), full(woa), full(gf), full(wrh), full(wrl), full(br)],
        out_specs=(rows(d_model), rows(PACK_W), rows(PACK_W), route_blk, route_blk, route_blk,
                   pl.BlockSpec((N_EXPERTS, 1), lambda i: (0, 0))),
        out_shape=out_shape,
        scratch_shapes=[pltpu.VMEM((N_EXPERTS, 1), f32)],
        compiler_params=_cparams("arbitrary"),
        name="merge",
    )(x_p, cv_p, at_p, x_s, cv_s, at_s, gc, ga, woc, woa, gf, wrh, wrl, br)


def _experts_kernel(be_ref, br_ref, xa_ref, xb_ref, wgu_ref, bgu_ref, wd_ref, bd_ref, ya_ref, yb_ref, wgu_s, wd_s):
    i = pl.program_id(0)
    d_expert = wd_ref.shape[1]
    blk = xa_ref.shape[0]

    @pl.when((i == 0) | (be_ref[i] != be_ref[jnp.maximum(i - 1, 0)]))
    def _():
        wgu_s[...] = wgu_ref[0].astype(bf16)
        wd_s[...] = wd_ref[0].astype(bf16)

    def ffn(n_rows):
        x = jnp.concatenate(_unpack_bf16_pairs(xa_ref[:n_rows, :]) + _unpack_bf16_pairs(xb_ref[:n_rows, :]), axis=1)
        live = lax.broadcasted_iota(jnp.int32, (n_rows, 1), 0) < br_ref[i]
        x = jnp.where(live, x, 0.0).astype(bf16)
        gu = jnp.dot(x, wgu_s[...], preferred_element_type=f32) + bgu_ref[0]
        gate = jnp.minimum(gu[:, :d_expert], SWIGLU_LIMIT)
        up = jnp.clip(gu[:, d_expert:], -SWIGLU_LIMIT, SWIGLU_LIMIT)
        hid = (up + 1.0) * gate * jax.nn.sigmoid(SWIGLU_ALPHA * gate)
        y = jnp.dot(hid.astype(bf16), wd_s[...], preferred_element_type=f32) + bd_ref[0]
        ya_ref[:n_rows, :], yb_ref[:n_rows, :] = _pack_bf16_pairs(y)
        if n_rows < blk:
            ya_ref[n_rows:, :] = jnp.zeros((blk - n_rows, ya_ref.shape[1]), ya_ref.dtype)
            yb_ref[n_rows:, :] = jnp.zeros((blk - n_rows, yb_ref.shape[1]), yb_ref.dtype)

    half = blk // 2
    pl.when(br_ref[i] > half)(lambda: ffn(blk))
    pl.when((br_ref[i] > 0) & (br_ref[i] <= half))(lambda: ffn(half))

    @pl.when(br_ref[i] <= 0)
    def _():
        ya_ref[...] = jnp.zeros(ya_ref.shape, ya_ref.dtype)
        yb_ref[...] = jnp.zeros(yb_ref.shape, yb_ref.dtype)


def _experts(block_expert, block_rows, xs_a, xs_b, w_gate_up, b_gate_up, w_down, b_down, blk):
    n_slots = xs_a.shape[0]
    n_exp, d_model, d_gu = w_gate_up.shape
    d_expert = w_down.shape[1]
    slots = pl.BlockSpec((blk, PACK_W), lambda i, be, br: (i, 0))
    grid_spec = pltpu.PrefetchScalarGridSpec(
        num_scalar_prefetch=2,
        grid=(n_slots // blk,),
        in_specs=[slots, slots,
                  pl.BlockSpec((1, d_model, d_gu), lambda i, be, br: (be[i], 0, 0)),
                  pl.BlockSpec((1, 1, d_gu), lambda i, be, br: (be[i], 0, 0)),
                  pl.BlockSpec((1, d_expert, d_model), lambda i, be, br: (be[i], 0, 0)),
                  pl.BlockSpec((1, 1, d_model), lambda i, be, br: (be[i], 0, 0))],
        out_specs=(slots, slots),
        scratch_shapes=[pltpu.VMEM((d_model, d_gu), bf16), pltpu.VMEM((d_expert, d_model), bf16)],
    )
    packed = jax.ShapeDtypeStruct((n_slots, PACK_W), u32)
    return pl.pallas_call(
        _experts_kernel,
        grid_spec=grid_spec,
        out_shape=(packed, packed),
        compiler_params=_cparams("arbitrary"),
        name="experts",
    )(block_expert, block_rows, xs_a, xs_b, w_gate_up, b_gate_up[:, None, :], w_down, b_down[:, None, :])


def _combine_kernel(h_ref, ya_ref, yb_ref, gate_ref, yp_ref, ys_ref, *, n_prompt_tiles):
    i = pl.program_id(0)
    g = gate_ref[...].T
    acc = None
    for k in range(TOP_K):
        gk = g[:, k:k + 1]
        pieces = [p * gk for p in _unpack_bf16_pairs(ya_ref[k]) + _unpack_bf16_pairs(yb_ref[k])]
        acc = pieces if acc is None else [a + p for a, p in zip(acc, pieces)]
    y = h_ref[...] + jnp.concatenate(acc, axis=1)

    @pl.when(i < n_prompt_tiles)
    def _():
        yp_ref[...] = y

    @pl.when(i >= n_prompt_tiles)
    def _():
        ys_ref[...] = y


def _combine(h, ya, yb, gates, n_p, tm):
    n, d_model = h.shape
    tiles_p = n_p // tm
    packed = pl.BlockSpec((TOP_K, tm, PACK_W), lambda i: (0, i, 0))
    return pl.pallas_call(
        functools.partial(_combine_kernel, n_prompt_tiles=tiles_p),
        grid=(n // tm,),
        in_specs=[pl.BlockSpec((tm, d_model), lambda i: (i, 0)), packed, packed,
                  pl.BlockSpec((ROUTE_ROWS, tm), lambda i: (0, i))],
        out_specs=(pl.BlockSpec((tm, d_model), lambda i: (jnp.minimum(i, tiles_p - 1), 0)),
                   pl.BlockSpec((tm, d_model), lambda i: (jnp.maximum(i - tiles_p, 0), 0))),
        out_shape=(jax.ShapeDtypeStruct((n_p, d_model), f32), jax.ShapeDtypeStruct((n - n_p, d_model), f32)),
        compiler_params=_cparams("arbitrary"),
        name="combine",
    )(h, ya, yb, gates)


SC_WINDOW = 128


def _sc_mesh():
    return plsc.VectorSubcoreMesh(core_axis_name="core", subcore_axis_name="subcore")


def _sc_scatter_rows(x, idx, n_k, n_out):
    n, w = x.shape

    @pl.kernel(out_type=jax.ShapeDtypeStruct((n_out, w), x.dtype), mesh=_sc_mesh(), scratch_types=[])
    def scatter(x_hbm, i_hbm, o_hbm):
        def body(x_vmem, *i_vmem):
            for iv in i_vmem:
                pltpu.sync_copy(x_vmem, o_hbm.at[iv.at[0]])

        pltpu.emit_pipeline(
            body,
            grid=(n // SC_WINDOW,),
            in_specs=[pl.BlockSpec((SC_WINDOW, w), lambda i: (i, 0))]
            + [pl.BlockSpec((1, SC_WINDOW), functools.partial(lambda k, i: (k, i), k)) for k in range(n_k)],
            out_specs=[],
            core_axis_name=("core", "subcore"),
            dimension_semantics=(pltpu.PARALLEL,),
        )(x_hbm, *([i_hbm] * n_k))

    return scatter(x, idx)


def _sc_gather_rows(x, idx):
    n_idx, w = idx.shape[0], x.shape[1]

    @pl.kernel(out_type=jax.ShapeDtypeStruct((n_idx, w), x.dtype), mesh=_sc_mesh(), scratch_types=[])
    def gather(x_hbm, i_hbm, o_hbm):
        def body(i_vmem, o_vmem):
            pltpu.sync_copy(x_hbm.at[i_vmem.at[0]], o_vmem)

        pltpu.emit_pipeline(
            body,
            grid=(n_idx // SC_WINDOW,),
            in_specs=[pl.BlockSpec((1, SC_WINDOW), lambda i: (0, i))],
            out_specs=[pl.BlockSpec((SC_WINDOW, w), lambda i: (i, 0))],
            core_axis_name=("core", "subcore"),
            dimension_semantics=(pltpu.PARALLEL,),
        )(i_hbm, o_hbm)

    return gather(x, idx.reshape(1, n_idx))


EXPERT_BLK = 512
SAMPLE_PAGES_PER_STEP = 64


def _slots_kernel(sp_ref, exp_ref, rank_ref, slot_ref):
    expert = exp_ref[...]
    slot = rank_ref[...]
    for e in range(N_EXPERTS):
        slot = slot + jnp.where(expert == e, sp_ref[e], 0)
    slot_ref[...] = slot


def _slots(start_padded, expert, rank):
    whole = pl.BlockSpec(expert.shape, lambda i, sp: (0, 0))
    return pl.pallas_call(
        _slots_kernel,
        grid_spec=pltpu.PrefetchScalarGridSpec(num_scalar_prefetch=1, grid=(1,), in_specs=[whole, whole],
                                               out_specs=whole),
        out_shape=jax.ShapeDtypeStruct(expert.shape, jnp.int32),
        compiler_params=_cparams("arbitrary"),
        name="slots",
    )(start_padded, expert, rank)


def _moe(h, xn_a, xn_b, expert, gates, rank, counts, n_p, w_gate_up, b_gate_up, w_down, b_down):
    n = h.shape[0]
    blk = EXPERT_BLK
    padded = (counts + blk - 1) // blk * blk
    padded_end = jnp.cumsum(padded)
    start_padded = padded_end - padded
    slot_t = _slots(start_padded, expert, rank)
    n_blocks = -(-(n * TOP_K + N_EXPERTS * (blk - 1)) // blk)
    n_slots = n_blocks * blk
    block_start = jnp.arange(n_blocks, dtype=jnp.int32) * blk
    block_expert = jnp.minimum(jnp.sum((block_start[:, None] >= padded_end[None, :]).astype(jnp.int32), axis=1),
                               N_EXPERTS - 1)
    of_block = block_expert[:, None] == jnp.arange(N_EXPERTS, dtype=jnp.int32)[None, :]
    block_rows = jnp.clip(jnp.sum(jnp.where(of_block, (counts + start_padded)[None, :], 0), axis=1) - block_start,
                          0, blk)
    xs_a = _sc_scatter_rows(xn_a, slot_t, TOP_K, n_slots)
    xs_b = _sc_scatter_rows(xn_b, slot_t, TOP_K, n_slots)
    ys_a, ys_b = _experts(block_expert, block_rows, xs_a, xs_b, w_gate_up, b_gate_up, w_down, b_down, blk)
    flat = slot_t[:TOP_K].reshape(-1)
    ya = _sc_gather_rows(ys_a, flat).reshape(TOP_K, n, PACK_W)
    yb = _sc_gather_rows(ys_b, flat).reshape(TOP_K, n, PACK_W)
    return _combine(h, ya, yb, gates, n_p, 512)


def _layer(x_p, x_s, cache_c, cache_kr, state, page_table, norm_mix, w_in, norm_q_lat, w_uq, norm_kv_lat, w_ukv,
           norm_qk_q, norm_qk_k, conv_w, conv_b, conv_ln_g, conv_ln_b, norm_out_conv, norm_out_attn, w_out,
           norm_ffn, w_router, b_router, w_gate_up, b_gate_up, w_down, b_down):
    n_b, t_p, d_model = x_p.shape
    n_seq, t_s, _ = x_s.shape
    d_conv = conv_w.shape[1]
    q_rank, kv_rank = norm_q_lat.shape[0], norm_kv_lat.shape[0]
    n_past = page_table.shape[1] * PAGE_SIZE
    win, wuq, wk, wvt = _mix_in_weights(w_in, w_uq, w_ukv, d_conv, q_rank, kv_rank)
    mix = functools.partial(_mix_in, g_mix=norm_mix, win=win, g_q_lat=norm_q_lat, wuq=wuq, g_kv_lat=norm_kv_lat,
                            g_qk_q=norm_qk_q, wk=wk, wvt=wvt, g_qk_k=norm_qk_k,
                            d_conv=d_conv, q_rank=q_rank, kv_rank=kv_rank)
    x_p2, x_s2 = x_p.reshape(n_b * t_p, d_model), x_s.reshape(n_seq * t_s, d_model)
    tab_p = _rope_head_tables(jnp.arange(t_p))
    tab_s = _rope_head_tables(n_past + jnp.arange(n_seq * t_s) % t_s)
    glu_p, q_p, c_p, kr_p, k_p, vt_p, cv_p = mix(x_p2, tab_p, MIX_TILE, seq_len=t_p, q_scale=ATTN_SCALE * LOG2_E,
                                                 conv=(conv_w, conv_b, conv_ln_g, conv_ln_b))
    glu_s, q_s, c_s, kr_s, _, _ = mix(x_s2, tab_s, MIX_TILE, seq_len=n_seq * t_s, q_scale=ATTN_SCALE)

    glu_p3, glu_s3 = glu_p.reshape(n_b, t_p, d_conv), glu_s.reshape(n_seq, t_s, d_conv)
    cv_s = _conv_sample(glu_s3, state, conv_w, conv_b, conv_ln_g, conv_ln_b)

    hp = N_HEADS * HEAD_PAD
    at_p = _attn_prompt(q_p.reshape(n_b, t_p, hp), k_p.reshape(n_b, t_p, hp), vt_p, 512, 512)
    c_s3, kr_s3 = c_s.reshape(n_seq, t_s, kv_rank), kr_s.reshape(n_seq, t_s, QK_ROPE_DIM)
    at_s = _attn_sample_all(q_s, c_s3, kr_s3, page_table, cache_c, cache_kr, w_ukv, norm_qk_k, SAMPLE_PAGES_PER_STEP)

    h, xn_a, xn_b, expert, gates, rank, cnt = _merge(
        x_p2, cv_p.reshape(n_b * t_p, d_conv), at_p.reshape(n_b * t_p, -1),
        x_s2, cv_s.reshape(n_seq * t_s, d_conv), at_s,
        norm_out_conv, norm_out_attn, w_out, norm_ffn, w_router, b_router, 512)
    counts = cnt[:, 0].astype(jnp.int32)
    y_p, y_s = _moe(h, xn_a, xn_b, expert, gates, rank, counts, n_b * t_p,
                    w_gate_up, b_gate_up, w_down, b_down)
    n_hist = CONV_WIDTH - 1
    conv_state_p = jnp.concatenate([jnp.zeros((n_b, max(n_hist - t_p, 0), d_conv), f32),
                                    glu_p3[:, max(t_p - n_hist, 0):]], axis=1)
    conv_state_s = jnp.concatenate([state, glu_s3], axis=1)[:, -n_hist:]
    return (y_p.reshape(x_p.shape), y_s.reshape(x_s.shape), c_p.reshape(n_b, t_p, kv_rank),
            kr_p.reshape(n_b, t_p, QK_ROPE_DIM), conv_state_p, c_s3, kr_s3, conv_state_s)


def kernel(x_prompt, x_sample, cache_kv_latent, cache_k_rope, state_conv, page_table, norm_mix, w_in, norm_q_lat,
           w_uq, norm_kv_lat, w_ukv, norm_qk_q, norm_qk_k, conv_w, conv_b, conv_ln_g, conv_ln_b, norm_out_conv,
           norm_out_attn, w_out, norm_ffn, w_router, b_router, w_gate_up, b_gate_up, w_down, b_down):
    h_p, h_s = x_prompt, x_sample
    per_layer = []
    for l in range(w_in.shape[0]):
        outs = _layer(h_p, h_s, cache_kv_latent[l], cache_k_rope[l], state_conv[l], page_table, norm_mix[l],
                      w_in[l], norm_q_lat[l], w_uq[l], norm_kv_lat[l], w_ukv[l], norm_qk_q[l], norm_qk_k[l],
                      conv_w[l], conv_b[l], conv_ln_g[l], conv_ln_b[l], norm_out_conv[l], norm_out_attn[l],
                      w_out[l], norm_ffn[l], w_router[l], b_router[l], w_gate_up[l], b_gate_up[l], w_down[l],
                      b_down[l])
        h_p, h_s = outs[0], outs[1]
        per_layer.append(outs[2:])
    stacked = [jnp.stack([pl_[i] for pl_ in per_layer]) for i in range(6)]
    return (h_p, h_s, *stacked)
```

```python
import functools

import jax
import jax.numpy as jnp
import numpy as np
from jax import lax
from jax.experimental import pallas as pl
from jax.experimental.pallas import tpu as pltpu
from jax.experimental.pallas import tpu_sc as plsc

N_HEADS = 8
QK_NOPE_DIM = 64
QK_ROPE_DIM = 32
ROPE_HALF = QK_ROPE_DIM // 2
QK_DIM = QK_NOPE_DIM + QK_ROPE_DIM
V_DIM = 64
CONV_WIDTH = 31
N_EXPERTS = 32
TOP_K = 4
SWIGLU_LIMIT = 7.0
SWIGLU_ALPHA = 1.702
ROPE_THETA = 10000.0
NORM_EPS = 1e-6
ATTN_SCALE = QK_DIM ** -0.5
LOG2_E = 1.4426950408889634
PAGE_SIZE = 128

LANES = 128
HEAD_PAD = LANES
VMEM_LIMIT = 56 * 1024 * 1024

f32 = jnp.float32
bf16 = jnp.bfloat16


def _cparams(*sem):
    return pltpu.CompilerParams(dimension_semantics=sem, vmem_limit_bytes=VMEM_LIMIT)


def _rope_head_tables(pos):
    inv_freq = ROPE_THETA ** (-jnp.arange(ROPE_HALF, dtype=f32) / ROPE_HALF)
    ang = pos.astype(f32)[:, None] * inv_freq[None, :]
    cos, sin = jnp.cos(ang), jnp.sin(ang)
    n = pos.shape[0]
    z = lambda w: jnp.zeros((n, w), f32)
    c = jnp.concatenate([jnp.ones((n, QK_NOPE_DIM), f32), cos, cos, z(32)], axis=1)
    s1 = jnp.concatenate([z(QK_NOPE_DIM + ROPE_HALF), sin, z(32)], axis=1)
    s2 = jnp.concatenate([z(QK_NOPE_DIM), -sin, z(ROPE_HALF + 32)], axis=1)
    return jnp.concatenate([c, s1, s2], axis=1)


def _rope_head(x, tab):
    c, s1, s2 = tab[:, :LANES], tab[:, LANES:2 * LANES], tab[:, 2 * LANES:]
    return x * c + pltpu.roll(x, ROPE_HALF, 1) * s1 + pltpu.roll(x, LANES - ROPE_HALF, 1) * s2


def _mix_in_kernel(x_ref, tab_ref, gmix_ref, win_ref, gq_ref, wuq_ref, gkv_ref, gqq_ref,
                   wk_ref, wvt_ref, gqk_ref, *rest, d_conv, q_rank, kv_rank, q_scale, conv_seq_tiles):
    if conv_seq_tiles:
        cw_ref, cb_ref, lg_ref, lb_ref, glu_ref, q_ref, c_ref, kr_ref, k_ref, vt_ref, cv_ref, hist_s = rest
    else:
        glu_ref, q_ref, c_ref, kr_ref, k_ref, vt_ref = rest
    tm = x_ref.shape[0]
    rows = tm // MIX_SUB_TILES
    gqq = gqq_ref[...] * q_scale
    gqk = gqk_ref[...]

    if conv_seq_tiles:
        @pl.when(pl.program_id(0) % conv_seq_tiles == 0)
        def _():
            hist_s[...] = jnp.zeros(hist_s.shape, f32)

    def conv(r0):
        tc = min(rows, CONV_CHUNK)
        for t0 in range(r0, r0 + rows, tc):
            for lt in range(d_conv // LANES):
                ls = slice(lt * LANES, (lt + 1) * LANES)
                hist = hist_s[:, ls] if t0 == 0 else glu_ref[t0 - CONV_PAD:t0, ls]
                window = jnp.concatenate([hist, glu_ref[t0:t0 + tc, ls]], axis=0)
                cv_ref[t0:t0 + tc, ls] = _conv_taps(window, cw_ref, cb_ref, ls, tc)
            cv_ref[t0:t0 + tc, :] = _ln_swish(cv_ref[t0:t0 + tc, :], lg_ref, lb_ref)

    def project(r):
        x = x_ref[r, :]
        xn = x * lax.rsqrt(jnp.mean(x * x, axis=-1, keepdims=True) + NORM_EPS) * gmix_ref[...]
        return jnp.dot(xn.astype(bf16), win_ref[...], preferred_element_type=f32)

    def latents(r, proj):
        o = 0
        a = proj[:, o:o + d_conv]; o += d_conv
        gt = proj[:, o:o + d_conv]; o += d_conv
        q_lat = proj[:, o:o + q_rank]; o += q_rank
        kv_lat = proj[:, o:o + kv_rank]; o += kv_rank
        kr_raw = proj[:, o:o + LANES]
        glu_ref[r, :] = a * jax.nn.sigmoid(gt)
        tab = tab_ref[r, :]
        qn = q_lat * lax.rsqrt(jnp.mean(q_lat * q_lat, axis=-1, keepdims=True) + NORM_EPS) * gq_ref[...]
        q = jnp.dot(qn.astype(bf16), wuq_ref[...], preferred_element_type=f32)
        c_kv = kv_lat * lax.rsqrt(jnp.mean(kv_lat * kv_lat, axis=-1, keepdims=True) + NORM_EPS) * gkv_ref[...]
        c_ref[r, :] = c_kv
        kr = _rope_head(kr_raw, tab)
        kr_ref[r, :] = kr[:, QK_NOPE_DIM:QK_NOPE_DIM + QK_ROPE_DIM]
        c_bf = c_kv.astype(bf16)
        kn = jnp.dot(c_bf, wk_ref[...], preferred_element_type=f32)
        vt_ref[0, :, r] = lax.dot_general(wvt_ref[...], c_bf, NT_DIMS,
                                          preferred_element_type=f32).astype(vt_ref.dtype)
        return q, kn, kr, tab

    def heads(r, q, kn, kr, tab):
        for h in range(N_HEADS):
            sl = slice(h * HEAD_PAD, (h + 1) * HEAD_PAD)
            qh = _rope_head(q[:, sl], tab)
            qh = qh * lax.rsqrt(jnp.sum(qh * qh, axis=-1, keepdims=True) * (1.0 / QK_DIM) + NORM_EPS) * gqq
            q_ref[r, sl] = qh.astype(q_ref.dtype)
            kh = kn[:, sl] + kr
            kh = kh * lax.rsqrt(jnp.sum(kh * kh, axis=-1, keepdims=True) * (1.0 / QK_DIM) + NORM_EPS) * gqk
            k_ref[r, sl] = kh.astype(k_ref.dtype)

    slices = [pl.ds(i * rows, rows) for i in range(MIX_SUB_TILES)]
    proj = project(slices[0])
    for i, r in enumerate(slices):
        nxt = project(slices[i + 1]) if i + 1 < len(slices) else None
        heads(r, *latents(r, proj))
        if conv_seq_tiles:
            conv(i * rows)
        proj = nxt
    if conv_seq_tiles:
        hist_s[...] = glu_ref[tm - CONV_PAD:tm, :]


MIX_TILE = 1024
MIX_SUB_TILES = 4
CONV_CHUNK = 128


def _head_gain(g_pairs):
    g_rot = g_pairs[QK_NOPE_DIM:]
    return jnp.concatenate([g_pairs[:QK_NOPE_DIM], g_rot, g_rot, jnp.zeros((32,), f32)])[None, :]


def _mix_in_weights(w_in, w_uq, w_ukv, d_conv, q_rank, kv_rank):
    d_model = w_in.shape[0]
    base = 2 * d_conv + q_rank + kv_rank
    w_kr = w_in[:, base:base + QK_ROPE_DIM]
    w_kr_pad = jnp.concatenate([jnp.zeros((d_model, QK_NOPE_DIM), f32), w_kr,
                                jnp.zeros((d_model, 32), f32)], axis=1)
    win = jnp.concatenate([w_in[:, :base], w_kr_pad], axis=1).astype(bf16)
    wuq = jnp.pad(w_uq, ((0, 0), (0, 0), (0, HEAD_PAD - QK_DIM))).reshape(q_rank, N_HEADS * HEAD_PAD).astype(bf16)
    wk = jnp.pad(w_ukv[:, :, :QK_NOPE_DIM], ((0, 0), (0, 0), (0, HEAD_PAD - QK_NOPE_DIM)))
    wk = wk.reshape(kv_rank, N_HEADS * HEAD_PAD).astype(bf16)
    wvt = w_ukv[:, :, QK_NOPE_DIM:].reshape(kv_rank, N_HEADS * V_DIM).T.astype(bf16)
    return win, wuq, wk, wvt


def _mix_in(x2d, tab, tm, g_mix, win, g_q_lat, wuq, g_kv_lat, g_qk_q, wk, wvt, g_qk_k, d_conv, q_rank, kv_rank,
            seq_len, q_scale, conv=None):
    n, d_model = x2d.shape
    n_tab_blocks = tab.shape[0] // tm
    seq_tiles = seq_len // tm
    row = lambda i: (i, 0)
    const = lambda i: (0, 0)
    full = lambda a: pl.BlockSpec(a.shape, const)
    gm, gq, gkv = g_mix[None, :], g_q_lat[None, :], g_kv_lat[None, :]
    gqq, gqk = _head_gain(g_qk_q), _head_gain(g_qk_k)
    hp = N_HEADS * HEAD_PAD
    out_shape = [
        jax.ShapeDtypeStruct((n, d_conv), f32),
        jax.ShapeDtypeStruct((n, hp), bf16),
        jax.ShapeDtypeStruct((n, kv_rank), f32),
        jax.ShapeDtypeStruct((n, QK_ROPE_DIM), f32),
        jax.ShapeDtypeStruct((n, hp), bf16),
        jax.ShapeDtypeStruct((n // seq_len, N_HEADS * V_DIM, seq_len), bf16),
    ]
    out_specs = [pl.BlockSpec((tm, s.shape[1]), row) for s in out_shape[:-1]]
    out_specs.append(pl.BlockSpec((1, N_HEADS * V_DIM, tm), lambda i: (i // seq_tiles, 0, i % seq_tiles)))
    operands = [x2d, tab, gm, win, gq, wuq, gkv, gqq, wk, wvt, gqk]
    in_specs = [pl.BlockSpec((tm, d_model), row),
                pl.BlockSpec((tm, tab.shape[1]), lambda i: (i % n_tab_blocks, 0)),
                full(gm), full(win), full(gq), full(wuq), full(gkv), full(gqq), full(wk), full(wvt), full(gqk)]
    scratch = []
    if conv is not None:
        conv_w, conv_b, ln_g, ln_b = conv
        extra = [conv_w, conv_b[None, :], ln_g[None, :], ln_b[None, :]]
        operands += extra
        in_specs += [full(a) for a in extra]
        out_shape.append(jax.ShapeDtypeStruct((n, d_conv), f32))
        out_specs.append(pl.BlockSpec((tm, d_conv), row))
        scratch = [pltpu.VMEM((CONV_PAD, d_conv), f32)]
    return pl.pallas_call(
        functools.partial(_mix_in_kernel, d_conv=d_conv, q_rank=q_rank, kv_rank=kv_rank, q_scale=q_scale,
                          conv_seq_tiles=seq_tiles if conv is not None else 0),
        grid=(n // tm,),
        in_specs=in_specs,
        out_specs=tuple(out_specs),
        out_shape=tuple(out_shape),
        scratch_shapes=scratch,
        compiler_params=_cparams("arbitrary"),
        name="mix_in",
    )(*operands)


CONV_PAD = 32


SUBLANES = 8


def _conv_taps(window, w_ref, b_ref, ls, tc):
    lead = CONV_PAD - (CONV_WIDTH - 1)
    acc = jnp.zeros((tc, LANES), f32) + b_ref[:, ls]
    for r in range(SUBLANES):
        taps = [(a, a * SUBLANES + r - lead) for a in range(CONV_PAD // SUBLANES + 1)
                if 0 <= a * SUBLANES + r - lead < CONV_WIDTH]
        shifted = window if r == 0 else pltpu.roll(window, window.shape[0] - r, 0)
        for a, j in taps:
            acc = acc + shifted[a * SUBLANES:a * SUBLANES + tc] * w_ref[j:j + 1, ls]
    return acc


def _ln_swish(y, g_ref, beta_ref):
    mu = jnp.mean(y, axis=-1, keepdims=True)
    yc = y - mu
    var = jnp.mean(yc * yc, axis=-1, keepdims=True)
    z = yc * lax.rsqrt(var + NORM_EPS) * g_ref[...] + beta_ref[...]
    return z * jax.nn.sigmoid(z)


def _conv_sample_kernel(x_ref, st_ref, w_ref, b_ref, g_ref, beta_ref, o_ref):
    bb, tc, n_ch = x_ref.shape

    def one(b, carry):
        for lt in range(n_ch // LANES):
            ls = slice(lt * LANES, (lt + 1) * LANES)
            window = jnp.concatenate([st_ref[b, :, ls], x_ref[b, :, ls]], axis=0)
            o_ref[b, :, ls] = _conv_taps(window, w_ref, b_ref, ls, tc)
        o_ref[b] = _ln_swish(o_ref[b], g_ref, beta_ref)
        return carry

    lax.fori_loop(0, bb, one, 0)


def _conv_specs(conv_w, n_ch):
    const = lambda i: (0, 0)
    vec = pl.BlockSpec((1, n_ch), const)
    return [pl.BlockSpec(conv_w.shape, const), vec, vec, vec]


def _conv_sample(glu, state, conv_w, conv_b, ln_g, ln_b, bb=32):
    n_b, t_len, n_ch = glu.shape
    hist = jnp.pad(state, ((0, 0), (CONV_PAD - state.shape[1], 0), (0, 0)))
    return pl.pallas_call(
        _conv_sample_kernel,
        grid=(n_b // bb,),
        in_specs=[pl.BlockSpec((bb, t_len, n_ch), lambda i: (i, 0, 0)),
                  pl.BlockSpec((bb, CONV_PAD, n_ch), lambda i: (i, 0, 0))] + _conv_specs(conv_w, n_ch),
        out_specs=pl.BlockSpec((bb, t_len, n_ch), lambda i: (i, 0, 0)),
        out_shape=jax.ShapeDtypeStruct(glu.shape, f32),
        compiler_params=_cparams("parallel"),
        name="conv_sample",
    )(glu, hist, conv_w, conv_b[None, :], ln_g[None, :], ln_b[None, :])


NT_DIMS = (((1,), (1,)), ((), ()))


def _attn_prompt_kernel(q_ref, k_ref, vt_ref, o_ref, *, tq, tk):
    assert tq == tk
    qi = pl.program_id(2)
    n_h = q_ref.shape[2] // HEAD_PAD
    qs = [q_ref[0, :, h * HEAD_PAD:(h + 1) * HEAD_PAD] for h in range(n_h)]

    def block(k0, nk, q_lo, carry, masked):
        nq = tq - q_lo
        if masked:
            key = k0 + lax.broadcasted_iota(jnp.int32, (nk, nq), 0)
            qry = qi * tq + q_lo + lax.broadcasted_iota(jnp.int32, (nk, nq), 1)
            visible = key <= qry
        ones = jnp.ones((ONES_ROWS, nk), bf16)
        sts = []
        for h in range(n_h):
            k_h = k_ref[0, pl.ds(k0, nk), h * HEAD_PAD:(h + 1) * HEAD_PAD]
            sts.append(lax.dot_general(k_h, qs[h][q_lo:], NT_DIMS, preferred_element_type=f32))
        out = []
        for h, (m_all, acc_all) in enumerate(carry):
            m, acc = m_all[:, q_lo:], acc_all[:, q_lo:]
            st = jnp.where(visible, sts[h], -jnp.inf) if masked else sts[h]
            m_new = jnp.maximum(m, jnp.max(st, axis=0, keepdims=True))
            alpha = jnp.exp2(m - m_new)
            p = jnp.exp2(st - m_new).astype(bf16)
            vt_h = jnp.concatenate([vt_ref[0, h * V_DIM:(h + 1) * V_DIM, pl.ds(k0, nk)], ones], axis=0)
            acc = alpha * acc + jnp.dot(vt_h, p, preferred_element_type=f32)
            if q_lo:
                m_new = jnp.concatenate([m_all[:, :q_lo], m_new], axis=1)
                acc = jnp.concatenate([acc_all[:, :q_lo], acc], axis=1)
            out.append((m_new, acc))
        return tuple(out)

    carry = tuple((jnp.full((1, tq), -jnp.inf, f32), jnp.zeros((V_DIM + ONES_ROWS, tq), f32)) for _ in range(n_h))
    carry = lax.fori_loop(
        0, qi, lambda kb, c: block(pl.multiple_of(kb * tk, tk), tk, 0, c, False), carry)
    half = tk // 2
    d0 = pl.multiple_of(qi * tq, tq)
    carry = block(d0, half, 0, carry, True)
    carry = block(pl.multiple_of(d0 + half, half), half, half, carry, True)
    o_t = jnp.concatenate([acc[:V_DIM] / acc[V_DIM:V_DIM + 1] for (_, acc) in carry], axis=0)
    o_ref[0] = o_t.T


ONES_ROWS = 16


ATTN_HEAD_GROUP = 8


def _attn_prompt(q, k, vt, tq, tk):
    n_b, t_len, _ = q.shape
    g = ATTN_HEAD_GROUP
    return pl.pallas_call(
        functools.partial(_attn_prompt_kernel, tq=tq, tk=tk),
        grid=(n_b, N_HEADS // g, t_len // tq),
        in_specs=[pl.BlockSpec((1, tq, g * HEAD_PAD), lambda b, h, i: (b, i, h)),
                  pl.BlockSpec((1, t_len, g * HEAD_PAD), lambda b, h, i: (b, 0, h)),
                  pl.BlockSpec((1, g * V_DIM, t_len), lambda b, h, i: (b, h, 0))],
        out_specs=pl.BlockSpec((1, tq, g * V_DIM), lambda b, h, i: (b, i, h)),
        out_shape=jax.ShapeDtypeStruct((n_b, t_len, N_HEADS * V_DIM), f32),
        compiler_params=_cparams("parallel", "parallel", "arbitrary"),
        name="attn_prompt",
    )(q, k, vt)


N_TH = 64


def _q_absorb_kernel(q_ref, gk_ref, wukt_ref, sel_ref, qabs_ref, qr_ref):
    gk = gk_ref[...]
    kv_rank = wukt_ref.shape[-1]
    for h in range(N_HEADS):
        qg = (q_ref[:, h * HEAD_PAD:(h + 1) * HEAD_PAD].astype(f32) * gk).astype(bf16)
        qabs_ref[:, h * kv_rank:(h + 1) * kv_rank] = jnp.dot(
            qg, wukt_ref[h], preferred_element_type=f32).astype(qabs_ref.dtype)
        qr_ref[:, h * QK_ROPE_DIM:(h + 1) * QK_ROPE_DIM] = jnp.dot(
            qg, sel_ref[...], preferred_element_type=f32).astype(qr_ref.dtype)


def _q_absorb(q2d, g_qk_k, w_ukv):
    n = q2d.shape[0]
    kv_rank = w_ukv.shape[0]
    wukt = jnp.transpose(w_ukv[:, :, :QK_NOPE_DIM], (1, 2, 0))
    wukt = jnp.pad(wukt, ((0, 0), (0, HEAD_PAD - QK_NOPE_DIM), (0, 0))).astype(bf16)
    sel = jnp.zeros((HEAD_PAD, QK_ROPE_DIM), f32).at[
        QK_NOPE_DIM + jnp.arange(QK_ROPE_DIM), jnp.arange(QK_ROPE_DIM)].set(1.0).astype(bf16)
    gk = _head_gain(g_qk_k)
    full = lambda a: pl.BlockSpec(a.shape, lambda i: (0,) * a.ndim)
    return pl.pallas_call(
        _q_absorb_kernel,
        grid=(1,),
        in_specs=[full(q2d), full(gk), full(wukt), full(sel)],
        out_specs=(pl.BlockSpec((n, N_HEADS * kv_rank), lambda i: (0, 0)),
                   pl.BlockSpec((n, N_HEADS * QK_ROPE_DIM), lambda i: (0, 0))),
        out_shape=(jax.ShapeDtypeStruct((n, N_HEADS * kv_rank), bf16),
                   jax.ShapeDtypeStruct((n, N_HEADS * QK_ROPE_DIM), bf16)),
        compiler_params=_cparams("arbitrary"),
        name="q_absorb",
    )(q2d, gk, wukt, sel)


SUB_PAGES = 32


def _latent_products_t(lhs, c_bf):
    n = c_bf.shape[0]
    full = lax.dot_general(lhs, c_bf, NT_DIMS, preferred_element_type=f32)
    knt = full[N_TH:]
    ssq = jnp.sum((knt * knt).reshape(QK_NOPE_DIM, N_HEADS, n), axis=0)
    return full[:N_TH], ssq


def _scores_t(st_lat, ssq_nope, krt, qr):
    n = krt.shape[1]
    ssq = ssq_nope + jnp.sum(krt * krt, axis=0, keepdims=True)
    rinv = lax.rsqrt(ssq * (1.0 / QK_DIM) + NORM_EPS)
    st = st_lat + jnp.dot(qr, krt.astype(bf16), preferred_element_type=f32)
    return (st.reshape(N_TH // N_HEADS, N_HEADS, n) * rinv[None]).reshape(N_TH, n)


def _softmax_update_t(st, c_bf, state, mask=None):
    m, l, acc = state
    if mask is not None:
        st = jnp.where(mask, st, -jnp.inf)
    m_new = jnp.maximum(m, jnp.max(st, axis=1, keepdims=True))
    alpha = jnp.exp(m - m_new)
    p = jnp.exp(st - m_new)
    l = alpha * l + jnp.sum(p, axis=1, keepdims=True)
    acc = alpha * acc + jnp.dot(p.astype(bf16), c_bf, preferred_element_type=f32)
    return m_new, l, acc


def _attn_sample_kernel(pt_ref, qabs_ref, qr_ref, cnew_ref, krtnew_ref, wukt_ref, cache_c, cache_krt, o_ref,
                            cbuf, krbuf, sems, m_s, l_s, acc_s, *, pp):
    s, j = pl.program_id(0), pl.program_id(1)
    n_j = pl.num_programs(1)
    step = s * n_j + j
    slot = step % 2

    def page_copies(step_idx, buf_slot):
        copies = []
        for i in range(pp):
            page = pt_ref[step_idx * pp + i]
            copies.append(pltpu.make_async_copy(cache_c.at[page], cbuf.at[buf_slot, i], sems.at[0, buf_slot]))
            copies.append(pltpu.make_async_copy(cache_krt.at[page], krbuf.at[buf_slot, i], sems.at[1, buf_slot]))
        return copies

    @pl.when(step == 0)
    def _():
        for cp in page_copies(0, 0):
            cp.start()

    @pl.when(step + 1 < pl.num_programs(0) * n_j)
    def _():
        for cp in page_copies(step + 1, 1 - slot):
            cp.start()

    @pl.when(j == 0)
    def _():
        m_s[...] = jnp.full(m_s.shape, -jnp.inf, f32)
        l_s[...] = jnp.zeros(l_s.shape, f32)
        acc_s[...] = jnp.zeros(acc_s.shape, f32)

    for cp in page_copies(step, slot):
        cp.wait()

    qr = qr_ref[0]
    lhs = jnp.concatenate([qabs_ref[0], wukt_ref[...]], axis=0)
    sub = min(SUB_PAGES, pp)
    blocks = list(range(0, pp, sub))

    def load(b):
        c_bf = jnp.concatenate([cbuf[slot, i].astype(bf16) for i in range(b, b + sub)], axis=0)
        return (c_bf,) + _latent_products_t(lhs, c_bf)

    state = (m_s[...], l_s[...], acc_s[...])
    nxt = load(blocks[0])
    for n, b in enumerate(blocks):
        c_bf, st_lat, ssq = nxt
        if n + 1 < len(blocks):
            nxt = load(blocks[n + 1])
        krt = jnp.concatenate([krbuf[slot, i] for i in range(b, b + sub)], axis=1)
        state = _softmax_update_t(_scores_t(st_lat, ssq, krt, qr), c_bf, state)
    m_s[...], l_s[...], acc_s[...] = state

    @pl.when(j == n_j - 1)
    def _():
        c_bf = cnew_ref[0].astype(bf16)
        n_pad = c_bf.shape[0]
        key_t = lax.broadcasted_iota(jnp.int32, (N_TH, n_pad), 1)
        q_t = lax.broadcasted_iota(jnp.int32, (N_TH, n_pad), 0) // N_HEADS
        st = _scores_t(*_latent_products_t(lhs, c_bf), krtnew_ref[0], qr)
        m, l, acc = _softmax_update_t(st, c_bf, state, mask=key_t <= q_t)
        o_ref[0] = acc / l


def _attn_sample(page_table, qabs, qr, c_new, kr_new, cache_c, cache_kr, wuk_perm, pp):
    n_seq, n_pages = page_table.shape
    kv_rank = cache_c.shape[-1]
    t_new = c_new.shape[1]
    seq3 = lambda s, j, pt: (s, 0, 0)
    cache_kr_t = jnp.swapaxes(cache_kr, 1, 2)
    c_new_pad = jnp.pad(c_new, ((0, 0), (0, PAGE_SIZE - t_new), (0, 0)))
    krt_new_pad = jnp.pad(jnp.swapaxes(kr_new, 1, 2), ((0, 0), (0, 0), (0, PAGE_SIZE - t_new)))
    grid_spec = pltpu.PrefetchScalarGridSpec(
        num_scalar_prefetch=1,
        grid=(n_seq, n_pages // pp),
        in_specs=[pl.BlockSpec((1, N_TH, kv_rank), seq3),
                  pl.BlockSpec((1, N_TH, QK_ROPE_DIM), seq3),
                  pl.BlockSpec((1, PAGE_SIZE, kv_rank), seq3),
                  pl.BlockSpec((1, QK_ROPE_DIM, PAGE_SIZE), seq3),
                  pl.BlockSpec(wuk_perm.shape, lambda s, j, pt: (0, 0)),
                  pl.BlockSpec(memory_space=pl.ANY),
                  pl.BlockSpec(memory_space=pl.ANY)],
        out_specs=pl.BlockSpec((1, N_TH, kv_rank), seq3),
        scratch_shapes=[pltpu.VMEM((2, pp, PAGE_SIZE, kv_rank), f32),
                        pltpu.VMEM((2, pp, QK_ROPE_DIM, PAGE_SIZE), f32),
                        pltpu.SemaphoreType.DMA((2, 2)),
                        pltpu.VMEM((N_TH, 1), f32), pltpu.VMEM((N_TH, 1), f32),
                        pltpu.VMEM((N_TH, kv_rank), f32)],
    )
    return pl.pallas_call(
        functools.partial(_attn_sample_kernel, pp=pp),
        grid_spec=grid_spec,
        out_shape=jax.ShapeDtypeStruct((n_seq, N_TH, kv_rank), f32),
        compiler_params=_cparams("arbitrary", "arbitrary"),
        name="attn_sample",
    )(page_table.reshape(-1), qabs, qr, c_new_pad, krt_new_pad, wuk_perm, cache_c, cache_kr_t)


def _v_up_kernel(lat_ref, wuv_ref, o_ref):
    rows = lat_ref.shape[0]
    full = jnp.dot(lat_ref[...].astype(bf16), wuv_ref[...], preferred_element_type=f32)
    head_of_row = lax.broadcasted_iota(jnp.int32, full.shape, 0) % N_HEADS
    head_of_lane = lax.broadcasted_iota(jnp.int32, full.shape, 1) // V_DIM
    own = jnp.where(head_of_row == head_of_lane, full, 0.0)
    o_ref[...] = jnp.sum(own.reshape(rows // N_HEADS, N_HEADS, full.shape[1]), axis=1)


def _v_up(lat2d, wuv, rows):
    n, kv_rank = lat2d.shape
    d_attn = wuv.shape[1]
    return pl.pallas_call(
        _v_up_kernel,
        grid=(n // rows,),
        in_specs=[pl.BlockSpec((rows, kv_rank), lambda i: (i, 0)), pl.BlockSpec(wuv.shape, lambda i: (0, 0))],
        out_specs=pl.BlockSpec((rows // N_HEADS, d_attn), lambda i: (i, 0)),
        out_shape=jax.ShapeDtypeStruct((n // N_HEADS, d_attn), f32),
        compiler_params=_cparams("parallel"),
        name="v_up",
    )(lat2d, wuv)


def _attn_sample_all(q_s2d, c_new, kr_new, page_table, cache_c, cache_kr, w_ukv, g_qk_k, pp):
    n_seq, t_new, kv_rank = c_new.shape
    qabs, qr = _q_absorb(q_s2d, g_qk_k, w_ukv)
    qabs = qabs.reshape(n_seq, t_new * N_HEADS, kv_rank)
    qr = qr.reshape(n_seq, t_new * N_HEADS, QK_ROPE_DIM)
    wuk_perm = jnp.transpose(w_ukv[:, :, :QK_NOPE_DIM], (2, 1, 0)).reshape(-1, kv_rank).astype(bf16)
    lat = _attn_sample(page_table, qabs, qr, c_new, kr_new, cache_c, cache_kr, wuk_perm, pp)
    wuv = w_ukv[:, :, QK_NOPE_DIM:].reshape(kv_rank, N_HEADS * V_DIM).astype(bf16)
    rows = min(512, n_seq * N_TH)
    return _v_up(lat.reshape(n_seq * N_TH, kv_rank), wuv, rows)


def _rms(x, g):
    return x * lax.rsqrt(jnp.mean(x * x, axis=-1, keepdims=True) + NORM_EPS) * g


def _split_bf16(x):
    hi = x.astype(bf16)
    return hi, (x - hi.astype(f32)).astype(bf16)


PACK_W = 256
u32 = jnp.uint32


def _pack_bf16_pairs(x):
    bits = lax.bitcast_convert_type(x.astype(bf16).astype(f32), u32)
    pieces = []
    for p in range(x.shape[1] // (2 * PACK_W)):
        lo = bits[:, 2 * p * PACK_W:(2 * p + 1) * PACK_W]
        hi = bits[:, (2 * p + 1) * PACK_W:(2 * p + 2) * PACK_W]
        pieces.append((lo >> 16) | (hi & jnp.uint32(0xFFFF0000)))
    return pieces


def _unpack_bf16_pairs(words):
    return (lax.bitcast_convert_type(words << 16, f32),
            lax.bitcast_convert_type(words & jnp.uint32(0xFFFF0000), f32))


def _merge_kernel(xp_ref, cvp_ref, atp_ref, xs_ref, cvs_ref, ats_ref, gc_ref, ga_ref, woc_ref, woa_ref, gf_ref,
                  wrh_ref, wrl_ref, br_ref, h_ref, xna_ref, xnb_ref, exp_ref, gate_ref, rank_ref, cnt_ref, cnt_s,
                  *, n_prompt_tiles):
    i = pl.program_id(0)
    tm = xp_ref.shape[0]

    @pl.when(i == 0)
    def _():
        cnt_s[...] = jnp.zeros(cnt_s.shape, f32)

    is_p = i < n_prompt_tiles
    rows = tm

    def project(r):
        x = jnp.where(is_p, xp_ref[r, :], xs_ref[r, :])
        yc = _rms(jnp.where(is_p, cvp_ref[r, :], cvs_ref[r, :]), gc_ref[...]).astype(bf16)
        ya = _rms(jnp.where(is_p, atp_ref[r, :], ats_ref[r, :]), ga_ref[...]).astype(bf16)
        h = x + jnp.dot(yc, woc_ref[...], preferred_element_type=f32) \
            + jnp.dot(ya, woa_ref[...], preferred_element_type=f32)
        h_ref[r, :] = h
        xn = _rms(h, gf_ref[...])
        xh, xl = _split_bf16(xn)
        xna_ref[r, :], xnb_ref[r, :] = _pack_bf16_pairs(xn)
        return (jnp.dot(xh, wrh_ref[...], preferred_element_type=f32)
                + jnp.dot(xl, wrh_ref[...], preferred_element_type=f32)
                + jnp.dot(xh, wrl_ref[...], preferred_element_type=f32)) + br_ref[...]

    def route(logits, cnt):
        lt = logits.T[:N_EXPERTS]
        eidx = lax.broadcasted_iota(jnp.int32, lt.shape, 0)
        tops, idxs = [], []
        cur = lt
        for _k in range(TOP_K):
            mk = jnp.max(cur, axis=0, keepdims=True)
            ik = jnp.min(jnp.where(cur == mk, eidx, N_EXPERTS), axis=0, keepdims=True)
            tops.append(mk)
            idxs.append(ik)
            cur = jnp.where(eidx == ik, -jnp.inf, cur)
        es = [jnp.exp(t - tops[0]) for t in tops]
        den = es[0] + es[1] + es[2] + es[3]
        onehots = [(eidx == ik).astype(f32) for ik in idxs]
        oh = onehots[0] + onehots[1] + onehots[2] + onehots[3]
        s_i = lax.broadcasted_iota(jnp.int32, (rows, rows), 0)
        t_i = lax.broadcasted_iota(jnp.int32, (rows, rows), 1)
        before = jnp.dot(oh.astype(bf16), (s_i < t_i).astype(bf16), preferred_element_type=f32) + cnt
        ranks = [jnp.sum(oh_k * before, axis=0, keepdims=True).astype(jnp.int32) for oh_k in onehots]
        pad_i = jnp.zeros((ROUTE_ROWS - TOP_K, rows), jnp.int32)
        pad_f = jnp.zeros((ROUTE_ROWS - TOP_K, rows), f32)
        exp_ref[...] = jnp.concatenate(idxs + [pad_i], axis=0)
        gate_ref[...] = jnp.concatenate([e / den for e in es] + [pad_f], axis=0)
        rank_ref[...] = jnp.concatenate(ranks + [pad_i], axis=0)
        return cnt + jnp.sum(oh, axis=1, keepdims=True)

    cnt = route(project(pl.ds(0, tm)), cnt_s[...])
    cnt_s[...] = cnt
    cnt_ref[...] = cnt


ROUTE_ROWS = 8


def _merge(x_p, cv_p, at_p, x_s, cv_s, at_s, g_out_conv, g_out_attn, w_out, g_ffn, w_router, b_router, tm):
    (n_p, d_model), n_s = x_p.shape, x_s.shape[0]
    n = n_p + n_s
    d_conv, d_attn = cv_p.shape[1], at_p.shape[1]
    woc, woa = w_out[:d_conv].astype(bf16), w_out[d_conv:].astype(bf16)
    wr = jnp.pad(w_router, ((0, 0), (0, LANES - N_EXPERTS)))
    wrh, wrl = _split_bf16(wr)
    br = jnp.concatenate([b_router.astype(f32), jnp.full((LANES - N_EXPERTS,), -jnp.inf, f32)])[None, :]
    gc, ga, gf = g_out_conv[None, :], g_out_attn[None, :], g_ffn[None, :]
    tiles_p, tiles_s = n_p // tm, n_s // tm
    full = lambda a: pl.BlockSpec(a.shape, lambda i: (0, 0))
    rows = lambda w: pl.BlockSpec((tm, w), lambda i: (i, 0))
    rows_p = lambda w: pl.BlockSpec((tm, w), lambda i: (jnp.minimum(i, tiles_p - 1), 0))
    rows_s = lambda w: pl.BlockSpec((tm, w), lambda i: (jnp.maximum(i - tiles_p, 0), 0))
    assert d_model == 4 * PACK_W
    route = lambda dt: jax.ShapeDtypeStruct((ROUTE_ROWS, n), dt)
    route_blk = pl.BlockSpec((ROUTE_ROWS, tm), lambda i: (0, i))
    out_shape = (jax.ShapeDtypeStruct((n, d_model), f32),
                 jax.ShapeDtypeStruct((n, PACK_W), u32), jax.ShapeDtypeStruct((n, PACK_W), u32),
                 route(jnp.int32), route(f32), route(jnp.int32), jax.ShapeDtypeStruct((N_EXPERTS, 1), f32))
    return pl.pallas_call(
        functools.partial(_merge_kernel, n_prompt_tiles=tiles_p),
        grid=(tiles_p + tiles_s,),
        in_specs=[rows_p(d_model), rows_p(d_conv), rows_p(d_attn), rows_s(d_model), rows_s(d_conv), rows_s(d_attn),
                  full(gc), full(ga), full(woc), full(woa), full(gf), full(wrh), full(wrl), full(br)],
        out_specs=(rows(d_model), rows(PACK_W), rows(PACK_W), route_blk, route_blk, route_blk,
                   pl.BlockSpec((N_EXPERTS, 1), lambda i: (0, 0))),
        out_shape=out_shape,
        scratch_shapes=[pltpu.VMEM((N_EXPERTS, 1), f32)],
        compiler_params=_cparams("arbitrary"),
        name="merge",
    )(x_p, cv_p, at_p, x_s, cv_s, at_s, gc, ga, woc, woa, gf, wrh, wrl, br)


def _experts_kernel(be_ref, br_ref, xa_ref, xb_ref, wgu_ref, bgu_ref, wd_ref, bd_ref, ya_ref, yb_ref, wgu_s, wd_s):
    i = pl.program_id(0)
    d_expert = wd_ref.shape[1]
    blk = xa_ref.shape[0]

    @pl.when((i == 0) | (be_ref[i] != be_ref[jnp.maximum(i - 1, 0)]))
    def _():
        wgu_s[...] = wgu_ref[0].astype(bf16)
        wd_s[...] = wd_ref[0].astype(bf16)

    @pl.when(br_ref[i] > 0)
    def _():
        x = jnp.concatenate(_unpack_bf16_pairs(xa_ref[...]) + _unpack_bf16_pairs(xb_ref[...]), axis=1)
        live = lax.broadcasted_iota(jnp.int32, (blk, 1), 0) < br_ref[i]
        x = jnp.where(live, x, 0.0).astype(bf16)
        gu = jnp.dot(x, wgu_s[...], preferred_element_type=f32) + bgu_ref[0]
        gate = jnp.minimum(gu[:, :d_expert], SWIGLU_LIMIT)
        up = jnp.clip(gu[:, d_expert:], -SWIGLU_LIMIT, SWIGLU_LIMIT)
        hid = (up + 1.0) * gate * jax.nn.sigmoid(SWIGLU_ALPHA * gate)
        y = jnp.dot(hid.astype(bf16), wd_s[...], preferred_element_type=f32) + bd_ref[0]
        ya_ref[...], yb_ref[...] = _pack_bf16_pairs(y)

    @pl.when(br_ref[i] <= 0)
    def _():
        ya_ref[...] = jnp.zeros(ya_ref.shape, ya_ref.dtype)
        yb_ref[...] = jnp.zeros(yb_ref.shape, yb_ref.dtype)


def _experts(block_expert, block_rows, xs_a, xs_b, w_gate_up, b_gate_up, w_down, b_down, blk):
    n_slots = xs_a.shape[0]
    n_exp, d_model, d_gu = w_gate_up.shape
    d_expert = w_down.shape[1]
    slots = pl.BlockSpec((blk, PACK_W), lambda i, be, br: (i, 0))
    grid_spec = pltpu.PrefetchScalarGridSpec(
        num_scalar_prefetch=2,
        grid=(n_slots // blk,),
        in_specs=[slots, slots,
                  pl.BlockSpec((1, d_model, d_gu), lambda i, be, br: (be[i], 0, 0)),
                  pl.BlockSpec((1, 1, d_gu), lambda i, be, br: (be[i], 0, 0)),
                  pl.BlockSpec((1, d_expert, d_model), lambda i, be, br: (be[i], 0, 0)),
                  pl.BlockSpec((1, 1, d_model), lambda i, be, br: (be[i], 0, 0))],
        out_specs=(slots, slots),
        scratch_shapes=[pltpu.VMEM((d_model, d_gu), bf16), pltpu.VMEM((d_expert, d_model), bf16)],
    )
    packed = jax.ShapeDtypeStruct((n_slots, PACK_W), u32)
    return pl.pallas_call(
        _experts_kernel,
        grid_spec=grid_spec,
        out_shape=(packed, packed),
        compiler_params=_cparams("arbitrary"),
        name="experts",
    )(block_expert, block_rows, xs_a, xs_b, w_gate_up, b_gate_up[:, None, :], w_down, b_down[:, None, :])


def _combine_kernel(h_ref, ya_ref, yb_ref, gate_ref, yp_ref, ys_ref, *, n_prompt_tiles):
    i = pl.program_id(0)
    g = gate_ref[...].T
    acc = None
    for k in range(TOP_K):
        gk = g[:, k:k + 1]
        pieces = [p * gk for p in _unpack_bf16_pairs(ya_ref[k]) + _unpack_bf16_pairs(yb_ref[k])]
        acc = pieces if acc is None else [a + p for a, p in zip(acc, pieces)]
    y = h_ref[...] + jnp.concatenate(acc, axis=1)

    @pl.when(i < n_prompt_tiles)
    def _():
        yp_ref[...] = y

    @pl.when(i >= n_prompt_tiles)
    def _():
        ys_ref[...] = y


def _combine(h, ya, yb, gates, n_p, tm):
    n, d_model = h.shape
    tiles_p = n_p // tm
    packed = pl.BlockSpec((TOP_K, tm, PACK_W), lambda i: (0, i, 0))
    return pl.pallas_call(
        functools.partial(_combine_kernel, n_prompt_tiles=tiles_p),
        grid=(n // tm,),
        in_specs=[pl.BlockSpec((tm, d_model), lambda i: (i, 0)), packed, packed,
                  pl.BlockSpec((ROUTE_ROWS, tm), lambda i: (0, i))],
        out_specs=(pl.BlockSpec((tm, d_model), lambda i: (jnp.minimum(i, tiles_p - 1), 0)),
                   pl.BlockSpec((tm, d_model), lambda i: (jnp.maximum(i - tiles_p, 0), 0))),
        out_shape=(jax.ShapeDtypeStruct((n_p, d_model), f32), jax.ShapeDtypeStruct((n - n_p, d_model), f32)),
        compiler_params=_cparams("arbitrary"),
        name="combine",
    )(h, ya, yb, gates)


SC_WINDOW = 128


def _sc_mesh():
    return plsc.VectorSubcoreMesh(core_axis_name="core", subcore_axis_name="subcore")


def _sc_scatter_rows(x, idx, n_k, n_out):
    n, w = x.shape

    @pl.kernel(out_type=jax.ShapeDtypeStruct((n_out, w), x.dtype), mesh=_sc_mesh(), scratch_types=[])
    def scatter(x_hbm, i_hbm, o_hbm):
        def body(x_vmem, *i_vmem):
            for iv in i_vmem:
                pltpu.sync_copy(x_vmem, o_hbm.at[iv.at[0]])

        pltpu.emit_pipeline(
            body,
            grid=(n // SC_WINDOW,),
            in_specs=[pl.BlockSpec((SC_WINDOW, w), lambda i: (i, 0))]
            + [pl.BlockSpec((1, SC_WINDOW), functools.partial(lambda k, i: (k, i), k)) for k in range(n_k)],
            out_specs=[],
            core_axis_name=("core", "subcore"),
            dimension_semantics=(pltpu.PARALLEL,),
        )(x_hbm, *([i_hbm] * n_k))

    return scatter(x, idx)


def _sc_gather_rows(x, idx):
    n_idx, w = idx.shape[0], x.shape[1]

    @pl.kernel(out_type=jax.ShapeDtypeStruct((n_idx, w), x.dtype), mesh=_sc_mesh(), scratch_types=[])
    def gather(x_hbm, i_hbm, o_hbm):
        def body(i_vmem, o_vmem):
            pltpu.sync_copy(x_hbm.at[i_vmem.at[0]], o_vmem)

        pltpu.emit_pipeline(
            body,
            grid=(n_idx // SC_WINDOW,),
            in_specs=[pl.BlockSpec((1, SC_WINDOW), lambda i: (0, i))],
            out_specs=[pl.BlockSpec((SC_WINDOW, w), lambda i: (i, 0))],
            core_axis_name=("core", "subcore"),
            dimension_semantics=(pltpu.PARALLEL,),
        )(i_hbm, o_hbm)

    return gather(x, idx.reshape(1, n_idx))


EXPERT_BLK = 512
SAMPLE_PAGES_PER_STEP = 64


def _slots_kernel(sp_ref, exp_ref, rank_ref, slot_ref):
    expert = exp_ref[...]
    slot = rank_ref[...]
    for e in range(N_EXPERTS):
        slot = slot + jnp.where(expert == e, sp_ref[e], 0)
    slot_ref[...] = slot


def _slots(start_padded, expert, rank):
    whole = pl.BlockSpec(expert.shape, lambda i, sp: (0, 0))
    return pl.pallas_call(
        _slots_kernel,
        grid_spec=pltpu.PrefetchScalarGridSpec(num_scalar_prefetch=1, grid=(1,), in_specs=[whole, whole],
                                               out_specs=whole),
        out_shape=jax.ShapeDtypeStruct(expert.shape, jnp.int32),
        compiler_params=_cparams("arbitrary"),
        name="slots",
    )(start_padded, expert, rank)


def _moe(h, xn_a, xn_b, expert, gates, rank, counts, n_p, w_gate_up, b_gate_up, w_down, b_down):
    n = h.shape[0]
    blk = EXPERT_BLK
    padded = (counts + blk - 1) // blk * blk
    padded_end = jnp.cumsum(padded)
    start_padded = padded_end - padded
    slot_t = _slots(start_padded, expert, rank)
    n_blocks = -(-(n * TOP_K + N_EXPERTS * (blk - 1)) // blk)
    n_slots = n_blocks * blk
    block_start = jnp.arange(n_blocks, dtype=jnp.int32) * blk
    block_expert = jnp.minimum(jnp.sum((block_start[:, None] >= padded_end[None, :]).astype(jnp.int32), axis=1),
                               N_EXPERTS - 1)
    of_block = block_expert[:, None] == jnp.arange(N_EXPERTS, dtype=jnp.int32)[None, :]
    block_rows = jnp.clip(jnp.sum(jnp.where(of_block, (counts + start_padded)[None, :], 0), axis=1) - block_start,
                          0, blk)
    xs_a = _sc_scatter_rows(xn_a, slot_t, TOP_K, n_slots)
    xs_b = _sc_scatter_rows(xn_b, slot_t, TOP_K, n_slots)
    ys_a, ys_b = _experts(block_expert, block_rows, xs_a, xs_b, w_gate_up, b_gate_up, w_down, b_down, blk)
    flat = slot_t[:TOP_K].reshape(-1)
    ya = _sc_gather_rows(ys_a, flat).reshape(TOP_K, n, PACK_W)
    yb = _sc_gather_rows(ys_b, flat).reshape(TOP_K, n, PACK_W)
    return _combine(h, ya, yb, gates, n_p, 512)


def _layer(x_p, x_s, cache_c, cache_kr, state, page_table, norm_mix, w_in, norm_q_lat, w_uq, norm_kv_lat, w_ukv,
           norm_qk_q, norm_qk_k, conv_w, conv_b, conv_ln_g, conv_ln_b, norm_out_conv, norm_out_attn, w_out,
           norm_ffn, w_router, b_router, w_gate_up, b_gate_up, w_down, b_down):
    n_b, t_p, d_model = x_p.shape
    n_seq, t_s, _ = x_s.shape
    d_conv = conv_w.shape[1]
    q_rank, kv_rank = norm_q_lat.shape[0], norm_kv_lat.shape[0]
    n_past = page_table.shape[1] * PAGE_SIZE
    win, wuq, wk, wvt = _mix_in_weights(w_in, w_uq, w_ukv, d_conv, q_rank, kv_rank)
    mix = functools.partial(_mix_in, g_mix=norm_mix, win=win, g_q_lat=norm_q_lat, wuq=wuq, g_kv_lat=norm_kv_lat,
                            g_qk_q=norm_qk_q, wk=wk, wvt=wvt, g_qk_k=norm_qk_k,
                            d_conv=d_conv, q_rank=q_rank, kv_rank=kv_rank)
    x_p2, x_s2 = x_p.reshape(n_b * t_p, d_model), x_s.reshape(n_seq * t_s, d_model)
    tab_p = _rope_head_tables(jnp.arange(t_p))
    tab_s = _rope_head_tables(n_past + jnp.arange(n_seq * t_s) % t_s)
    glu_p, q_p, c_p, kr_p, k_p, vt_p, cv_p = mix(x_p2, tab_p, MIX_TILE, seq_len=t_p, q_scale=ATTN_SCALE * LOG2_E,
                                                 conv=(conv_w, conv_b, conv_ln_g, conv_ln_b))
    glu_s, q_s, c_s, kr_s, _, _ = mix(x_s2, tab_s, MIX_TILE, seq_len=n_seq * t_s, q_scale=ATTN_SCALE)

    glu_p3, glu_s3 = glu_p.reshape(n_b, t_p, d_conv), glu_s.reshape(n_seq, t_s, d_conv)
    cv_s = _conv_sample(glu_s3, state, conv_w, conv_b, conv_ln_g, conv_ln_b)

    hp = N_HEADS * HEAD_PAD
    at_p = _attn_prompt(q_p.reshape(n_b, t_p, hp), k_p.reshape(n_b, t_p, hp), vt_p, 512, 512)
    c_s3, kr_s3 = c_s.reshape(n_seq, t_s, kv_rank), kr_s.reshape(n_seq, t_s, QK_ROPE_DIM)
    at_s = _attn_sample_all(q_s, c_s3, kr_s3, page_table, cache_c, cache_kr, w_ukv, norm_qk_k, SAMPLE_PAGES_PER_STEP)

    h, xn_a, xn_b, expert, gates, rank, cnt = _merge(
        x_p2, cv_p.reshape(n_b * t_p, d_conv), at_p.reshape(n_b * t_p, -1),
        x_s2, cv_s.reshape(n_seq * t_s, d_conv), at_s,
        norm_out_conv, norm_out_attn, w_out, norm_ffn, w_router, b_router, 512)
    counts = cnt[:, 0].astype(jnp.int32)
    y_p, y_s = _moe(h, xn_a, xn_b, expert, gates, rank, counts, n_b * t_p,
                    w_gate_up, b_gate_up, w_down, b_down)
    n_hist = CONV_WIDTH - 1
    conv_state_p = jnp.concatenate([jnp.zeros((n_b, max(n_hist - t_p, 0), d_conv), f32),
                                    glu_p3[:, max(t_p - n_hist, 0):]], axis=1)
    conv_state_s = jnp.concatenate([state, glu_s3], axis=1)[:, -n_hist:]
    return (y_p.reshape(x_p.shape), y_s.reshape(x_s.shape), c_p.reshape(n_b, t_p, kv_rank),
            kr_p.reshape(n_b, t_p, QK_ROPE_DIM), conv_state_p, c_s3, kr_s3, conv_state_s)


def kernel(x_prompt, x_sample, cache_kv_latent, cache_k_rope, state_conv, page_table, norm_mix, w_in, norm_q_lat,
           w_uq, norm_kv_lat, w_ukv, norm_qk_q, norm_qk_k, conv_w, conv_b, conv_ln_g, conv_ln_b, norm_out_conv,
           norm_out_attn, w_out, norm_ffn, w_router, b_router, w_gate_up, b_gate_up, w_down, b_down):
    h_p, h_s = x_prompt, x_sample
    per_layer = []
    for l in range(w_in.shape[0]):
        outs = _layer(h_p, h_s, cache_kv_latent[l], cache_k_rope[l], state_conv[l], page_table, norm_mix[l],
                      w_in[l], norm_q_lat[l], w_uq[l], norm_kv_lat[l], w_ukv[l], norm_qk_q[l], norm_qk_k[l],
                      conv_w[l], conv_b[l], conv_ln_g[l], conv_ln_b[l], norm_out_conv[l], norm_out_attn[l],
                      w_out[l], norm_ffn[l], w_router[l], b_router[l], w_gate_up[l], b_gate_up[l], w_down[l],
                      b_down[l])
        h_p, h_s = outs[0], outs[1]
        per_layer.append(outs[2:])
    stacked = [jnp.stack([pl_[i] for pl_ in per_layer]) for i in range(6)]
    return (h_p, h_s, *stacked)
```
